```python
import math
import jax
import jax.numpy as jnp
from jax import lax
import numpy as np

D_MODEL = 1024
BATCH = 32
SEQ = 256
DEPTH = 2
DEC_BATCH = 4
DEC_SEQ = 2048
PAST_LEN = 256

GRID_W = 64
N_EVEN = (DEPTH + 1) // 2
N_ODD = DEPTH // 2
HEAD_DIM = 64
DIFF_HEADS = 4
DIFF_V_DIM = 2 * HEAD_DIM
NA_HEADS = 8
MIX_WIDTH = DIFF_HEADS * DIFF_V_DIM + NA_HEADS * HEAD_DIM
DIFF_QK_COLS = DIFF_HEADS * 2 * HEAD_DIM
DIFF_V_COLS = DIFF_HEADS * DIFF_V_DIM
NA_COLS = NA_HEADS * HEAD_DIM
IN_COLS = 2 * DIFF_QK_COLS + DIFF_V_COLS + 3 * NA_COLS
IN_SPLITS = (DIFF_QK_COLS, 2 * DIFF_QK_COLS, 2 * DIFF_QK_COLS + DIFF_V_COLS,
             2 * DIFF_QK_COLS + DIFF_V_COLS + NA_COLS, 2 * DIFF_QK_COLS + DIFF_V_COLS + 2 * NA_COLS)
NA_KH_MAX = 8
NA_KW = 16
ROPE_BASE = 10000.0
Q_BLOCK = 128
RWKV_N = 64
RWKV_HEADS = D_MODEL // RWKV_N
DECAY_LORA = 64
AAA_LORA = 64
GATE_LORA = 160
FFN_DIM = 2816
N_EXPERTS = 8
TOP_K = 2
EXPERT_DIM = 3584
EPS = 1e-6
GN_EPS = 64e-5

kernel_name = "hybrid_diffusion_diffattn_natten_rwkv7_moe_step"


def rms_norm(x, g):
    xf = x.astype(jnp.float32)
    y = xf * lax.rsqrt(jnp.mean(xf * xf, axis=-1, keepdims=True) + EPS)
    return (y * g).astype(x.dtype)


def adaln(cond, w, b):
    m = (jax.nn.silu(cond) @ w + b)[:, None, :]
    return jnp.split(m, 6, axis=-1)


def modulate(x, g, shift, scale):
    return rms_norm(x, g) * (1.0 + scale) + shift


def axial_rope_angles(n_tokens, dim):
    t = jnp.arange(n_tokens)
    rows = (t // GRID_W).astype(jnp.float32)
    cols = (t % GRID_W).astype(jnp.float32)
    nf = dim // 4
    inv = ROPE_BASE ** (-jnp.arange(nf, dtype=jnp.float32) / nf)
    ang = jnp.concatenate([rows[:, None] * inv, cols[:, None] * inv], axis=-1)
    return jnp.cos(ang), jnp.sin(ang)


def apply_rope(x, cos, sin):
    xp = x.reshape(x.shape[:-1] + (x.shape[-1] // 2, 2))
    a, b = xp[..., 0], xp[..., 1]
    c = cos[None, :, None, :].astype(x.dtype)
    s = sin[None, :, None, :].astype(x.dtype)
    return jnp.stack([a * c - b * s, a * s + b * c], axis=-1).reshape(x.shape)


def sweep_query_blocks(fn, *qs):
    B, T = qs[0].shape[:2]
    nb = T // Q_BLOCK
    blocks = tuple(jnp.moveaxis(q.reshape((B, nb, Q_BLOCK) + q.shape[2:]), 1, 0) for q in qs)
    out = lax.map(lambda blk: fn(*blk), blocks)
    out = jnp.moveaxis(out, 0, 1)
    return out.reshape((B, T) + out.shape[3:])


def diff_attention(q1, q2, k1, k2, v, lam):
    scale = HEAD_DIM ** -0.5

    def block(q1b, q2b):
        s1 = jnp.einsum('bqhd,bkhd->bhqk', q1b, k1).astype(jnp.float32) * scale
        s2 = jnp.einsum('bqhd,bkhd->bhqk', q2b, k2).astype(jnp.float32) * scale
        p = jax.nn.softmax(s1, axis=-1) - lam * jax.nn.softmax(s2, axis=-1)
        return jnp.einsum('bhqk,bkhe->bqhe', p.astype(v.dtype), v)

    return sweep_query_blocks(block, q1, q2)


def dense_attention(q, k, v):
    scale = HEAD_DIM ** -0.5

    def block(qb):
        s = jnp.einsum('bqhd,bkhd->bhqk', qb, k).astype(jnp.float32) * scale
        p = jax.nn.softmax(s, axis=-1)
        return jnp.einsum('bhqk,bkhd->bqhd', p.astype(v.dtype), v)

    return sweep_query_blocks(block, q)


def neighbourhood_attention(q, k, v, k_ctx, v_ctx, rpb):
    B, T, H, d = q.shape
    R = T // GRID_W
    kh = min(NA_KH_MAX, R)
    scale = HEAD_DIM ** -0.5
    qg = q.reshape(B, R, GRID_W, H, d)
    kg = k.reshape(B, R, GRID_W, H, d)
    vg = v.reshape(B, R, GRID_W, H, d)
    r = jnp.arange(R)
    row_start = jnp.clip(r - kh // 2, 0, R - kh)
    rows_idx = row_start[:, None] + jnp.arange(kh)[None, :]
    k_rows = kg[:, rows_idx]
    v_rows = vg[:, rows_idx]
    s_loc = jnp.einsum('brqhd,brjkhd->bhrqjk', qg, k_rows).astype(jnp.float32) * scale
    wq = jnp.arange(GRID_W)
    col_start = jnp.clip(wq - NA_KW // 2, 0, GRID_W - NA_KW)
    col_in = (wq[None, :] >= col_start[:, None]) & (wq[None, :] < col_start[:, None] + NA_KW)
    col_off = jnp.clip(wq[None, :] - wq[:, None], -(NA_KW - 1), NA_KW - 1) + (NA_KW - 1)
    row_off = rows_idx - r[:, None] + (NA_KH_MAX - 1)
    bias = rpb[:, row_off[:, None, :, None], col_off[None, :, None, :]]
    s_loc = jnp.where(col_in[None, None, None, :, None, :],
                      s_loc + bias[None].astype(jnp.float32), -jnp.inf)
    s_ctx = jnp.einsum('brqhd,bkhd->bhrqk', qg, k_ctx).astype(jnp.float32) * scale
    n_loc = kh * GRID_W
    s = jnp.concatenate([s_loc.reshape(B, H, R, GRID_W, n_loc), s_ctx], axis=-1)
    p = jax.nn.softmax(s, axis=-1).astype(v.dtype)
    p_loc = p[..., :n_loc].reshape(B, H, R, GRID_W, kh, GRID_W)
    p_ctx = p[..., n_loc:]
    o = (jnp.einsum('bhrqjk,brjkhd->brqhd', p_loc, v_rows)
         + jnp.einsum('bhrqk,bkhd->brqhd', p_ctx, v_ctx))
    return o.reshape(B, T, H, d)


def diff_lambda(lq1, lk1, lq2, lk2, lam_init):
    return (jnp.exp(jnp.sum((lq1 * lk1).astype(jnp.float32)))
            - jnp.exp(jnp.sum((lq2 * lk2).astype(jnp.float32))) + lam_init)


def even_projections(h, w_in, dq_g, dk_g, nq_g, nk_g):
    B, T, _ = h.shape
    dq, dk, dv, nq, nk, nv = jnp.split(h @ w_in, IN_SPLITS, axis=-1)
    dq = rms_norm(dq.reshape(B, T, DIFF_HEADS, 2, HEAD_DIM), dq_g)
    dk = rms_norm(dk.reshape(B, T, DIFF_HEADS, 2, HEAD_DIM), dk_g)
    dv = dv.reshape(B, T, DIFF_HEADS, DIFF_V_DIM)
    nq = rms_norm(nq.reshape(B, T, NA_HEADS, HEAD_DIM), nq_g)
    nk = rms_norm(nk.reshape(B, T, NA_HEADS, HEAD_DIM), nk_g)
    nv = nv.reshape(B, T, NA_HEADS, HEAD_DIM)
    return dq[..., 0, :], dq[..., 1, :], dk[..., 0, :], dk[..., 1, :], dv, nq, nk, nv


def even_output(o_diff, o_na, subln_g, w_out, lam_init):
    B, T = o_diff.shape[:2]
    o_diff = rms_norm(o_diff, subln_g) * (1.0 - lam_init)
    return jnp.concatenate([o_diff.reshape(B, T, -1), o_na.reshape(B, T, -1)], axis=-1) @ w_out


def even_mixer_context(h, w_in, w_out, dq_g, dk_g, nq_g, nk_g, subln_g, lam, lam_init):
    q1, q2, k1, k2, dv, nq, nk, nv = even_projections(h, w_in, dq_g, dk_g, nq_g, nk_g)
    o_d = diff_attention(q1, q2, k1, k2, dv, lam)
    o_n = dense_attention(nq, nk, nv)
    out = even_output(o_d, o_n, subln_g, w_out, lam_init)
    return out, jnp.concatenate([k1, k2], axis=-1), dv, nk, nv


def even_mixer_latent(h, c_dk, c_dv, c_nk, c_nv, w_in, w_out, dq_g, dk_g, nq_g, nk_g,
                      subln_g, lam, lam_init, rpb):
    T = h.shape[1]
    q1, q2, k1, k2, dv, nq, nk, nv = even_projections(h, w_in, dq_g, dk_g, nq_g, nk_g)
    cos, sin = axial_rope_angles(T, HEAD_DIM)
    q1, q2, k1, k2 = (apply_rope(t, cos, sin) for t in (q1, q2, k1, k2))
    k1_all = jnp.concatenate([c_dk[..., :HEAD_DIM], k1], axis=1)
    k2_all = jnp.concatenate([c_dk[..., HEAD_DIM:], k2], axis=1)
    v_all = jnp.concatenate([c_dv, dv], axis=1)
    o_d = diff_attention(q1, q2, k1_all, k2_all, v_all, lam)
    o_n = neighbourhood_attention(nq, nk, nv, c_nk, c_nv, rpb)
    return even_output(o_d, o_n, subln_g, w_out, lam_init)


def centred_shift(x):
    xp = jnp.pad(x, ((0, 0), (1, 1), (0, 0)))
    return 0.5 * (xp[:, :-2] + xp[:, 2:])


def rwkv_scan(s0, r, w, k, v, kk, a, reverse):
    def step(S, inp):
        r_t, w_t, k_t, v_t, kk_t, a_t = inp
        sa = jnp.einsum('bhij,bhj->bhi', S, -kk_t)
        S = (S * w_t[:, :, None, :] + sa[..., None] * (kk_t * a_t)[:, :, None, :]
             + v_t[..., None] * k_t[:, :, None, :])
        return S, jnp.einsum('bhij,bhj->bhi', S, r_t)

    xs = tuple(jnp.moveaxis(t, 1, 0) for t in (r, w, k, v, kk, a))
    S, ys = lax.scan(step, s0, xs, reverse=reverse)
    return S, jnp.moveaxis(ys, 0, 1)


def head_group_norm(y, g, b):
    B, T, H, N = y.shape
    mu = jnp.mean(y, axis=-1, keepdims=True)
    var = jnp.mean(jnp.square(y - mu), axis=-1, keepdims=True)
    return ((y - mu) * lax.rsqrt(var + GN_EPS)).reshape(B, T, H * N) * g + b


def rwkv_mixer(h, s0, mu, wr, wk, wv, wo, w0, w1, w2, a0, a1, a2, g1, g2, k_k, k_a, r_k, ln_g, ln_b):
    B, T, D = h.shape
    H, N = RWKV_HEADS, RWKV_N
    f32 = jnp.float32
    heads = lambda t: t.reshape(B, T, H, N).astype(f32)
    xx = centred_shift(h) - h
    xr, xw, xk, xv, xa, xg = (h + xx * mu[i] for i in range(6))
    r = heads(xr @ wr)
    k = heads(xk @ wk)
    v = heads(xv @ wv)
    g = jax.nn.sigmoid(xg @ g1) @ g2
    kk = k * k_k.reshape(H, N).astype(f32)
    kk = kk * lax.rsqrt(jnp.sum(kk * kk, axis=-1, keepdims=True) + 1e-12)
    k_a_h = k_a.reshape(H, N).astype(f32)
    r_k_f = r_k.astype(f32)
    s0 = s0.astype(f32)
    y_sum = jnp.zeros((B, T, H, N), f32)
    bonus = jnp.zeros((B, T, H, N), f32)
    finals = []
    for d in range(2):
        w_log = -jax.nn.softplus(-(w0[d] + jnp.tanh(xw @ w1[d]) @ w2[d]).astype(f32)) - 0.5
        decay = heads(jnp.exp(-jnp.exp(w_log)))
        a = heads(jax.nn.sigmoid((a0[d] + (xa @ a1[d]) @ a2[d]).astype(f32)))
        k_d = k * (1.0 + (a - 1.0) * k_a_h)
        S, y = rwkv_scan(s0[:, d], r, decay, k_d, v, kk, a, reverse=(d == 1))
        finals.append(S)
        y_sum = y_sum + y
        bonus = bonus + jnp.sum(r * k_d * r_k_f, axis=-1, keepdims=True) * v
    y = head_group_norm(y_sum, ln_g.astype(f32), ln_b.astype(f32)) + bonus.reshape(B, T, D)
    out = (y.astype(h.dtype) * g) @ wo
    return out, jnp.stack(finals, axis=1).astype(h.dtype)


def swiglu(h, w1, w3, w2):
    return (jax.nn.silu(h @ w1) * (h @ w3)) @ w2


def moe_swiglu(h, w_router, b_router, w1, w3, w2):
    B, T, D = h.shape
    x = h.reshape(B * T, D)
    logits = (x @ w_router + b_router).astype(jnp.float32)
    top_v, top_i = lax.top_k(logits, TOP_K)
    top_w = jax.nn.softmax(top_v, axis=-1)
    gates = jnp.sum(jax.nn.one_hot(top_i, N_EXPERTS, dtype=jnp.float32) * top_w[..., None], axis=1)
    gates = gates.astype(x.dtype)
    y = jnp.zeros_like(x)
    for e in range(N_EXPERTS):
        y = y + gates[:, e:e + 1] * swiglu(x, w1[e], w3[e], w2[e])
    return y.reshape(B, T, D)


def setup_inputs(seed: int = 0) -> dict:
    key = jax.random.key(seed)
    ks = jax.random.split(key, 80)
    counter = [0]

    def nrm(shape, scale):
        kk = ks[counter[0]]
        counter[0] += 1
        return jax.random.normal(kk, shape, jnp.float32) * scale

    D = D_MODEL
    H, N = RWKV_HEADS, RWKV_N
    inp = {}
    inp['x_prompt'] = nrm((BATCH, SEQ, D), 1.0)
    inp['x_sample'] = nrm((DEC_BATCH, DEC_SEQ, D), 1.0)
    inp['cache_diff_k'] = nrm((DEC_BATCH, N_EVEN, PAST_LEN, DIFF_HEADS, 2 * HEAD_DIM), 1.0)
    inp['cache_diff_v'] = nrm((DEC_BATCH, N_EVEN, PAST_LEN, DIFF_HEADS, DIFF_V_DIM), 1.0)
    inp['cache_na_k'] = nrm((DEC_BATCH, N_EVEN, PAST_LEN, NA_HEADS, HEAD_DIM), 1.0)
    inp['cache_na_v'] = nrm((DEC_BATCH, N_EVEN, PAST_LEN, NA_HEADS, HEAD_DIM), 1.0)
    inp['state_rwkv'] = nrm((DEC_BATCH, N_ODD, 2, H, N, N), 0.5)
    inp['c'] = nrm((DEC_BATCH, D), 1.0)
    inp['c_ctx'] = nrm((D,), 1.0)
    inp['w_ada'] = nrm((DEPTH, D, 6 * D), 0.5 * D ** -0.5)
    inp['b_ada'] = nrm((DEPTH, 6 * D), 0.02)
    inp['g_mix'] = 1.0 + nrm((DEPTH, D), 0.02)
    inp['g_ffn'] = 1.0 + nrm((DEPTH, D), 0.02)
    inp['w_in'] = nrm((N_EVEN, D, IN_COLS), D ** -0.5)
    inp['w_out'] = nrm((N_EVEN, MIX_WIDTH, D), MIX_WIDTH ** -0.5)
    inp['diff_q_g'] = 1.0 + nrm((N_EVEN, HEAD_DIM), 0.02)
    inp['diff_k_g'] = 1.0 + nrm((N_EVEN, HEAD_DIM), 0.02)
    inp['diff_lam_q1'] = nrm((N_EVEN, HEAD_DIM), 0.1)
    inp['diff_lam_k1'] = nrm((N_EVEN, HEAD_DIM), 0.1)
    inp['diff_lam_q2'] = nrm((N_EVEN, HEAD_DIM), 0.1)
    inp['diff_lam_k2'] = nrm((N_EVEN, HEAD_DIM), 0.1)
    inp['diff_subln_g'] = 1.0 + nrm((N_EVEN, DIFF_V_DIM), 0.02)
    inp['na_q_g'] = 1.0 + nrm((N_EVEN, HEAD_DIM), 0.02)
    inp['na_k_g'] = 1.0 + nrm((N_EVEN, HEAD_DIM), 0.02)
    inp['na_rpb'] = nrm((N_EVEN, NA_HEADS, 2 * NA_KH_MAX - 1, 2 * NA_KW - 1), 0.1)
    inp['ffn_w1'] = nrm((N_EVEN, D, FFN_DIM), D ** -0.5)
    inp['ffn_w3'] = nrm((N_EVEN, D, FFN_DIM), D ** -0.5)
    inp['ffn_w2'] = nrm((N_EVEN, FFN_DIM, D), FFN_DIM ** -0.5)
    inp['rw_mu'] = 0.5 + nrm((N_ODD, 6, D), 0.2)
    inp['rw_wr'] = nrm((N_ODD, D, D), D ** -0.5)
    inp['rw_wk'] = nrm((N_ODD, D, D), D ** -0.5)
    inp['rw_wv'] = nrm((N_ODD, D, D), D ** -0.5)
    inp['rw_wo'] = nrm((N_ODD, D, D), D ** -0.5)
    inp['rw_w0'] = nrm((N_ODD, 2, D), 0.5)
    inp['rw_w1'] = nrm((N_ODD, 2, D, DECAY_LORA), D ** -0.5)
    inp['rw_w2'] = nrm((N_ODD, 2, DECAY_LORA, D), DECAY_LORA ** -0.5)
    inp['rw_a0'] = nrm((N_ODD, 2, D), 0.5)
    inp['rw_a1'] = nrm((N_ODD, 2, D, AAA_LORA), D ** -0.5)
    inp['rw_a2'] = nrm((N_ODD, 2, AAA_LORA, D), AAA_LORA ** -0.5)
    inp['rw_g1'] = nrm((N_ODD, D, GATE_LORA), D ** -0.5)
    inp['rw_g2'] = nrm((N_ODD, GATE_LORA, D), GATE_LORA ** -0.5)
    inp['rw_k_k'] = 0.85 + nrm((N_ODD, D), 0.02)
    inp['rw_k_a'] = 1.0 + nrm((N_ODD, D), 0.02)
    inp['rw_r_k'] = nrm((N_ODD, H, N), 0.1)
    inp['rw_ln_g'] = 1.0 + nrm((N_ODD, D), 0.02)
    inp['rw_ln_b'] = nrm((N_ODD, D), 0.02)
    inp['moe_router'] = nrm((N_ODD, D, N_EXPERTS), D ** -0.5)
    inp['moe_router_b'] = nrm((N_ODD, N_EXPERTS), 0.01)
    inp['moe_w1'] = nrm((N_ODD, N_EXPERTS, D, EXPERT_DIM), D ** -0.5)
    inp['moe_w3'] = nrm((N_ODD, N_EXPERTS, D, EXPERT_DIM), D ** -0.5)
    inp['moe_w2'] = nrm((N_ODD, N_EXPERTS, EXPERT_DIM, D), EXPERT_DIM ** -0.5)
    return inp


def reference(x_prompt, x_sample, cache_diff_k, cache_diff_v, cache_na_k, cache_na_v, state_rwkv,
              c, c_ctx, w_ada, b_ada, g_mix, g_ffn, w_in, w_out, diff_q_g, diff_k_g,
              diff_lam_q1, diff_lam_k1, diff_lam_q2, diff_lam_k2, diff_subln_g, na_q_g, na_k_g,
              na_rpb, ffn_w1, ffn_w3, ffn_w2, rw_mu, rw_wr, rw_wk, rw_wv, rw_wo, rw_w0, rw_w1,
              rw_w2, rw_a0, rw_a1, rw_a2, rw_g1, rw_g2, rw_k_k, rw_k_a, rw_r_k, rw_ln_g, rw_ln_b,
              moe_router, moe_router_b, moe_w1, moe_w3, moe_w2):
    xp, xs = x_prompt, x_sample
    ctx_cond = c_ctx[None, :]
    new_dk, new_dv, new_nk, new_nv, new_st = [], [], [], [], []
    for l in range(DEPTH):
        p_sh_m, p_sc_m, p_gt_m, p_sh_f, p_sc_f, p_gt_f = adaln(ctx_cond, w_ada[l], b_ada[l])
        s_sh_m, s_sc_m, s_gt_m, s_sh_f, s_sc_f, s_gt_f = adaln(c, w_ada[l], b_ada[l])
        hp = modulate(xp, g_mix[l], p_sh_m, p_sc_m)
        hs = modulate(xs, g_mix[l], s_sh_m, s_sc_m)
        if l % 2 == 0:
            e = l // 2
            lam_init = 0.8 - 0.6 * math.exp(-0.3 * l)
            lam = diff_lambda(diff_lam_q1[e], diff_lam_k1[e], diff_lam_q2[e], diff_lam_k2[e], lam_init)
            op, dk, dv, nk, nv = even_mixer_context(
                hp, w_in[e], w_out[e], diff_q_g[e], diff_k_g[e], na_q_g[e], na_k_g[e],
                diff_subln_g[e], lam, lam_init)
            os_ = even_mixer_latent(
                hs, cache_diff_k[:, e], cache_diff_v[:, e], cache_na_k[:, e], cache_na_v[:, e],
                w_in[e], w_out[e], diff_q_g[e], diff_k_g[e], na_q_g[e], na_k_g[e],
                diff_subln_g[e], lam, lam_init, na_rpb[e])
            new_dk.append(dk)
            new_dv.append(dv)
            new_nk.append(nk)
            new_nv.append(nv)
        else:
            o = l // 2
            rw = (rw_mu[o], rw_wr[o], rw_wk[o], rw_wv[o], rw_wo[o], rw_w0[o], rw_w1[o], rw_w2[o],
                  rw_a0[o], rw_a1[o], rw_a2[o], rw_g1[o], rw_g2[o], rw_k_k[o], rw_k_a[o],
                  rw_r_k[o], rw_ln_g[o], rw_ln_b[o])
            zero_state = jnp.zeros((xp.shape[0], 2, RWKV_HEADS, RWKV_N, RWKV_N), jnp.float32)
            op, st = rwkv_mixer(hp, zero_state, *rw)
            os_, _ = rwkv_mixer(hs, state_rwkv[:, o], *rw)
            new_st.append(st)
        xp = xp + p_gt_m * op
        xs = xs + s_gt_m * os_
        hp = modulate(xp, g_ffn[l], p_sh_f, p_sc_f)
        hs = modulate(xs, g_ffn[l], s_sh_f, s_sc_f)
        if l % 2 == 0:
            e = l // 2
            xp = xp + p_gt_f * swiglu(hp, ffn_w1[e], ffn_w3[e], ffn_w2[e])
            xs = xs + s_gt_f * swiglu(hs, ffn_w1[e], ffn_w3[e], ffn_w2[e])
        else:
            o = l // 2
            xp = xp + p_gt_f * moe_swiglu(hp, moe_router[o], moe_router_b[o], moe_w1[o], moe_w3[o], moe_w2[o])
            xs = xs + s_gt_f * moe_swiglu(hs, moe_router[o], moe_router_b[o], moe_w1[o], moe_w3[o], moe_w2[o])
    return (xp, xs, jnp.stack(new_dk, axis=1), jnp.stack(new_dv, axis=1), jnp.stack(new_nk, axis=1),
            jnp.stack(new_nv, axis=1), jnp.stack(new_st, axis=1))
```

```python
import functools
import math

import jax
import jax.numpy as jnp
from jax import lax
from jax.experimental import pallas as pl
from jax.experimental.pallas import tpu as pltpu

f32 = jnp.float32
bf16 = jnp.bfloat16

D = 1024
N_PROMPT_SEQ, PROMPT_LEN = 32, 256
N_SAMPLE_SEQ, SAMPLE_LEN = 4, 2048
NP = N_PROMPT_SEQ * PROMPT_LEN
NS = N_SAMPLE_SEQ * SAMPLE_LEN
NT = NP + NS
PAST = 256
GRID_W = 64
GRID_R = SAMPLE_LEN // GRID_W
HD = 64
DIFF_HEADS = 4
NA_HEADS = 8
NA_KH = 8
NA_KW = 16
SEG = 512
IN_COLS = 6 * SEG
FFN_DIM = 2816
N_EXPERTS = 8
EXPERT_DIM = 3584
RW_HEADS = 16
LORA = 64
GATE_LORA = 160
GATE_LORA_PAD = 256
EPS = 1e-6
GN_EPS = 64e-5
NEG_BIG = -1e30
CHUNK = 64
VMEM_LIMIT = 56 * 1024 * 1024


def _cond_idx(i, tm):
    return jnp.maximum((i * tm) // SAMPLE_LEN - (NP // SAMPLE_LEN - 1), 0)


def _dot(a, b):
    return jnp.dot(a, b, preferred_element_type=f32)


def _dot_nt(a, b):
    return lax.dot_general(a, b, (((1,), (1,)), ((), ())), preferred_element_type=f32)


def _dot_tn(a, b):
    return lax.dot_general(a, b, (((0,), (0,)), ((), ())), preferred_element_type=f32)


def _sigmoid(x):
    return 1.0 / (1.0 + jnp.exp(-x))


def _norm_mod(x, g, sh, sc):
    ms = jnp.mean(x * x, axis=-1, keepdims=True)
    return (x * lax.rsqrt(ms + EPS) * g) * (1.0 + sc) + sh


def _group_sum(x, gmat):
    hi = x.astype(bf16)
    lo = (x - hi.astype(f32)).astype(bf16)
    cols = []
    for c in range(x.shape[1] // 256):
        sl = slice(c * 256, (c + 1) * 256)
        cols.append(_dot(hi[:, sl], gmat) + _dot(lo[:, sl], gmat))
    return cols[0] if len(cols) == 1 else jnp.concatenate(cols, axis=1)


def _softmax_parts(parts):
    m = parts[0].max(axis=-1, keepdims=True)
    for p in parts[1:]:
        m = jnp.maximum(m, p.max(axis=-1, keepdims=True))
    es = [jnp.exp(p - m) for p in parts]
    l = es[0].sum(axis=-1, keepdims=True)
    for e in es[1:]:
        l = l + e.sum(axis=-1, keepdims=True)
    inv = 1.0 / l
    return [e * inv for e in es]


def _lane_lo(n=128):
    return lax.broadcasted_iota(jnp.int32, (1, n), 1) < HD


def _stack_halves(q):
    lo = _lane_lo()
    return jnp.concatenate([jnp.where(lo, q, 0.0), jnp.where(lo, 0.0, q)], axis=0)


def _ada_kernel(cond_ref, w_ref, b_ref, o_ref):
    x = cond_ref[...]
    s = x * _sigmoid(x)
    o_ref[0] = _dot(s.astype(bf16), w_ref[0].astype(bf16)) + b_ref[0]


def _ada_table(cond8, w_ada, b_ada):
    depth = w_ada.shape[0]
    tn = 1536
    out = pl.pallas_call(
        _ada_kernel,
        grid=(depth, 6 * D // tn),
        in_specs=[
            pl.BlockSpec((8, D), lambda l, n: (0, 0)),
            pl.BlockSpec((1, D, tn), lambda l, n: (l, 0, n)),
            pl.BlockSpec((1, 1, tn), lambda l, n: (l, 0, n)),
        ],
        out_specs=pl.BlockSpec((1, 8, tn), lambda l, n: (l, 0, n)),
        out_shape=jax.ShapeDtypeStruct((depth, 8, 6 * D), f32),
        compiler_params=pltpu.CompilerParams(vmem_limit_bytes=VMEM_LIMIT),
        name="ada_table",
    )(cond8, w_ada, b_ada.reshape(depth, 1, 6 * D))
    out = out.reshape(depth, 8, 6, D)
    return [[out[l, :, k, :].reshape(8, 1, D) for k in range(6)] for l in range(depth)]


def _qk_norm(y, gain, gmat):
    ss = _group_sum(y * y, gmat) * (1.0 / HD)
    return y * lax.rsqrt(ss + EPS) * gain


def _rope(y, cos, sin):
    even = (lax.broadcasted_iota(jnp.int32, (1, 128), 1) % 2) == 0
    outs = []
    for c in range(y.shape[1] // 128):
        yc = y[:, c * 128:(c + 1) * 128]
        swapped = jnp.where(even, pltpu.roll(yc, 127, 1), pltpu.roll(yc, 1, 1))
        outs.append(yc * cos + swapped * sin)
    return jnp.concatenate(outs, axis=1)


def _inproj_kernel(x_ref, sh_ref, sc_ref, g_ref, w_ref, gain_ref, gmat_ref, cos_ref, sin_ref,
                   o_ref, h_scr, *, tm):
    i = pl.program_id(0)
    j = pl.program_id(1)

    @pl.when(j == 0)
    def _():
        h_scr[...] = _norm_mod(x_ref[...], g_ref[...], sh_ref[0], sc_ref[0]).astype(bf16)

    y = _dot(h_scr[...], w_ref[...])
    is_norm = jnp.logical_and(j != 2, j != 5)
    is_rope = jnp.logical_and(j < 2, i >= NP // tm)

    @pl.when(jnp.logical_not(is_norm))
    def _():
        o_ref[...] = y

    @pl.when(jnp.logical_and(is_norm, jnp.logical_not(is_rope)))
    def _():
        o_ref[...] = _qk_norm(y, gain_ref[0], gmat_ref[...])

    @pl.when(is_rope)
    def _():
        o_ref[...] = _rope(_qk_norm(y, gain_ref[0], gmat_ref[...]), cos_ref[...], sin_ref[...])


def _in_projection(x, sh, sc, g, w_bf, gains, gmat, cos_t, sin_t):
    tm = 512
    n_prompt_tiles = NP // tm
    tiles_per_seq = SAMPLE_LEN // tm
    cidx = functools.partial(_cond_idx, tm=tm)
    rope_idx = lambda i, j: (jnp.maximum(i - n_prompt_tiles, 0) % tiles_per_seq, 0)
    return pl.pallas_call(
        functools.partial(_inproj_kernel, tm=tm),
        grid=(NT // tm, IN_COLS // SEG),
        in_specs=[
            pl.BlockSpec((tm, D), lambda i, j: (i, 0)),
            pl.BlockSpec((1, 1, D), lambda i, j: (cidx(i), 0, 0)),
            pl.BlockSpec((1, 1, D), lambda i, j: (cidx(i), 0, 0)),
            pl.BlockSpec((1, D), lambda i, j: (0, 0)),
            pl.BlockSpec((D, SEG), lambda i, j: (0, j)),
            pl.BlockSpec((1, 1, SEG), lambda i, j: (j, 0, 0)),
            pl.BlockSpec((256, 256), lambda i, j: (0, 0)),
            pl.BlockSpec((tm, 128), rope_idx),
            pl.BlockSpec((tm, 128), rope_idx),
        ],
        out_specs=pl.BlockSpec((tm, SEG), lambda i, j: (i, j)),
        out_shape=jax.ShapeDtypeStruct((NT, IN_COLS), f32),
        scratch_shapes=[pltpu.VMEM((tm, D), bf16)],
        compiler_params=pltpu.CompilerParams(
            dimension_semantics=("parallel", "arbitrary"), vmem_limit_bytes=VMEM_LIMIT),
        name="in_projection",
    )(x, sh, sc, g, w_bf, gains, gmat, cos_t, sin_t)


def _lambda_value(lamp_ref, lam_init):
    lp = lamp_ref[...]
    e1 = jnp.exp(jnp.sum(lp[0:1] * lp[1:2], axis=-1, keepdims=True))
    e2 = jnp.exp(jnp.sum(lp[2:3] * lp[3:4], axis=-1, keepdims=True))
    return e1 - e2 + lam_init


def _sub_ln(o, subg, lam_init):
    ms = jnp.mean(o * o, axis=-1, keepdims=True)
    return o * lax.rsqrt(ms + EPS) * subg * (1.0 - lam_init)


def _prompt_attn_kernel(p_ref, lamp_ref, subg_ref, o_ref, *, lam_init):
    lam = _lambda_value(lamp_ref, lam_init)
    t = PROMPT_LEN
    lo = _lane_lo()
    scale = HD ** -0.5
    for h in range(DIFF_HEADS):
        q = p_ref[:, h * 128:(h + 1) * 128]
        k = p_ref[:, SEG + h * 128:SEG + (h + 1) * 128].astype(bf16)
        v = p_ref[:, 2 * SEG + h * 128:2 * SEG + (h + 1) * 128].astype(bf16)
        s = _dot_nt((_stack_halves(q) * scale).astype(bf16), k)
        (p,) = _softmax_parts([s])
        pd = p[:t] - lam * p[t:]
        o = _dot(pd.astype(bf16), v)
        o_ref[:, h * 128:(h + 1) * 128] = _sub_ln(o, subg_ref[...], lam_init).astype(bf16)
    for hp in range(NA_HEADS // 2):
        q = p_ref[:, 3 * SEG + hp * 128:3 * SEG + (hp + 1) * 128]
        k = p_ref[:, 4 * SEG + hp * 128:4 * SEG + (hp + 1) * 128].astype(bf16)
        v = p_ref[:, 5 * SEG + hp * 128:5 * SEG + (hp + 1) * 128].astype(bf16)
        s = _dot_nt((_stack_halves(q) * scale).astype(bf16), k)
        (p,) = _softmax_parts([s])
        o = _dot(p.astype(bf16), v)
        o_ref[:, SEG + hp * 128:SEG + (hp + 1) * 128] = jnp.where(lo, o[:t], o[t:]).astype(bf16)


def _prompt_attention(proj, lamp, subg, lam_init):
    return pl.pallas_call(
        functools.partial(_prompt_attn_kernel, lam_init=lam_init),
        grid=(N_PROMPT_SEQ,),
        in_specs=[
            pl.BlockSpec((PROMPT_LEN, IN_COLS), lambda b: (b, 0)),
            pl.BlockSpec((4, HD), lambda b: (0, 0)),
            pl.BlockSpec((1, 128), lambda b: (0, 0)),
        ],
        out_specs=pl.BlockSpec((PROMPT_LEN, D), lambda b: (b, 0)),
        out_shape=jax.ShapeDtypeStruct((NP, D), bf16),
        compiler_params=pltpu.CompilerParams(
            dimension_semantics=("parallel",), vmem_limit_bytes=VMEM_LIMIT),
        name="prompt_attention",
    )(proj, lamp, subg)


def _latent_diff_kernel(q_ref, kn_ref, vn_ref, kc_ref, vc_ref, lamp_ref, subg_ref, o_ref,
                        *, lam_init, tq):
    lam = _lambda_value(lamp_ref, lam_init)
    scale = HD ** -0.5
    qq = (_stack_halves(q_ref[...]) * scale).astype(bf16)
    s_c = _dot_nt(qq, kc_ref[0].astype(bf16))
    s_n = _dot_nt(qq, kn_ref[...].astype(bf16))
    p_c, p_n = _softmax_parts([s_c, s_n])
    pd_c = p_c[:tq] - lam * p_c[tq:]
    pd_n = p_n[:tq] - lam * p_n[tq:]
    o = _dot(pd_c.astype(bf16), vc_ref[0].astype(bf16)) + _dot(pd_n.astype(bf16), vn_ref[...].astype(bf16))
    o_ref[...] = _sub_ln(o, subg_ref[...], lam_init).astype(bf16)


def _latent_diff_attention(proj, cache_k, cache_v, lamp, subg, lam_init):
    tq = 256
    nqb = SAMPLE_LEN // tq
    q0 = NP // tq
    s0 = NP // SAMPLE_LEN
    return pl.pallas_call(
        functools.partial(_latent_diff_kernel, lam_init=lam_init, tq=tq),
        grid=(N_SAMPLE_SEQ, DIFF_HEADS, nqb),
        in_specs=[
            pl.BlockSpec((tq, 128), lambda b, h, q: (q0 + b * nqb + q, h)),
            pl.BlockSpec((SAMPLE_LEN, 128), lambda b, h, q: (s0 + b, 4 + h)),
            pl.BlockSpec((SAMPLE_LEN, 128), lambda b, h, q: (s0 + b, 8 + h)),
            pl.BlockSpec((1, PAST, 128), lambda b, h, q: (b, 0, h)),
            pl.BlockSpec((1, PAST, 128), lambda b, h, q: (b, 0, h)),
            pl.BlockSpec((4, HD), lambda b, h, q: (0, 0)),
            pl.BlockSpec((1, 128), lambda b, h, q: (0, 0)),
        ],
        out_specs=pl.BlockSpec((tq, 128), lambda b, h, q: (b * nqb + q, h)),
        out_shape=jax.ShapeDtypeStruct((NS, SEG), bf16),
        compiler_params=pltpu.CompilerParams(
            dimension_semantics=("parallel", "parallel", "arbitrary"), vmem_limit_bytes=VMEM_LIMIT),
        name="latent_diff_attention",
    )(proj, proj, proj, cache_k, cache_v, lamp, subg)


def _rpb_table_kernel(rpb_ref, o_ref):
    h = pl.program_id(0)
    wq = lax.broadcasted_iota(jnp.int32, (GRID_W, GRID_W), 0)
    wk = lax.broadcasted_iota(jnp.int32, (GRID_W, GRID_W), 1)
    col_start = jnp.clip(wq - NA_KW // 2, 0, GRID_W - NA_KW)
    col_in = jnp.logical_and(wk >= col_start, wk < col_start + NA_KW)
    col_off = jnp.clip(wk - wq, -(NA_KW - 1), NA_KW - 1) + (NA_KW - 1)
    n_dr = 2 * NA_KH - 1
    n_dc = 2 * NA_KW - 1
    for dr in range(n_dr):
        t = jnp.zeros((GRID_W, GRID_W), f32)
        for c in range(n_dc):
            t = jnp.where(col_off == c, rpb_ref[h * (n_dr * n_dc) + dr * n_dc + c], t)
        o_ref[0, dr] = jnp.where(col_in, t, NEG_BIG)


def _rpb_table(rpb):
    n_dr = 2 * NA_KH - 1
    tcol = pl.pallas_call(
        _rpb_table_kernel,
        grid=(NA_HEADS,),
        in_specs=[pl.BlockSpec(memory_space=pltpu.SMEM)],
        out_specs=pl.BlockSpec((1, n_dr, GRID_W, GRID_W), lambda h: (h, 0, 0, 0)),
        out_shape=jax.ShapeDtypeStruct((NA_HEADS, n_dr, GRID_W, GRID_W), f32),
        name="rpb_table",
    )(rpb.reshape(-1))
    return jnp.stack(
        [jnp.concatenate([tcol[:, j - s + NA_KH - 1] for j in range(NA_KH)], axis=-1) for s in range(NA_KH)],
        axis=1)


def _latent_na_kernel(q_ref, k_ref, v_ref, kc_ref, vc_ref, bias_ref, o_ref):
    scale = HD ** -0.5
    lo = _lane_lo()
    kc = kc_ref[0].astype(bf16)
    vc = vc_ref[0].astype(bf16)
    win = NA_KH * GRID_W

    def row(r, carry):
        rs = jnp.clip(r - NA_KH // 2, 0, GRID_R - NA_KH)
        sidx = r - rs
        q = q_ref[pl.ds(pl.multiple_of(r * GRID_W, GRID_W), GRID_W), :]
        qq = (_stack_halves(q) * scale).astype(bf16)
        k0 = pl.multiple_of(rs * GRID_W, GRID_W)
        kw = k_ref[pl.ds(k0, win), :].astype(bf16)
        vw = v_ref[pl.ds(k0, win), :].astype(bf16)
        bias = jnp.concatenate([bias_ref[0, sidx], bias_ref[1, sidx]], axis=0)
        s_loc = _dot_nt(qq, kw) + bias
        s_ctx = _dot_nt(qq, kc)
        p_loc, p_ctx = _softmax_parts([s_loc, s_ctx])
        o = _dot(p_loc.astype(bf16), vw) + _dot(p_ctx.astype(bf16), vc)
        o_ref[pl.ds(pl.multiple_of(r * GRID_W, GRID_W), GRID_W), :] = (
            jnp.where(lo, o[:GRID_W], o[GRID_W:]).astype(bf16))
        return carry

    lax.fori_loop(0, GRID_R, row, 0)


def _latent_na_attention(proj, cache_k, cache_v, bias):
    s0 = NP // SAMPLE_LEN
    return pl.pallas_call(
        _latent_na_kernel,
        grid=(N_SAMPLE_SEQ, NA_HEADS // 2),
        in_specs=[
            pl.BlockSpec((SAMPLE_LEN, 128), lambda b, h: (s0 + b, 12 + h)),
            pl.BlockSpec((SAMPLE_LEN, 128), lambda b, h: (s0 + b, 16 + h)),
            pl.BlockSpec((SAMPLE_LEN, 128), lambda b, h: (s0 + b, 20 + h)),
            pl.BlockSpec((1, PAST, 128), lambda b, h: (b, 0, h)),
            pl.BlockSpec((1, PAST, 128), lambda b, h: (b, 0, h)),
            pl.BlockSpec((2, NA_KH, GRID_W, NA_KH * GRID_W), lambda b, h: (h, 0, 0, 0)),
        ],
        out_specs=pl.BlockSpec((SAMPLE_LEN, 128), lambda b, h: (b, h)),
        out_shape=jax.ShapeDtypeStruct((NS, SEG), bf16),
        compiler_params=pltpu.CompilerParams(
            dimension_semantics=("parallel", "parallel"), vmem_limit_bytes=VMEM_LIMIT),
        name="latent_na_attention",
    )(proj, proj, proj, cache_k, cache_v, bias)


def _out_proj_kernel(a_ref, w_ref, x_ref, gt_ref, o_ref):
    o_ref[...] = x_ref[...] + gt_ref[0] * _dot(a_ref[...], w_ref[...])


def _out_projection(a_bf, w_bf, x, gate):
    tm = 512
    cidx = functools.partial(_cond_idx, tm=tm)
    return pl.pallas_call(
        _out_proj_kernel,
        grid=(NT // tm,),
        in_specs=[
            pl.BlockSpec((tm, D), lambda i: (i, 0)),
            pl.BlockSpec((D, D), lambda i: (0, 0)),
            pl.BlockSpec((tm, D), lambda i: (i, 0)),
            pl.BlockSpec((1, 1, D), lambda i: (cidx(i), 0, 0)),
        ],
        out_specs=pl.BlockSpec((tm, D), lambda i: (i, 0)),
        out_shape=jax.ShapeDtypeStruct((NT, D), f32),
        compiler_params=pltpu.CompilerParams(
            dimension_semantics=("parallel",), vmem_limit_bytes=VMEM_LIMIT),
        name="out_projection",
    )(a_bf, w_bf, x, gate)


def _rwkv_out_kernel(yf_ref, yb_ref, bonus_ref, gate_ref, lng_ref, lnb_ref, gmat_ref, w_ref,
                     x_ref, gt_ref, o_ref):
    y = yf_ref[...] + yb_ref[...]
    gmat = gmat_ref[...]
    mu = _group_sum(y, gmat) * (1.0 / HD)
    yc = y - mu
    var = _group_sum(yc * yc, gmat) * (1.0 / HD)
    z = yc * lax.rsqrt(var + GN_EPS) * lng_ref[...] + lnb_ref[...] + bonus_ref[...]
    z = (z * gate_ref[...]).astype(bf16)
    o_ref[...] = x_ref[...] + gt_ref[0] * _dot(z, w_ref[...])


def _rwkv_out_projection(yf, yb, bonus, gate_lora, ln_g, ln_b, gmat, w_bf, x, gate):
    tm = 512
    cidx = functools.partial(_cond_idx, tm=tm)
    row = pl.BlockSpec((tm, D), lambda i: (i, 0))
    vec = pl.BlockSpec((1, D), lambda i: (0, 0))
    return pl.pallas_call(
        _rwkv_out_kernel,
        grid=(NT // tm,),
        in_specs=[row, row, row, row, vec, vec,
                  pl.BlockSpec((256, 256), lambda i: (0, 0)),
                  pl.BlockSpec((D, D), lambda i: (0, 0)),
                  row,
                  pl.BlockSpec((1, 1, D), lambda i: (cidx(i), 0, 0))],
        out_specs=row,
        out_shape=jax.ShapeDtypeStruct((NT, D), f32),
        compiler_params=pltpu.CompilerParams(
            dimension_semantics=("parallel",), vmem_limit_bytes=VMEM_LIMIT),
        name="rwkv_out_projection",
    )(yf, yb, bonus, gate_lora, ln_g, ln_b, gmat, w_bf, x, gate)


def _ffn_kernel(x_ref, sh_ref, sc_ref, gt_ref, g_ref, wr_ref, br_ref, w1_ref, w3_ref, w2_ref,
                o_ref, h_scr, acc_scr, gates_scr, *, routed):
    e = pl.program_id(1)
    f = pl.program_id(2)
    first = jnp.logical_and(e == 0, f == 0)
    last = jnp.logical_and(e == pl.num_programs(1) - 1, f == pl.num_programs(2) - 1)

    @pl.when(first)
    def _():
        h = _norm_mod(x_ref[...], g_ref[...], sh_ref[0], sc_ref[0])
        h_scr[...] = h.astype(bf16)
        acc_scr[...] = jnp.zeros_like(acc_scr)
        if routed:
            logits = jnp.dot(h, wr_ref[...], preferred_element_type=f32,
                             precision=lax.Precision.HIGHEST) + br_ref[...]
            lane = lax.broadcasted_iota(jnp.int32, logits.shape, 1)
            logits = jnp.where(lane < N_EXPERTS, logits, -jnp.inf)
            m1 = logits.max(axis=-1, keepdims=True)
            i1 = jnp.min(jnp.where(logits == m1, lane, 128), axis=-1, keepdims=True)
            rest = jnp.where(lane == i1, -jnp.inf, logits)
            m2 = rest.max(axis=-1, keepdims=True)
            i2 = jnp.min(jnp.where(rest == m2, lane, 128), axis=-1, keepdims=True)
            e2 = jnp.exp(m2 - m1)
            den = 1.0 / (1.0 + e2)
            gates_scr[...] = jnp.where(lane == i1, den, jnp.where(lane == i2, e2 * den, 0.0))

    h = h_scr[...]
    a = _dot(h, w1_ref[0])
    u = (a * _sigmoid(a)) * _dot(h, w3_ref[0])
    if routed:
        lane = lax.broadcasted_iota(jnp.int32, gates_scr.shape, 1)
        u = u * jnp.sum(jnp.where(lane == e, gates_scr[...], 0.0), axis=-1, keepdims=True)
    acc_scr[...] += _dot(u.astype(bf16), w2_ref[0])

    @pl.when(last)
    def _():
        o_ref[...] = x_ref[...] + gt_ref[0] * acc_scr[...]


def _channel_mixer(x, sh, sc, gt, g, w_router, b_router, w1, w3, w2, *, routed, tf):
    tm = 512
    n_e, _, width = w1.shape
    cidx = functools.partial(_cond_idx, tm=tm)
    mod = pl.BlockSpec((1, 1, D), lambda i, e, f: (cidx(i), 0, 0))
    return pl.pallas_call(
        functools.partial(_ffn_kernel, routed=routed),
        grid=(NT // tm, n_e, width // tf),
        in_specs=[
            pl.BlockSpec((tm, D), lambda i, e, f: (i, 0)),
            mod, mod, mod,
            pl.BlockSpec((1, D), lambda i, e, f: (0, 0)),
            pl.BlockSpec((D, 128), lambda i, e, f: (0, 0)),
            pl.BlockSpec((1, 128), lambda i, e, f: (0, 0)),
            pl.BlockSpec((1, D, tf), lambda i, e, f: (e, 0, f)),
            pl.BlockSpec((1, D, tf), lambda i, e, f: (e, 0, f)),
            pl.BlockSpec((1, tf, D), lambda i, e, f: (e, f, 0)),
        ],
        out_specs=pl.BlockSpec((tm, D), lambda i, e, f: (i, 0)),
        out_shape=jax.ShapeDtypeStruct((NT, D), f32),
        scratch_shapes=[pltpu.VMEM((tm, D), bf16), pltpu.VMEM((tm, D), f32), pltpu.VMEM((tm, 128), f32)],
        compiler_params=pltpu.CompilerParams(
            dimension_semantics=("parallel", "arbitrary", "arbitrary"), vmem_limit_bytes=VMEM_LIMIT),
        name="moe_mixer" if routed else "ffn_mixer",
    )(x, sh, sc, gt, g, w_router, b_router, w1, w3, w2)


def _rwkv_proj_kernel(x_ref, xp_ref, xn_ref, sh_ref, sc_ref, g_ref, mu_ref,
                      wr_ref, wk_ref, wv_ref, g1_ref, g2_ref, w1_ref, w2_ref, a1_ref, a2_ref,
                      w0_ref, a0_ref, kk_ref, ka_ref, rk_ref, gmat_ref,
                      r_out, v_out, kkn_out, bonus_out, gate_out,
                      lwf_out, lwb_out, af_out, ab_out, kdf_out, kdb_out, *, tm):
    i = pl.program_id(0)
    g, sh, sc = g_ref[...], sh_ref[0], sc_ref[0]
    h = _norm_mod(x_ref[...], g, sh, sc)
    n_prompt_tiles = NP // tm
    tiles_per_seq = SAMPLE_LEN // tm
    pos = (i - n_prompt_tiles) % tiles_per_seq
    has_prev = jnp.logical_and(i >= n_prompt_tiles, pos != 0)
    has_next = jnp.logical_and(i >= n_prompt_tiles, pos != tiles_per_seq - 1)
    h_before = jnp.where(has_prev, _norm_mod(xp_ref[...], g, sh, sc)[7:8], 0.0)
    h_after = jnp.where(has_next, _norm_mod(xn_ref[...], g, sh, sc)[0:1], 0.0)
    rowi = lax.broadcasted_iota(jnp.int32, (tm, 1), 0)
    h_prev = jnp.where(rowi == 0, h_before, pltpu.roll(h, 1, 0))
    h_next = jnp.where(rowi == tm - 1, h_after, pltpu.roll(h, tm - 1, 0))
    xx = 0.5 * (h_prev + h_next) - h
    mix = lambda n: (h + xx * mu_ref[n:n + 1]).astype(bf16)

    r = _dot(mix(0), wr_ref[...])
    k = _dot(mix(2), wk_ref[...])
    v = _dot(mix(3), wv_ref[...])
    gate_out[...] = _dot(_sigmoid(_dot(mix(5), g1_ref[...])).astype(bf16), g2_ref[...])

    lo = _lane_lo()
    tw = jnp.tanh(_dot(mix(1), w1_ref[...]))
    ta = _dot(mix(4), a1_ref[...])
    gmat = gmat_ref[...]
    kk = k * kk_ref[...]
    kkn_out[...] = kk * lax.rsqrt(_group_sum(kk * kk, gmat) + 1e-12)
    r_out[...] = r
    v_out[...] = v
    kd_sum = jnp.zeros_like(k)
    for d, (lw_out, a_out, kd_out) in enumerate(((lwf_out, af_out, kdf_out), (lwb_out, ab_out, kdb_out))):
        keep = lo if d == 0 else jnp.logical_not(lo)
        zw = w0_ref[d:d + 1] + _dot(jnp.where(keep, tw, 0.0).astype(bf16), w2_ref[...])
        lw_out[...] = -math.exp(-0.5) * _sigmoid(zw)
        a = _sigmoid(a0_ref[d:d + 1] + _dot(jnp.where(keep, ta, 0.0).astype(bf16), a2_ref[...]))
        a_out[...] = a
        kd = k * (1.0 + (a - 1.0) * ka_ref[...])
        kd_out[...] = kd
        kd_sum = kd_sum + kd
    bonus_out[...] = _group_sum(r * kd_sum * rk_ref[...], gmat) * v


def _rwkv_projection(x, sh, sc, g, mu, wr, wk, wv, g1, g2, w1, w2, a1, a2, w0, a0, k_k, k_a, r_k, gmat):
    tm = 256
    cidx = functools.partial(_cond_idx, tm=tm)
    hb = tm // 8
    n8 = NT // 8
    full = lambda shape: pl.BlockSpec(shape, lambda i: tuple(0 for _ in shape))
    row = pl.BlockSpec((tm, D), lambda i: (i, 0))
    return pl.pallas_call(
        functools.partial(_rwkv_proj_kernel, tm=tm),
        grid=(NT // tm,),
        in_specs=[
            row,
            pl.BlockSpec((8, D), lambda i: (jnp.maximum(i * hb - 1, 0), 0)),
            pl.BlockSpec((8, D), lambda i: (jnp.minimum((i + 1) * hb, n8 - 1), 0)),
            pl.BlockSpec((1, 1, D), lambda i: (cidx(i), 0, 0)),
            pl.BlockSpec((1, 1, D), lambda i: (cidx(i), 0, 0)),
            full((1, D)), full((6, D)),
            full((D, D)), full((D, D)), full((D, D)),
            full((D, GATE_LORA_PAD)), full((GATE_LORA_PAD, D)),
            full((D, 2 * LORA)), full((2 * LORA, D)), full((D, 2 * LORA)), full((2 * LORA, D)),
            full((2, D)), full((2, D)), full((1, D)), full((1, D)), full((1, D)),
            full((256, 256)),
        ],
        out_specs=[row] * 11,
        out_shape=[jax.ShapeDtypeStruct((NT, D), f32)] * 11,
        compiler_params=pltpu.CompilerParams(
            dimension_semantics=("parallel",), vmem_limit_bytes=VMEM_LIMIT),
        name="rwkv_projection",
    )(x, x, x, sh, sc, g, mu, wr, wk, wv, g1, g2, w1, w2, a1, a2, w0, a0, k_k, k_a, r_k, gmat)


def _split3(x):
    x1 = x.astype(bf16)
    r1 = x - x1.astype(f32)
    x2 = r1.astype(bf16)
    x3 = (r1 - x2.astype(f32)).astype(bf16)
    return x1, x2, x3


def _scan_kernel(*refs, nc, has_init, emit_state):
    fwd_refs = refs[0:6]
    bwd_refs = refs[6:12]
    tri_ref, mask_ref = refs[12:14]
    pos = 14
    s0_ref = None
    if has_init:
        s0_ref = refs[pos]
        pos += 1
    yf_ref, yb_ref = refs[pos:pos + 2]
    pos += 2
    sout_ref = None
    if emit_state:
        sout_ref = refs[pos]
        pos += 1
    st_scr, cl_scr = refs[pos:pos + 2]

    s = pl.program_id(1)

    @pl.when(s == 0)
    def _():
        if has_init:
            st_scr[...] = s0_ref[0]
        else:
            st_scr[...] = jnp.zeros_like(st_scr)

    c = CHUNK
    for d, drefs in enumerate((fwd_refs, bwd_refs)):
        lw = drefs[3][...]
        tri = tri_ref[d]
        p1, p2, p3 = _split3(lw)
        cl_scr[d] = _dot(tri, p1) + _dot(tri, p2) + _dot(tri, p3)

    lane = lax.broadcasted_iota(jnp.int32, (1, 128), 1)
    m0 = (lane < HD).astype(f32)
    m1 = 1.0 - m0
    rid = lax.broadcasted_iota(jnp.int32, (128, 128), 0)
    cid = lax.broadcasted_iota(jnp.int32, (128, 128), 1)
    eye = (rid == cid).astype(f32)

    def per_head_rows(x):
        return jnp.concatenate([x * m0, x * m1], axis=0)

    for d, (drefs, y_ref) in enumerate(((fwd_refs, yf_ref), (bwd_refs, yb_ref))):
        r_ref, v_ref, kk_ref, lw_ref, a_ref, kd_ref = drefs
        end_row = c - 1 if d == 0 else 0
        mask = mask_ref[d]
        for p in range(RW_HEADS // 2):
            ln = slice(p * 128, (p + 1) * 128)
            cl = cl_scr[d, :, ln]
            lw = lw_ref[:, ln]
            kk = kk_ref[:, ln]
            tot = cl[end_row:end_row + 1]
            e_inv = jnp.exp(-cl)
            e_end = jnp.exp(tot - cl)
            kka = kk * a_ref[:, ln]
            kd = kd_ref[:, ln]
            v = v_ref[:, ln]
            at = per_head_rows(-kk * jnp.exp(cl - lw))
            rt = per_head_rows(r_ref[:, ln] * jnp.exp(cl))
            ar = jnp.concatenate([at, rt], axis=0).astype(bf16)
            bk = jnp.concatenate([per_head_rows(kka * e_inv), per_head_rows(kd * e_inv)],
                                 axis=0).astype(bf16)
            bkh = jnp.concatenate([per_head_rows(kka * e_end), per_head_rows(kd * e_end)],
                                  axis=0).astype(bf16)
            v2 = per_head_rows(v).astype(bf16)

            st = st_scr[d, p]
            g2 = _dot_nt(ar, bk) * mask
            ars = _dot_nt(ar, st.astype(bf16))
            gv = _dot(g2[:, 128:].astype(bf16), v2)
            l_bd = g2[:128, :128]
            pk = _dot(l_bd.astype(bf16), l_bd.astype(bf16))
            q = eye + l_bd
            for it in range(5):
                pkb = pk.astype(bf16)
                if it < 4:
                    res = _dot(jnp.concatenate([q, pk], axis=0).astype(bf16), pkb)
                    q = q + res[:128]
                    pk = res[128:]
                else:
                    q = q + _dot(q.astype(bf16), pkb)
            rhs = ars[:128] + gv[:128]
            u2 = _dot(q.astype(bf16), rhs.astype(bf16))
            u2b = u2.astype(bf16)
            y2 = ars[128:] + gv[128:] + _dot(g2[128:, :128].astype(bf16), u2b)
            y_ref[:, ln] = y2[:c] + y2[c:]
            uv = jnp.concatenate([u2b, v2], axis=0)
            st_scr[d, p] = st * jnp.exp(tot) + _dot_tn(uv, bkh)

    if emit_state:
        @pl.when(s == nc - 1)
        def _():
            sout_ref[0] = st_scr[...]


def _rwkv_scan(streams_f, streams_b, tri, mask, s0_bd, *, n_seq, seq_len, row0, emit_state):
    nc = seq_len // CHUNK
    blk0 = row0 // CHUNK
    fwd_spec = pl.BlockSpec((CHUNK, D), lambda b, s: (blk0 + b * nc + s, 0))
    bwd_spec = pl.BlockSpec((CHUNK, D), lambda b, s: (blk0 + b * nc + nc - 1 - s, 0))
    in_specs = [fwd_spec] * 6 + [bwd_spec] * 6 + [
        pl.BlockSpec((2, CHUNK, CHUNK), lambda b, s: (0, 0, 0)),
        pl.BlockSpec((2, 256, 256), lambda b, s: (0, 0, 0)),
    ]
    args = list(streams_f) + list(streams_b) + [tri, mask]
    state_block = (1, 2, RW_HEADS // 2, 128, 128)
    if s0_bd is not None:
        in_specs.append(pl.BlockSpec(state_block, lambda b, s: (b, 0, 0, 0, 0)))
        args.append(s0_bd)
    out_specs = [pl.BlockSpec((CHUNK, D), lambda b, s: (b * nc + s, 0)),
                 pl.BlockSpec((CHUNK, D), lambda b, s: (b * nc + nc - 1 - s, 0))]
    out_shape = [jax.ShapeDtypeStruct((n_seq * seq_len, D), f32)] * 2
    if emit_state:
        out_specs.append(pl.BlockSpec(state_block, lambda b, s: (b, 0, 0, 0, 0)))
        out_shape.append(jax.ShapeDtypeStruct((n_seq,) + state_block[1:], f32))
    return pl.pallas_call(
        functools.partial(_scan_kernel, nc=nc, has_init=s0_bd is not None, emit_state=emit_state),
        grid=(n_seq, nc),
        in_specs=in_specs,
        out_specs=out_specs,
        out_shape=out_shape,
        scratch_shapes=[pltpu.VMEM(state_block[1:], f32), pltpu.VMEM((2, CHUNK, D), f32)],
        compiler_params=pltpu.CompilerParams(
            dimension_semantics=("parallel", "arbitrary"), vmem_limit_bytes=VMEM_LIMIT),
        name="rwkv_scan_prompt" if emit_state else "rwkv_scan_sample",
    )(*args)


def _scan_constants():
    t = jnp.arange(CHUNK)
    lower = (t[:, None] >= t[None, :])
    tri = jnp.stack([lower, lower.T]).astype(bf16)
    masks = []
    for d in range(2):
        strict = (t[:, None] > t[None, :]) if d == 0 else (t[:, None] < t[None, :])
        incl = lower if d == 0 else lower.T
        blocks = []
        for m in (strict, incl):
            bd = jnp.kron(jnp.eye(2, dtype=f32), m.astype(f32))
            blocks.append(jnp.concatenate([bd, bd], axis=1))
        masks.append(jnp.concatenate(blocks, axis=0))
    return tri, jnp.stack(masks)


def _state_to_blockdiag(s):
    n = s.shape[0]
    s = s.reshape(n, 2, RW_HEADS // 2, 2, HD, HD)
    z = jnp.zeros_like(s[:, :, :, 0])
    top = jnp.concatenate([s[:, :, :, 0], z], axis=-1)
    bot = jnp.concatenate([z, s[:, :, :, 1]], axis=-1)
    return jnp.concatenate([top, bot], axis=-2)


def _state_from_blockdiag(sb):
    n = sb.shape[0]
    a = sb[:, :, :, :HD, :HD]
    b = sb[:, :, :, HD:, HD:]
    return jnp.stack([a, b], axis=3).reshape(n, 2, RW_HEADS, HD, HD)


def _rope_tables():
    t = jnp.arange(SAMPLE_LEN)
    rows = (t // GRID_W).astype(f32)
    cols = (t % GRID_W).astype(f32)
    nf = HD // 4
    inv = 10000.0 ** (-jnp.arange(nf, dtype=f32) / nf)
    ang = jnp.concatenate([rows[:, None] * inv, cols[:, None] * inv], axis=-1)
    cos = jnp.repeat(jnp.cos(ang), 2, axis=-1)
    sin = jnp.repeat(jnp.sin(ang), 2, axis=-1) * jnp.tile(jnp.array([-1.0, 1.0], f32), HD // 2)
    return jnp.tile(cos, (1, 2)), jnp.tile(sin, (1, 2))


def kernel(x_prompt, x_sample, cache_diff_k, cache_diff_v, cache_na_k, cache_na_v, state_rwkv, c, c_ctx, w_ada, b_ada, g_mix, g_ffn, w_in, w_out, diff_q_g, diff_k_g, diff_lam_q1, diff_lam_k1, diff_lam_q2, diff_lam_k2, diff_subln_g, na_q_g, na_k_g, na_rpb, ffn_w1, ffn_w3, ffn_w2, rw_mu, rw_wr, rw_wk, rw_wv, rw_wo, rw_w0, rw_w1, rw_w2, rw_a0, rw_a1, rw_a2, rw_g1, rw_g2, rw_k_k, rw_k_a, rw_r_k, rw_ln_g, rw_ln_b, moe_router, moe_router_b, moe_w1, moe_w3, moe_w2):
    x = jnp.concatenate([x_prompt.reshape(NP, D), x_sample.reshape(NS, D)], axis=0)
    cond8 = jnp.concatenate([c_ctx[None, :], c, jnp.zeros((3, D), f32)], axis=0)
    mod = _ada_table(cond8, w_ada, b_ada)
    gmat = jnp.kron(jnp.eye(4, dtype=f32), jnp.ones((HD, HD), f32)).astype(bf16)
    zero_router = jnp.zeros((D, 128), f32)
    zero_router_b = jnp.zeros((1, 128), f32)

    lam_init = 0.8 - 0.6 * math.exp(-0.3 * 0)
    ones_seg = jnp.ones((SEG,), f32)
    tile8 = lambda gvec: jnp.tile(gvec, SEG // HD)
    gains = jnp.stack([tile8(diff_q_g[0]), tile8(diff_k_g[0]), ones_seg,
                       tile8(na_q_g[0]), tile8(na_k_g[0]), ones_seg]).reshape(6, 1, SEG)
    cos_t, sin_t = _rope_tables()
    proj = _in_projection(x, mod[0][0], mod[0][1], g_mix[0][None, :], w_in[0].astype(bf16),
                          gains, gmat, cos_t, sin_t)
    lamp = jnp.stack([diff_lam_q1[0], diff_lam_k1[0], diff_lam_q2[0], diff_lam_k2[0]])
    subg = diff_subln_g[0][None, :]
    o_prompt = _prompt_attention(proj, lamp, subg, lam_init)
    o_diff = _latent_diff_attention(proj, cache_diff_k[:, 0].reshape(N_SAMPLE_SEQ, PAST, SEG),
                                    cache_diff_v[:, 0].reshape(N_SAMPLE_SEQ, PAST, SEG), lamp, subg, lam_init)
    o_na = _latent_na_attention(proj, cache_na_k[:, 0].reshape(N_SAMPLE_SEQ, PAST, SEG),
                                cache_na_v[:, 0].reshape(N_SAMPLE_SEQ, PAST, SEG), _rpb_table(na_rpb[0]))
    o_all = jnp.concatenate([o_prompt, jnp.concatenate([o_diff, o_na], axis=1)], axis=0)
    x = _out_projection(o_all, w_out[0].astype(bf16), x, mod[0][2])
    x = _channel_mixer(x, mod[0][3], mod[0][4], mod[0][5], g_ffn[0][None, :], zero_router, zero_router_b,
                       ffn_w1.astype(bf16), ffn_w3.astype(bf16), ffn_w2.astype(bf16),
                       routed=False, tf=FFN_DIM // 2)

    pad_g = GATE_LORA_PAD - GATE_LORA
    g1 = jnp.pad(rw_g1[0], ((0, 0), (0, pad_g))).astype(bf16)
    g2 = jnp.pad(rw_g2[0], ((0, pad_g), (0, 0))).astype(bf16)
    w1cat = jnp.concatenate([rw_w1[0, 0], rw_w1[0, 1]], axis=1).astype(bf16)
    w2cat = jnp.concatenate([rw_w2[0, 0], rw_w2[0, 1]], axis=0).astype(bf16)
    a1cat = jnp.concatenate([rw_a1[0, 0], rw_a1[0, 1]], axis=1).astype(bf16)
    a2cat = jnp.concatenate([rw_a2[0, 0], rw_a2[0, 1]], axis=0).astype(bf16)
    (r, v, kkn, bonus, gate_lora, lwf, lwb, af, ab, kdf, kdb) = _rwkv_projection(
        x, mod[1][0], mod[1][1], g_mix[1][None, :], rw_mu[0],
        rw_wr[0].astype(bf16), rw_wk[0].astype(bf16), rw_wv[0].astype(bf16), g1, g2,
        w1cat, w2cat, a1cat, a2cat, rw_w0[0], rw_a0[0],
        rw_k_k[0][None, :], rw_k_a[0][None, :], rw_r_k[0].reshape(1, D), gmat)
    tri, mask = _scan_constants()
    streams_f = (r, v, kkn, lwf, af, kdf)
    streams_b = (r, v, kkn, lwb, ab, kdb)
    yf_p, yb_p, st_p = _rwkv_scan(streams_f, streams_b, tri, mask, None,
                                  n_seq=N_PROMPT_SEQ, seq_len=PROMPT_LEN, row0=0, emit_state=True)
    yf_s, yb_s = _rwkv_scan(streams_f, streams_b, tri, mask, _state_to_blockdiag(state_rwkv[:, 0]),
                            n_seq=N_SAMPLE_SEQ, seq_len=SAMPLE_LEN, row0=NP, emit_state=False)
    yf = jnp.concatenate([yf_p, yf_s], axis=0)
    yb = jnp.concatenate([yb_p, yb_s], axis=0)
    x = _rwkv_out_projection(yf, yb, bonus, gate_lora, rw_ln_g[0][None, :], rw_ln_b[0][None, :], gmat,
                             rw_wo[0].astype(bf16), x, mod[1][2])
    w_router = jnp.pad(moe_router[0], ((0, 0), (0, 128 - N_EXPERTS)))
    b_router = jnp.pad(moe_router_b[0], (0, 128 - N_EXPERTS))[None, :]
    x = _channel_mixer(x, mod[1][3], mod[1][4], mod[1][5], g_ffn[1][None, :], w_router, b_router,
                       moe_w1[0].astype(bf16), moe_w3[0].astype(bf16), moe_w2[0].astype(bf16),
                       routed=True, tf=EXPERT_DIM // 2)

    new_dk = proj[:NP, SEG:2 * SEG].reshape(N_PROMPT_SEQ, 1, PROMPT_LEN, DIFF_HEADS, 2 * HD)
    new_dv = proj[:NP, 2 * SEG:3 * SEG].reshape(N_PROMPT_SEQ, 1, PROMPT_LEN, DIFF_HEADS, 2 * HD)
    new_nk = proj[:NP, 4 * SEG:5 * SEG].reshape(N_PROMPT_SEQ, 1, PROMPT_LEN, NA_HEADS, HD)
    new_nv = proj[:NP, 5 * SEG:6 * SEG].reshape(N_PROMPT_SEQ, 1, PROMPT_LEN, NA_HEADS, HD)
    new_state = _state_from_blockdiag(st_p)[:, None]
    return (x[:NP].reshape(N_PROMPT_SEQ, PROMPT_LEN, D), x[NP:].reshape(N_SAMPLE_SEQ, SAMPLE_LEN, D),
            new_dk, new_dv, new_nk, new_nv, new_state)
```

```python
import functools
import math

import jax
import jax.numpy as jnp
from jax import lax
from jax.experimental import pallas as pl
from jax.experimental.pallas import tpu as pltpu

f32 = jnp.float32
bf16 = jnp.bfloat16

D = 1024
N_PROMPT_SEQ, PROMPT_LEN = 32, 256
N_SAMPLE_SEQ, SAMPLE_LEN = 4, 2048
NP = N_PROMPT_SEQ * PROMPT_LEN
NS = N_SAMPLE_SEQ * SAMPLE_LEN
NT = NP + NS
PAST = 256
GRID_W = 64
GRID_R = SAMPLE_LEN // GRID_W
HD = 64
DIFF_HEADS = 4
NA_HEADS = 8
NA_KH = 8
NA_KW = 16
SEG = 512
IN_COLS = 6 * SEG
FFN_DIM = 2816
N_EXPERTS = 8
EXPERT_DIM = 3584
RW_HEADS = 16
LORA = 64
GATE_LORA = 160
GATE_LORA_PAD = 256
EPS = 1e-6
GN_EPS = 64e-5
NEG_BIG = -1e30
CHUNK = 64
VMEM_LIMIT = 56 * 1024 * 1024


def _cond_idx(i, tm):
    return jnp.maximum((i * tm) // SAMPLE_LEN - (NP // SAMPLE_LEN - 1), 0)


def _dot(a, b):
    return jnp.dot(a, b, preferred_element_type=f32)


def _dot_nt(a, b):
    return lax.dot_general(a, b, (((1,), (1,)), ((), ())), preferred_element_type=f32)


def _dot_tn(a, b):
    return lax.dot_general(a, b, (((0,), (0,)), ((), ())), preferred_element_type=f32)


def _bdot(a, b):
    return lax.dot_general(a, b, (((2,), (1,)), ((0,), (0,))), preferred_element_type=f32)


def _bdot_nt(a, b):
    return lax.dot_general(a, b, (((2,), (2,)), ((0,), (0,))), preferred_element_type=f32)


def _bdot_tn(a, b):
    return lax.dot_general(a, b, (((1,), (1,)), ((0,), (0,))), preferred_element_type=f32)


def _sigmoid(x):
    return 1.0 / (1.0 + jnp.exp(-x))


def _norm_mod(x, g, sh, sc):
    ms = jnp.mean(x * x, axis=-1, keepdims=True)
    return (x * lax.rsqrt(ms + EPS) * g) * (1.0 + sc) + sh


def _group_sum(x, gmat):
    hi = x.astype(bf16)
    lo = (x - hi.astype(f32)).astype(bf16)
    cols = []
    for c in range(x.shape[1] // 256):
        sl = slice(c * 256, (c + 1) * 256)
        cols.append(_dot(hi[:, sl], gmat) + _dot(lo[:, sl], gmat))
    return cols[0] if len(cols) == 1 else jnp.concatenate(cols, axis=1)


def _softmax_parts(parts):
    m = parts[0].max(axis=-1, keepdims=True)
    for p in parts[1:]:
        m = jnp.maximum(m, p.max(axis=-1, keepdims=True))
    es = [jnp.exp(p - m) for p in parts]
    l = es[0].sum(axis=-1, keepdims=True)
    for e in es[1:]:
        l = l + e.sum(axis=-1, keepdims=True)
    inv = 1.0 / l
    return [e * inv for e in es]


def _lane_lo(n=128):
    return lax.broadcasted_iota(jnp.int32, (1, n), 1) < HD


def _stack_halves(q):
    lo = _lane_lo()
    return jnp.concatenate([jnp.where(lo, q, 0.0), jnp.where(lo, 0.0, q)], axis=0)


def _ada_kernel(cond_ref, w_ref, b_ref, o_ref):
    x = cond_ref[...]
    s = x * _sigmoid(x)
    o_ref[0] = _dot(s.astype(bf16), w_ref[0].astype(bf16)) + b_ref[0]


def _ada_table(cond8, w_ada, b_ada):
    depth = w_ada.shape[0]
    tn = 1536
    out = pl.pallas_call(
        _ada_kernel,
        grid=(depth, 6 * D // tn),
        in_specs=[
            pl.BlockSpec((8, D), lambda l, n: (0, 0)),
            pl.BlockSpec((1, D, tn), lambda l, n: (l, 0, n)),
            pl.BlockSpec((1, 1, tn), lambda l, n: (l, 0, n)),
        ],
        out_specs=pl.BlockSpec((1, 8, tn), lambda l, n: (l, 0, n)),
        out_shape=jax.ShapeDtypeStruct((depth, 8, 6 * D), f32),
        compiler_params=pltpu.CompilerParams(vmem_limit_bytes=VMEM_LIMIT),
        name="ada_table",
    )(cond8, w_ada, b_ada.reshape(depth, 1, 6 * D))
    out = out.reshape(depth, 8, 6, D)
    return [[out[l, :, k, :].reshape(8, 1, D) for k in range(6)] for l in range(depth)]


def _qk_norm(y, gain, gmat):
    ss = _group_sum(y * y, gmat) * (1.0 / HD)
    return y * lax.rsqrt(ss + EPS) * gain


def _rope(y, cos, sin):
    even = (lax.broadcasted_iota(jnp.int32, (1, 128), 1) % 2) == 0
    outs = []
    for c in range(y.shape[1] // 128):
        yc = y[:, c * 128:(c + 1) * 128]
        swapped = jnp.where(even, pltpu.roll(yc, 127, 1), pltpu.roll(yc, 1, 1))
        outs.append(yc * cos + swapped * sin)
    return jnp.concatenate(outs, axis=1)


def _inproj_kernel(x_ref, sh_ref, sc_ref, g_ref, w_ref, gain_ref, gmat_ref, cos_ref, sin_ref,
                   o_ref, h_scr, *, tm):
    i = pl.program_id(0)
    j = pl.program_id(1)

    @pl.when(j == 0)
    def _():
        h_scr[...] = _norm_mod(x_ref[...], g_ref[...], sh_ref[0], sc_ref[0]).astype(bf16)

    y = _dot(h_scr[...], w_ref[...])
    is_norm = jnp.logical_and(j != 2, j != 5)
    is_rope = jnp.logical_and(j < 2, i >= NP // tm)

    @pl.when(jnp.logical_not(is_norm))
    def _():
        o_ref[...] = y

    @pl.when(jnp.logical_and(is_norm, jnp.logical_not(is_rope)))
    def _():
        o_ref[...] = _qk_norm(y, gain_ref[0], gmat_ref[...])

    @pl.when(is_rope)
    def _():
        o_ref[...] = _rope(_qk_norm(y, gain_ref[0], gmat_ref[...]), cos_ref[...], sin_ref[...])


def _in_projection(x, sh, sc, g, w_bf, gains, gmat, cos_t, sin_t):
    tm = 512
    n_prompt_tiles = NP // tm
    tiles_per_seq = SAMPLE_LEN // tm
    cidx = functools.partial(_cond_idx, tm=tm)
    rope_idx = lambda i, j: (jnp.maximum(i - n_prompt_tiles, 0) % tiles_per_seq, 0)
    return pl.pallas_call(
        functools.partial(_inproj_kernel, tm=tm),
        grid=(NT // tm, IN_COLS // SEG),
        in_specs=[
            pl.BlockSpec((tm, D), lambda i, j: (i, 0)),
            pl.BlockSpec((1, 1, D), lambda i, j: (cidx(i), 0, 0)),
            pl.BlockSpec((1, 1, D), lambda i, j: (cidx(i), 0, 0)),
            pl.BlockSpec((1, D), lambda i, j: (0, 0)),
            pl.BlockSpec((D, SEG), lambda i, j: (0, j)),
            pl.BlockSpec((1, 1, SEG), lambda i, j: (j, 0, 0)),
            pl.BlockSpec((256, 256), lambda i, j: (0, 0)),
            pl.BlockSpec((tm, 128), rope_idx),
            pl.BlockSpec((tm, 128), rope_idx),
        ],
        out_specs=pl.BlockSpec((tm, SEG), lambda i, j: (i, j)),
        out_shape=jax.ShapeDtypeStruct((NT, IN_COLS), f32),
        scratch_shapes=[pltpu.VMEM((tm, D), bf16)],
        compiler_params=pltpu.CompilerParams(
            dimension_semantics=("parallel", "arbitrary"), vmem_limit_bytes=VMEM_LIMIT),
        name="in_projection",
    )(x, sh, sc, g, w_bf, gains, gmat, cos_t, sin_t)


def _lambda_value(lamp_ref, lam_init):
    lp = lamp_ref[...]
    e1 = jnp.exp(jnp.sum(lp[0:1] * lp[1:2], axis=-1, keepdims=True))
    e2 = jnp.exp(jnp.sum(lp[2:3] * lp[3:4], axis=-1, keepdims=True))
    return e1 - e2 + lam_init


def _sub_ln(o, subg, lam_init):
    ms = jnp.mean(o * o, axis=-1, keepdims=True)
    return o * lax.rsqrt(ms + EPS) * subg * (1.0 - lam_init)


def _prompt_attn_kernel(p_ref, lamp_ref, subg_ref, o_ref, *, lam_init):
    lam = _lambda_value(lamp_ref, lam_init)
    t = PROMPT_LEN
    lo = _lane_lo()
    scale = HD ** -0.5
    for h in range(DIFF_HEADS):
        q = p_ref[:, h * 128:(h + 1) * 128]
        k = p_ref[:, SEG + h * 128:SEG + (h + 1) * 128].astype(bf16)
        v = p_ref[:, 2 * SEG + h * 128:2 * SEG + (h + 1) * 128].astype(bf16)
        s = _dot_nt((_stack_halves(q) * scale).astype(bf16), k)
        (p,) = _softmax_parts([s])
        pd = p[:t] - lam * p[t:]
        o = _dot(pd.astype(bf16), v)
        o_ref[:, h * 128:(h + 1) * 128] = _sub_ln(o, subg_ref[...], lam_init).astype(bf16)
    for hp in range(NA_HEADS // 2):
        q = p_ref[:, 3 * SEG + hp * 128:3 * SEG + (hp + 1) * 128]
        k = p_ref[:, 4 * SEG + hp * 128:4 * SEG + (hp + 1) * 128].astype(bf16)
        v = p_ref[:, 5 * SEG + hp * 128:5 * SEG + (hp + 1) * 128].astype(bf16)
        s = _dot_nt((_stack_halves(q) * scale).astype(bf16), k)
        (p,) = _softmax_parts([s])
        o = _dot(p.astype(bf16), v)
        o_ref[:, SEG + hp * 128:SEG + (hp + 1) * 128] = jnp.where(lo, o[:t], o[t:]).astype(bf16)


def _prompt_attention(proj, lamp, subg, lam_init):
    return pl.pallas_call(
        functools.partial(_prompt_attn_kernel, lam_init=lam_init),
        grid=(N_PROMPT_SEQ,),
        in_specs=[
            pl.BlockSpec((PROMPT_LEN, IN_COLS), lambda b: (b, 0)),
            pl.BlockSpec((4, HD), lambda b: (0, 0)),
            pl.BlockSpec((1, 128), lambda b: (0, 0)),
        ],
        out_specs=pl.BlockSpec((PROMPT_LEN, D), lambda b: (b, 0)),
        out_shape=jax.ShapeDtypeStruct((NP, D), bf16),
        compiler_params=pltpu.CompilerParams(
            dimension_semantics=("parallel",), vmem_limit_bytes=VMEM_LIMIT),
        name="prompt_attention",
    )(proj, lamp, subg)


def _latent_diff_kernel(q_ref, kn_ref, vn_ref, kc_ref, vc_ref, lamp_ref, subg_ref, o_ref,
                        *, lam_init, tq):
    lam = _lambda_value(lamp_ref, lam_init)
    scale = HD ** -0.5
    qq = (_stack_halves(q_ref[...]) * scale).astype(bf16)
    s_c = _dot_nt(qq, kc_ref[0].astype(bf16))
    s_n = _dot_nt(qq, kn_ref[...].astype(bf16))
    p_c, p_n = _softmax_parts([s_c, s_n])
    pd_c = p_c[:tq] - lam * p_c[tq:]
    pd_n = p_n[:tq] - lam * p_n[tq:]
    o = _dot(pd_c.astype(bf16), vc_ref[0].astype(bf16)) + _dot(pd_n.astype(bf16), vn_ref[...].astype(bf16))
    o_ref[...] = _sub_ln(o, subg_ref[...], lam_init).astype(bf16)


def _latent_diff_attention(proj, cache_k, cache_v, lamp, subg, lam_init):
    tq = 256
    nqb = SAMPLE_LEN // tq
    q0 = NP // tq
    s0 = NP // SAMPLE_LEN
    return pl.pallas_call(
        functools.partial(_latent_diff_kernel, lam_init=lam_init, tq=tq),
        grid=(N_SAMPLE_SEQ, DIFF_HEADS, nqb),
        in_specs=[
            pl.BlockSpec((tq, 128), lambda b, h, q: (q0 + b * nqb + q, h)),
            pl.BlockSpec((SAMPLE_LEN, 128), lambda b, h, q: (s0 + b, 4 + h)),
            pl.BlockSpec((SAMPLE_LEN, 128), lambda b, h, q: (s0 + b, 8 + h)),
            pl.BlockSpec((1, PAST, 128), lambda b, h, q: (b, 0, h)),
            pl.BlockSpec((1, PAST, 128), lambda b, h, q: (b, 0, h)),
            pl.BlockSpec((4, HD), lambda b, h, q: (0, 0)),
            pl.BlockSpec((1, 128), lambda b, h, q: (0, 0)),
        ],
        out_specs=pl.BlockSpec((tq, 128), lambda b, h, q: (b * nqb + q, h)),
        out_shape=jax.ShapeDtypeStruct((NS, SEG), bf16),
        compiler_params=pltpu.CompilerParams(
            dimension_semantics=("parallel", "parallel", "arbitrary"), vmem_limit_bytes=VMEM_LIMIT),
        name="latent_diff_attention",
    )(proj, proj, proj, cache_k, cache_v, lamp, subg)


def _rpb_table_kernel(rpb_ref, o_ref):
    h = pl.program_id(0)
    wq = lax.broadcasted_iota(jnp.int32, (GRID_W, GRID_W), 0)
    wk = lax.broadcasted_iota(jnp.int32, (GRID_W, GRID_W), 1)
    col_start = jnp.clip(wq - NA_KW // 2, 0, GRID_W - NA_KW)
    col_in = jnp.logical_and(wk >= col_start, wk < col_start + NA_KW)
    col_off = jnp.clip(wk - wq, -(NA_KW - 1), NA_KW - 1) + (NA_KW - 1)
    n_dr = 2 * NA_KH - 1
    n_dc = 2 * NA_KW - 1
    for dr in range(n_dr):
        t = jnp.zeros((GRID_W, GRID_W), f32)
        for c in range(n_dc):
            t = jnp.where(col_off == c, rpb_ref[h * (n_dr * n_dc) + dr * n_dc + c], t)
        o_ref[0, dr] = jnp.where(col_in, t, NEG_BIG)


def _rpb_table(rpb):
    n_dr = 2 * NA_KH - 1
    tcol = pl.pallas_call(
        _rpb_table_kernel,
        grid=(NA_HEADS,),
        in_specs=[pl.BlockSpec(memory_space=pltpu.SMEM)],
        out_specs=pl.BlockSpec((1, n_dr, GRID_W, GRID_W), lambda h: (h, 0, 0, 0)),
        out_shape=jax.ShapeDtypeStruct((NA_HEADS, n_dr, GRID_W, GRID_W), f32),
        name="rpb_table",
    )(rpb.reshape(-1))
    return jnp.stack(
        [jnp.concatenate([tcol[:, j - s + NA_KH - 1] for j in range(NA_KH)], axis=-1) for s in range(NA_KH)],
        axis=1)


def _latent_na_kernel(q_ref, k_ref, v_ref, kc_ref, vc_ref, bias_ref, o_ref):
    scale = HD ** -0.5
    lo = _lane_lo()
    kc = kc_ref[0].astype(bf16)
    vc = vc_ref[0].astype(bf16)
    win = NA_KH * GRID_W

    def row(r, carry):
        rs = jnp.clip(r - NA_KH // 2, 0, GRID_R - NA_KH)
        sidx = r - rs
        q = q_ref[pl.ds(pl.multiple_of(r * GRID_W, GRID_W), GRID_W), :]
        qq = (_stack_halves(q) * scale).astype(bf16)
        k0 = pl.multiple_of(rs * GRID_W, GRID_W)
        kw = k_ref[pl.ds(k0, win), :].astype(bf16)
        vw = v_ref[pl.ds(k0, win), :].astype(bf16)
        bias = jnp.concatenate([bias_ref[0, sidx], bias_ref[1, sidx]], axis=0)
        s_loc = _dot_nt(qq, kw) + bias
        s_ctx = _dot_nt(qq, kc)
        p_loc, p_ctx = _softmax_parts([s_loc, s_ctx])
        o = _dot(p_loc.astype(bf16), vw) + _dot(p_ctx.astype(bf16), vc)
        o_ref[pl.ds(pl.multiple_of(r * GRID_W, GRID_W), GRID_W), :] = (
            jnp.where(lo, o[:GRID_W], o[GRID_W:]).astype(bf16))
        return carry

    lax.fori_loop(0, GRID_R, row, 0)


def _latent_na_attention(proj, cache_k, cache_v, bias):
    s0 = NP // SAMPLE_LEN
    return pl.pallas_call(
        _latent_na_kernel,
        grid=(N_SAMPLE_SEQ, NA_HEADS // 2),
        in_specs=[
            pl.BlockSpec((SAMPLE_LEN, 128), lambda b, h: (s0 + b, 12 + h)),
            pl.BlockSpec((SAMPLE_LEN, 128), lambda b, h: (s0 + b, 16 + h)),
            pl.BlockSpec((SAMPLE_LEN, 128), lambda b, h: (s0 + b, 20 + h)),
            pl.BlockSpec((1, PAST, 128), lambda b, h: (b, 0, h)),
            pl.BlockSpec((1, PAST, 128), lambda b, h: (b, 0, h)),
            pl.BlockSpec((2, NA_KH, GRID_W, NA_KH * GRID_W), lambda b, h: (h, 0, 0, 0)),
        ],
        out_specs=pl.BlockSpec((SAMPLE_LEN, 128), lambda b, h: (b, h)),
        out_shape=jax.ShapeDtypeStruct((NS, SEG), bf16),
        compiler_params=pltpu.CompilerParams(
            dimension_semantics=("parallel", "parallel"), vmem_limit_bytes=VMEM_LIMIT),
        name="latent_na_attention",
    )(proj, proj, proj, cache_k, cache_v, bias)


def _out_proj_kernel(a_ref, w_ref, x_ref, gt_ref, o_ref):
    o_ref[...] = x_ref[...] + gt_ref[0] * _dot(a_ref[...], w_ref[...])


def _out_projection(a_bf, w_bf, x, gate):
    tm = 512
    cidx = functools.partial(_cond_idx, tm=tm)
    return pl.pallas_call(
        _out_proj_kernel,
        grid=(NT // tm,),
        in_specs=[
            pl.BlockSpec((tm, D), lambda i: (i, 0)),
            pl.BlockSpec((D, D), lambda i: (0, 0)),
            pl.BlockSpec((tm, D), lambda i: (i, 0)),
            pl.BlockSpec((1, 1, D), lambda i: (cidx(i), 0, 0)),
        ],
        out_specs=pl.BlockSpec((tm, D), lambda i: (i, 0)),
        out_shape=jax.ShapeDtypeStruct((NT, D), f32),
        compiler_params=pltpu.CompilerParams(
            dimension_semantics=("parallel",), vmem_limit_bytes=VMEM_LIMIT),
        name="out_projection",
    )(a_bf, w_bf, x, gate)


def _rwkv_out_kernel(yf_ref, yb_ref, bonus_ref, gate_ref, lng_ref, lnb_ref, gmat_ref, w_ref,
                     x_ref, gt_ref, o_ref):
    y = yf_ref[...] + yb_ref[...]
    gmat = gmat_ref[...]
    mu = _group_sum(y, gmat) * (1.0 / HD)
    yc = y - mu
    var = _group_sum(yc * yc, gmat) * (1.0 / HD)
    z = yc * lax.rsqrt(var + GN_EPS) * lng_ref[...] + lnb_ref[...] + bonus_ref[...]
    z = (z * gate_ref[...]).astype(bf16)
    o_ref[...] = x_ref[...] + gt_ref[0] * _dot(z, w_ref[...])


def _rwkv_out_projection(yf, yb, bonus, gate_lora, ln_g, ln_b, gmat, w_bf, x, gate):
    tm = 512
    cidx = functools.partial(_cond_idx, tm=tm)
    row = pl.BlockSpec((tm, D), lambda i: (i, 0))
    vec = pl.BlockSpec((1, D), lambda i: (0, 0))
    return pl.pallas_call(
        _rwkv_out_kernel,
        grid=(NT // tm,),
        in_specs=[row, row, row, row, vec, vec,
                  pl.BlockSpec((256, 256), lambda i: (0, 0)),
                  pl.BlockSpec((D, D), lambda i: (0, 0)),
                  row,
                  pl.BlockSpec((1, 1, D), lambda i: (cidx(i), 0, 0))],
        out_specs=row,
        out_shape=jax.ShapeDtypeStruct((NT, D), f32),
        compiler_params=pltpu.CompilerParams(
            dimension_semantics=("parallel",), vmem_limit_bytes=VMEM_LIMIT),
        name="rwkv_out_projection",
    )(yf, yb, bonus, gate_lora, ln_g, ln_b, gmat, w_bf, x, gate)


def _swiglu_hidden(xb, w1, w3):
    a = _dot(xb, w1)
    return ((a * _sigmoid(a)) * _dot(xb, w3)).astype(bf16)


def _ffn_kernel(x_ref, sh_ref, sc_ref, gt_ref, g_ref, w1_ref, w3_ref, w2_ref, o_ref, h_scr, acc_scr):
    f = pl.program_id(1)

    @pl.when(f == 0)
    def _():
        h_scr[...] = _norm_mod(x_ref[...], g_ref[...], sh_ref[0], sc_ref[0]).astype(bf16)
        acc_scr[...] = jnp.zeros_like(acc_scr)

    acc_scr[...] += _dot(_swiglu_hidden(h_scr[...], w1_ref[...], w3_ref[...]), w2_ref[...])

    @pl.when(f == pl.num_programs(1) - 1)
    def _():
        o_ref[...] = x_ref[...] + gt_ref[0] * acc_scr[...]


def _dense_ffn(x, sh, sc, gt, g, w1, w3, w2):
    tm = 512
    tf = FFN_DIM // 2
    cidx = functools.partial(_cond_idx, tm=tm)
    mod = pl.BlockSpec((1, 1, D), lambda i, f: (cidx(i), 0, 0))
    return pl.pallas_call(
        _ffn_kernel,
        grid=(NT // tm, FFN_DIM // tf),
        in_specs=[
            pl.BlockSpec((tm, D), lambda i, f: (i, 0)),
            mod, mod, mod,
            pl.BlockSpec((1, D), lambda i, f: (0, 0)),
            pl.BlockSpec((D, tf), lambda i, f: (0, f)),
            pl.BlockSpec((D, tf), lambda i, f: (0, f)),
            pl.BlockSpec((tf, D), lambda i, f: (f, 0)),
        ],
        out_specs=pl.BlockSpec((tm, D), lambda i, f: (i, 0)),
        out_shape=jax.ShapeDtypeStruct((NT, D), f32),
        scratch_shapes=[pltpu.VMEM((tm, D), bf16), pltpu.VMEM((tm, D), f32)],
        compiler_params=pltpu.CompilerParams(
            dimension_semantics=("parallel", "arbitrary"), vmem_limit_bytes=VMEM_LIMIT),
        name="ffn_mixer",
    )(x, sh, sc, gt, g, w1, w3, w2)


MOE_TILE = 1024
MOE_MAIN = 288
MOE_EXTRA = 128
MOE_ROWS = MOE_MAIN + -(-(MOE_TILE - MOE_MAIN) // MOE_EXTRA) * MOE_EXTRA


def _router_kernel(x_ref, sh_ref, sc_ref, g_ref, wr_ref, br_ref, tri_ref,
                   h_out, gates_out, rank_out, cnt_out):
    h = _norm_mod(x_ref[...], g_ref[...], sh_ref[0], sc_ref[0])
    h_out[...] = h.astype(bf16)
    logits = jnp.dot(h, wr_ref[...], preferred_element_type=f32,
                     precision=lax.Precision.HIGHEST) + br_ref[...]
    lane = lax.broadcasted_iota(jnp.int32, logits.shape, 1)
    logits = jnp.where(lane < N_EXPERTS, logits, -jnp.inf)
    m1 = logits.max(axis=-1, keepdims=True)
    i1 = jnp.min(jnp.where(logits == m1, lane, 128), axis=-1, keepdims=True)
    rest = jnp.where(lane == i1, -jnp.inf, logits)
    m2 = rest.max(axis=-1, keepdims=True)
    i2 = jnp.min(jnp.where(rest == m2, lane, 128), axis=-1, keepdims=True)
    e2 = jnp.exp(m2 - m1)
    den = 1.0 / (1.0 + e2)
    gates_out[...] = jnp.where(lane == i1, den, jnp.where(lane == i2, e2 * den, 0.0))
    sel = jnp.logical_or(lane == i1, lane == i2)
    self32 = sel.astype(f32)
    before = _dot(tri_ref[...], self32.astype(bf16))
    rank_out[...] = jnp.where(sel, before, -1.0)
    cnt_out[0] = jnp.sum(self32, axis=0, keepdims=True).astype(jnp.int32)


def _moe_route(x, sh, sc, g, w_router, b_router):
    tm = MOE_TILE
    cidx = functools.partial(_cond_idx, tm=tm)
    t = jnp.arange(tm)
    tri = (t[:, None] > t[None, :]).astype(bf16)
    mod = pl.BlockSpec((1, 1, D), lambda i: (cidx(i), 0, 0))
    return pl.pallas_call(
        _router_kernel,
        grid=(NT // tm,),
        in_specs=[
            pl.BlockSpec((tm, D), lambda i: (i, 0)),
            mod, mod,
            pl.BlockSpec((1, D), lambda i: (0, 0)),
            pl.BlockSpec((D, 128), lambda i: (0, 0)),
            pl.BlockSpec((1, 128), lambda i: (0, 0)),
            pl.BlockSpec((tm, tm), lambda i: (0, 0)),
        ],
        out_specs=[
            pl.BlockSpec((tm, D), lambda i: (i, 0)),
            pl.BlockSpec((tm, 128), lambda i: (i, 0)),
            pl.BlockSpec((tm, 128), lambda i: (i, 0)),
            pl.BlockSpec((1, 1, 128), lambda i: (i, 0, 0)),
        ],
        out_shape=[
            jax.ShapeDtypeStruct((NT, D), bf16),
            jax.ShapeDtypeStruct((NT, 128), f32),
            jax.ShapeDtypeStruct((NT, 128), f32),
            jax.ShapeDtypeStruct((NT // tm, 1, 128), jnp.int32),
        ],
        compiler_params=pltpu.CompilerParams(
            dimension_semantics=("parallel",), vmem_limit_bytes=VMEM_LIMIT),
        name="moe_router",
    )(x, sh, sc, g, w_router, b_router, tri)


def _moe_kernel(cnt_ref, h_ref, rank_ref, gates_ref, x_ref, gt_ref, w1_ref, w3_ref, w2_ref,
                o_ref, xc_scr, acc_scr, rcol_scr, gcol_scr):
    i = pl.program_id(0)
    e = pl.program_id(1)
    f = pl.program_id(2)
    cnt = cnt_ref[i * N_EXPERTS + e]
    n_extra = jnp.maximum(cnt - MOE_MAIN + MOE_EXTRA - 1, 0) // MOE_EXTRA

    def for_each_block(fn):
        fn(0, MOE_MAIN)

        def body(b, carry):
            fn(pl.multiple_of(MOE_MAIN + b * MOE_EXTRA, 32), MOE_EXTRA)
            return carry

        lax.fori_loop(0, n_extra, body, 0)

    def one_hot_t(slot0, nrows):
        slot = lax.broadcasted_iota(jnp.int32, (MOE_TILE, nrows), 1) + slot0
        return (rcol_scr[...] == slot).astype(bf16)

    @pl.when(jnp.logical_and(e == 0, f == 0))
    def _():
        o_ref[...] = x_ref[...]

    @pl.when(f == 0)
    def _():
        lane = lax.broadcasted_iota(jnp.int32, (MOE_TILE, 128), 1)
        mine = lane == e
        rcol_scr[...] = jnp.sum(jnp.where(mine, rank_ref[...], 0.0), axis=1, keepdims=True).astype(jnp.int32)
        gcol_scr[...] = jnp.sum(jnp.where(mine, gates_ref[...], 0.0), axis=1, keepdims=True)

        def gather(slot0, nrows):
            xc_scr[pl.ds(slot0, nrows), :] = _dot_tn(one_hot_t(slot0, nrows), h_ref[...]).astype(bf16)

        for_each_block(gather)

    def expert(slot0, nrows):
        rows = pl.ds(slot0, nrows)
        part = _dot(_swiglu_hidden(xc_scr[rows, :], w1_ref[0], w3_ref[0]), w2_ref[0])

        @pl.when(f == 0)
        def _():
            acc_scr[rows, :] = part

        @pl.when(f != 0)
        def _():
            acc_scr[rows, :] += part

    for_each_block(expert)

    @pl.when(f == pl.num_programs(2) - 1)
    def _():
        def scatter(slot0, nrows):
            out = acc_scr[pl.ds(slot0, nrows), :]
            hi = out.astype(bf16)
            lo = (out - hi.astype(f32)).astype(bf16)
            for t0 in range(0, MOE_TILE, 256):
                rows = slice(t0, t0 + 256)
                slot = lax.broadcasted_iota(jnp.int32, (256, nrows), 1) + slot0
                pt = (rcol_scr[rows, :] == slot).astype(bf16)
                o_ref[rows, :] += (gcol_scr[rows, :] * gt_ref[0]) * (_dot(pt, hi) + _dot(pt, lo))

        for_each_block(scatter)


def _moe_experts(cnt, h_bf, rank, gates, x, gt, w1, w3, w2):
    tm = MOE_TILE
    tf = EXPERT_DIM // 4
    cidx = functools.partial(_cond_idx, tm=tm)
    grid_spec = pltpu.PrefetchScalarGridSpec(
        num_scalar_prefetch=1,
        grid=(NT // tm, N_EXPERTS, EXPERT_DIM // tf),
        in_specs=[
            pl.BlockSpec((tm, D), lambda i, e, f, c: (i, 0)),
            pl.BlockSpec((tm, 128), lambda i, e, f, c: (i, 0)),
            pl.BlockSpec((tm, 128), lambda i, e, f, c: (i, 0)),
            pl.BlockSpec((tm, D), lambda i, e, f, c: (i, 0)),
            pl.BlockSpec((1, 1, D), lambda i, e, f, c: (cidx(i), 0, 0)),
            pl.BlockSpec((1, D, tf), lambda i, e, f, c: (e, 0, f)),
            pl.BlockSpec((1, D, tf), lambda i, e, f, c: (e, 0, f)),
            pl.BlockSpec((1, tf, D), lambda i, e, f, c: (e, f, 0)),
        ],
        out_specs=pl.BlockSpec((tm, D), lambda i, e, f, c: (i, 0)),
        scratch_shapes=[pltpu.VMEM((MOE_ROWS, D), bf16), pltpu.VMEM((MOE_ROWS, D), f32),
                        pltpu.VMEM((tm, 1), jnp.int32), pltpu.VMEM((tm, 1), f32)],
    )
    return pl.pallas_call(
        _moe_kernel,
        grid_spec=grid_spec,
        out_shape=jax.ShapeDtypeStruct((NT, D), f32),
        compiler_params=pltpu.CompilerParams(
            dimension_semantics=("parallel", "arbitrary", "arbitrary"), vmem_limit_bytes=VMEM_LIMIT),
        name="moe_experts",
    )(cnt, h_bf, rank, gates, x, gt, w1, w3, w2)


def _rwkv_proj_kernel(x_ref, xp_ref, xn_ref, sh_ref, sc_ref, g_ref, mu_ref,
                      wr_ref, wk_ref, wv_ref, g1_ref, g2_ref, w1_ref, w2_ref, a1_ref, a2_ref,
                      w0_ref, a0_ref, kk_ref, ka_ref, rk_ref, gmat_ref,
                      r_out, v_out, kkn_out, bonus_out, gate_out,
                      lwf_out, lwb_out, af_out, ab_out, kdf_out, kdb_out, *, tm):
    i = pl.program_id(0)
    g, sh, sc = g_ref[...], sh_ref[0], sc_ref[0]
    h = _norm_mod(x_ref[...], g, sh, sc)
    n_prompt_tiles = NP // tm
    tiles_per_seq = SAMPLE_LEN // tm
    pos = (i - n_prompt_tiles) % tiles_per_seq
    has_prev = jnp.logical_and(i >= n_prompt_tiles, pos != 0)
    has_next = jnp.logical_and(i >= n_prompt_tiles, pos != tiles_per_seq - 1)
    h_before = jnp.where(has_prev, _norm_mod(xp_ref[...], g, sh, sc)[7:8], 0.0)
    h_after = jnp.where(has_next, _norm_mod(xn_ref[...], g, sh, sc)[0:1], 0.0)
    rowi = lax.broadcasted_iota(jnp.int32, (tm, 1), 0)
    h_prev = jnp.where(rowi == 0, h_before, pltpu.roll(h, 1, 0))
    h_next = jnp.where(rowi == tm - 1, h_after, pltpu.roll(h, tm - 1, 0))
    xx = 0.5 * (h_prev + h_next) - h
    mix = lambda n: (h + xx * mu_ref[n:n + 1]).astype(bf16)

    r = _dot(mix(0), wr_ref[...])
    k = _dot(mix(2), wk_ref[...])
    v = _dot(mix(3), wv_ref[...])
    gate_out[...] = _dot(_sigmoid(_dot(mix(5), g1_ref[...])).astype(bf16), g2_ref[...])

    lo = _lane_lo()
    tw = jnp.tanh(_dot(mix(1), w1_ref[...]))
    ta = _dot(mix(4), a1_ref[...])
    gmat = gmat_ref[...]
    kk = k * kk_ref[...]
    kkn_out[...] = kk * lax.rsqrt(_group_sum(kk * kk, gmat) + 1e-12)
    r_out[...] = r
    v_out[...] = v
    kd_sum = jnp.zeros_like(k)
    for d, (lw_out, a_out, kd_out) in enumerate(((lwf_out, af_out, kdf_out), (lwb_out, ab_out, kdb_out))):
        keep = lo if d == 0 else jnp.logical_not(lo)
        zw = w0_ref[d:d + 1] + _dot(jnp.where(keep, tw, 0.0).astype(bf16), w2_ref[...])
        lw_out[...] = -math.exp(-0.5) * _sigmoid(zw)
        a = _sigmoid(a0_ref[d:d + 1] + _dot(jnp.where(keep, ta, 0.0).astype(bf16), a2_ref[...]))
        a_out[...] = a
        kd = k * (1.0 + (a - 1.0) * ka_ref[...])
        kd_out[...] = kd
        kd_sum = kd_sum + kd
    bonus_out[...] = _group_sum(r * kd_sum * rk_ref[...], gmat) * v


def _rwkv_projection(x, sh, sc, g, mu, wr, wk, wv, g1, g2, w1, w2, a1, a2, w0, a0, k_k, k_a, r_k, gmat):
    tm = 256
    cidx = functools.partial(_cond_idx, tm=tm)
    hb = tm // 8
    n8 = NT // 8
    full = lambda shape: pl.BlockSpec(shape, lambda i: tuple(0 for _ in shape))
    row = pl.BlockSpec((tm, D), lambda i: (i, 0))
    return pl.pallas_call(
        functools.partial(_rwkv_proj_kernel, tm=tm),
        grid=(NT // tm,),
        in_specs=[
            row,
            pl.BlockSpec((8, D), lambda i: (jnp.maximum(i * hb - 1, 0), 0)),
            pl.BlockSpec((8, D), lambda i: (jnp.minimum((i + 1) * hb, n8 - 1), 0)),
            pl.BlockSpec((1, 1, D), lambda i: (cidx(i), 0, 0)),
            pl.BlockSpec((1, 1, D), lambda i: (cidx(i), 0, 0)),
            full((1, D)), full((6, D)),
            full((D, D)), full((D, D)), full((D, D)),
            full((D, GATE_LORA_PAD)), full((GATE_LORA_PAD, D)),
            full((D, 2 * LORA)), full((2 * LORA, D)), full((D, 2 * LORA)), full((2 * LORA, D)),
            full((2, D)), full((2, D)), full((1, D)), full((1, D)), full((1, D)),
            full((256, 256)),
        ],
        out_specs=[row] * 11,
        out_shape=[jax.ShapeDtypeStruct((NT, D), f32)] * 11,
        compiler_params=pltpu.CompilerParams(
            dimension_semantics=("parallel",), vmem_limit_bytes=VMEM_LIMIT),
        name="rwkv_projection",
    )(x, x, x, sh, sc, g, mu, wr, wk, wv, g1, g2, w1, w2, a1, a2, w0, a0, k_k, k_a, r_k, gmat)


def _split3(x):
    x1 = x.astype(bf16)
    r1 = x - x1.astype(f32)
    x2 = r1.astype(bf16)
    x3 = (r1 - x2.astype(f32)).astype(bf16)
    return x1, x2, x3


def _scan_kernel(*refs, nc, has_init, emit_state):
    fwd_refs = refs[0:6]
    bwd_refs = refs[6:12]
    tri_ref, mask_ref = refs[12:14]
    pos = 14
    s0_ref = None
    if has_init:
        s0_ref = refs[pos]
        pos += 1
    yf_ref, yb_ref = refs[pos:pos + 2]
    pos += 2
    sout_ref = None
    if emit_state:
        sout_ref = refs[pos]
        pos += 1
    st_scr, cl_scr = refs[pos:pos + 2]

    s = pl.program_id(1)

    @pl.when(s == 0)
    def _():
        if has_init:
            st_scr[...] = s0_ref[0]
        else:
            st_scr[...] = jnp.zeros_like(st_scr)

    c = CHUNK
    for d, drefs in enumerate((fwd_refs, bwd_refs)):
        lw = drefs[3][...]
        tri = tri_ref[d]
        p1, p2, p3 = _split3(lw)
        cl_scr[d] = _dot(tri, p1) + _dot(tri, p2) + _dot(tri, p3)

    lane = lax.broadcasted_iota(jnp.int32, (1, 128), 1)
    m0 = (lane < HD).astype(f32)
    m1 = 1.0 - m0
    rid = lax.broadcasted_iota(jnp.int32, (128, 128), 0)
    cid = lax.broadcasted_iota(jnp.int32, (128, 128), 1)
    eye = (rid == cid).astype(f32)

    n_pairs = RW_HEADS // 2
    nb = 2 * n_pairs

    def per_head_rows(x):
        return jnp.concatenate([x * m0, x * m1], axis=0)

    ars_l, bk_l, bkh_l, v2_l, dec_l = [], [], [], [], []
    for d, drefs in enumerate((fwd_refs, bwd_refs)):
        r_ref, v_ref, kk_ref, lw_ref, a_ref, kd_ref = drefs
        end_row = c - 1 if d == 0 else 0
        for p in range(n_pairs):
            ln = slice(p * 128, (p + 1) * 128)
            cl = cl_scr[d, :, ln]
            kk = kk_ref[:, ln]
            tot = cl[end_row:end_row + 1]
            e_inv = jnp.exp(-cl)
            e_end = jnp.exp(tot - cl)
            kka = kk * a_ref[:, ln]
            kd = kd_ref[:, ln]
            at = per_head_rows(-kk * jnp.exp(cl - lw_ref[:, ln]))
            rt = per_head_rows(r_ref[:, ln] * jnp.exp(cl))
            ars_l.append(jnp.concatenate([at, rt], axis=0).astype(bf16))
            bk_l.append(jnp.concatenate([per_head_rows(kka * e_inv), per_head_rows(kd * e_inv)],
                                        axis=0).astype(bf16))
            bkh_l.append(jnp.concatenate([per_head_rows(kka * e_end), per_head_rows(kd * e_end)],
                                         axis=0).astype(bf16))
            v2_l.append(per_head_rows(v_ref[:, ln]).astype(bf16))
            dec_l.append(jnp.exp(tot))
    ar = jnp.stack(ars_l)
    bk = jnp.stack(bk_l)
    bkh = jnp.stack(bkh_l)
    v2 = jnp.stack(v2_l)
    dec = jnp.stack(dec_l)

    st = st_scr[...].reshape(nb, 128, 128)
    g2 = _bdot_nt(ar, bk).reshape(2, n_pairs, 256, 256) * mask_ref[...][:, None]
    g2 = g2.reshape(nb, 256, 256)
    ars = _bdot_nt(ar, st.astype(bf16))
    gv = _bdot(g2[:, :, 128:].astype(bf16), v2)
    l_bd = g2[:, :128, :128]
    l_bf = l_bd.astype(bf16)
    pk = _bdot(l_bf, l_bf)
    q = eye[None] + l_bd
    for it in range(5):
        pkb = pk.astype(bf16)
        if it < 4:
            res = _bdot(jnp.concatenate([q, pk], axis=1).astype(bf16), pkb)
            q = q + res[:, :128]
            pk = res[:, 128:]
        else:
            q = q + _bdot(q.astype(bf16), pkb)
    rhs = ars[:, :128] + gv[:, :128]
    u2b = _bdot(q.astype(bf16), rhs.astype(bf16)).astype(bf16)
    y2 = ars[:, 128:] + gv[:, 128:] + _bdot(g2[:, 128:, :128].astype(bf16), u2b)
    y = y2[:, :c] + y2[:, c:]
    for d, y_ref in enumerate((yf_ref, yb_ref)):
        for p in range(n_pairs):
            y_ref[:, p * 128:(p + 1) * 128] = y[d * n_pairs + p]
    uv = jnp.concatenate([u2b, v2], axis=1)
    st_scr[...] = (st * dec + _bdot_tn(uv, bkh)).reshape(2, n_pairs, 128, 128)

    if emit_state:
        @pl.when(s == nc - 1)
        def _():
            sout_ref[0] = st_scr[...]


def _rwkv_scan(streams_f, streams_b, tri, mask, s0_bd, *, n_seq, seq_len, row0, emit_state):
    nc = seq_len // CHUNK
    blk0 = row0 // CHUNK
    fwd_spec = pl.BlockSpec((CHUNK, D), lambda b, s: (blk0 + b * nc + s, 0))
    bwd_spec = pl.BlockSpec((CHUNK, D), lambda b, s: (blk0 + b * nc + nc - 1 - s, 0))
    in_specs = [fwd_spec] * 6 + [bwd_spec] * 6 + [
        pl.BlockSpec((2, CHUNK, CHUNK), lambda b, s: (0, 0, 0)),
        pl.BlockSpec((2, 256, 256), lambda b, s: (0, 0, 0)),
    ]
    args = list(streams_f) + list(streams_b) + [tri, mask]
    state_block = (1, 2, RW_HEADS // 2, 128, 128)
    if s0_bd is not None:
        in_specs.append(pl.BlockSpec(state_block, lambda b, s: (b, 0, 0, 0, 0)))
        args.append(s0_bd)
    out_specs = [pl.BlockSpec((CHUNK, D), lambda b, s: (b * nc + s, 0)),
                 pl.BlockSpec((CHUNK, D), lambda b, s: (b * nc + nc - 1 - s, 0))]
    out_shape = [jax.ShapeDtypeStruct((n_seq * seq_len, D), f32)] * 2
    if emit_state:
        out_specs.append(pl.BlockSpec(state_block, lambda b, s: (b, 0, 0, 0, 0)))
        out_shape.append(jax.ShapeDtypeStruct((n_seq,) + state_block[1:], f32))
    return pl.pallas_call(
        functools.partial(_scan_kernel, nc=nc, has_init=s0_bd is not None, emit_state=emit_state),
        grid=(n_seq, nc),
        in_specs=in_specs,
        out_specs=out_specs,
        out_shape=out_shape,
        scratch_shapes=[pltpu.VMEM(state_block[1:], f32), pltpu.VMEM((2, CHUNK, D), f32)],
        compiler_params=pltpu.CompilerParams(
            dimension_semantics=("parallel", "arbitrary"), vmem_limit_bytes=VMEM_LIMIT),
        name="rwkv_scan_prompt" if emit_state else "rwkv_scan_sample",
    )(*args)


def _scan_constants():
    t = jnp.arange(CHUNK)
    lower = (t[:, None] >= t[None, :])
    tri = jnp.stack([lower, lower.T]).astype(bf16)
    masks = []
    for d in range(2):
        strict = (t[:, None] > t[None, :]) if d == 0 else (t[:, None] < t[None, :])
        incl = lower if d == 0 else lower.T
        blocks = []
        for m in (strict, incl):
            bd = jnp.kron(jnp.eye(2, dtype=f32), m.astype(f32))
            blocks.append(jnp.concatenate([bd, bd], axis=1))
        masks.append(jnp.concatenate(blocks, axis=0))
    return tri, jnp.stack(masks)


def _state_to_blockdiag(s):
    n = s.shape[0]
    s = s.reshape(n, 2, RW_HEADS // 2, 2, HD, HD)
    z = jnp.zeros_like(s[:, :, :, 0])
    top = jnp.concatenate([s[:, :, :, 0], z], axis=-1)
    bot = jnp.concatenate([z, s[:, :, :, 1]], axis=-1)
    return jnp.concatenate([top, bot], axis=-2)


def _state_from_blockdiag(sb):
    n = sb.shape[0]
    a = sb[:, :, :, :HD, :HD]
    b = sb[:, :, :, HD:, HD:]
    return jnp.stack([a, b], axis=3).reshape(n, 2, RW_HEADS, HD, HD)


def _rope_tables():
    t = jnp.arange(SAMPLE_LEN)
    rows = (t // GRID_W).astype(f32)
    cols = (t % GRID_W).astype(f32)
    nf = HD // 4
    inv = 10000.0 ** (-jnp.arange(nf, dtype=f32) / nf)
    ang = jnp.concatenate([rows[:, None] * inv, cols[:, None] * inv], axis=-1)
    cos = jnp.repeat(jnp.cos(ang), 2, axis=-1)
    sin = jnp.repeat(jnp.sin(ang), 2, axis=-1) * jnp.tile(jnp.array([-1.0, 1.0], f32), HD // 2)
    return jnp.tile(cos, (1, 2)), jnp.tile(sin, (1, 2))


def kernel(x_prompt, x_sample, cache_diff_k, cache_diff_v, cache_na_k, cache_na_v, state_rwkv, c, c_ctx, w_ada, b_ada, g_mix, g_ffn, w_in, w_out, diff_q_g, diff_k_g, diff_lam_q1, diff_lam_k1, diff_lam_q2, diff_lam_k2, diff_subln_g, na_q_g, na_k_g, na_rpb, ffn_w1, ffn_w3, ffn_w2, rw_mu, rw_wr, rw_wk, rw_wv, rw_wo, rw_w0, rw_w1, rw_w2, rw_a0, rw_a1, rw_a2, rw_g1, rw_g2, rw_k_k, rw_k_a, rw_r_k, rw_ln_g, rw_ln_b, moe_router, moe_router_b, moe_w1, moe_w3, moe_w2):
    x = jnp.concatenate([x_prompt.reshape(NP, D), x_sample.reshape(NS, D)], axis=0)
    cond8 = jnp.concatenate([c_ctx[None, :], c, jnp.zeros((3, D), f32)], axis=0)
    mod = _ada_table(cond8, w_ada, b_ada)
    gmat = jnp.kron(jnp.eye(4, dtype=f32), jnp.ones((HD, HD), f32)).astype(bf16)

    lam_init = 0.8 - 0.6 * math.exp(-0.3 * 0)
    ones_seg = jnp.ones((SEG,), f32)
    tile8 = lambda gvec: jnp.tile(gvec, SEG // HD)
    gains = jnp.stack([tile8(diff_q_g[0]), tile8(diff_k_g[0]), ones_seg,
                       tile8(na_q_g[0]), tile8(na_k_g[0]), ones_seg]).reshape(6, 1, SEG)
    cos_t, sin_t = _rope_tables()
    proj = _in_projection(x, mod[0][0], mod[0][1], g_mix[0][None, :], w_in[0].astype(bf16),
                          gains, gmat, cos_t, sin_t)
    lamp = jnp.stack([diff_lam_q1[0], diff_lam_k1[0], diff_lam_q2[0], diff_lam_k2[0]])
    subg = diff_subln_g[0][None, :]
    o_prompt = _prompt_attention(proj, lamp, subg, lam_init)
    o_diff = _latent_diff_attention(proj, cache_diff_k[:, 0].reshape(N_SAMPLE_SEQ, PAST, SEG),
                                    cache_diff_v[:, 0].reshape(N_SAMPLE_SEQ, PAST, SEG), lamp, subg, lam_init)
    o_na = _latent_na_attention(proj, cache_na_k[:, 0].reshape(N_SAMPLE_SEQ, PAST, SEG),
                                cache_na_v[:, 0].reshape(N_SAMPLE_SEQ, PAST, SEG), _rpb_table(na_rpb[0]))
    o_all = jnp.concatenate([o_prompt, jnp.concatenate([o_diff, o_na], axis=1)], axis=0)
    x = _out_projection(o_all, w_out[0].astype(bf16), x, mod[0][2])
    x = _dense_ffn(x, mod[0][3], mod[0][4], mod[0][5], g_ffn[0][None, :],
                   ffn_w1[0].astype(bf16), ffn_w3[0].astype(bf16), ffn_w2[0].astype(bf16))

    pad_g = GATE_LORA_PAD - GATE_LORA
    g1 = jnp.pad(rw_g1[0], ((0, 0), (0, pad_g))).astype(bf16)
    g2 = jnp.pad(rw_g2[0], ((0, pad_g), (0, 0))).astype(bf16)
    w1cat = jnp.concatenate([rw_w1[0, 0], rw_w1[0, 1]], axis=1).astype(bf16)
    w2cat = jnp.concatenate([rw_w2[0, 0], rw_w2[0, 1]], axis=0).astype(bf16)
    a1cat = jnp.concatenate([rw_a1[0, 0], rw_a1[0, 1]], axis=1).astype(bf16)
    a2cat = jnp.concatenate([rw_a2[0, 0], rw_a2[0, 1]], axis=0).astype(bf16)
    (r, v, kkn, bonus, gate_lora, lwf, lwb, af, ab, kdf, kdb) = _rwkv_projection(
        x, mod[1][0], mod[1][1], g_mix[1][None, :], rw_mu[0],
        rw_wr[0].astype(bf16), rw_wk[0].astype(bf16), rw_wv[0].astype(bf16), g1, g2,
        w1cat, w2cat, a1cat, a2cat, rw_w0[0], rw_a0[0],
        rw_k_k[0][None, :], rw_k_a[0][None, :], rw_r_k[0].reshape(1, D), gmat)
    tri, mask = _scan_constants()
    streams_f = (r, v, kkn, lwf, af, kdf)
    streams_b = (r, v, kkn, lwb, ab, kdb)
    yf_p, yb_p, st_p = _rwkv_scan(streams_f, streams_b, tri, mask, None,
                                  n_seq=N_PROMPT_SEQ, seq_len=PROMPT_LEN, row0=0, emit_state=True)
    yf_s, yb_s = _rwkv_scan(streams_f, streams_b, tri, mask, _state_to_blockdiag(state_rwkv[:, 0]),
                            n_seq=N_SAMPLE_SEQ, seq_len=SAMPLE_LEN, row0=NP, emit_state=False)
    yf = jnp.concatenate([yf_p, yf_s], axis=0)
    yb = jnp.concatenate([yb_p, yb_s], axis=0)
    x = _rwkv_out_projection(yf, yb, bonus, gate_lora, rw_ln_g[0][None, :], rw_ln_b[0][None, :], gmat,
                             rw_wo[0].astype(bf16), x, mod[1][2])
    w_router = jnp.pad(moe_router[0], ((0, 0), (0, 128 - N_EXPERTS)))
    b_router = jnp.pad(moe_router_b[0], (0, 128 - N_EXPERTS))[None, :]
    h_bf, gates, rank, cnt = _moe_route(x, mod[1][3], mod[1][4], g_ffn[1][None, :], w_router, b_router)
    x = _moe_experts(cnt[:, 0, :N_EXPERTS].reshape(-1), h_bf, rank, gates, x, mod[1][5],
                     moe_w1[0].astype(bf16), moe_w3[0].astype(bf16), moe_w2[0].astype(bf16))

    new_dk = proj[:NP, SEG:2 * SEG].reshape(N_PROMPT_SEQ, 1, PROMPT_LEN, DIFF_HEADS, 2 * HD)
    new_dv = proj[:NP, 2 * SEG:3 * SEG].reshape(N_PROMPT_SEQ, 1, PROMPT_LEN, DIFF_HEADS, 2 * HD)
    new_nk = proj[:NP, 4 * SEG:5 * SEG].reshape(N_PROMPT_SEQ, 1, PROMPT_LEN, NA_HEADS, HD)
    new_nv = proj[:NP, 5 * SEG:6 * SEG].reshape(N_PROMPT_SEQ, 1, PROMPT_LEN, NA_HEADS, HD)
    new_state = _state_from_blockdiag(st_p)[:, None]
    return (x[:NP].reshape(N_PROMPT_SEQ, PROMPT_LEN, D), x[NP:].reshape(N_SAMPLE_SEQ, SAMPLE_LEN, D),
            new_dk, new_dv, new_nk, new_nv, new_state)
```

```python
import functools
import math

import jax
import jax.numpy as jnp
from jax import lax
from jax.experimental import pallas as pl
from jax.experimental.pallas import tpu as pltpu

f32 = jnp.float32
bf16 = jnp.bfloat16

D = 1024
N_PROMPT_SEQ, PROMPT_LEN = 32, 256
N_SAMPLE_SEQ, SAMPLE_LEN = 4, 2048
NP = N_PROMPT_SEQ * PROMPT_LEN
NS = N_SAMPLE_SEQ * SAMPLE_LEN
NT = NP + NS
PAST = 256
GRID_W = 64
GRID_R = SAMPLE_LEN // GRID_W
HD = 64
DIFF_HEADS = 4
NA_HEADS = 8
NA_KH = 8
NA_KW = 16
SEG = 512
IN_COLS = 6 * SEG
FFN_DIM = 2816
N_EXPERTS = 8
EXPERT_DIM = 3584
RW_HEADS = 16
LORA = 64
GATE_LORA = 160
GATE_LORA_PAD = 256
EPS = 1e-6
GN_EPS = 64e-5
NEG_BIG = -1e30
NA_ROWS_PER_STEP = 8
CHUNK = 64
VMEM_LIMIT = 56 * 1024 * 1024


def _cond_idx(i, tm):
    return jnp.maximum((i * tm) // SAMPLE_LEN - (NP // SAMPLE_LEN - 1), 0)


def _dot(a, b):
    return jnp.dot(a, b, preferred_element_type=f32)


def _dot_nt(a, b):
    return lax.dot_general(a, b, (((1,), (1,)), ((), ())), preferred_element_type=f32)


def _dot_tn(a, b):
    return lax.dot_general(a, b, (((0,), (0,)), ((), ())), preferred_element_type=f32)


def _bdot(a, b):
    return lax.dot_general(a, b, (((2,), (1,)), ((0,), (0,))), preferred_element_type=f32)


def _bdot_nt(a, b):
    return lax.dot_general(a, b, (((2,), (2,)), ((0,), (0,))), preferred_element_type=f32)


def _bdot_tn(a, b):
    return lax.dot_general(a, b, (((1,), (1,)), ((0,), (0,))), preferred_element_type=f32)


def _sigmoid(x):
    return 1.0 / (1.0 + jnp.exp(-x))


def _norm_mod(x, g, sh, sc):
    ms = jnp.mean(x * x, axis=-1, keepdims=True)
    return (x * lax.rsqrt(ms + EPS) * g) * (1.0 + sc) + sh


def _group_sum(x, gmat):
    xb = x.astype(bf16)
    cols = [_dot(xb[:, c * 256:(c + 1) * 256], gmat) for c in range(x.shape[1] // 256)]
    return cols[0] if len(cols) == 1 else jnp.concatenate(cols, axis=1)


def _softmax_parts(parts):
    m = parts[0].max(axis=-1, keepdims=True)
    for p in parts[1:]:
        m = jnp.maximum(m, p.max(axis=-1, keepdims=True))
    es = [jnp.exp(p - m) for p in parts]
    l = es[0].sum(axis=-1, keepdims=True)
    for e in es[1:]:
        l = l + e.sum(axis=-1, keepdims=True)
    inv = 1.0 / l
    return [e * inv for e in es]


def _lane_lo(n=128):
    return lax.broadcasted_iota(jnp.int32, (1, n), 1) < HD


def _stack_halves(q):
    lo = _lane_lo()
    return jnp.concatenate([jnp.where(lo, q, 0.0), jnp.where(lo, 0.0, q)], axis=0)


def _ada_kernel(cond_ref, w_ref, b_ref, o_ref):
    x = cond_ref[...]
    s = x * _sigmoid(x)
    o_ref[0] = _dot(s.astype(bf16), w_ref[0].astype(bf16)) + b_ref[0]


def _ada_table(cond8, w_ada, b_ada):
    depth = w_ada.shape[0]
    tn = 1536
    out = pl.pallas_call(
        _ada_kernel,
        grid=(depth, 6 * D // tn),
        in_specs=[
            pl.BlockSpec((8, D), lambda l, n: (0, 0)),
            pl.BlockSpec((1, D, tn), lambda l, n: (l, 0, n)),
            pl.BlockSpec((1, 1, tn), lambda l, n: (l, 0, n)),
        ],
        out_specs=pl.BlockSpec((1, 8, tn), lambda l, n: (l, 0, n)),
        out_shape=jax.ShapeDtypeStruct((depth, 8, 6 * D), f32),
        compiler_params=pltpu.CompilerParams(vmem_limit_bytes=VMEM_LIMIT),
        name="ada_table",
    )(cond8, w_ada, b_ada.reshape(depth, 1, 6 * D))
    out = out.reshape(depth, 8, 6, D)
    return [[out[l, :, k, :].reshape(8, 1, D) for k in range(6)] for l in range(depth)]


def _qk_norm(y, gain, gmat):
    ss = _group_sum(y * y, gmat) * (1.0 / HD)
    return y * lax.rsqrt(ss + EPS) * gain


def _rope(y, cos, sin):
    even = (lax.broadcasted_iota(jnp.int32, (1, 128), 1) % 2) == 0
    outs = []
    for c in range(y.shape[1] // 128):
        yc = y[:, c * 128:(c + 1) * 128]
        swapped = jnp.where(even, pltpu.roll(yc, 127, 1), pltpu.roll(yc, 1, 1))
        outs.append(yc * cos + swapped * sin)
    return jnp.concatenate(outs, axis=1)


def _inproj_kernel(xp_ref, xs_ref, sh_ref, sc_ref, g_ref, w_ref, gain_ref, gmat_ref, cos_ref, sin_ref,
                   o_ref, h_scr, *, tm):
    i = pl.program_id(0)
    j = pl.program_id(1)

    for x_ref, active in ((xp_ref, i < NP // tm), (xs_ref, i >= NP // tm)):
        @pl.when(jnp.logical_and(j == 0, active))
        def _():
            h_scr[...] = _norm_mod(x_ref[...], g_ref[...], sh_ref[0], sc_ref[0]).astype(bf16)

    y = _dot(h_scr[...], w_ref[...])
    is_norm = jnp.logical_and(j != 2, j != 5)
    is_rope = jnp.logical_and(j < 2, i >= NP // tm)

    @pl.when(jnp.logical_not(is_norm))
    def _():
        o_ref[...] = y

    @pl.when(jnp.logical_and(is_norm, jnp.logical_not(is_rope)))
    def _():
        o_ref[...] = _qk_norm(y, gain_ref[0], gmat_ref[...])

    @pl.when(is_rope)
    def _():
        o_ref[...] = _rope(_qk_norm(y, gain_ref[0], gmat_ref[...]), cos_ref[...], sin_ref[...])


def _in_projection(xp, xs, sh, sc, g, w_bf, gains, gmat, cos_t, sin_t):
    tm = 1024
    n_prompt_tiles = NP // tm
    tiles_per_seq = SAMPLE_LEN // tm
    cidx = functools.partial(_cond_idx, tm=tm)
    rope_idx = lambda i, j: (jnp.maximum(i - n_prompt_tiles, 0) % tiles_per_seq, 0)
    return pl.pallas_call(
        functools.partial(_inproj_kernel, tm=tm),
        grid=(NT // tm, IN_COLS // SEG),
        in_specs=[
            pl.BlockSpec((tm, D), lambda i, j: (jnp.minimum(i, n_prompt_tiles - 1), 0)),
            pl.BlockSpec((tm, D), lambda i, j: (jnp.maximum(i - n_prompt_tiles, 0), 0)),
            pl.BlockSpec((1, 1, D), lambda i, j: (cidx(i), 0, 0)),
            pl.BlockSpec((1, 1, D), lambda i, j: (cidx(i), 0, 0)),
            pl.BlockSpec((1, D), lambda i, j: (0, 0)),
            pl.BlockSpec((D, SEG), lambda i, j: (0, j)),
            pl.BlockSpec((1, 1, SEG), lambda i, j: (j, 0, 0)),
            pl.BlockSpec((256, 256), lambda i, j: (0, 0)),
            pl.BlockSpec((tm, 128), rope_idx),
            pl.BlockSpec((tm, 128), rope_idx),
        ],
        out_specs=pl.BlockSpec((tm, SEG), lambda i, j: (i, j)),
        out_shape=jax.ShapeDtypeStruct((NT, IN_COLS), f32),
        scratch_shapes=[pltpu.VMEM((tm, D), bf16)],
        compiler_params=pltpu.CompilerParams(
            dimension_semantics=("parallel", "arbitrary"), vmem_limit_bytes=VMEM_LIMIT),
        name="in_projection",
    )(xp, xs, sh, sc, g, w_bf, gains, gmat, cos_t, sin_t)


def _lambda_value(lamp_ref, lam_init):
    lp = lamp_ref[...]
    e1 = jnp.exp(jnp.sum(lp[0:1] * lp[1:2], axis=-1, keepdims=True))
    e2 = jnp.exp(jnp.sum(lp[2:3] * lp[3:4], axis=-1, keepdims=True))
    return e1 - e2 + lam_init


def _sub_ln(o, subg, lam_init):
    ms = jnp.mean(o * o, axis=-1, keepdims=True)
    return o * lax.rsqrt(ms + EPS) * subg * (1.0 - lam_init)


def _prompt_attn_kernel(p_ref, lamp_ref, subg_ref, o_ref, *, lam_init):
    lam = _lambda_value(lamp_ref, lam_init)
    t = PROMPT_LEN
    lo = _lane_lo()
    scale = HD ** -0.5
    for h in range(DIFF_HEADS):
        q = p_ref[:, h * 128:(h + 1) * 128]
        k = p_ref[:, SEG + h * 128:SEG + (h + 1) * 128].astype(bf16)
        v = p_ref[:, 2 * SEG + h * 128:2 * SEG + (h + 1) * 128].astype(bf16)
        s = _dot_nt((_stack_halves(q) * scale).astype(bf16), k)
        (p,) = _softmax_parts([s])
        pd = p[:t] - lam * p[t:]
        o = _dot(pd.astype(bf16), v)
        o_ref[:, h * 128:(h + 1) * 128] = _sub_ln(o, subg_ref[...], lam_init).astype(bf16)
    for hp in range(NA_HEADS // 2):
        q = p_ref[:, 3 * SEG + hp * 128:3 * SEG + (hp + 1) * 128]
        k = p_ref[:, 4 * SEG + hp * 128:4 * SEG + (hp + 1) * 128].astype(bf16)
        v = p_ref[:, 5 * SEG + hp * 128:5 * SEG + (hp + 1) * 128].astype(bf16)
        s = _dot_nt((_stack_halves(q) * scale).astype(bf16), k)
        (p,) = _softmax_parts([s])
        o = _dot(p.astype(bf16), v)
        o_ref[:, SEG + hp * 128:SEG + (hp + 1) * 128] = jnp.where(lo, o[:t], o[t:]).astype(bf16)


def _prompt_attention(proj, lamp, subg, lam_init):
    return pl.pallas_call(
        functools.partial(_prompt_attn_kernel, lam_init=lam_init),
        grid=(N_PROMPT_SEQ,),
        in_specs=[
            pl.BlockSpec((PROMPT_LEN, IN_COLS), lambda b: (b, 0)),
            pl.BlockSpec((4, HD), lambda b: (0, 0)),
            pl.BlockSpec((1, 128), lambda b: (0, 0)),
        ],
        out_specs=pl.BlockSpec((PROMPT_LEN, D), lambda b: (b, 0)),
        out_shape=jax.ShapeDtypeStruct((NP, D), bf16),
        compiler_params=pltpu.CompilerParams(
            dimension_semantics=("parallel",), vmem_limit_bytes=VMEM_LIMIT),
        name="prompt_attention",
    )(proj, lamp, subg)


def _latent_diff_kernel(q_ref, kn_ref, vn_ref, kc_ref, vc_ref, lamp_ref, subg_ref, o_ref,
                        *, lam_init, tq):
    lam = _lambda_value(lamp_ref, lam_init)
    scale = HD ** -0.5
    qq = (_stack_halves(q_ref[...]) * scale).astype(bf16)
    s_c = _dot_nt(qq, kc_ref[0].astype(bf16))
    s_n = _dot_nt(qq, kn_ref[...].astype(bf16))
    p_c, p_n = _softmax_parts([s_c, s_n])
    pd_c = p_c[:tq] - lam * p_c[tq:]
    pd_n = p_n[:tq] - lam * p_n[tq:]
    o = _dot(pd_c.astype(bf16), vc_ref[0].astype(bf16)) + _dot(pd_n.astype(bf16), vn_ref[...].astype(bf16))
    o_ref[...] = _sub_ln(o, subg_ref[...], lam_init).astype(bf16)


def _latent_diff_attention(proj, cache_k, cache_v, lamp, subg, lam_init):
    tq = 256
    nqb = SAMPLE_LEN // tq
    q0 = NP // tq
    s0 = NP // SAMPLE_LEN
    return pl.pallas_call(
        functools.partial(_latent_diff_kernel, lam_init=lam_init, tq=tq),
        grid=(N_SAMPLE_SEQ, DIFF_HEADS, nqb),
        in_specs=[
            pl.BlockSpec((tq, 128), lambda b, h, q: (q0 + b * nqb + q, h)),
            pl.BlockSpec((SAMPLE_LEN, 128), lambda b, h, q: (s0 + b, 4 + h)),
            pl.BlockSpec((SAMPLE_LEN, 128), lambda b, h, q: (s0 + b, 8 + h)),
            pl.BlockSpec((1, PAST, 128), lambda b, h, q: (b, 0, h)),
            pl.BlockSpec((1, PAST, 128), lambda b, h, q: (b, 0, h)),
            pl.BlockSpec((4, HD), lambda b, h, q: (0, 0)),
            pl.BlockSpec((1, 128), lambda b, h, q: (0, 0)),
        ],
        out_specs=pl.BlockSpec((tq, 128), lambda b, h, q: (b * nqb + q, h)),
        out_shape=jax.ShapeDtypeStruct((NS, SEG), bf16),
        compiler_params=pltpu.CompilerParams(
            dimension_semantics=("parallel", "parallel", "arbitrary"), vmem_limit_bytes=VMEM_LIMIT),
        name="latent_diff_attention",
    )(proj, proj, proj, cache_k, cache_v, lamp, subg)


def _rpb_table_kernel(rpb_ref, o_ref):
    h = pl.program_id(0)
    wq = lax.broadcasted_iota(jnp.int32, (GRID_W, GRID_W), 0)
    wk = lax.broadcasted_iota(jnp.int32, (GRID_W, GRID_W), 1)
    col_start = jnp.clip(wq - NA_KW // 2, 0, GRID_W - NA_KW)
    col_in = jnp.logical_and(wk >= col_start, wk < col_start + NA_KW)
    col_off = jnp.clip(wk - wq, -(NA_KW - 1), NA_KW - 1) + (NA_KW - 1)
    n_dr = 2 * NA_KH - 1
    n_dc = 2 * NA_KW - 1
    for dr in range(n_dr):
        t = jnp.zeros((GRID_W, GRID_W), f32)
        for c in range(n_dc):
            t = jnp.where(col_off == c, rpb_ref[h * (n_dr * n_dc) + dr * n_dc + c], t)
        o_ref[0, dr] = jnp.where(col_in, t, NEG_BIG)


def _rpb_table(rpb):
    n_dr = 2 * NA_KH - 1
    tcol = pl.pallas_call(
        _rpb_table_kernel,
        grid=(NA_HEADS,),
        in_specs=[pl.BlockSpec(memory_space=pltpu.SMEM)],
        out_specs=pl.BlockSpec((1, n_dr, GRID_W, GRID_W), lambda h: (h, 0, 0, 0)),
        out_shape=jax.ShapeDtypeStruct((NA_HEADS, n_dr, GRID_W, GRID_W), f32),
        name="rpb_table",
    )(rpb.reshape(-1))
    return jnp.stack(
        [jnp.concatenate([tcol[:, j - s + NA_KH - 1] for j in range(NA_KH)], axis=-1) for s in range(NA_KH)],
        axis=1)


def _latent_na_kernel(q_ref, k_ref, v_ref, kc_ref, vc_ref, bias_ref, o_ref):
    scale = HD ** -0.5
    lo = _lane_lo()
    kc = kc_ref[0].astype(bf16)
    vc = vc_ref[0].astype(bf16)
    win = NA_KH * GRID_W

    nr = NA_ROWS_PER_STEP
    kcb = jnp.broadcast_to(kc[None], (nr,) + kc.shape)
    vcb = jnp.broadcast_to(vc[None], (nr,) + vc.shape)

    def rows(g, carry):
        qs, kws, vws, biases = [], [], [], []
        for t in range(nr):
            r = g * nr + t
            rs = jnp.clip(r - NA_KH // 2, 0, GRID_R - NA_KH)
            sidx = r - rs
            q = q_ref[pl.ds(pl.multiple_of(r * GRID_W, GRID_W), GRID_W), :]
            qs.append((_stack_halves(q) * scale).astype(bf16))
            k0 = pl.multiple_of(rs * GRID_W, GRID_W)
            kws.append(k_ref[pl.ds(k0, win), :].astype(bf16))
            vws.append(v_ref[pl.ds(k0, win), :].astype(bf16))
            biases.append(jnp.concatenate([bias_ref[0, sidx], bias_ref[1, sidx]], axis=0))
        qq = jnp.stack(qs)
        s_loc = _bdot_nt(qq, jnp.stack(kws)) + jnp.stack(biases)
        s_ctx = _bdot_nt(qq, kcb)
        p_loc, p_ctx = _softmax_parts([s_loc, s_ctx])
        o = _bdot(p_loc.astype(bf16), jnp.stack(vws)) + _bdot(p_ctx.astype(bf16), vcb)
        o = jnp.where(lo, o[:, :GRID_W], o[:, GRID_W:]).astype(bf16)
        row0 = pl.multiple_of(g * (nr * GRID_W), nr * GRID_W)
        o_ref[pl.ds(row0, nr * GRID_W), :] = o.reshape(nr * GRID_W, 128)
        return carry

    lax.fori_loop(0, GRID_R // nr, rows, 0)


def _latent_na_attention(proj, cache_k, cache_v, bias):
    s0 = NP // SAMPLE_LEN
    return pl.pallas_call(
        _latent_na_kernel,
        grid=(N_SAMPLE_SEQ, NA_HEADS // 2),
        in_specs=[
            pl.BlockSpec((SAMPLE_LEN, 128), lambda b, h: (s0 + b, 12 + h)),
            pl.BlockSpec((SAMPLE_LEN, 128), lambda b, h: (s0 + b, 16 + h)),
            pl.BlockSpec((SAMPLE_LEN, 128), lambda b, h: (s0 + b, 20 + h)),
            pl.BlockSpec((1, PAST, 128), lambda b, h: (b, 0, h)),
            pl.BlockSpec((1, PAST, 128), lambda b, h: (b, 0, h)),
            pl.BlockSpec((2, NA_KH, GRID_W, NA_KH * GRID_W), lambda b, h: (h, 0, 0, 0)),
        ],
        out_specs=pl.BlockSpec((SAMPLE_LEN, 128), lambda b, h: (b, h)),
        out_shape=jax.ShapeDtypeStruct((NS, SEG), bf16),
        compiler_params=pltpu.CompilerParams(
            dimension_semantics=("parallel", "parallel"), vmem_limit_bytes=VMEM_LIMIT),
        name="latent_na_attention",
    )(proj, proj, proj, cache_k, cache_v, bias)


def _out_proj_kernel(ap_ref, ad_ref, an_ref, w_ref, xp_ref, xs_ref, gt_ref, o_ref, *, tm):
    i = pl.program_id(0)

    @pl.when(i < NP // tm)
    def _():
        o_ref[...] = xp_ref[...] + gt_ref[0] * _dot(ap_ref[...], w_ref[...])

    @pl.when(i >= NP // tm)
    def _():
        mixed = _dot(ad_ref[...], w_ref[:SEG, :]) + _dot(an_ref[...], w_ref[SEG:, :])
        o_ref[...] = xs_ref[...] + gt_ref[0] * mixed


def _out_projection(a_prompt, a_diff, a_na, w_bf, xp, xs, gate):
    tm = 512
    n_p = NP // tm
    cidx = functools.partial(_cond_idx, tm=tm)
    prompt_rows = lambda i: (jnp.minimum(i, n_p - 1), 0)
    sample_rows = lambda i: (jnp.maximum(i - n_p, 0), 0)
    return pl.pallas_call(
        functools.partial(_out_proj_kernel, tm=tm),
        grid=(NT // tm,),
        in_specs=[
            pl.BlockSpec((tm, D), prompt_rows),
            pl.BlockSpec((tm, SEG), sample_rows),
            pl.BlockSpec((tm, SEG), sample_rows),
            pl.BlockSpec((D, D), lambda i: (0, 0)),
            pl.BlockSpec((tm, D), prompt_rows),
            pl.BlockSpec((tm, D), sample_rows),
            pl.BlockSpec((1, 1, D), lambda i: (cidx(i), 0, 0)),
        ],
        out_specs=pl.BlockSpec((tm, D), lambda i: (i, 0)),
        out_shape=jax.ShapeDtypeStruct((NT, D), f32),
        compiler_params=pltpu.CompilerParams(
            dimension_semantics=("arbitrary",), vmem_limit_bytes=VMEM_LIMIT),
        name="out_projection",
    )(a_prompt, a_diff, a_na, w_bf, xp, xs, gate)


def _rwkv_out_kernel(yfp_ref, ybp_ref, yfs_ref, ybs_ref, bonus_ref, gate_ref, lng_ref, lnb_ref,
                     gmat_ref, w_ref, x_ref, gt_ref, o_ref, *, tm):
    i = pl.program_id(0)

    def finish(y):
        gmat = gmat_ref[...]
        mu = _group_sum(y, gmat) * (1.0 / HD)
        yc = y - mu
        var = _group_sum(yc * yc, gmat) * (1.0 / HD)
        z = yc * lax.rsqrt(var + GN_EPS) * lng_ref[...] + lnb_ref[...] + bonus_ref[...]
        z = (z * gate_ref[...]).astype(bf16)
        o_ref[...] = x_ref[...] + gt_ref[0] * _dot(z, w_ref[...])

    @pl.when(i < NP // tm)
    def _():
        finish(yfp_ref[...] + ybp_ref[...])

    @pl.when(i >= NP // tm)
    def _():
        finish(yfs_ref[...] + ybs_ref[...])


def _rwkv_out_projection(yf_p, yb_p, yf_s, yb_s, bonus, gate_lora, ln_g, ln_b, gmat, w_bf, x, gate):
    tm = 512
    n_p = NP // tm
    cidx = functools.partial(_cond_idx, tm=tm)
    row = pl.BlockSpec((tm, D), lambda i: (i, 0))
    prompt_row = pl.BlockSpec((tm, D), lambda i: (jnp.minimum(i, n_p - 1), 0))
    sample_row = pl.BlockSpec((tm, D), lambda i: (jnp.maximum(i - n_p, 0), 0))
    vec = pl.BlockSpec((1, D), lambda i: (0, 0))
    return pl.pallas_call(
        functools.partial(_rwkv_out_kernel, tm=tm),
        grid=(NT // tm,),
        in_specs=[prompt_row, prompt_row, sample_row, sample_row, row, row, vec, vec,
                  pl.BlockSpec((256, 256), lambda i: (0, 0)),
                  pl.BlockSpec((D, D), lambda i: (0, 0)),
                  row,
                  pl.BlockSpec((1, 1, D), lambda i: (cidx(i), 0, 0))],
        out_specs=row,
        out_shape=jax.ShapeDtypeStruct((NT, D), f32),
        compiler_params=pltpu.CompilerParams(
            dimension_semantics=("arbitrary",), vmem_limit_bytes=VMEM_LIMIT),
        name="rwkv_out_projection",
    )(yf_p, yb_p, yf_s, yb_s, bonus, gate_lora, ln_g, ln_b, gmat, w_bf, x, gate)


def _swiglu_hidden(xb, w1, w3):
    a = _dot(xb, w1)
    return ((a * _sigmoid(a)) * _dot(xb, w3)).astype(bf16)


def _ffn_kernel(x_ref, sh_ref, sc_ref, gt_ref, g_ref, w1_ref, w3_ref, w2_ref, o_ref, h_scr, acc_scr):
    f = pl.program_id(1)

    @pl.when(f == 0)
    def _():
        h_scr[...] = _norm_mod(x_ref[...], g_ref[...], sh_ref[0], sc_ref[0]).astype(bf16)
        acc_scr[...] = jnp.zeros_like(acc_scr)

    acc_scr[...] += _dot(_swiglu_hidden(h_scr[...], w1_ref[...], w3_ref[...]), w2_ref[...])

    @pl.when(f == pl.num_programs(1) - 1)
    def _():
        o_ref[...] = x_ref[...] + gt_ref[0] * acc_scr[...]


def _dense_ffn(x, sh, sc, gt, g, w1, w3, w2):
    tm = 512
    tf = FFN_DIM // 2
    cidx = functools.partial(_cond_idx, tm=tm)
    mod = pl.BlockSpec((1, 1, D), lambda i, f: (cidx(i), 0, 0))
    return pl.pallas_call(
        _ffn_kernel,
        grid=(NT // tm, FFN_DIM // tf),
        in_specs=[
            pl.BlockSpec((tm, D), lambda i, f: (i, 0)),
            mod, mod, mod,
            pl.BlockSpec((1, D), lambda i, f: (0, 0)),
            pl.BlockSpec((D, tf), lambda i, f: (0, f)),
            pl.BlockSpec((D, tf), lambda i, f: (0, f)),
            pl.BlockSpec((tf, D), lambda i, f: (f, 0)),
        ],
        out_specs=pl.BlockSpec((tm, D), lambda i, f: (i, 0)),
        out_shape=jax.ShapeDtypeStruct((NT, D), f32),
        scratch_shapes=[pltpu.VMEM((tm, D), bf16), pltpu.VMEM((tm, D), f32)],
        compiler_params=pltpu.CompilerParams(
            dimension_semantics=("parallel", "arbitrary"), vmem_limit_bytes=VMEM_LIMIT),
        name="ffn_mixer",
    )(x, sh, sc, gt, g, w1, w3, w2)


MOE_TILE = 1024
MOE_MAIN = 288
MOE_EXTRA = 128
MOE_ROWS = MOE_MAIN + -(-(MOE_TILE - MOE_MAIN) // MOE_EXTRA) * MOE_EXTRA


def _router_kernel(x_ref, sh_ref, sc_ref, g_ref, wr_ref, br_ref, tri_ref,
                   h_out, gates_out, rank_out, cnt_out):
    h = _norm_mod(x_ref[...], g_ref[...], sh_ref[0], sc_ref[0])
    h_out[...] = h.astype(bf16)
    logits = jnp.dot(h, wr_ref[...], preferred_element_type=f32,
                     precision=lax.Precision.HIGHEST) + br_ref[...]
    lane = lax.broadcasted_iota(jnp.int32, logits.shape, 1)
    logits = jnp.where(lane < N_EXPERTS, logits, -jnp.inf)
    m1 = logits.max(axis=-1, keepdims=True)
    i1 = jnp.min(jnp.where(logits == m1, lane, 128), axis=-1, keepdims=True)
    rest = jnp.where(lane == i1, -jnp.inf, logits)
    m2 = rest.max(axis=-1, keepdims=True)
    i2 = jnp.min(jnp.where(rest == m2, lane, 128), axis=-1, keepdims=True)
    e2 = jnp.exp(m2 - m1)
    den = 1.0 / (1.0 + e2)
    gates_out[...] = jnp.where(lane == i1, den, jnp.where(lane == i2, e2 * den, 0.0))
    sel = jnp.logical_or(lane == i1, lane == i2)
    self32 = sel.astype(f32)
    before = _dot(tri_ref[...], self32.astype(bf16))
    rank_out[...] = jnp.where(sel, before, -1.0)
    cnt_out[0] = jnp.sum(self32, axis=0, keepdims=True).astype(jnp.int32)


def _moe_route(x, sh, sc, g, w_router, b_router):
    tm = MOE_TILE
    cidx = functools.partial(_cond_idx, tm=tm)
    t = jnp.arange(tm)
    tri = (t[:, None] > t[None, :]).astype(bf16)
    mod = pl.BlockSpec((1, 1, D), lambda i: (cidx(i), 0, 0))
    return pl.pallas_call(
        _router_kernel,
        grid=(NT // tm,),
        in_specs=[
            pl.BlockSpec((tm, D), lambda i: (i, 0)),
            mod, mod,
            pl.BlockSpec((1, D), lambda i: (0, 0)),
            pl.BlockSpec((D, 128), lambda i: (0, 0)),
            pl.BlockSpec((1, 128), lambda i: (0, 0)),
            pl.BlockSpec((tm, tm), lambda i: (0, 0)),
        ],
        out_specs=[
            pl.BlockSpec((tm, D), lambda i: (i, 0)),
            pl.BlockSpec((tm, 128), lambda i: (i, 0)),
            pl.BlockSpec((tm, 128), lambda i: (i, 0)),
            pl.BlockSpec((1, 1, 128), lambda i: (i, 0, 0)),
        ],
        out_shape=[
            jax.ShapeDtypeStruct((NT, D), bf16),
            jax.ShapeDtypeStruct((NT, 128), f32),
            jax.ShapeDtypeStruct((NT, 128), f32),
            jax.ShapeDtypeStruct((NT // tm, 1, 128), jnp.int32),
        ],
        compiler_params=pltpu.CompilerParams(
            dimension_semantics=("parallel",), vmem_limit_bytes=VMEM_LIMIT),
        name="moe_router",
    )(x, sh, sc, g, w_router, b_router, tri)


def _moe_kernel(cnt_ref, h_ref, rank_ref, gates_ref, x_ref, gt_ref, w1_ref, w3_ref, w2_ref,
                o_ref, xc_scr, acc_scr, rcol_scr, gcol_scr):
    i = pl.program_id(0)
    e = pl.program_id(1)
    f = pl.program_id(2)
    cnt = cnt_ref[i * N_EXPERTS + e]
    n_extra = jnp.maximum(cnt - MOE_MAIN + MOE_EXTRA - 1, 0) // MOE_EXTRA

    def for_each_block(fn):
        fn(0, MOE_MAIN)

        def body(b, carry):
            fn(pl.multiple_of(MOE_MAIN + b * MOE_EXTRA, 32), MOE_EXTRA)
            return carry

        lax.fori_loop(0, n_extra, body, 0)

    def one_hot_t(slot0, nrows):
        slot = lax.broadcasted_iota(jnp.int32, (MOE_TILE, nrows), 1) + slot0
        return (rcol_scr[...] == slot).astype(bf16)

    @pl.when(jnp.logical_and(e == 0, f == 0))
    def _():
        o_ref[...] = x_ref[...]

    @pl.when(f == 0)
    def _():
        lane = lax.broadcasted_iota(jnp.int32, (MOE_TILE, 128), 1)
        mine = lane == e
        rcol_scr[...] = jnp.sum(jnp.where(mine, rank_ref[...], 0.0), axis=1, keepdims=True).astype(jnp.int32)
        gcol_scr[...] = jnp.sum(jnp.where(mine, gates_ref[...], 0.0), axis=1, keepdims=True)

        def gather(slot0, nrows):
            xc_scr[pl.ds(slot0, nrows), :] = _dot_tn(one_hot_t(slot0, nrows), h_ref[...]).astype(bf16)

        for_each_block(gather)

    def expert(slot0, nrows):
        rows = pl.ds(slot0, nrows)
        part = _dot(_swiglu_hidden(xc_scr[rows, :], w1_ref[0], w3_ref[0]), w2_ref[0])

        @pl.when(f == 0)
        def _():
            acc_scr[rows, :] = part

        @pl.when(f != 0)
        def _():
            acc_scr[rows, :] += part

    for_each_block(expert)

    @pl.when(f == pl.num_programs(2) - 1)
    def _():
        def scatter(slot0, nrows):
            out = acc_scr[pl.ds(slot0, nrows), :].astype(bf16)
            for t0 in range(0, MOE_TILE, 256):
                rows = slice(t0, t0 + 256)
                slot = lax.broadcasted_iota(jnp.int32, (256, nrows), 1) + slot0
                pt = (rcol_scr[rows, :] == slot).astype(bf16)
                o_ref[rows, :] += (gcol_scr[rows, :] * gt_ref[0]) * _dot(pt, out)

        for_each_block(scatter)


def _moe_experts(cnt, h_bf, rank, gates, x, gt, w1, w3, w2):
    tm = MOE_TILE
    tf = EXPERT_DIM // 2
    cidx = functools.partial(_cond_idx, tm=tm)
    once = pl.Buffered(1)
    grid_spec = pltpu.PrefetchScalarGridSpec(
        num_scalar_prefetch=1,
        grid=(NT // tm, N_EXPERTS, EXPERT_DIM // tf),
        in_specs=[
            pl.BlockSpec((tm, D), lambda i, e, f, c: (i, 0), pipeline_mode=once),
            pl.BlockSpec((tm, 128), lambda i, e, f, c: (i, 0), pipeline_mode=once),
            pl.BlockSpec((tm, 128), lambda i, e, f, c: (i, 0), pipeline_mode=once),
            pl.BlockSpec((tm, D), lambda i, e, f, c: (i, 0), pipeline_mode=once),
            pl.BlockSpec((1, 1, D), lambda i, e, f, c: (cidx(i), 0, 0)),
            pl.BlockSpec((1, D, tf), lambda i, e, f, c: (e, 0, f)),
            pl.BlockSpec((1, D, tf), lambda i, e, f, c: (e, 0, f)),
            pl.BlockSpec((1, tf, D), lambda i, e, f, c: (e, f, 0)),
        ],
        out_specs=pl.BlockSpec((tm, D), lambda i, e, f, c: (i, 0)),
        scratch_shapes=[pltpu.VMEM((MOE_ROWS, D), bf16), pltpu.VMEM((MOE_ROWS, D), f32),
                        pltpu.VMEM((tm, 1), jnp.int32), pltpu.VMEM((tm, 1), f32)],
    )
    return pl.pallas_call(
        _moe_kernel,
        grid_spec=grid_spec,
        out_shape=jax.ShapeDtypeStruct((NT, D), f32),
        compiler_params=pltpu.CompilerParams(
            dimension_semantics=("parallel", "arbitrary", "arbitrary"), vmem_limit_bytes=VMEM_LIMIT),
        name="moe_experts",
    )(cnt, h_bf, rank, gates, x, gt, w1, w3, w2)


def _rwkv_proj_kernel(x_ref, xp_ref, xn_ref, sh_ref, sc_ref, g_ref, mu_ref,
                      wr_ref, wk_ref, wv_ref, g1_ref, g2_ref, w1_ref, w2_ref, a1_ref, a2_ref,
                      w0_ref, a0_ref, kk_ref, ka_ref, rk_ref, gmat_ref,
                      r_out, v_out, kkn_out, bonus_out, gate_out,
                      lwf_out, lwb_out, af_out, ab_out, kdf_out, kdb_out, *, tm):
    i = pl.program_id(0)
    g, sh, sc = g_ref[...], sh_ref[0], sc_ref[0]
    h = _norm_mod(x_ref[...], g, sh, sc)
    n_prompt_tiles = NP // tm
    tiles_per_seq = SAMPLE_LEN // tm
    pos = (i - n_prompt_tiles) % tiles_per_seq
    has_prev = jnp.logical_and(i >= n_prompt_tiles, pos != 0)
    has_next = jnp.logical_and(i >= n_prompt_tiles, pos != tiles_per_seq - 1)
    h_before = jnp.where(has_prev, _norm_mod(xp_ref[...], g, sh, sc)[7:8], 0.0)
    h_after = jnp.where(has_next, _norm_mod(xn_ref[...], g, sh, sc)[0:1], 0.0)
    rowi = lax.broadcasted_iota(jnp.int32, (tm, 1), 0)
    h_prev = jnp.where(rowi == 0, h_before, pltpu.roll(h, 1, 0))
    h_next = jnp.where(rowi == tm - 1, h_after, pltpu.roll(h, tm - 1, 0))
    xx = 0.5 * (h_prev + h_next) - h
    mix = lambda n: (h + xx * mu_ref[n:n + 1]).astype(bf16)

    r = _dot(mix(0), wr_ref[...])
    k = _dot(mix(2), wk_ref[...])
    v = _dot(mix(3), wv_ref[...])
    gate_out[...] = _dot(_sigmoid(_dot(mix(5), g1_ref[...])).astype(bf16), g2_ref[...])

    lo = _lane_lo()
    tw = jnp.tanh(_dot(mix(1), w1_ref[...]))
    ta = _dot(mix(4), a1_ref[...])
    gmat = gmat_ref[...]
    kk = k * kk_ref[...]
    kkn_out[...] = kk * lax.rsqrt(_group_sum(kk * kk, gmat) + 1e-12)
    r_out[...] = r
    v_out[...] = v
    kd_sum = jnp.zeros_like(k)
    for d, (lw_out, a_out, kd_out) in enumerate(((lwf_out, af_out, kdf_out), (lwb_out, ab_out, kdb_out))):
        keep = lo if d == 0 else jnp.logical_not(lo)
        zw = w0_ref[d:d + 1] + _dot(jnp.where(keep, tw, 0.0).astype(bf16), w2_ref[...])
        lw_out[...] = -math.exp(-0.5) * _sigmoid(zw)
        a = _sigmoid(a0_ref[d:d + 1] + _dot(jnp.where(keep, ta, 0.0).astype(bf16), a2_ref[...]))
        a_out[...] = a
        kd = k * (1.0 + (a - 1.0) * ka_ref[...])
        kd_out[...] = kd
        kd_sum = kd_sum + kd
    bonus_out[...] = _group_sum(r * kd_sum * rk_ref[...], gmat) * v


def _rwkv_projection(x, sh, sc, g, mu, wr, wk, wv, g1, g2, w1, w2, a1, a2, w0, a0, k_k, k_a, r_k, gmat):
    tm = 256
    cidx = functools.partial(_cond_idx, tm=tm)
    hb = tm // 8
    n8 = NT // 8
    full = lambda shape: pl.BlockSpec(shape, lambda i: tuple(0 for _ in shape))
    row = pl.BlockSpec((tm, D), lambda i: (i, 0))
    return pl.pallas_call(
        functools.partial(_rwkv_proj_kernel, tm=tm),
        grid=(NT // tm,),
        in_specs=[
            row,
            pl.BlockSpec((8, D), lambda i: (jnp.maximum(i * hb - 1, 0), 0)),
            pl.BlockSpec((8, D), lambda i: (jnp.minimum((i + 1) * hb, n8 - 1), 0)),
            pl.BlockSpec((1, 1, D), lambda i: (cidx(i), 0, 0)),
            pl.BlockSpec((1, 1, D), lambda i: (cidx(i), 0, 0)),
            full((1, D)), full((6, D)),
            full((D, D)), full((D, D)), full((D, D)),
            full((D, GATE_LORA_PAD)), full((GATE_LORA_PAD, D)),
            full((D, 2 * LORA)), full((2 * LORA, D)), full((D, 2 * LORA)), full((2 * LORA, D)),
            full((2, D)), full((2, D)), full((1, D)), full((1, D)), full((1, D)),
            full((256, 256)),
        ],
        out_specs=[row] * 11,
        out_shape=[jax.ShapeDtypeStruct((NT, D), f32)] * 11,
        compiler_params=pltpu.CompilerParams(
            dimension_semantics=("parallel",), vmem_limit_bytes=VMEM_LIMIT),
        name="rwkv_projection",
    )(x, x, x, sh, sc, g, mu, wr, wk, wv, g1, g2, w1, w2, a1, a2, w0, a0, k_k, k_a, r_k, gmat)


def _split3(x):
    x1 = x.astype(bf16)
    r1 = x - x1.astype(f32)
    x2 = r1.astype(bf16)
    x3 = (r1 - x2.astype(f32)).astype(bf16)
    return x1, x2, x3


def _scan_kernel(*refs, nc, has_init, emit_state):
    fwd_refs = refs[0:6]
    bwd_refs = refs[6:12]
    tri_ref, mask_ref = refs[12:14]
    pos = 14
    s0_ref = None
    if has_init:
        s0_ref = refs[pos]
        pos += 1
    yf_ref, yb_ref = refs[pos:pos + 2]
    pos += 2
    sout_ref = None
    if emit_state:
        sout_ref = refs[pos]
        pos += 1
    st_scr, cl_scr = refs[pos:pos + 2]

    s = pl.program_id(1)

    @pl.when(s == 0)
    def _():
        if has_init:
            st_scr[...] = s0_ref[0]
        else:
            st_scr[...] = jnp.zeros_like(st_scr)

    c = CHUNK
    for d, drefs in enumerate((fwd_refs, bwd_refs)):
        lw = drefs[3][...]
        tri = tri_ref[d]
        p1, p2, p3 = _split3(lw)
        cl_scr[d] = _dot(tri, p1) + _dot(tri, p2) + _dot(tri, p3)

    lane = lax.broadcasted_iota(jnp.int32, (1, 128), 1)
    m0 = (lane < HD).astype(f32)
    m1 = 1.0 - m0
    rid = lax.broadcasted_iota(jnp.int32, (128, 128), 0)
    cid = lax.broadcasted_iota(jnp.int32, (128, 128), 1)
    eye = (rid == cid).astype(f32)

    n_pairs = RW_HEADS // 2
    nb = 2 * n_pairs

    def per_head_rows(x):
        return jnp.concatenate([x * m0, x * m1], axis=0)

    ars_l, bk_l, bkh_l, v2_l, dec_l = [], [], [], [], []
    for d, drefs in enumerate((fwd_refs, bwd_refs)):
        r_ref, v_ref, kk_ref, lw_ref, a_ref, kd_ref = drefs
        end_row = c - 1 if d == 0 else 0
        for p in range(n_pairs):
            ln = slice(p * 128, (p + 1) * 128)
            cl = cl_scr[d, :, ln]
            kk = kk_ref[:, ln]
            tot = cl[end_row:end_row + 1]
            e_inv = jnp.exp(-cl)
            e_end = jnp.exp(tot - cl)
            kka = kk * a_ref[:, ln]
            kd = kd_ref[:, ln]
            at = per_head_rows(-kk * jnp.exp(cl - lw_ref[:, ln]))
            rt = per_head_rows(r_ref[:, ln] * jnp.exp(cl))
            ars_l.append(jnp.concatenate([at, rt], axis=0).astype(bf16))
            bk_l.append(jnp.concatenate([per_head_rows(kka * e_inv), per_head_rows(kd * e_inv)],
                                        axis=0).astype(bf16))
            bkh_l.append(jnp.concatenate([per_head_rows(kka * e_end), per_head_rows(kd * e_end)],
                                         axis=0).astype(bf16))
            v2_l.append(per_head_rows(v_ref[:, ln]).astype(bf16))
            dec_l.append(jnp.exp(tot))
    ar = jnp.stack(ars_l)
    bk = jnp.stack(bk_l)
    bkh = jnp.stack(bkh_l)
    v2 = jnp.stack(v2_l)
    dec = jnp.stack(dec_l)

    st = st_scr[...].reshape(nb, 128, 128)
    g2 = _bdot_nt(ar, bk).reshape(2, n_pairs, 256, 256) * mask_ref[...][:, None]
    g2 = g2.reshape(nb, 256, 256)
    ars = _bdot_nt(ar, st.astype(bf16))
    gv = _bdot(g2[:, :, 128:].astype(bf16), v2)
    l_bd = g2[:, :128, :128]
    l_bf = l_bd.astype(bf16)
    pk = _bdot(l_bf, l_bf)
    q = eye[None] + l_bd
    for it in range(5):
        pkb = pk.astype(bf16)
        if it < 4:
            res = _bdot(jnp.concatenate([q, pk], axis=1).astype(bf16), pkb)
            q = q + res[:, :128]
            pk = res[:, 128:]
        else:
            q = q + _bdot(q.astype(bf16), pkb)
    rhs = ars[:, :128] + gv[:, :128]
    u2b = _bdot(q.astype(bf16), rhs.astype(bf16)).astype(bf16)
    y2 = ars[:, 128:] + gv[:, 128:] + _bdot(g2[:, 128:, :128].astype(bf16), u2b)
    y = y2[:, :c] + y2[:, c:]
    for d, y_ref in enumerate((yf_ref, yb_ref)):
        for p in range(n_pairs):
            y_ref[:, p * 128:(p + 1) * 128] = y[d * n_pairs + p]
    uv = jnp.concatenate([u2b, v2], axis=1)
    st_scr[...] = (st * dec + _bdot_tn(uv, bkh)).reshape(2, n_pairs, 128, 128)

    if emit_state:
        @pl.when(s == nc - 1)
        def _():
            sout_ref[0] = st_scr[...]


def _rwkv_scan(streams_f, streams_b, tri, mask, s0_bd, *, n_seq, seq_len, row0, emit_state):
    nc = seq_len // CHUNK
    blk0 = row0 // CHUNK
    fwd_spec = pl.BlockSpec((CHUNK, D), lambda b, s: (blk0 + b * nc + s, 0))
    bwd_spec = pl.BlockSpec((CHUNK, D), lambda b, s: (blk0 + b * nc + nc - 1 - s, 0))
    in_specs = [fwd_spec] * 6 + [bwd_spec] * 6 + [
        pl.BlockSpec((2, CHUNK, CHUNK), lambda b, s: (0, 0, 0)),
        pl.BlockSpec((2, 256, 256), lambda b, s: (0, 0, 0)),
    ]
    args = list(streams_f) + list(streams_b) + [tri, mask]
    state_block = (1, 2, RW_HEADS // 2, 128, 128)
    if s0_bd is not None:
        in_specs.append(pl.BlockSpec(state_block, lambda b, s: (b, 0, 0, 0, 0)))
        args.append(s0_bd)
    out_specs = [pl.BlockSpec((CHUNK, D), lambda b, s: (b * nc + s, 0)),
                 pl.BlockSpec((CHUNK, D), lambda b, s: (b * nc + nc - 1 - s, 0))]
    out_shape = [jax.ShapeDtypeStruct((n_seq * seq_len, D), f32)] * 2
    if emit_state:
        out_specs.append(pl.BlockSpec(state_block, lambda b, s: (b, 0, 0, 0, 0)))
        out_shape.append(jax.ShapeDtypeStruct((n_seq,) + state_block[1:], f32))
    return pl.pallas_call(
        functools.partial(_scan_kernel, nc=nc, has_init=s0_bd is not None, emit_state=emit_state),
        grid=(n_seq, nc),
        in_specs=in_specs,
        out_specs=out_specs,
        out_shape=out_shape,
        scratch_shapes=[pltpu.VMEM(state_block[1:], f32), pltpu.VMEM((2, CHUNK, D), f32)],
        compiler_params=pltpu.CompilerParams(
            dimension_semantics=("parallel", "arbitrary"), vmem_limit_bytes=VMEM_LIMIT),
        name="rwkv_scan_prompt" if emit_state else "rwkv_scan_sample",
    )(*args)


def _scan_constants():
    t = jnp.arange(CHUNK)
    lower = (t[:, None] >= t[None, :])
    tri = jnp.stack([lower, lower.T]).astype(bf16)
    masks = []
    for d in range(2):
        strict = (t[:, None] > t[None, :]) if d == 0 else (t[:, None] < t[None, :])
        incl = lower if d == 0 else lower.T
        blocks = []
        for m in (strict, incl):
            bd = jnp.kron(jnp.eye(2, dtype=f32), m.astype(f32))
            blocks.append(jnp.concatenate([bd, bd], axis=1))
        masks.append(jnp.concatenate(blocks, axis=0))
    return tri, jnp.stack(masks)


def _state_to_blockdiag(s):
    n = s.shape[0]
    s = s.reshape(n, 2, RW_HEADS // 2, 2, HD, HD)
    z = jnp.zeros_like(s[:, :, :, 0])
    top = jnp.concatenate([s[:, :, :, 0], z], axis=-1)
    bot = jnp.concatenate([z, s[:, :, :, 1]], axis=-1)
    return jnp.concatenate([top, bot], axis=-2)


def _state_from_blockdiag(sb):
    n = sb.shape[0]
    a = sb[:, :, :, :HD, :HD]
    b = sb[:, :, :, HD:, HD:]
    return jnp.stack([a, b], axis=3).reshape(n, 2, RW_HEADS, HD, HD)


def _rope_tables():
    t = jnp.arange(SAMPLE_LEN)
    rows = (t // GRID_W).astype(f32)
    cols = (t % GRID_W).astype(f32)
    nf = HD // 4
    inv = 10000.0 ** (-jnp.arange(nf, dtype=f32) / nf)
    ang = jnp.concatenate([rows[:, None] * inv, cols[:, None] * inv], axis=-1)
    cos = jnp.repeat(jnp.cos(ang), 2, axis=-1)
    sin = jnp.repeat(jnp.sin(ang), 2, axis=-1) * jnp.tile(jnp.array([-1.0, 1.0], f32), HD // 2)
    return jnp.tile(cos, (1, 2)), jnp.tile(sin, (1, 2))


def kernel(x_prompt, x_sample, cache_diff_k, cache_diff_v, cache_na_k, cache_na_v, state_rwkv, c, c_ctx, w_ada, b_ada, g_mix, g_ffn, w_in, w_out, diff_q_g, diff_k_g, diff_lam_q1, diff_lam_k1, diff_lam_q2, diff_lam_k2, diff_subln_g, na_q_g, na_k_g, na_rpb, ffn_w1, ffn_w3, ffn_w2, rw_mu, rw_wr, rw_wk, rw_wv, rw_wo, rw_w0, rw_w1, rw_w2, rw_a0, rw_a1, rw_a2, rw_g1, rw_g2, rw_k_k, rw_k_a, rw_r_k, rw_ln_g, rw_ln_b, moe_router, moe_router_b, moe_w1, moe_w3, moe_w2):
    xp = x_prompt.reshape(NP, D)
    xs = x_sample.reshape(NS, D)
    cond8 = jnp.concatenate([c_ctx[None, :], c, jnp.zeros((3, D), f32)], axis=0)
    mod = _ada_table(cond8, w_ada, b_ada)
    gmat = jnp.kron(jnp.eye(4, dtype=f32), jnp.ones((HD, HD), f32)).astype(bf16)

    lam_init = 0.8 - 0.6 * math.exp(-0.3 * 0)
    ones_seg = jnp.ones((SEG,), f32)
    tile8 = lambda gvec: jnp.tile(gvec, SEG // HD)
    gains = jnp.stack([tile8(diff_q_g[0]), tile8(diff_k_g[0]), ones_seg,
                       tile8(na_q_g[0]), tile8(na_k_g[0]), ones_seg]).reshape(6, 1, SEG)
    cos_t, sin_t = _rope_tables()
    proj = _in_projection(xp, xs, mod[0][0], mod[0][1], g_mix[0][None, :], w_in[0].astype(bf16),
                          gains, gmat, cos_t, sin_t)
    lamp = jnp.stack([diff_lam_q1[0], diff_lam_k1[0], diff_lam_q2[0], diff_lam_k2[0]])
    subg = diff_subln_g[0][None, :]
    o_prompt = _prompt_attention(proj, lamp, subg, lam_init)
    o_diff = _latent_diff_attention(proj, cache_diff_k[:, 0].reshape(N_SAMPLE_SEQ, PAST, SEG),
                                    cache_diff_v[:, 0].reshape(N_SAMPLE_SEQ, PAST, SEG), lamp, subg, lam_init)
    o_na = _latent_na_attention(proj, cache_na_k[:, 0].reshape(N_SAMPLE_SEQ, PAST, SEG),
                                cache_na_v[:, 0].reshape(N_SAMPLE_SEQ, PAST, SEG), _rpb_table(na_rpb[0]))
    x = _out_projection(o_prompt, o_diff, o_na, w_out[0].astype(bf16), xp, xs, mod[0][2])
    x = _dense_ffn(x, mod[0][3], mod[0][4], mod[0][5], g_ffn[0][None, :],
                   ffn_w1[0].astype(bf16), ffn_w3[0].astype(bf16), ffn_w2[0].astype(bf16))

    pad_g = GATE_LORA_PAD - GATE_LORA
    g1 = jnp.pad(rw_g1[0], ((0, 0), (0, pad_g))).astype(bf16)
    g2 = jnp.pad(rw_g2[0], ((0, pad_g), (0, 0))).astype(bf16)
    w1cat = jnp.concatenate([rw_w1[0, 0], rw_w1[0, 1]], axis=1).astype(bf16)
    w2cat = jnp.concatenate([rw_w2[0, 0], rw_w2[0, 1]], axis=0).astype(bf16)
    a1cat = jnp.concatenate([rw_a1[0, 0], rw_a1[0, 1]], axis=1).astype(bf16)
    a2cat = jnp.concatenate([rw_a2[0, 0], rw_a2[0, 1]], axis=0).astype(bf16)
    (r, v, kkn, bonus, gate_lora, lwf, lwb, af, ab, kdf, kdb) = _rwkv_projection(
        x, mod[1][0], mod[1][1], g_mix[1][None, :], rw_mu[0],
        rw_wr[0].astype(bf16), rw_wk[0].astype(bf16), rw_wv[0].astype(bf16), g1, g2,
        w1cat, w2cat, a1cat, a2cat, rw_w0[0], rw_a0[0],
        rw_k_k[0][None, :], rw_k_a[0][None, :], rw_r_k[0].reshape(1, D), gmat)
    tri, mask = _scan_constants()
    streams_f = (r, v, kkn, lwf, af, kdf)
    streams_b = (r, v, kkn, lwb, ab, kdb)
    yf_p, yb_p, st_p = _rwkv_scan(streams_f, streams_b, tri, mask, None,
                                  n_seq=N_PROMPT_SEQ, seq_len=PROMPT_LEN, row0=0, emit_state=True)
    yf_s, yb_s = _rwkv_scan(streams_f, streams_b, tri, mask, _state_to_blockdiag(state_rwkv[:, 0]),
                            n_seq=N_SAMPLE_SEQ, seq_len=SAMPLE_LEN, row0=NP, emit_state=False)
    x = _rwkv_out_projection(yf_p, yb_p, yf_s, yb_s, bonus, gate_lora, rw_ln_g[0][None, :], rw_ln_b[0][None, :], gmat,
                             rw_wo[0].astype(bf16), x, mod[1][2])
    w_router = jnp.pad(moe_router[0], ((0, 0), (0, 128 - N_EXPERTS)))
    b_router = jnp.pad(moe_router_b[0], (0, 128 - N_EXPERTS))[None, :]
    h_bf, gates, rank, cnt = _moe_route(x, mod[1][3], mod[1][4], g_ffn[1][None, :], w_router, b_router)
    x = _moe_experts(cnt[:, 0, :N_EXPERTS].reshape(-1), h_bf, rank, gates, x, mod[1][5],
                     moe_w1[0].astype(bf16), moe_w3[0].astype(bf16), moe_w2[0].astype(bf16))

    new_dk = proj[:NP, SEG:2 * SEG].reshape(N_PROMPT_SEQ, 1, PROMPT_LEN, DIFF_HEADS, 2 * HD)
    new_dv = proj[:NP, 2 * SEG:3 * SEG].reshape(N_PROMPT_SEQ, 1, PROMPT_LEN, DIFF_HEADS, 2 * HD)
    new_nk = proj[:NP, 4 * SEG:5 * SEG].reshape(N_PROMPT_SEQ, 1, PROMPT_LEN, NA_HEADS, HD)
    new_nv = proj[:NP, 5 * SEG:6 * SEG].reshape(N_PROMPT_SEQ, 1, PROMPT_LEN, NA_HEADS, HD)
    new_state = _state_from_blockdiag(st_p)[:, None]
    return (x[:NP].reshape(N_PROMPT_SEQ, PROMPT_LEN, D), x[NP:].reshape(N_SAMPLE_SEQ, SAMPLE_LEN, D),
            new_dk, new_dv, new_nk, new_nv, new_state)
```

```python
import functools
import math

import jax
import jax.numpy as jnp
from jax import lax
from jax.experimental import pallas as pl
from jax.experimental.pallas import tpu as pltpu

f32 = jnp.float32
bf16 = jnp.bfloat16

D = 1024
N_PROMPT_SEQ, PROMPT_LEN = 32, 256
N_SAMPLE_SEQ, SAMPLE_LEN = 4, 2048
NP = N_PROMPT_SEQ * PROMPT_LEN
NS = N_SAMPLE_SEQ * SAMPLE_LEN
NT = NP + NS
PAST = 256
GRID_W = 64
GRID_R = SAMPLE_LEN // GRID_W
HD = 64
DIFF_HEADS = 4
NA_HEADS = 8
NA_KH = 8
NA_KW = 16
SEG = 512
IN_COLS = 6 * SEG
FFN_DIM = 2816
N_EXPERTS = 8
EXPERT_DIM = 3584
RW_HEADS = 16
LORA = 64
GATE_LORA = 160
GATE_LORA_PAD = 256
EPS = 1e-6
GN_EPS = 64e-5
NEG_BIG = -1e30
DIFF_HEADS_PER_STEP = 2
NA_ROWS_PER_STEP = 8
CHUNK = 64
VMEM_LIMIT = 56 * 1024 * 1024


def _cond_idx(i, tm):
    return jnp.maximum((i * tm) // SAMPLE_LEN - (NP // SAMPLE_LEN - 1), 0)


def _dot(a, b):
    return jnp.dot(a, b, preferred_element_type=f32)


def _dot_nt(a, b):
    return lax.dot_general(a, b, (((1,), (1,)), ((), ())), preferred_element_type=f32)


def _dot_tn(a, b):
    return lax.dot_general(a, b, (((0,), (0,)), ((), ())), preferred_element_type=f32)


def _bdot(a, b):
    return lax.dot_general(a, b, (((2,), (1,)), ((0,), (0,))), preferred_element_type=f32)


def _bdot_nt(a, b):
    return lax.dot_general(a, b, (((2,), (2,)), ((0,), (0,))), preferred_element_type=f32)


def _bdot_tn(a, b):
    return lax.dot_general(a, b, (((1,), (1,)), ((0,), (0,))), preferred_element_type=f32)


def _sigmoid(x):
    return 1.0 / (1.0 + jnp.exp(-x))


def _norm_mod(x, g, sh, sc):
    ms = jnp.mean(x * x, axis=-1, keepdims=True)
    return (x * lax.rsqrt(ms + EPS) * g) * (1.0 + sc) + sh


def _group_sum(x, gmat):
    xb = x.astype(bf16)
    cols = [_dot(xb[:, c * 256:(c + 1) * 256], gmat) for c in range(x.shape[1] // 256)]
    return cols[0] if len(cols) == 1 else jnp.concatenate(cols, axis=1)


def _softmax_parts(parts):
    m = parts[0].max(axis=-1, keepdims=True)
    for p in parts[1:]:
        m = jnp.maximum(m, p.max(axis=-1, keepdims=True))
    es = [jnp.exp(p - m) for p in parts]
    l = es[0].sum(axis=-1, keepdims=True)
    for e in es[1:]:
        l = l + e.sum(axis=-1, keepdims=True)
    inv = 1.0 / l
    return [e * inv for e in es]


def _lane_lo(n=128):
    return lax.broadcasted_iota(jnp.int32, (1, n), 1) < HD


def _stack_halves(q):
    lo = _lane_lo()
    return jnp.concatenate([jnp.where(lo, q, 0.0), jnp.where(lo, 0.0, q)], axis=0)


def _ada_kernel(cond_ref, w_ref, b_ref, o_ref):
    x = cond_ref[...]
    s = x * _sigmoid(x)
    o_ref[0] = _dot(s.astype(bf16), w_ref[0].astype(bf16)) + b_ref[0]


def _ada_table(cond8, w_ada, b_ada):
    depth = w_ada.shape[0]
    tn = 1536
    out = pl.pallas_call(
        _ada_kernel,
        grid=(depth, 6 * D // tn),
        in_specs=[
            pl.BlockSpec((8, D), lambda l, n: (0, 0)),
            pl.BlockSpec((1, D, tn), lambda l, n: (l, 0, n)),
            pl.BlockSpec((1, 1, tn), lambda l, n: (l, 0, n)),
        ],
        out_specs=pl.BlockSpec((1, 8, tn), lambda l, n: (l, 0, n)),
        out_shape=jax.ShapeDtypeStruct((depth, 8, 6 * D), f32),
        compiler_params=pltpu.CompilerParams(vmem_limit_bytes=VMEM_LIMIT),
        name="ada_table",
    )(cond8, w_ada, b_ada.reshape(depth, 1, 6 * D))
    out = out.reshape(depth, 8, 6, D)
    return [[out[l, :, k, :].reshape(8, 1, D) for k in range(6)] for l in range(depth)]


def _qk_norm(y, gain, gmat):
    ss = _group_sum(y * y, gmat) * (1.0 / HD)
    return y * lax.rsqrt(ss + EPS) * gain


def _rope(y, cos, sin):
    even = (lax.broadcasted_iota(jnp.int32, (1, 128), 1) % 2) == 0
    outs = []
    for c in range(y.shape[1] // 128):
        yc = y[:, c * 128:(c + 1) * 128]
        swapped = jnp.where(even, pltpu.roll(yc, 127, 1), pltpu.roll(yc, 1, 1))
        outs.append(yc * cos + swapped * sin)
    return jnp.concatenate(outs, axis=1)


def _inproj_kernel(xp_ref, xs_ref, sh_ref, sc_ref, g_ref, w_ref, gain_ref, gmat_ref, cos_ref, sin_ref,
                   o_ref, dk_ref, dv_ref, nk_ref, nv_ref, h_scr, *, tm):
    i = pl.program_id(0)
    j = pl.program_id(1)

    def emit(val):
        o_ref[...] = val
        for seg, cache_ref in ((1, dk_ref), (2, dv_ref), (4, nk_ref), (5, nv_ref)):
            @pl.when(jnp.logical_and(j == seg, i < NP // tm))
            def _():
                cache_ref[...] = val

    for x_ref, active in ((xp_ref, i < NP // tm), (xs_ref, i >= NP // tm)):
        @pl.when(jnp.logical_and(j == 0, active))
        def _():
            h_scr[...] = _norm_mod(x_ref[...], g_ref[...], sh_ref[0], sc_ref[0]).astype(bf16)

    y = _dot(h_scr[...], w_ref[...])
    is_norm = jnp.logical_and(j != 2, j != 5)
    is_rope = jnp.logical_and(j < 2, i >= NP // tm)

    @pl.when(jnp.logical_not(is_norm))
    def _():
        emit(y)

    @pl.when(jnp.logical_and(is_norm, jnp.logical_not(is_rope)))
    def _():
        emit(_qk_norm(y, gain_ref[0], gmat_ref[...]))

    @pl.when(is_rope)
    def _():
        o_ref[...] = _rope(_qk_norm(y, gain_ref[0], gmat_ref[...]), cos_ref[...], sin_ref[...])


def _in_projection(xp, xs, sh, sc, g, w_bf, gains, gmat, cos_t, sin_t):
    tm = 1024
    n_prompt_tiles = NP // tm
    tiles_per_seq = SAMPLE_LEN // tm
    cidx = functools.partial(_cond_idx, tm=tm)
    rope_idx = lambda i, j: (jnp.maximum(i - n_prompt_tiles, 0) % tiles_per_seq, 0)
    return pl.pallas_call(
        functools.partial(_inproj_kernel, tm=tm),
        grid=(NT // tm, IN_COLS // SEG),
        in_specs=[
            pl.BlockSpec((tm, D), lambda i, j: (jnp.minimum(i, n_prompt_tiles - 1), 0)),
            pl.BlockSpec((tm, D), lambda i, j: (jnp.maximum(i - n_prompt_tiles, 0), 0)),
            pl.BlockSpec((1, 1, D), lambda i, j: (cidx(i), 0, 0)),
            pl.BlockSpec((1, 1, D), lambda i, j: (cidx(i), 0, 0)),
            pl.BlockSpec((1, D), lambda i, j: (0, 0)),
            pl.BlockSpec((D, SEG), lambda i, j: (0, j)),
            pl.BlockSpec((1, 1, SEG), lambda i, j: (j, 0, 0)),
            pl.BlockSpec((256, 256), lambda i, j: (0, 0)),
            pl.BlockSpec((tm, 128), rope_idx),
            pl.BlockSpec((tm, 128), rope_idx),
        ],
        out_specs=[pl.BlockSpec((tm, SEG), lambda i, j: (i, j))] + [
            pl.BlockSpec((tm, SEG), lambda i, j: (jnp.minimum(i, n_prompt_tiles - 1), 0))] * 4,
        out_shape=[jax.ShapeDtypeStruct((NT, IN_COLS), f32)] + [jax.ShapeDtypeStruct((NP, SEG), f32)] * 4,
        scratch_shapes=[pltpu.VMEM((tm, D), bf16)],
        compiler_params=pltpu.CompilerParams(
            dimension_semantics=("arbitrary", "arbitrary"), vmem_limit_bytes=VMEM_LIMIT),
        name="in_projection",
    )(xp, xs, sh, sc, g, w_bf, gains, gmat, cos_t, sin_t)


def _lambda_value(lamp_ref, lam_init):
    lp = lamp_ref[...]
    e1 = jnp.exp(jnp.sum(lp[0:1] * lp[1:2], axis=-1, keepdims=True))
    e2 = jnp.exp(jnp.sum(lp[2:3] * lp[3:4], axis=-1, keepdims=True))
    return e1 - e2 + lam_init


def _sub_ln(o, subg, lam_init):
    ms = jnp.mean(o * o, axis=-1, keepdims=True)
    return o * lax.rsqrt(ms + EPS) * subg * (1.0 - lam_init)


def _prompt_attn_kernel(p_ref, lamp_ref, subg_ref, o_ref, *, lam_init):
    lam = _lambda_value(lamp_ref, lam_init)
    t = PROMPT_LEN
    lo = _lane_lo()
    scale = HD ** -0.5
    for h in range(DIFF_HEADS):
        q = p_ref[:, h * 128:(h + 1) * 128]
        k = p_ref[:, SEG + h * 128:SEG + (h + 1) * 128].astype(bf16)
        v = p_ref[:, 2 * SEG + h * 128:2 * SEG + (h + 1) * 128].astype(bf16)
        s = _dot_nt((_stack_halves(q) * scale).astype(bf16), k)
        (p,) = _softmax_parts([s])
        pd = p[:t] - lam * p[t:]
        o = _dot(pd.astype(bf16), v)
        o_ref[:, h * 128:(h + 1) * 128] = _sub_ln(o, subg_ref[...], lam_init).astype(bf16)
    for hp in range(NA_HEADS // 2):
        q = p_ref[:, 3 * SEG + hp * 128:3 * SEG + (hp + 1) * 128]
        k = p_ref[:, 4 * SEG + hp * 128:4 * SEG + (hp + 1) * 128].astype(bf16)
        v = p_ref[:, 5 * SEG + hp * 128:5 * SEG + (hp + 1) * 128].astype(bf16)
        s = _dot_nt((_stack_halves(q) * scale).astype(bf16), k)
        (p,) = _softmax_parts([s])
        o = _dot(p.astype(bf16), v)
        o_ref[:, SEG + hp * 128:SEG + (hp + 1) * 128] = jnp.where(lo, o[:t], o[t:]).astype(bf16)


def _prompt_attention(proj, lamp, subg, lam_init):
    return pl.pallas_call(
        functools.partial(_prompt_attn_kernel, lam_init=lam_init),
        grid=(N_PROMPT_SEQ,),
        in_specs=[
            pl.BlockSpec((PROMPT_LEN, IN_COLS), lambda b: (b, 0)),
            pl.BlockSpec((4, HD), lambda b: (0, 0)),
            pl.BlockSpec((1, 128), lambda b: (0, 0)),
        ],
        out_specs=pl.BlockSpec((PROMPT_LEN, D), lambda b: (b, 0)),
        out_shape=jax.ShapeDtypeStruct((NP, D), bf16),
        compiler_params=pltpu.CompilerParams(
            dimension_semantics=("parallel",), vmem_limit_bytes=VMEM_LIMIT),
        name="prompt_attention",
    )(proj, lamp, subg)


def _latent_diff_kernel(q_ref, kn_ref, vn_ref, kc_ref, vc_ref, lamp_ref, subg_ref, o_ref,
                        *, lam_init, tq):
    lam = _lambda_value(lamp_ref, lam_init)
    scale = HD ** -0.5
    for hh in range(DIFF_HEADS_PER_STEP):
        ln = slice(hh * 128, (hh + 1) * 128)
        qq = (_stack_halves(q_ref[:, ln]) * scale).astype(bf16)
        s_c = _dot_nt(qq, kc_ref[0, :, ln].astype(bf16))
        s_n = _dot_nt(qq, kn_ref[:, ln].astype(bf16))
        p_c, p_n = _softmax_parts([s_c, s_n])
        pd_c = p_c[:tq] - lam * p_c[tq:]
        pd_n = p_n[:tq] - lam * p_n[tq:]
        o = (_dot(pd_c.astype(bf16), vc_ref[0, :, ln].astype(bf16))
             + _dot(pd_n.astype(bf16), vn_ref[:, ln].astype(bf16)))
        o_ref[:, ln] = _sub_ln(o, subg_ref[...], lam_init).astype(bf16)


def _latent_diff_attention(proj, cache_k, cache_v, lamp, subg, lam_init):
    tq = 256
    nqb = SAMPLE_LEN // tq
    q0 = NP // tq
    s0 = NP // SAMPLE_LEN
    hw = 128 * DIFF_HEADS_PER_STEP
    kcol = SEG // hw
    return pl.pallas_call(
        functools.partial(_latent_diff_kernel, lam_init=lam_init, tq=tq),
        grid=(N_SAMPLE_SEQ, DIFF_HEADS // DIFF_HEADS_PER_STEP, nqb),
        in_specs=[
            pl.BlockSpec((tq, hw), lambda b, h, q: (q0 + b * nqb + q, h)),
            pl.BlockSpec((SAMPLE_LEN, hw), lambda b, h, q: (s0 + b, kcol + h)),
            pl.BlockSpec((SAMPLE_LEN, hw), lambda b, h, q: (s0 + b, 2 * kcol + h)),
            pl.BlockSpec((1, PAST, hw), lambda b, h, q: (b, 0, h)),
            pl.BlockSpec((1, PAST, hw), lambda b, h, q: (b, 0, h)),
            pl.BlockSpec((4, HD), lambda b, h, q: (0, 0)),
            pl.BlockSpec((1, 128), lambda b, h, q: (0, 0)),
        ],
        out_specs=pl.BlockSpec((tq, hw), lambda b, h, q: (b * nqb + q, h)),
        out_shape=jax.ShapeDtypeStruct((NS, SEG), bf16),
        compiler_params=pltpu.CompilerParams(
            dimension_semantics=("parallel", "parallel", "arbitrary"), vmem_limit_bytes=VMEM_LIMIT),
        name="latent_diff_attention",
    )(proj, proj, proj, cache_k, cache_v, lamp, subg)


def _rpb_table_kernel(rpb_ref, o_ref):
    h = pl.program_id(0)
    wq = lax.broadcasted_iota(jnp.int32, (GRID_W, GRID_W), 0)
    wk = lax.broadcasted_iota(jnp.int32, (GRID_W, GRID_W), 1)
    col_start = jnp.clip(wq - NA_KW // 2, 0, GRID_W - NA_KW)
    col_in = jnp.logical_and(wk >= col_start, wk < col_start + NA_KW)
    col_off = jnp.clip(wk - wq, -(NA_KW - 1), NA_KW - 1) + (NA_KW - 1)
    n_dr = 2 * NA_KH - 1
    n_dc = 2 * NA_KW - 1
    for dr in range(n_dr):
        t = jnp.zeros((GRID_W, GRID_W), f32)
        for c in range(n_dc):
            t = jnp.where(col_off == c, rpb_ref[h * (n_dr * n_dc) + dr * n_dc + c], t)
        o_ref[0, dr] = jnp.where(col_in, t, NEG_BIG)


def _rpb_table(rpb):
    n_dr = 2 * NA_KH - 1
    tcol = pl.pallas_call(
        _rpb_table_kernel,
        grid=(NA_HEADS,),
        in_specs=[pl.BlockSpec(memory_space=pltpu.SMEM)],
        out_specs=pl.BlockSpec((1, n_dr, GRID_W, GRID_W), lambda h: (h, 0, 0, 0)),
        out_shape=jax.ShapeDtypeStruct((NA_HEADS, n_dr, GRID_W, GRID_W), f32),
        name="rpb_table",
    )(rpb.reshape(-1))
    return jnp.stack(
        [jnp.concatenate([tcol[:, j - s + NA_KH - 1] for j in range(NA_KH)], axis=-1) for s in range(NA_KH)],
        axis=1)


def _latent_na_kernel(q_ref, k_ref, v_ref, kc_ref, vc_ref, bias_ref, o_ref):
    scale = HD ** -0.5
    lo = _lane_lo()
    kc = kc_ref[0].astype(bf16)
    vc = vc_ref[0].astype(bf16)
    win = NA_KH * GRID_W

    nr = NA_ROWS_PER_STEP
    kcb = jnp.broadcast_to(kc[None], (nr,) + kc.shape)
    vcb = jnp.broadcast_to(vc[None], (nr,) + vc.shape)

    def rows(g, carry):
        qs, kws, vws, biases = [], [], [], []
        for t in range(nr):
            r = g * nr + t
            rs = jnp.clip(r - NA_KH // 2, 0, GRID_R - NA_KH)
            sidx = r - rs
            q = q_ref[pl.ds(pl.multiple_of(r * GRID_W, GRID_W), GRID_W), :]
            qs.append((_stack_halves(q) * scale).astype(bf16))
            k0 = pl.multiple_of(rs * GRID_W, GRID_W)
            kws.append(k_ref[pl.ds(k0, win), :].astype(bf16))
            vws.append(v_ref[pl.ds(k0, win), :].astype(bf16))
            biases.append(jnp.concatenate([bias_ref[0, sidx], bias_ref[1, sidx]], axis=0))
        qq = jnp.stack(qs)
        s_loc = _bdot_nt(qq, jnp.stack(kws)) + jnp.stack(biases)
        s_ctx = _bdot_nt(qq, kcb)
        p_loc, p_ctx = _softmax_parts([s_loc, s_ctx])
        o = _bdot(p_loc.astype(bf16), jnp.stack(vws)) + _bdot(p_ctx.astype(bf16), vcb)
        o = jnp.where(lo, o[:, :GRID_W], o[:, GRID_W:]).astype(bf16)
        row0 = pl.multiple_of(g * (nr * GRID_W), nr * GRID_W)
        o_ref[pl.ds(row0, nr * GRID_W), :] = o.reshape(nr * GRID_W, 128)
        return carry

    lax.fori_loop(0, GRID_R // nr, rows, 0)


def _latent_na_attention(proj, cache_k, cache_v, bias):
    s0 = NP // SAMPLE_LEN
    return pl.pallas_call(
        _latent_na_kernel,
        grid=(N_SAMPLE_SEQ, NA_HEADS // 2),
        in_specs=[
            pl.BlockSpec((SAMPLE_LEN, 128), lambda b, h: (s0 + b, 12 + h)),
            pl.BlockSpec((SAMPLE_LEN, 128), lambda b, h: (s0 + b, 16 + h)),
            pl.BlockSpec((SAMPLE_LEN, 128), lambda b, h: (s0 + b, 20 + h)),
            pl.BlockSpec((1, PAST, 128), lambda b, h: (b, 0, h)),
            pl.BlockSpec((1, PAST, 128), lambda b, h: (b, 0, h)),
            pl.BlockSpec((2, NA_KH, GRID_W, NA_KH * GRID_W), lambda b, h: (h, 0, 0, 0)),
        ],
        out_specs=pl.BlockSpec((SAMPLE_LEN, 128), lambda b, h: (b, h)),
        out_shape=jax.ShapeDtypeStruct((NS, SEG), bf16),
        compiler_params=pltpu.CompilerParams(
            dimension_semantics=("parallel", "parallel"), vmem_limit_bytes=VMEM_LIMIT),
        name="latent_na_attention",
    )(proj, proj, proj, cache_k, cache_v, bias)


def _out_proj_kernel(ap_ref, ad_ref, an_ref, w_ref, xp_ref, xs_ref, gt_ref, o_ref, *, tm):
    i = pl.program_id(0)

    @pl.when(i < NP // tm)
    def _():
        o_ref[...] = xp_ref[...] + gt_ref[0] * _dot(ap_ref[...], w_ref[...])

    @pl.when(i >= NP // tm)
    def _():
        mixed = _dot(ad_ref[...], w_ref[:SEG, :]) + _dot(an_ref[...], w_ref[SEG:, :])
        o_ref[...] = xs_ref[...] + gt_ref[0] * mixed


def _out_projection(a_prompt, a_diff, a_na, w_bf, xp, xs, gate):
    tm = 512
    n_p = NP // tm
    cidx = functools.partial(_cond_idx, tm=tm)
    prompt_rows = lambda i: (jnp.minimum(i, n_p - 1), 0)
    sample_rows = lambda i: (jnp.maximum(i - n_p, 0), 0)
    return pl.pallas_call(
        functools.partial(_out_proj_kernel, tm=tm),
        grid=(NT // tm,),
        in_specs=[
            pl.BlockSpec((tm, D), prompt_rows),
            pl.BlockSpec((tm, SEG), sample_rows),
            pl.BlockSpec((tm, SEG), sample_rows),
            pl.BlockSpec((D, D), lambda i: (0, 0)),
            pl.BlockSpec((tm, D), prompt_rows),
            pl.BlockSpec((tm, D), sample_rows),
            pl.BlockSpec((1, 1, D), lambda i: (cidx(i), 0, 0)),
        ],
        out_specs=pl.BlockSpec((tm, D), lambda i: (i, 0)),
        out_shape=jax.ShapeDtypeStruct((NT, D), f32),
        compiler_params=pltpu.CompilerParams(
            dimension_semantics=("arbitrary",), vmem_limit_bytes=VMEM_LIMIT),
        name="out_projection",
    )(a_prompt, a_diff, a_na, w_bf, xp, xs, gate)


def _rwkv_out_kernel(yfp_ref, ybp_ref, yfs_ref, ybs_ref, bonus_ref, gate_ref, lng_ref, lnb_ref,
                     gmat_ref, w_ref, x_ref, gt_ref, o_ref, *, tm):
    i = pl.program_id(0)

    def finish(y):
        gmat = gmat_ref[...]
        mu = _group_sum(y, gmat) * (1.0 / HD)
        yc = y - mu
        var = _group_sum(yc * yc, gmat) * (1.0 / HD)
        z = yc * lax.rsqrt(var + GN_EPS) * lng_ref[...] + lnb_ref[...] + bonus_ref[...]
        z = (z * gate_ref[...]).astype(bf16)
        o_ref[...] = x_ref[...] + gt_ref[0] * _dot(z, w_ref[...])

    @pl.when(i < NP // tm)
    def _():
        finish(yfp_ref[...] + ybp_ref[...])

    @pl.when(i >= NP // tm)
    def _():
        finish(yfs_ref[...] + ybs_ref[...])


def _rwkv_out_projection(yf_p, yb_p, yf_s, yb_s, bonus, gate_lora, ln_g, ln_b, gmat, w_bf, x, gate):
    tm = 512
    n_p = NP // tm
    cidx = functools.partial(_cond_idx, tm=tm)
    row = pl.BlockSpec((tm, D), lambda i: (i, 0))
    prompt_row = pl.BlockSpec((tm, D), lambda i: (jnp.minimum(i, n_p - 1), 0))
    sample_row = pl.BlockSpec((tm, D), lambda i: (jnp.maximum(i - n_p, 0), 0))
    vec = pl.BlockSpec((1, D), lambda i: (0, 0))
    return pl.pallas_call(
        functools.partial(_rwkv_out_kernel, tm=tm),
        grid=(NT // tm,),
        in_specs=[prompt_row, prompt_row, sample_row, sample_row, row, row, vec, vec,
                  pl.BlockSpec((256, 256), lambda i: (0, 0)),
                  pl.BlockSpec((D, D), lambda i: (0, 0)),
                  row,
                  pl.BlockSpec((1, 1, D), lambda i: (cidx(i), 0, 0))],
        out_specs=row,
        out_shape=jax.ShapeDtypeStruct((NT, D), f32),
        compiler_params=pltpu.CompilerParams(
            dimension_semantics=("arbitrary",), vmem_limit_bytes=VMEM_LIMIT),
        name="rwkv_out_projection",
    )(yf_p, yb_p, yf_s, yb_s, bonus, gate_lora, ln_g, ln_b, gmat, w_bf, x, gate)


def _swiglu_hidden(xb, w1, w3):
    a = _dot(xb, w1)
    return ((a * _sigmoid(a)) * _dot(xb, w3)).astype(bf16)


def _ffn_kernel(x_ref, sh_ref, sc_ref, gt_ref, g_ref, w1_ref, w3_ref, w2_ref, o_ref, h_scr, acc_scr):
    f = pl.program_id(1)

    @pl.when(f == 0)
    def _():
        h_scr[...] = _norm_mod(x_ref[...], g_ref[...], sh_ref[0], sc_ref[0]).astype(bf16)
        acc_scr[...] = jnp.zeros_like(acc_scr)

    acc_scr[...] += _dot(_swiglu_hidden(h_scr[...], w1_ref[...], w3_ref[...]), w2_ref[...])

    @pl.when(f == pl.num_programs(1) - 1)
    def _():
        o_ref[...] = x_ref[...] + gt_ref[0] * acc_scr[...]


def _dense_ffn(x, sh, sc, gt, g, w1, w3, w2):
    tm = 1024
    tf = FFN_DIM // 2
    cidx = functools.partial(_cond_idx, tm=tm)
    mod = pl.BlockSpec((1, 1, D), lambda i, f: (cidx(i), 0, 0))
    return pl.pallas_call(
        _ffn_kernel,
        grid=(NT // tm, FFN_DIM // tf),
        in_specs=[
            pl.BlockSpec((tm, D), lambda i, f: (i, 0), pipeline_mode=pl.Buffered(1)),
            mod, mod, mod,
            pl.BlockSpec((1, D), lambda i, f: (0, 0)),
            pl.BlockSpec((D, tf), lambda i, f: (0, f)),
            pl.BlockSpec((D, tf), lambda i, f: (0, f)),
            pl.BlockSpec((tf, D), lambda i, f: (f, 0)),
        ],
        out_specs=pl.BlockSpec((tm, D), lambda i, f: (i, 0)),
        out_shape=jax.ShapeDtypeStruct((NT, D), f32),
        scratch_shapes=[pltpu.VMEM((tm, D), bf16), pltpu.VMEM((tm, D), f32)],
        compiler_params=pltpu.CompilerParams(
            dimension_semantics=("parallel", "arbitrary"), vmem_limit_bytes=VMEM_LIMIT),
        name="ffn_mixer",
    )(x, sh, sc, gt, g, w1, w3, w2)


MOE_TILE = 1024
MOE_MAIN = 288
MOE_EXTRA = 128
MOE_ROWS = MOE_MAIN + -(-(MOE_TILE - MOE_MAIN) // MOE_EXTRA) * MOE_EXTRA


RANK_SEL = 4096


def _router_kernel(x_ref, sh_ref, sc_ref, g_ref, wr_ref, br_ref, tri_ref, triu_ref,
                   h_out, gates_out, rank_out, rank_t_out, cnt_out):
    h = _norm_mod(x_ref[...], g_ref[...], sh_ref[0], sc_ref[0])
    h_out[...] = h.astype(bf16)
    w = wr_ref[...]
    h_hi, w_hi = h.astype(bf16), w.astype(bf16)
    h_lo = (h - h_hi.astype(f32)).astype(bf16)
    w_lo = (w - w_hi.astype(f32)).astype(bf16)
    logits = _dot(h_hi, w_hi) + (_dot(h_hi, w_lo) + _dot(h_lo, w_hi)) + br_ref[...]
    lane = lax.broadcasted_iota(jnp.int32, logits.shape, 1)
    logits = jnp.where(lane < N_EXPERTS, logits, -jnp.inf)
    m1 = logits.max(axis=-1, keepdims=True)
    i1 = jnp.min(jnp.where(logits == m1, lane, 128), axis=-1, keepdims=True)
    rest = jnp.where(lane == i1, -jnp.inf, logits)
    m2 = rest.max(axis=-1, keepdims=True)
    i2 = jnp.min(jnp.where(rest == m2, lane, 128), axis=-1, keepdims=True)
    e2 = jnp.exp(m2 - m1)
    den = 1.0 / (1.0 + e2)
    gates_out[...] = jnp.where(lane == i1, den, jnp.where(lane == i2, e2 * den, 0.0))
    sel = jnp.logical_or(lane == i1, lane == i2)
    self32 = sel.astype(f32)
    selb = self32.astype(bf16)
    rank_out[...] = jnp.where(sel, _dot(tri_ref[...], selb), -1.0)
    rank_t_out[...] = _dot_tn(selb, triu_ref[...])
    cnt_out[0] = jnp.sum(self32, axis=0, keepdims=True).astype(jnp.int32)


def _moe_route(x, sh, sc, g, w_router, b_router):
    tm = MOE_TILE
    cidx = functools.partial(_cond_idx, tm=tm)
    t = jnp.arange(tm)
    tri = (t[:, None] > t[None, :]).astype(bf16)
    triu = jnp.where(t[:, None] == t[None, :], float(RANK_SEL), tri.T.astype(f32)).astype(bf16)
    mod = pl.BlockSpec((1, 1, D), lambda i: (cidx(i), 0, 0))
    return pl.pallas_call(
        _router_kernel,
        grid=(NT // tm,),
        in_specs=[
            pl.BlockSpec((tm, D), lambda i: (i, 0)),
            mod, mod,
            pl.BlockSpec((1, D), lambda i: (0, 0)),
            pl.BlockSpec((D, 128), lambda i: (0, 0)),
            pl.BlockSpec((1, 128), lambda i: (0, 0)),
            pl.BlockSpec((tm, tm), lambda i: (0, 0)),
            pl.BlockSpec((tm, tm), lambda i: (0, 0)),
        ],
        out_specs=[
            pl.BlockSpec((tm, D), lambda i: (i, 0)),
            pl.BlockSpec((tm, 128), lambda i: (i, 0)),
            pl.BlockSpec((tm, 128), lambda i: (i, 0)),
            pl.BlockSpec((128, tm), lambda i: (0, i)),
            pl.BlockSpec((1, 1, 128), lambda i: (i, 0, 0)),
        ],
        out_shape=[
            jax.ShapeDtypeStruct((NT, D), bf16),
            jax.ShapeDtypeStruct((NT, 128), f32),
            jax.ShapeDtypeStruct((NT, 128), f32),
            jax.ShapeDtypeStruct((128, NT), f32),
            jax.ShapeDtypeStruct((NT // tm, 1, 128), jnp.int32),
        ],
        compiler_params=pltpu.CompilerParams(
            dimension_semantics=("parallel",), vmem_limit_bytes=VMEM_LIMIT),
        name="moe_router",
    )(x, sh, sc, g, w_router, b_router, tri, triu)


def _moe_kernel(cnt_ref, h_ref, rank_ref, rank_t_ref, gates_ref, x_ref, gt_ref, w1_ref, w3_ref, w2_ref,
                o_ref, xc_scr, acc_scr, rcol_scr, gcol_scr):
    i = pl.program_id(0)
    e = pl.program_id(1)
    f = pl.program_id(2)
    cnt = cnt_ref[i * N_EXPERTS + e]
    n_extra = jnp.maximum(cnt - MOE_MAIN + MOE_EXTRA - 1, 0) // MOE_EXTRA

    def for_each_block(fn):
        fn(0, MOE_MAIN)

        def body(b, carry):
            fn(pl.multiple_of(MOE_MAIN + b * MOE_EXTRA, 32), MOE_EXTRA)
            return carry

        lax.fori_loop(0, n_extra, body, 0)

    @pl.when(jnp.logical_and(e == 0, f == 0))
    def _():
        o_ref[...] = x_ref[...]

    @pl.when(f == 0)
    def _():
        lane = lax.broadcasted_iota(jnp.int32, (MOE_TILE, 128), 1)
        mine = lane == e
        rcol_scr[...] = jnp.sum(jnp.where(mine, rank_ref[...], 0.0), axis=1, keepdims=True).astype(jnp.int32)
        gcol_scr[...] = jnp.sum(jnp.where(mine, gates_ref[...], 0.0), axis=1, keepdims=True)

        rrow = rank_t_ref[pl.ds(e, 1), :].astype(jnp.int32)

        def gather(slot0, nrows):
            slot = lax.broadcasted_iota(jnp.int32, (nrows, MOE_TILE), 0) + (slot0 + RANK_SEL)
            xc_scr[pl.ds(slot0, nrows), :] = _dot((rrow == slot).astype(bf16), h_ref[...]).astype(bf16)

        for_each_block(gather)

    def expert(slot0, nrows):
        rows = pl.ds(slot0, nrows)
        part = _dot(_swiglu_hidden(xc_scr[rows, :], w1_ref[0], w3_ref[0]), w2_ref[0])

        @pl.when(f == 0)
        def _():
            acc_scr[rows, :] = part

        @pl.when(f != 0)
        def _():
            acc_scr[rows, :] += part

    for_each_block(expert)

    @pl.when(f == pl.num_programs(2) - 1)
    def _():
        def scatter(slot0, nrows):
            out = acc_scr[pl.ds(slot0, nrows), :].astype(bf16)
            for t0 in range(0, MOE_TILE, 256):
                rows = slice(t0, t0 + 256)
                slot = lax.broadcasted_iota(jnp.int32, (256, nrows), 1) + slot0
                pt = (rcol_scr[rows, :] == slot).astype(bf16)
                o_ref[rows, :] += (gcol_scr[rows, :] * gt_ref[0]) * _dot(pt, out)

        for_each_block(scatter)


def _moe_experts(cnt, h_bf, rank, rank_t, gates, x, gt, w1, w3, w2):
    tm = MOE_TILE
    tf = EXPERT_DIM // 2
    cidx = functools.partial(_cond_idx, tm=tm)
    once = pl.Buffered(1)
    grid_spec = pltpu.PrefetchScalarGridSpec(
        num_scalar_prefetch=1,
        grid=(NT // tm, N_EXPERTS, EXPERT_DIM // tf),
        in_specs=[
            pl.BlockSpec((tm, D), lambda i, e, f, c: (i, 0), pipeline_mode=once),
            pl.BlockSpec((tm, 128), lambda i, e, f, c: (i, 0), pipeline_mode=once),
            pl.BlockSpec((128, tm), lambda i, e, f, c: (0, i), pipeline_mode=once),
            pl.BlockSpec((tm, 128), lambda i, e, f, c: (i, 0), pipeline_mode=once),
            pl.BlockSpec((tm, D), lambda i, e, f, c: (i, 0), pipeline_mode=once),
            pl.BlockSpec((1, 1, D), lambda i, e, f, c: (cidx(i), 0, 0)),
            pl.BlockSpec((1, D, tf), lambda i, e, f, c: (e, 0, f)),
            pl.BlockSpec((1, D, tf), lambda i, e, f, c: (e, 0, f)),
            pl.BlockSpec((1, tf, D), lambda i, e, f, c: (e, f, 0)),
        ],
        out_specs=pl.BlockSpec((tm, D), lambda i, e, f, c: (i, 0)),
        scratch_shapes=[pltpu.VMEM((MOE_ROWS, D), bf16), pltpu.VMEM((MOE_ROWS, D), f32),
                        pltpu.VMEM((tm, 1), jnp.int32), pltpu.VMEM((tm, 1), f32)],
    )
    return pl.pallas_call(
        _moe_kernel,
        grid_spec=grid_spec,
        out_shape=jax.ShapeDtypeStruct((NT, D), f32),
        compiler_params=pltpu.CompilerParams(
            dimension_semantics=("parallel", "arbitrary", "arbitrary"), vmem_limit_bytes=VMEM_LIMIT),
        name="moe_experts",
    )(cnt, h_bf, rank, rank_t, gates, x, gt, w1, w3, w2)


def _rwkv_proj_kernel(x_ref, xp_ref, xn_ref, sh_ref, sc_ref, g_ref, mu_ref,
                      wr_ref, wk_ref, wv_ref, g1_ref, g2_ref, w1_ref, w2_ref, a1_ref, a2_ref,
                      w0_ref, a0_ref, kk_ref, ka_ref, rk_ref, gmat_ref,
                      r_out, v_out, kkn_out, bonus_out, gate_out,
                      lwf_out, lwb_out, af_out, ab_out, kdf_out, kdb_out, *, tm):
    i = pl.program_id(0)
    g, sh, sc = g_ref[...], sh_ref[0], sc_ref[0]
    h = _norm_mod(x_ref[...], g, sh, sc)
    n_prompt_tiles = NP // tm
    tiles_per_seq = SAMPLE_LEN // tm
    pos = (i - n_prompt_tiles) % tiles_per_seq
    has_prev = jnp.logical_and(i >= n_prompt_tiles, pos != 0)
    has_next = jnp.logical_and(i >= n_prompt_tiles, pos != tiles_per_seq - 1)
    h_before = jnp.where(has_prev, _norm_mod(xp_ref[...], g, sh, sc)[7:8], 0.0)
    h_after = jnp.where(has_next, _norm_mod(xn_ref[...], g, sh, sc)[0:1], 0.0)
    rowi = lax.broadcasted_iota(jnp.int32, (tm, 1), 0)
    h_prev = jnp.where(rowi == 0, h_before, pltpu.roll(h, 1, 0))
    h_next = jnp.where(rowi == tm - 1, h_after, pltpu.roll(h, tm - 1, 0))
    xx = 0.5 * (h_prev + h_next) - h
    mix = lambda n: (h + xx * mu_ref[n:n + 1]).astype(bf16)

    r = _dot(mix(0), wr_ref[...])
    k = _dot(mix(2), wk_ref[...])
    v = _dot(mix(3), wv_ref[...])
    gate_out[...] = _dot(_sigmoid(_dot(mix(5), g1_ref[...])).astype(bf16), g2_ref[...])

    lo = _lane_lo()
    tw = jnp.tanh(_dot(mix(1), w1_ref[...]))
    ta = _dot(mix(4), a1_ref[...])
    gmat = gmat_ref[...]
    kk = k * kk_ref[...]
    kkn_out[...] = kk * lax.rsqrt(_group_sum(kk * kk, gmat) + 1e-12)
    r_out[...] = r
    v_out[...] = v
    kd_sum = jnp.zeros_like(k)
    for d, (lw_out, a_out, kd_out) in enumerate(((lwf_out, af_out, kdf_out), (lwb_out, ab_out, kdb_out))):
        keep = lo if d == 0 else jnp.logical_not(lo)
        zw = w0_ref[d:d + 1] + _dot(jnp.where(keep, tw, 0.0).astype(bf16), w2_ref[...])
        lw_out[...] = -math.exp(-0.5) * _sigmoid(zw)
        a = _sigmoid(a0_ref[d:d + 1] + _dot(jnp.where(keep, ta, 0.0).astype(bf16), a2_ref[...]))
        a_out[...] = a
        kd = k * (1.0 + (a - 1.0) * ka_ref[...])
        kd_out[...] = kd
        kd_sum = kd_sum + kd
    bonus_out[...] = _group_sum(r * kd_sum * rk_ref[...], gmat) * v


def _rwkv_projection(x, sh, sc, g, mu, wr, wk, wv, g1, g2, w1, w2, a1, a2, w0, a0, k_k, k_a, r_k, gmat):
    tm = 256
    cidx = functools.partial(_cond_idx, tm=tm)
    hb = tm // 8
    n8 = NT // 8
    full = lambda shape: pl.BlockSpec(shape, lambda i: tuple(0 for _ in shape))
    row = pl.BlockSpec((tm, D), lambda i: (i, 0))
    return pl.pallas_call(
        functools.partial(_rwkv_proj_kernel, tm=tm),
        grid=(NT // tm,),
        in_specs=[
            row,
            pl.BlockSpec((8, D), lambda i: (jnp.maximum(i * hb - 1, 0), 0)),
            pl.BlockSpec((8, D), lambda i: (jnp.minimum((i + 1) * hb, n8 - 1), 0)),
            pl.BlockSpec((1, 1, D), lambda i: (cidx(i), 0, 0)),
            pl.BlockSpec((1, 1, D), lambda i: (cidx(i), 0, 0)),
            full((1, D)), full((6, D)),
            full((D, D)), full((D, D)), full((D, D)),
            full((D, GATE_LORA_PAD)), full((GATE_LORA_PAD, D)),
            full((D, 2 * LORA)), full((2 * LORA, D)), full((D, 2 * LORA)), full((2 * LORA, D)),
            full((2, D)), full((2, D)), full((1, D)), full((1, D)), full((1, D)),
            full((256, 256)),
        ],
        out_specs=[row] * 11,
        out_shape=[jax.ShapeDtypeStruct((NT, D), f32)] * 11,
        compiler_params=pltpu.CompilerParams(
            dimension_semantics=("parallel",), vmem_limit_bytes=VMEM_LIMIT),
        name="rwkv_projection",
    )(x, x, x, sh, sc, g, mu, wr, wk, wv, g1, g2, w1, w2, a1, a2, w0, a0, k_k, k_a, r_k, gmat)


def _split3(x):
    x1 = x.astype(bf16)
    r1 = x - x1.astype(f32)
    x2 = r1.astype(bf16)
    x3 = (r1 - x2.astype(f32)).astype(bf16)
    return x1, x2, x3


def _scan_kernel(*refs, nc, has_init, emit_state):
    fwd_refs = refs[0:6]
    bwd_refs = refs[6:12]
    tri_ref, mask_ref = refs[12:14]
    pos = 14
    s0_ref = None
    if has_init:
        s0_ref = refs[pos]
        pos += 1
    yf_ref, yb_ref = refs[pos:pos + 2]
    pos += 2
    sout_ref = None
    if emit_state:
        sout_ref = refs[pos]
        pos += 1
    st_scr, cl_scr = refs[pos:pos + 2]

    s = pl.program_id(1)

    @pl.when(s == 0)
    def _():
        if has_init:
            st_scr[...] = s0_ref[0]
        else:
            st_scr[...] = jnp.zeros_like(st_scr)

    c = CHUNK
    for d, drefs in enumerate((fwd_refs, bwd_refs)):
        lw = drefs[3][...]
        tri = tri_ref[d]
        p1, p2, p3 = _split3(lw)
        cl_scr[d] = _dot(tri, p1) + _dot(tri, p2) + _dot(tri, p3)

    lane = lax.broadcasted_iota(jnp.int32, (1, 128), 1)
    m0 = (lane < HD).astype(f32)
    m1 = 1.0 - m0
    rid = lax.broadcasted_iota(jnp.int32, (128, 128), 0)
    cid = lax.broadcasted_iota(jnp.int32, (128, 128), 1)
    eye = (rid == cid).astype(f32)

    n_pairs = RW_HEADS // 2
    nb = 2 * n_pairs

    def per_head_rows(x):
        return jnp.concatenate([x * m0, x * m1], axis=0)

    ars_l, bk_l, bkh_l, v2_l, dec_l = [], [], [], [], []
    for d, drefs in enumerate((fwd_refs, bwd_refs)):
        r_ref, v_ref, kk_ref, lw_ref, a_ref, kd_ref = drefs
        end_row = c - 1 if d == 0 else 0
        for p in range(n_pairs):
            ln = slice(p * 128, (p + 1) * 128)
            cl = cl_scr[d, :, ln]
            kk = kk_ref[:, ln]
            tot = cl[end_row:end_row + 1]
            e_inv = jnp.exp(-cl)
            e_end = jnp.exp(tot - cl)
            kka = kk * a_ref[:, ln]
            kd = kd_ref[:, ln]
            at = per_head_rows(-kk * jnp.exp(cl - lw_ref[:, ln]))
            rt = per_head_rows(r_ref[:, ln] * jnp.exp(cl))
            ars_l.append(jnp.concatenate([at, rt], axis=0).astype(bf16))
            bk_l.append(jnp.concatenate([per_head_rows(kka * e_inv), per_head_rows(kd * e_inv)],
                                        axis=0).astype(bf16))
            bkh_l.append(jnp.concatenate([per_head_rows(kka * e_end), per_head_rows(kd * e_end)],
                                         axis=0).astype(bf16))
            v2_l.append(per_head_rows(v_ref[:, ln]).astype(bf16))
            dec_l.append(jnp.exp(tot))
    ar = jnp.stack(ars_l)
    bk = jnp.stack(bk_l)
    bkh = jnp.stack(bkh_l)
    v2 = jnp.stack(v2_l)
    dec = jnp.stack(dec_l)

    st = st_scr[...].reshape(nb, 128, 128)
    g2 = _bdot_nt(ar, bk).reshape(2, n_pairs, 256, 256) * mask_ref[...][:, None]
    g2 = g2.reshape(nb, 256, 256)
    ars = _bdot_nt(ar, st.astype(bf16))
    gv = _bdot(g2[:, :, 128:].astype(bf16), v2)
    l_bd = g2[:, :128, :128]
    l_bf = l_bd.astype(bf16)
    pk = _bdot(l_bf, l_bf)
    q = eye[None] + l_bd
    for it in range(5):
        pkb = pk.astype(bf16)
        if it < 4:
            res = _bdot(jnp.concatenate([q, pk], axis=1).astype(bf16), pkb)
            q = q + res[:, :128]
            pk = res[:, 128:]
        else:
            q = q + _bdot(q.astype(bf16), pkb)
    rhs = ars[:, :128] + gv[:, :128]
    u2b = _bdot(q.astype(bf16), rhs.astype(bf16)).astype(bf16)
    y2 = ars[:, 128:] + gv[:, 128:] + _bdot(g2[:, 128:, :128].astype(bf16), u2b)
    y = y2[:, :c] + y2[:, c:]
    for d, y_ref in enumerate((yf_ref, yb_ref)):
        for p in range(n_pairs):
            y_ref[:, p * 128:(p + 1) * 128] = y[d * n_pairs + p]
    uv = jnp.concatenate([u2b, v2], axis=1)
    st_scr[...] = (st * dec + _bdot_tn(uv, bkh)).reshape(2, n_pairs, 128, 128)

    if emit_state:
        @pl.when(s == nc - 1)
        def _():
            for d in range(2):
                for p in range(n_pairs):
                    m = st_scr[d, p]
                    sout_ref[0, d, 2 * p] = m[:HD, :HD]
                    sout_ref[0, d, 2 * p + 1] = pltpu.roll(m[HD:, :], HD, 1)[:, :HD]


def _rwkv_scan(streams_f, streams_b, tri, mask, s0_bd, *, n_seq, seq_len, row0, emit_state):
    nc = seq_len // CHUNK
    blk0 = row0 // CHUNK
    fwd_spec = pl.BlockSpec((CHUNK, D), lambda b, s: (blk0 + b * nc + s, 0))
    bwd_spec = pl.BlockSpec((CHUNK, D), lambda b, s: (blk0 + b * nc + nc - 1 - s, 0))
    in_specs = [fwd_spec] * 6 + [bwd_spec] * 6 + [
        pl.BlockSpec((2, CHUNK, CHUNK), lambda b, s: (0, 0, 0)),
        pl.BlockSpec((2, 256, 256), lambda b, s: (0, 0, 0)),
    ]
    args = list(streams_f) + list(streams_b) + [tri, mask]
    state_block = (1, 2, RW_HEADS // 2, 128, 128)
    if s0_bd is not None:
        in_specs.append(pl.BlockSpec(state_block, lambda b, s: (b, 0, 0, 0, 0)))
        args.append(s0_bd)
    out_specs = [pl.BlockSpec((CHUNK, D), lambda b, s: (b * nc + s, 0)),
                 pl.BlockSpec((CHUNK, D), lambda b, s: (b * nc + nc - 1 - s, 0))]
    out_shape = [jax.ShapeDtypeStruct((n_seq * seq_len, D), f32)] * 2
    if emit_state:
        out_specs.append(pl.BlockSpec((1, 2, RW_HEADS, HD, HD), lambda b, s: (b, 0, 0, 0, 0)))
        out_shape.append(jax.ShapeDtypeStruct((n_seq, 2, RW_HEADS, HD, HD), f32))
    return pl.pallas_call(
        functools.partial(_scan_kernel, nc=nc, has_init=s0_bd is not None, emit_state=emit_state),
        grid=(n_seq, nc),
        in_specs=in_specs,
        out_specs=out_specs,
        out_shape=out_shape,
        scratch_shapes=[pltpu.VMEM(state_block[1:], f32), pltpu.VMEM((2, CHUNK, D), f32)],
        compiler_params=pltpu.CompilerParams(
            dimension_semantics=("parallel", "arbitrary"), vmem_limit_bytes=VMEM_LIMIT),
        name="rwkv_scan_prompt" if emit_state else "rwkv_scan_sample",
    )(*args)


def _scan_constants():
    t = jnp.arange(CHUNK)
    lower = (t[:, None] >= t[None, :])
    tri = jnp.stack([lower, lower.T]).astype(bf16)
    masks = []
    for d in range(2):
        strict = (t[:, None] > t[None, :]) if d == 0 else (t[:, None] < t[None, :])
        incl = lower if d == 0 else lower.T
        blocks = []
        for m in (strict, incl):
            bd = jnp.kron(jnp.eye(2, dtype=f32), m.astype(f32))
            blocks.append(jnp.concatenate([bd, bd], axis=1))
        masks.append(jnp.concatenate(blocks, axis=0))
    return tri, jnp.stack(masks)


def _state_to_blockdiag(s):
    n = s.shape[0]
    s = s.reshape(n, 2, RW_HEADS // 2, 2, HD, HD)
    z = jnp.zeros_like(s[:, :, :, 0])
    top = jnp.concatenate([s[:, :, :, 0], z], axis=-1)
    bot = jnp.concatenate([z, s[:, :, :, 1]], axis=-1)
    return jnp.concatenate([top, bot], axis=-2)


def _rope_tables():
    t = jnp.arange(SAMPLE_LEN)
    rows = (t // GRID_W).astype(f32)
    cols = (t % GRID_W).astype(f32)
    nf = HD // 4
    inv = 10000.0 ** (-jnp.arange(nf, dtype=f32) / nf)
    ang = jnp.concatenate([rows[:, None] * inv, cols[:, None] * inv], axis=-1)
    cos = jnp.repeat(jnp.cos(ang), 2, axis=-1)
    sin = jnp.repeat(jnp.sin(ang), 2, axis=-1) * jnp.tile(jnp.array([-1.0, 1.0], f32), HD // 2)
    return jnp.tile(cos, (1, 2)), jnp.tile(sin, (1, 2))


def kernel(x_prompt, x_sample, cache_diff_k, cache_diff_v, cache_na_k, cache_na_v, state_rwkv, c, c_ctx, w_ada, b_ada, g_mix, g_ffn, w_in, w_out, diff_q_g, diff_k_g, diff_lam_q1, diff_lam_k1, diff_lam_q2, diff_lam_k2, diff_subln_g, na_q_g, na_k_g, na_rpb, ffn_w1, ffn_w3, ffn_w2, rw_mu, rw_wr, rw_wk, rw_wv, rw_wo, rw_w0, rw_w1, rw_w2, rw_a0, rw_a1, rw_a2, rw_g1, rw_g2, rw_k_k, rw_k_a, rw_r_k, rw_ln_g, rw_ln_b, moe_router, moe_router_b, moe_w1, moe_w3, moe_w2):
    xp = x_prompt.reshape(NP, D)
    xs = x_sample.reshape(NS, D)
    cond8 = jnp.concatenate([c_ctx[None, :], c, jnp.zeros((3, D), f32)], axis=0)
    mod = _ada_table(cond8, w_ada, b_ada)
    gmat = jnp.kron(jnp.eye(4, dtype=f32), jnp.ones((HD, HD), f32)).astype(bf16)

    lam_init = 0.8 - 0.6 * math.exp(-0.3 * 0)
    ones_seg = jnp.ones((SEG,), f32)
    tile8 = lambda gvec: jnp.tile(gvec, SEG // HD)
    gains = jnp.stack([tile8(diff_q_g[0]), tile8(diff_k_g[0]), ones_seg,
                       tile8(na_q_g[0]), tile8(na_k_g[0]), ones_seg]).reshape(6, 1, SEG)
    cos_t, sin_t = _rope_tables()
    proj, dk_p, dv_p, nk_p, nv_p = _in_projection(
        xp, xs, mod[0][0], mod[0][1], g_mix[0][None, :], w_in[0].astype(bf16), gains, gmat, cos_t, sin_t)
    lamp = jnp.stack([diff_lam_q1[0], diff_lam_k1[0], diff_lam_q2[0], diff_lam_k2[0]])
    subg = diff_subln_g[0][None, :]
    o_prompt = _prompt_attention(proj, lamp, subg, lam_init)
    o_diff = _latent_diff_attention(proj, cache_diff_k[:, 0].reshape(N_SAMPLE_SEQ, PAST, SEG),
                                    cache_diff_v[:, 0].reshape(N_SAMPLE_SEQ, PAST, SEG), lamp, subg, lam_init)
    o_na = _latent_na_attention(proj, cache_na_k[:, 0].reshape(N_SAMPLE_SEQ, PAST, SEG),
                                cache_na_v[:, 0].reshape(N_SAMPLE_SEQ, PAST, SEG), _rpb_table(na_rpb[0]))
    x = _out_projection(o_prompt, o_diff, o_na, w_out[0].astype(bf16), xp, xs, mod[0][2])
    x = _dense_ffn(x, mod[0][3], mod[0][4], mod[0][5], g_ffn[0][None, :],
                   ffn_w1[0].astype(bf16), ffn_w3[0].astype(bf16), ffn_w2[0].astype(bf16))

    pad_g = GATE_LORA_PAD - GATE_LORA
    g1 = jnp.pad(rw_g1[0], ((0, 0), (0, pad_g))).astype(bf16)
    g2 = jnp.pad(rw_g2[0], ((0, pad_g), (0, 0))).astype(bf16)
    w1cat = jnp.concatenate([rw_w1[0, 0], rw_w1[0, 1]], axis=1).astype(bf16)
    w2cat = jnp.concatenate([rw_w2[0, 0], rw_w2[0, 1]], axis=0).astype(bf16)
    a1cat = jnp.concatenate([rw_a1[0, 0], rw_a1[0, 1]], axis=1).astype(bf16)
    a2cat = jnp.concatenate([rw_a2[0, 0], rw_a2[0, 1]], axis=0).astype(bf16)
    (r, v, kkn, bonus, gate_lora, lwf, lwb, af, ab, kdf, kdb) = _rwkv_projection(
        x, mod[1][0], mod[1][1], g_mix[1][None, :], rw_mu[0],
        rw_wr[0].astype(bf16), rw_wk[0].astype(bf16), rw_wv[0].astype(bf16), g1, g2,
        w1cat, w2cat, a1cat, a2cat, rw_w0[0], rw_a0[0],
        rw_k_k[0][None, :], rw_k_a[0][None, :], rw_r_k[0].reshape(1, D), gmat)
    tri, mask = _scan_constants()
    streams_f = (r, v, kkn, lwf, af, kdf)
    streams_b = (r, v, kkn, lwb, ab, kdb)
    yf_p, yb_p, st_p = _rwkv_scan(streams_f, streams_b, tri, mask, None,
                                  n_seq=N_PROMPT_SEQ, seq_len=PROMPT_LEN, row0=0, emit_state=True)
    yf_s, yb_s = _rwkv_scan(streams_f, streams_b, tri, mask, _state_to_blockdiag(state_rwkv[:, 0]),
                            n_seq=N_SAMPLE_SEQ, seq_len=SAMPLE_LEN, row0=NP, emit_state=False)
    x = _rwkv_out_projection(yf_p, yb_p, yf_s, yb_s, bonus, gate_lora, rw_ln_g[0][None, :], rw_ln_b[0][None, :], gmat,
                             rw_wo[0].astype(bf16), x, mod[1][2])
    w_router = jnp.pad(moe_router[0], ((0, 0), (0, 128 - N_EXPERTS)))
    b_router = jnp.pad(moe_router_b[0], (0, 128 - N_EXPERTS))[None, :]
    h_bf, gates, rank, rank_t, cnt = _moe_route(x, mod[1][3], mod[1][4], g_ffn[1][None, :], w_router, b_router)
    x = _moe_experts(cnt[:, 0, :N_EXPERTS].reshape(-1), h_bf, rank, rank_t, gates, x, mod[1][5],
                     moe_w1[0].astype(bf16), moe_w3[0].astype(bf16), moe_w2[0].astype(bf16))

    new_dk = dk_p.reshape(N_PROMPT_SEQ, 1, PROMPT_LEN, DIFF_HEADS, 2 * HD)
    new_dv = dv_p.reshape(N_PROMPT_SEQ, 1, PROMPT_LEN, DIFF_HEADS, 2 * HD)
    new_nk = nk_p.reshape(N_PROMPT_SEQ, 1, PROMPT_LEN, NA_HEADS, HD)
    new_nv = nv_p.reshape(N_PROMPT_SEQ, 1, PROMPT_LEN, NA_HEADS, HD)
    new_state = st_p[:, None]
    return (x[:NP].reshape(N_PROMPT_SEQ, PROMPT_LEN, D), x[NP:].reshape(N_SAMPLE_SEQ, SAMPLE_LEN, D),
            new_dk, new_dv, new_nk, new_nv, new_state)
```

```python
import functools
import math

import jax
import jax.numpy as jnp
from jax import lax
from jax.experimental import pallas as pl
from jax.experimental.pallas import tpu as pltpu

f32 = jnp.float32
bf16 = jnp.bfloat16

D = 1024
N_PROMPT_SEQ, PROMPT_LEN = 32, 256
N_SAMPLE_SEQ, SAMPLE_LEN = 4, 2048
NP = N_PROMPT_SEQ * PROMPT_LEN
NS = N_SAMPLE_SEQ * SAMPLE_LEN
NT = NP + NS
PAST = 256
GRID_W = 64
GRID_R = SAMPLE_LEN // GRID_W
HD = 64
DIFF_HEADS = 4
NA_HEADS = 8
NA_KH = 8
NA_KW = 16
SEG = 512
IN_COLS = 6 * SEG
FFN_DIM = 2816
N_EXPERTS = 8
EXPERT_DIM = 3584
RW_HEADS = 16
LORA = 64
GATE_LORA = 160
GATE_LORA_PAD = 256
EPS = 1e-6
GN_EPS = 64e-5
NEG_BIG = -1e30
DIFF_HEADS_PER_STEP = 2
NA_ROWS_PER_STEP = 8
CHUNK = 64
VMEM_LIMIT = 56 * 1024 * 1024


def _cond_idx(i, tm):
    return jnp.maximum((i * tm) // SAMPLE_LEN - (NP // SAMPLE_LEN - 1), 0)


def _dot(a, b):
    return jnp.dot(a, b, preferred_element_type=f32)


def _dot_nt(a, b):
    return lax.dot_general(a, b, (((1,), (1,)), ((), ())), preferred_element_type=f32)


def _dot_tn(a, b):
    return lax.dot_general(a, b, (((0,), (0,)), ((), ())), preferred_element_type=f32)


def _bdot(a, b):
    return lax.dot_general(a, b, (((2,), (1,)), ((0,), (0,))), preferred_element_type=f32)


def _bdot_nt(a, b):
    return lax.dot_general(a, b, (((2,), (2,)), ((0,), (0,))), preferred_element_type=f32)


def _bdot_tn(a, b):
    return lax.dot_general(a, b, (((1,), (1,)), ((0,), (0,))), preferred_element_type=f32)


def _sigmoid(x):
    return 1.0 / (1.0 + jnp.exp(-x))


def _norm_mod(x, g, sh, sc):
    ms = jnp.mean(x * x, axis=-1, keepdims=True)
    return (x * lax.rsqrt(ms + EPS) * g) * (1.0 + sc) + sh


def _group_sum(x, gmat):
    xb = x.astype(bf16)
    cols = [_dot(xb[:, c * 256:(c + 1) * 256], gmat) for c in range(x.shape[1] // 256)]
    return cols[0] if len(cols) == 1 else jnp.concatenate(cols, axis=1)


def _softmax_parts(parts):
    m = parts[0].max(axis=-1, keepdims=True)
    for p in parts[1:]:
        m = jnp.maximum(m, p.max(axis=-1, keepdims=True))
    es = [jnp.exp(p - m) for p in parts]
    l = es[0].sum(axis=-1, keepdims=True)
    for e in es[1:]:
        l = l + e.sum(axis=-1, keepdims=True)
    inv = 1.0 / l
    return [e * inv for e in es]


def _lane_lo(n=128):
    return lax.broadcasted_iota(jnp.int32, (1, n), 1) < HD


def _stack_halves(q):
    lo = _lane_lo()
    return jnp.concatenate([jnp.where(lo, q, 0.0), jnp.where(lo, 0.0, q)], axis=0)


def _ada_kernel(cond_ref, w_ref, b_ref, o_ref):
    x = cond_ref[...]
    s = x * _sigmoid(x)
    o_ref[0] = _dot(s.astype(bf16), w_ref[0].astype(bf16)) + b_ref[0]


def _ada_table(cond8, w_ada, b_ada):
    depth = w_ada.shape[0]
    tn = 1536
    out = pl.pallas_call(
        _ada_kernel,
        grid=(depth, 6 * D // tn),
        in_specs=[
            pl.BlockSpec((8, D), lambda l, n: (0, 0)),
            pl.BlockSpec((1, D, tn), lambda l, n: (l, 0, n)),
            pl.BlockSpec((1, 1, tn), lambda l, n: (l, 0, n)),
        ],
        out_specs=pl.BlockSpec((1, 8, tn), lambda l, n: (l, 0, n)),
        out_shape=jax.ShapeDtypeStruct((depth, 8, 6 * D), f32),
        compiler_params=pltpu.CompilerParams(vmem_limit_bytes=VMEM_LIMIT),
        name="ada_table",
    )(cond8, w_ada, b_ada.reshape(depth, 1, 6 * D))
    out = out.reshape(depth, 8, 6, D)
    return [[out[l, :, k, :].reshape(8, 1, D) for k in range(6)] for l in range(depth)]


def _qk_norm(y, gain, gmat):
    ss = _group_sum(y * y, gmat) * (1.0 / HD)
    return y * lax.rsqrt(ss + EPS) * gain


def _rope(y, cos, sin):
    even = (lax.broadcasted_iota(jnp.int32, (1, 128), 1) % 2) == 0
    outs = []
    for c in range(y.shape[1] // 128):
        yc = y[:, c * 128:(c + 1) * 128]
        swapped = jnp.where(even, pltpu.roll(yc, 127, 1), pltpu.roll(yc, 1, 1))
        outs.append(yc * cos + swapped * sin)
    return jnp.concatenate(outs, axis=1)


def _inproj_kernel(xp_ref, xs_ref, sh_ref, sc_ref, g_ref, w_ref, gain_ref, gmat_ref, cos_ref, sin_ref,
                   o_ref, dk_ref, dv_ref, nk_ref, nv_ref, h_scr, *, tm):
    i = pl.program_id(0)
    j = pl.program_id(1)

    def emit(val):
        o_ref[...] = val.astype(bf16)
        for seg, cache_ref in ((1, dk_ref), (2, dv_ref), (4, nk_ref), (5, nv_ref)):
            @pl.when(jnp.logical_and(j == seg, i < NP // tm))
            def _():
                cache_ref[...] = val

    for x_ref, active in ((xp_ref, i < NP // tm), (xs_ref, i >= NP // tm)):
        @pl.when(jnp.logical_and(j == 0, active))
        def _():
            h_scr[...] = _norm_mod(x_ref[...], g_ref[...], sh_ref[0], sc_ref[0]).astype(bf16)

    y = _dot(h_scr[...], w_ref[...])
    is_norm = jnp.logical_and(j != 2, j != 5)
    is_rope = jnp.logical_and(j < 2, i >= NP // tm)

    @pl.when(jnp.logical_not(is_norm))
    def _():
        emit(y)

    @pl.when(jnp.logical_and(is_norm, jnp.logical_not(is_rope)))
    def _():
        emit(_qk_norm(y, gain_ref[0], gmat_ref[...]))

    @pl.when(is_rope)
    def _():
        o_ref[...] = _rope(_qk_norm(y, gain_ref[0], gmat_ref[...]), cos_ref[...], sin_ref[...]).astype(bf16)


def _in_projection(xp, xs, sh, sc, g, w_bf, gains, gmat, cos_t, sin_t):
    tm = 1024
    n_prompt_tiles = NP // tm
    tiles_per_seq = SAMPLE_LEN // tm
    cidx = functools.partial(_cond_idx, tm=tm)
    rope_idx = lambda i, j: (jnp.maximum(i - n_prompt_tiles, 0) % tiles_per_seq, 0)
    return pl.pallas_call(
        functools.partial(_inproj_kernel, tm=tm),
        grid=(NT // tm, IN_COLS // SEG),
        in_specs=[
            pl.BlockSpec((tm, D), lambda i, j: (jnp.minimum(i, n_prompt_tiles - 1), 0)),
            pl.BlockSpec((tm, D), lambda i, j: (jnp.maximum(i - n_prompt_tiles, 0), 0)),
            pl.BlockSpec((1, 1, D), lambda i, j: (cidx(i), 0, 0)),
            pl.BlockSpec((1, 1, D), lambda i, j: (cidx(i), 0, 0)),
            pl.BlockSpec((1, D), lambda i, j: (0, 0)),
            pl.BlockSpec((D, SEG), lambda i, j: (0, j)),
            pl.BlockSpec((1, 1, SEG), lambda i, j: (j, 0, 0)),
            pl.BlockSpec((256, 256), lambda i, j: (0, 0)),
            pl.BlockSpec((tm, 128), rope_idx),
            pl.BlockSpec((tm, 128), rope_idx),
        ],
        out_specs=[pl.BlockSpec((tm, SEG), lambda i, j: (i, j))] + [
            pl.BlockSpec((tm, SEG), lambda i, j: (jnp.minimum(i, n_prompt_tiles - 1), 0))] * 4,
        out_shape=[jax.ShapeDtypeStruct((NT, IN_COLS), bf16)] + [jax.ShapeDtypeStruct((NP, SEG), f32)] * 4,
        scratch_shapes=[pltpu.VMEM((tm, D), bf16)],
        compiler_params=pltpu.CompilerParams(
            dimension_semantics=("arbitrary", "arbitrary"), vmem_limit_bytes=VMEM_LIMIT),
        name="in_projection",
    )(xp, xs, sh, sc, g, w_bf, gains, gmat, cos_t, sin_t)


def _lambda_value(lamp_ref, lam_init):
    lp = lamp_ref[...]
    e1 = jnp.exp(jnp.sum(lp[0:1] * lp[1:2], axis=-1, keepdims=True))
    e2 = jnp.exp(jnp.sum(lp[2:3] * lp[3:4], axis=-1, keepdims=True))
    return e1 - e2 + lam_init


def _sub_ln(o, subg, lam_init):
    ms = jnp.mean(o * o, axis=-1, keepdims=True)
    return o * lax.rsqrt(ms + EPS) * subg * (1.0 - lam_init)


def _prompt_attn_kernel(p_ref, lamp_ref, subg_ref, o_ref, *, lam_init):
    lam = _lambda_value(lamp_ref, lam_init)
    t = PROMPT_LEN
    lo = _lane_lo()
    scale = HD ** -0.5
    for h in range(DIFF_HEADS):
        q = p_ref[:, h * 128:(h + 1) * 128]
        k = p_ref[:, SEG + h * 128:SEG + (h + 1) * 128].astype(bf16)
        v = p_ref[:, 2 * SEG + h * 128:2 * SEG + (h + 1) * 128].astype(bf16)
        s = _dot_nt((_stack_halves(q) * scale).astype(bf16), k)
        (p,) = _softmax_parts([s])
        pd = p[:t] - lam * p[t:]
        o = _dot(pd.astype(bf16), v)
        o_ref[:, h * 128:(h + 1) * 128] = _sub_ln(o, subg_ref[...], lam_init).astype(bf16)
    for hp in range(NA_HEADS // 2):
        q = p_ref[:, 3 * SEG + hp * 128:3 * SEG + (hp + 1) * 128]
        k = p_ref[:, 4 * SEG + hp * 128:4 * SEG + (hp + 1) * 128].astype(bf16)
        v = p_ref[:, 5 * SEG + hp * 128:5 * SEG + (hp + 1) * 128].astype(bf16)
        s = _dot_nt((_stack_halves(q) * scale).astype(bf16), k)
        (p,) = _softmax_parts([s])
        o = _dot(p.astype(bf16), v)
        o_ref[:, SEG + hp * 128:SEG + (hp + 1) * 128] = jnp.where(lo, o[:t], o[t:]).astype(bf16)


def _prompt_attention(proj, lamp, subg, lam_init):
    return pl.pallas_call(
        functools.partial(_prompt_attn_kernel, lam_init=lam_init),
        grid=(N_PROMPT_SEQ,),
        in_specs=[
            pl.BlockSpec((PROMPT_LEN, IN_COLS), lambda b: (b, 0)),
            pl.BlockSpec((4, HD), lambda b: (0, 0)),
            pl.BlockSpec((1, 128), lambda b: (0, 0)),
        ],
        out_specs=pl.BlockSpec((PROMPT_LEN, D), lambda b: (b, 0)),
        out_shape=jax.ShapeDtypeStruct((NP, D), bf16),
        compiler_params=pltpu.CompilerParams(
            dimension_semantics=("parallel",), vmem_limit_bytes=VMEM_LIMIT),
        name="prompt_attention",
    )(proj, lamp, subg)


def _latent_diff_kernel(q_ref, kn_ref, vn_ref, kc_ref, vc_ref, lamp_ref, subg_ref, o_ref,
                        *, lam_init, tq):
    lam = _lambda_value(lamp_ref, lam_init)
    scale = HD ** -0.5
    for hh in range(DIFF_HEADS_PER_STEP):
        ln = slice(hh * 128, (hh + 1) * 128)
        qq = (_stack_halves(q_ref[:, ln]) * scale).astype(bf16)
        s_c = _dot_nt(qq, kc_ref[0, :, ln].astype(bf16))
        s_n = _dot_nt(qq, kn_ref[:, ln].astype(bf16))
        p_c, p_n = _softmax_parts([s_c, s_n])
        pd_c = p_c[:tq] - lam * p_c[tq:]
        pd_n = p_n[:tq] - lam * p_n[tq:]
        o = (_dot(pd_c.astype(bf16), vc_ref[0, :, ln].astype(bf16))
             + _dot(pd_n.astype(bf16), vn_ref[:, ln].astype(bf16)))
        o_ref[:, ln] = _sub_ln(o, subg_ref[...], lam_init).astype(bf16)


def _latent_diff_attention(proj, cache_k, cache_v, lamp, subg, lam_init):
    tq = 256
    nqb = SAMPLE_LEN // tq
    q0 = NP // tq
    s0 = NP // SAMPLE_LEN
    hw = 128 * DIFF_HEADS_PER_STEP
    kcol = SEG // hw
    return pl.pallas_call(
        functools.partial(_latent_diff_kernel, lam_init=lam_init, tq=tq),
        grid=(N_SAMPLE_SEQ, DIFF_HEADS // DIFF_HEADS_PER_STEP, nqb),
        in_specs=[
            pl.BlockSpec((tq, hw), lambda b, h, q: (q0 + b * nqb + q, h)),
            pl.BlockSpec((SAMPLE_LEN, hw), lambda b, h, q: (s0 + b, kcol + h)),
            pl.BlockSpec((SAMPLE_LEN, hw), lambda b, h, q: (s0 + b, 2 * kcol + h)),
            pl.BlockSpec((1, PAST, hw), lambda b, h, q: (b, 0, h)),
            pl.BlockSpec((1, PAST, hw), lambda b, h, q: (b, 0, h)),
            pl.BlockSpec((4, HD), lambda b, h, q: (0, 0)),
            pl.BlockSpec((1, 128), lambda b, h, q: (0, 0)),
        ],
        out_specs=pl.BlockSpec((tq, hw), lambda b, h, q: (b * nqb + q, h)),
        out_shape=jax.ShapeDtypeStruct((NS, SEG), bf16),
        compiler_params=pltpu.CompilerParams(
            dimension_semantics=("parallel", "parallel", "arbitrary"), vmem_limit_bytes=VMEM_LIMIT),
        name="latent_diff_attention",
    )(proj, proj, proj, cache_k, cache_v, lamp, subg)


def _rpb_table_kernel(rpb_ref, o_ref):
    h = pl.program_id(0)
    wq = lax.broadcasted_iota(jnp.int32, (GRID_W, GRID_W), 0)
    wk = lax.broadcasted_iota(jnp.int32, (GRID_W, GRID_W), 1)
    col_start = jnp.clip(wq - NA_KW // 2, 0, GRID_W - NA_KW)
    col_in = jnp.logical_and(wk >= col_start, wk < col_start + NA_KW)
    col_off = jnp.clip(wk - wq, -(NA_KW - 1), NA_KW - 1) + (NA_KW - 1)
    n_dr = 2 * NA_KH - 1
    n_dc = 2 * NA_KW - 1
    for dr in range(n_dr):
        t = jnp.zeros((GRID_W, GRID_W), f32)
        for c in range(n_dc):
            t = jnp.where(col_off == c, rpb_ref[h * (n_dr * n_dc) + dr * n_dc + c], t)
        o_ref[0, dr] = jnp.where(col_in, t, NEG_BIG)


def _rpb_table(rpb):
    n_dr = 2 * NA_KH - 1
    tcol = pl.pallas_call(
        _rpb_table_kernel,
        grid=(NA_HEADS,),
        in_specs=[pl.BlockSpec(memory_space=pltpu.SMEM)],
        out_specs=pl.BlockSpec((1, n_dr, GRID_W, GRID_W), lambda h: (h, 0, 0, 0)),
        out_shape=jax.ShapeDtypeStruct((NA_HEADS, n_dr, GRID_W, GRID_W), f32),
        name="rpb_table",
    )(rpb.reshape(-1))
    return jnp.stack(
        [jnp.concatenate([tcol[:, j - s + NA_KH - 1] for j in range(NA_KH)], axis=-1) for s in range(NA_KH)],
        axis=1)


def _latent_na_kernel(q_ref, k_ref, v_ref, kc_ref, vc_ref, bias_ref, o_ref):
    scale = HD ** -0.5
    lo = _lane_lo()
    kc = kc_ref[0].astype(bf16)
    vc = vc_ref[0].astype(bf16)
    win = NA_KH * GRID_W

    nr = NA_ROWS_PER_STEP
    kcb = jnp.broadcast_to(kc[None], (nr,) + kc.shape)
    vcb = jnp.broadcast_to(vc[None], (nr,) + vc.shape)

    def rows(g, carry):
        qs, kws, vws, biases = [], [], [], []
        for t in range(nr):
            r = g * nr + t
            rs = jnp.clip(r - NA_KH // 2, 0, GRID_R - NA_KH)
            sidx = r - rs
            q = q_ref[pl.ds(pl.multiple_of(r * GRID_W, GRID_W), GRID_W), :]
            qs.append((_stack_halves(q) * scale).astype(bf16))
            k0 = pl.multiple_of(rs * GRID_W, GRID_W)
            kws.append(k_ref[pl.ds(k0, win), :].astype(bf16))
            vws.append(v_ref[pl.ds(k0, win), :].astype(bf16))
            biases.append(jnp.concatenate([bias_ref[0, sidx], bias_ref[1, sidx]], axis=0))
        qq = jnp.stack(qs)
        s_loc = _bdot_nt(qq, jnp.stack(kws)) + jnp.stack(biases)
        s_ctx = _bdot_nt(qq, kcb)
        p_loc, p_ctx = _softmax_parts([s_loc, s_ctx])
        o = _bdot(p_loc.astype(bf16), jnp.stack(vws)) + _bdot(p_ctx.astype(bf16), vcb)
        o = jnp.where(lo, o[:, :GRID_W], o[:, GRID_W:]).astype(bf16)
        row0 = pl.multiple_of(g * (nr * GRID_W), nr * GRID_W)
        o_ref[pl.ds(row0, nr * GRID_W), :] = o.reshape(nr * GRID_W, 128)
        return carry

    lax.fori_loop(0, GRID_R // nr, rows, 0)


def _latent_na_attention(proj, cache_k, cache_v, bias):
    s0 = NP // SAMPLE_LEN
    return pl.pallas_call(
        _latent_na_kernel,
        grid=(N_SAMPLE_SEQ, NA_HEADS // 2),
        in_specs=[
            pl.BlockSpec((SAMPLE_LEN, 128), lambda b, h: (s0 + b, 12 + h)),
            pl.BlockSpec((SAMPLE_LEN, 128), lambda b, h: (s0 + b, 16 + h)),
            pl.BlockSpec((SAMPLE_LEN, 128), lambda b, h: (s0 + b, 20 + h)),
            pl.BlockSpec((1, PAST, 128), lambda b, h: (b, 0, h)),
            pl.BlockSpec((1, PAST, 128), lambda b, h: (b, 0, h)),
            pl.BlockSpec((2, NA_KH, GRID_W, NA_KH * GRID_W), lambda b, h: (h, 0, 0, 0)),
        ],
        out_specs=pl.BlockSpec((SAMPLE_LEN, 128), lambda b, h: (b, h)),
        out_shape=jax.ShapeDtypeStruct((NS, SEG), bf16),
        compiler_params=pltpu.CompilerParams(
            dimension_semantics=("parallel", "parallel"), vmem_limit_bytes=VMEM_LIMIT),
        name="latent_na_attention",
    )(proj, proj, proj, cache_k, cache_v, bias)


def _out_proj_kernel(ap_ref, ad_ref, an_ref, w_ref, xp_ref, xs_ref, gt_ref, o_ref, *, tm):
    i = pl.program_id(0)

    @pl.when(i < NP // tm)
    def _():
        o_ref[...] = xp_ref[...] + gt_ref[0] * _dot(ap_ref[...], w_ref[...])

    @pl.when(i >= NP // tm)
    def _():
        mixed = _dot(ad_ref[...], w_ref[:SEG, :]) + _dot(an_ref[...], w_ref[SEG:, :])
        o_ref[...] = xs_ref[...] + gt_ref[0] * mixed


def _out_projection(a_prompt, a_diff, a_na, w_bf, xp, xs, gate):
    tm = 512
    n_p = NP // tm
    cidx = functools.partial(_cond_idx, tm=tm)
    prompt_rows = lambda i: (jnp.minimum(i, n_p - 1), 0)
    sample_rows = lambda i: (jnp.maximum(i - n_p, 0), 0)
    return pl.pallas_call(
        functools.partial(_out_proj_kernel, tm=tm),
        grid=(NT // tm,),
        in_specs=[
            pl.BlockSpec((tm, D), prompt_rows),
            pl.BlockSpec((tm, SEG), sample_rows),
            pl.BlockSpec((tm, SEG), sample_rows),
            pl.BlockSpec((D, D), lambda i: (0, 0)),
            pl.BlockSpec((tm, D), prompt_rows),
            pl.BlockSpec((tm, D), sample_rows),
            pl.BlockSpec((1, 1, D), lambda i: (cidx(i), 0, 0)),
        ],
        out_specs=pl.BlockSpec((tm, D), lambda i: (i, 0)),
        out_shape=jax.ShapeDtypeStruct((NT, D), f32),
        compiler_params=pltpu.CompilerParams(
            dimension_semantics=("arbitrary",), vmem_limit_bytes=VMEM_LIMIT),
        name="out_projection",
    )(a_prompt, a_diff, a_na, w_bf, xp, xs, gate)


def _rwkv_out_kernel(yfp_ref, ybp_ref, yfs_ref, ybs_ref, bonus_ref, gate_ref, lng_ref, lnb_ref,
                     gmat_ref, w_ref, x_ref, gt_ref, o_ref, *, tm):
    i = pl.program_id(0)

    def finish(y):
        gmat = gmat_ref[...]
        mu = _group_sum(y, gmat) * (1.0 / HD)
        yc = y - mu
        var = _group_sum(yc * yc, gmat) * (1.0 / HD)
        z = yc * lax.rsqrt(var + GN_EPS) * lng_ref[...] + lnb_ref[...] + bonus_ref[...].astype(f32)
        z = (z * gate_ref[...].astype(f32)).astype(bf16)
        o_ref[...] = x_ref[...] + gt_ref[0] * _dot(z, w_ref[...])

    @pl.when(i < NP // tm)
    def _():
        finish(yfp_ref[...] + ybp_ref[...])

    @pl.when(i >= NP // tm)
    def _():
        finish(yfs_ref[...] + ybs_ref[...])


def _rwkv_out_projection(yf_p, yb_p, yf_s, yb_s, bonus, gate_lora, ln_g, ln_b, gmat, w_bf, x, gate):
    tm = 512
    n_p = NP // tm
    cidx = functools.partial(_cond_idx, tm=tm)
    row = pl.BlockSpec((tm, D), lambda i: (i, 0))
    prompt_row = pl.BlockSpec((tm, D), lambda i: (jnp.minimum(i, n_p - 1), 0))
    sample_row = pl.BlockSpec((tm, D), lambda i: (jnp.maximum(i - n_p, 0), 0))
    vec = pl.BlockSpec((1, D), lambda i: (0, 0))
    return pl.pallas_call(
        functools.partial(_rwkv_out_kernel, tm=tm),
        grid=(NT // tm,),
        in_specs=[prompt_row, prompt_row, sample_row, sample_row, row, row, vec, vec,
                  pl.BlockSpec((256, 256), lambda i: (0, 0)),
                  pl.BlockSpec((D, D), lambda i: (0, 0)),
                  row,
                  pl.BlockSpec((1, 1, D), lambda i: (cidx(i), 0, 0))],
        out_specs=row,
        out_shape=jax.ShapeDtypeStruct((NT, D), f32),
        compiler_params=pltpu.CompilerParams(
            dimension_semantics=("arbitrary",), vmem_limit_bytes=VMEM_LIMIT),
        name="rwkv_out_projection",
    )(yf_p, yb_p, yf_s, yb_s, bonus, gate_lora, ln_g, ln_b, gmat, w_bf, x, gate)


def _swiglu_hidden(xb, w1, w3):
    a = _dot(xb, w1)
    return ((a * _sigmoid(a)) * _dot(xb, w3)).astype(bf16)


def _ffn_kernel(x_ref, sh_ref, sc_ref, gt_ref, g_ref, w1_ref, w3_ref, w2_ref, o_ref, h_scr, acc_scr):
    f = pl.program_id(1)

    @pl.when(f == 0)
    def _():
        h_scr[...] = _norm_mod(x_ref[...], g_ref[...], sh_ref[0], sc_ref[0]).astype(bf16)
        acc_scr[...] = jnp.zeros_like(acc_scr)

    acc_scr[...] += _dot(_swiglu_hidden(h_scr[...], w1_ref[...], w3_ref[...]), w2_ref[...])

    @pl.when(f == pl.num_programs(1) - 1)
    def _():
        o_ref[...] = x_ref[...] + gt_ref[0] * acc_scr[...]


def _dense_ffn(x, sh, sc, gt, g, w1, w3, w2):
    tm = 512
    tf = FFN_DIM // 2
    cidx = functools.partial(_cond_idx, tm=tm)
    mod = pl.BlockSpec((1, 1, D), lambda i, f: (cidx(i), 0, 0))
    return pl.pallas_call(
        _ffn_kernel,
        grid=(NT // tm, FFN_DIM // tf),
        in_specs=[
            pl.BlockSpec((tm, D), lambda i, f: (i, 0)),
            mod, mod, mod,
            pl.BlockSpec((1, D), lambda i, f: (0, 0)),
            pl.BlockSpec((D, tf), lambda i, f: (0, f)),
            pl.BlockSpec((D, tf), lambda i, f: (0, f)),
            pl.BlockSpec((tf, D), lambda i, f: (f, 0)),
        ],
        out_specs=pl.BlockSpec((tm, D), lambda i, f: (i, 0)),
        out_shape=jax.ShapeDtypeStruct((NT, D), f32),
        scratch_shapes=[pltpu.VMEM((tm, D), bf16), pltpu.VMEM((tm, D), f32)],
        compiler_params=pltpu.CompilerParams(
            dimension_semantics=("parallel", "arbitrary"), vmem_limit_bytes=VMEM_LIMIT),
        name="ffn_mixer",
    )(x, sh, sc, gt, g, w1, w3, w2)


MOE_TILE = 1024
MOE_MAIN = 288
MOE_EXTRA = 128
MOE_ROWS = MOE_MAIN + -(-(MOE_TILE - MOE_MAIN) // MOE_EXTRA) * MOE_EXTRA


RANK_SEL = 4096


def _router_kernel(x_ref, sh_ref, sc_ref, g_ref, wr_ref, br_ref, tri_ref, triu_ref,
                   h_out, gates_out, rank_out, rank_t_out, cnt_out):
    h = _norm_mod(x_ref[...], g_ref[...], sh_ref[0], sc_ref[0])
    h_out[...] = h.astype(bf16)
    w = wr_ref[...]
    h_hi, w_hi = h.astype(bf16), w.astype(bf16)
    h_lo = (h - h_hi.astype(f32)).astype(bf16)
    w_lo = (w - w_hi.astype(f32)).astype(bf16)
    logits = _dot(h_hi, w_hi) + (_dot(h_hi, w_lo) + _dot(h_lo, w_hi)) + br_ref[...]
    lane = lax.broadcasted_iota(jnp.int32, logits.shape, 1)
    logits = jnp.where(lane < N_EXPERTS, logits, -jnp.inf)
    m1 = logits.max(axis=-1, keepdims=True)
    i1 = jnp.min(jnp.where(logits == m1, lane, 128), axis=-1, keepdims=True)
    rest = jnp.where(lane == i1, -jnp.inf, logits)
    m2 = rest.max(axis=-1, keepdims=True)
    i2 = jnp.min(jnp.where(rest == m2, lane, 128), axis=-1, keepdims=True)
    e2 = jnp.exp(m2 - m1)
    den = 1.0 / (1.0 + e2)
    gates_out[...] = jnp.where(lane == i1, den, jnp.where(lane == i2, e2 * den, 0.0))
    sel = jnp.logical_or(lane == i1, lane == i2)
    self32 = sel.astype(f32)
    selb = self32.astype(bf16)
    rank_out[...] = jnp.where(sel, _dot(tri_ref[...], selb), -1.0)
    rank_t_out[...] = _dot_tn(selb, triu_ref[...])
    cnt_out[0] = jnp.sum(self32, axis=0, keepdims=True).astype(jnp.int32)


def _moe_route(x, sh, sc, g, w_router, b_router):
    tm = MOE_TILE
    cidx = functools.partial(_cond_idx, tm=tm)
    t = jnp.arange(tm)
    tri = (t[:, None] > t[None, :]).astype(bf16)
    triu = jnp.where(t[:, None] == t[None, :], float(RANK_SEL), tri.T.astype(f32)).astype(bf16)
    mod = pl.BlockSpec((1, 1, D), lambda i: (cidx(i), 0, 0))
    return pl.pallas_call(
        _router_kernel,
        grid=(NT // tm,),
        in_specs=[
            pl.BlockSpec((tm, D), lambda i: (i, 0)),
            mod, mod,
            pl.BlockSpec((1, D), lambda i: (0, 0)),
            pl.BlockSpec((D, 128), lambda i: (0, 0)),
            pl.BlockSpec((1, 128), lambda i: (0, 0)),
            pl.BlockSpec((tm, tm), lambda i: (0, 0)),
            pl.BlockSpec((tm, tm), lambda i: (0, 0)),
        ],
        out_specs=[
            pl.BlockSpec((tm, D), lambda i: (i, 0)),
            pl.BlockSpec((tm, 128), lambda i: (i, 0)),
            pl.BlockSpec((tm, 128), lambda i: (i, 0)),
            pl.BlockSpec((128, tm), lambda i: (0, i)),
            pl.BlockSpec((1, 1, 128), lambda i: (i, 0, 0)),
        ],
        out_shape=[
            jax.ShapeDtypeStruct((NT, D), bf16),
            jax.ShapeDtypeStruct((NT, 128), f32),
            jax.ShapeDtypeStruct((NT, 128), f32),
            jax.ShapeDtypeStruct((128, NT), f32),
            jax.ShapeDtypeStruct((NT // tm, 1, 128), jnp.int32),
        ],
        compiler_params=pltpu.CompilerParams(
            dimension_semantics=("parallel",), vmem_limit_bytes=VMEM_LIMIT),
        name="moe_router",
    )(x, sh, sc, g, w_router, b_router, tri, triu)


def _moe_kernel(cnt_ref, h_ref, rank_ref, rank_t_ref, gates_ref, x_ref, gt_ref, w1_ref, w3_ref, w2_ref,
                o_ref, xc_scr, acc_scr, rcol_scr, gcol_scr):
    i = pl.program_id(0)
    e = pl.program_id(1)
    f = pl.program_id(2)
    cnt = cnt_ref[i * N_EXPERTS + e]
    n_extra = jnp.maximum(cnt - MOE_MAIN + MOE_EXTRA - 1, 0) // MOE_EXTRA

    def for_each_block(fn):
        fn(0, MOE_MAIN)

        def body(b, carry):
            fn(pl.multiple_of(MOE_MAIN + b * MOE_EXTRA, 32), MOE_EXTRA)
            return carry

        lax.fori_loop(0, n_extra, body, 0)

    @pl.when(jnp.logical_and(e == 0, f == 0))
    def _():
        o_ref[...] = x_ref[...]

    @pl.when(f == 0)
    def _():
        lane = lax.broadcasted_iota(jnp.int32, (MOE_TILE, 128), 1)
        mine = lane == e
        rcol_scr[...] = jnp.sum(jnp.where(mine, rank_ref[...], 0.0), axis=1, keepdims=True).astype(jnp.int32)
        gcol_scr[...] = jnp.sum(jnp.where(mine, gates_ref[...], 0.0), axis=1, keepdims=True)

        rrow = rank_t_ref[pl.ds(e, 1), :].astype(jnp.int32)

        def gather(slot0, nrows):
            slot = lax.broadcasted_iota(jnp.int32, (nrows, MOE_TILE), 0) + (slot0 + RANK_SEL)
            xc_scr[pl.ds(slot0, nrows), :] = _dot((rrow == slot).astype(bf16), h_ref[...]).astype(bf16)

        for_each_block(gather)

    def expert(slot0, nrows):
        rows = pl.ds(slot0, nrows)
        part = _dot(_swiglu_hidden(xc_scr[rows, :], w1_ref[0], w3_ref[0]), w2_ref[0])

        @pl.when(f == 0)
        def _():
            acc_scr[rows, :] = part

        @pl.when(f != 0)
        def _():
            acc_scr[rows, :] += part

    for_each_block(expert)

    @pl.when(f == pl.num_programs(2) - 1)
    def _():
        def scatter(slot0, nrows):
            out = acc_scr[pl.ds(slot0, nrows), :].astype(bf16)
            for t0 in range(0, MOE_TILE, 256):
                rows = slice(t0, t0 + 256)
                slot = lax.broadcasted_iota(jnp.int32, (256, nrows), 1) + slot0
                pt = (rcol_scr[rows, :] == slot).astype(bf16)
                o_ref[rows, :] += (gcol_scr[rows, :] * gt_ref[0]) * _dot(pt, out)

        for_each_block(scatter)


def _moe_experts(cnt, h_bf, rank, rank_t, gates, x, gt, w1, w3, w2):
    tm = MOE_TILE
    tf = EXPERT_DIM // 2
    cidx = functools.partial(_cond_idx, tm=tm)
    once = pl.Buffered(1)
    grid_spec = pltpu.PrefetchScalarGridSpec(
        num_scalar_prefetch=1,
        grid=(NT // tm, N_EXPERTS, EXPERT_DIM // tf),
        in_specs=[
            pl.BlockSpec((tm, D), lambda i, e, f, c: (i, 0), pipeline_mode=once),
            pl.BlockSpec((tm, 128), lambda i, e, f, c: (i, 0), pipeline_mode=once),
            pl.BlockSpec((128, tm), lambda i, e, f, c: (0, i), pipeline_mode=once),
            pl.BlockSpec((tm, 128), lambda i, e, f, c: (i, 0), pipeline_mode=once),
            pl.BlockSpec((tm, D), lambda i, e, f, c: (i, 0), pipeline_mode=once),
            pl.BlockSpec((1, 1, D), lambda i, e, f, c: (cidx(i), 0, 0)),
            pl.BlockSpec((1, D, tf), lambda i, e, f, c: (e, 0, f)),
            pl.BlockSpec((1, D, tf), lambda i, e, f, c: (e, 0, f)),
            pl.BlockSpec((1, tf, D), lambda i, e, f, c: (e, f, 0)),
        ],
        out_specs=pl.BlockSpec((tm, D), lambda i, e, f, c: (i, 0)),
        scratch_shapes=[pltpu.VMEM((MOE_ROWS, D), bf16), pltpu.VMEM((MOE_ROWS, D), f32),
                        pltpu.VMEM((tm, 1), jnp.int32), pltpu.VMEM((tm, 1), f32)],
    )
    return pl.pallas_call(
        _moe_kernel,
        grid_spec=grid_spec,
        out_shape=jax.ShapeDtypeStruct((NT, D), f32),
        compiler_params=pltpu.CompilerParams(
            dimension_semantics=("parallel", "arbitrary", "arbitrary"), vmem_limit_bytes=VMEM_LIMIT),
        name="moe_experts",
    )(cnt, h_bf, rank, rank_t, gates, x, gt, w1, w3, w2)


def _in_context_key(k, a, k_a):
    return k * (1.0 + (a - 1.0) * k_a)


def _rwkv_proj_kernel(x_ref, xp_ref, xn_ref, sh_ref, sc_ref, g_ref, mu_ref,
                      wr_ref, wk_ref, wv_ref, g1_ref, g2_ref, w1_ref, w2_ref, a1_ref, a2_ref,
                      w0_ref, a0_ref, kk_ref, ka_ref, rk_ref, gmat_ref,
                      r_out, v_out, kkn_out, bonus_out, gate_out,
                      lwf_out, lwb_out, af_out, ab_out, k_out, *, tm):
    i = pl.program_id(0)
    g, sh, sc = g_ref[...], sh_ref[0], sc_ref[0]
    h = _norm_mod(x_ref[...], g, sh, sc)
    n_prompt_tiles = NP // tm
    tiles_per_seq = SAMPLE_LEN // tm
    pos = (i - n_prompt_tiles) % tiles_per_seq
    has_prev = jnp.logical_and(i >= n_prompt_tiles, pos != 0)
    has_next = jnp.logical_and(i >= n_prompt_tiles, pos != tiles_per_seq - 1)
    h_before = jnp.where(has_prev, _norm_mod(xp_ref[...], g, sh, sc)[7:8], 0.0)
    h_after = jnp.where(has_next, _norm_mod(xn_ref[...], g, sh, sc)[0:1], 0.0)
    rowi = lax.broadcasted_iota(jnp.int32, (tm, 1), 0)
    h_prev = jnp.where(rowi == 0, h_before, pltpu.roll(h, 1, 0))
    h_next = jnp.where(rowi == tm - 1, h_after, pltpu.roll(h, tm - 1, 0))
    xx = 0.5 * (h_prev + h_next) - h
    mix = lambda n: (h + xx * mu_ref[n:n + 1]).astype(bf16)

    r = _dot(mix(0), wr_ref[...])
    k = _dot(mix(2), wk_ref[...])
    v = _dot(mix(3), wv_ref[...])
    gate_out[...] = _dot(_sigmoid(_dot(mix(5), g1_ref[...])).astype(bf16), g2_ref[...]).astype(bf16)

    lo = _lane_lo()
    tw = jnp.tanh(_dot(mix(1), w1_ref[...]))
    ta = _dot(mix(4), a1_ref[...])
    gmat = gmat_ref[...]
    kk = k * kk_ref[...]
    kkn_out[...] = kk * lax.rsqrt(_group_sum(kk * kk, gmat) + 1e-12)
    r_out[...] = r
    v_out[...] = v
    k_out[...] = k
    kd_sum = jnp.zeros_like(k)
    for d, (lw_out, a_out) in enumerate(((lwf_out, af_out), (lwb_out, ab_out))):
        keep = lo if d == 0 else jnp.logical_not(lo)
        zw = w0_ref[d:d + 1] + _dot(jnp.where(keep, tw, 0.0).astype(bf16), w2_ref[...])
        lw_out[...] = -math.exp(-0.5) * _sigmoid(zw)
        a = _sigmoid(a0_ref[d:d + 1] + _dot(jnp.where(keep, ta, 0.0).astype(bf16), a2_ref[...]))
        a_out[...] = a
        kd_sum = kd_sum + _in_context_key(k, a, ka_ref[...])
    bonus_out[...] = (_group_sum(r * kd_sum * rk_ref[...], gmat) * v).astype(bf16)


def _rwkv_projection(x, sh, sc, g, mu, wr, wk, wv, g1, g2, w1, w2, a1, a2, w0, a0, k_k, k_a, r_k, gmat):
    tm = 256
    cidx = functools.partial(_cond_idx, tm=tm)
    hb = tm // 8
    n8 = NT // 8
    full = lambda shape: pl.BlockSpec(shape, lambda i: tuple(0 for _ in shape))
    row = pl.BlockSpec((tm, D), lambda i: (i, 0))
    return pl.pallas_call(
        functools.partial(_rwkv_proj_kernel, tm=tm),
        grid=(NT // tm,),
        in_specs=[
            row,
            pl.BlockSpec((8, D), lambda i: (jnp.maximum(i * hb - 1, 0), 0)),
            pl.BlockSpec((8, D), lambda i: (jnp.minimum((i + 1) * hb, n8 - 1), 0)),
            pl.BlockSpec((1, 1, D), lambda i: (cidx(i), 0, 0)),
            pl.BlockSpec((1, 1, D), lambda i: (cidx(i), 0, 0)),
            full((1, D)), full((6, D)),
            full((D, D)), full((D, D)), full((D, D)),
            full((D, GATE_LORA_PAD)), full((GATE_LORA_PAD, D)),
            full((D, 2 * LORA)), full((2 * LORA, D)), full((D, 2 * LORA)), full((2 * LORA, D)),
            full((2, D)), full((2, D)), full((1, D)), full((1, D)), full((1, D)),
            full((256, 256)),
        ],
        out_specs=[row] * 10,
        out_shape=[jax.ShapeDtypeStruct((NT, D), dt) for dt in (f32, f32, f32, bf16, bf16) + (f32,) * 5],
        compiler_params=pltpu.CompilerParams(
            dimension_semantics=("parallel",), vmem_limit_bytes=VMEM_LIMIT),
        name="rwkv_projection",
    )(x, x, x, sh, sc, g, mu, wr, wk, wv, g1, g2, w1, w2, a1, a2, w0, a0, k_k, k_a, r_k, gmat)


def _split3(x):
    x1 = x.astype(bf16)
    r1 = x - x1.astype(f32)
    x2 = r1.astype(bf16)
    x3 = (r1 - x2.astype(f32)).astype(bf16)
    return x1, x2, x3


def _scan_kernel(*refs, nc, has_init, emit_state):
    fwd_refs = refs[0:6]
    bwd_refs = refs[6:12]
    ka_ref, tri_ref, mask_ref = refs[12:15]
    pos = 15
    s0_ref = None
    if has_init:
        s0_ref = refs[pos]
        pos += 1
    yf_ref, yb_ref = refs[pos:pos + 2]
    pos += 2
    sout_ref = None
    if emit_state:
        sout_ref = refs[pos]
        pos += 1
    st_scr, cl_scr = refs[pos:pos + 2]

    s = pl.program_id(1)

    @pl.when(s == 0)
    def _():
        if has_init:
            st_scr[...] = s0_ref[0]
        else:
            st_scr[...] = jnp.zeros_like(st_scr)

    c = CHUNK
    for d, drefs in enumerate((fwd_refs, bwd_refs)):
        lw = drefs[3][...]
        tri = tri_ref[d]
        p1, p2, p3 = _split3(lw)
        cl_scr[d] = _dot(tri, p1) + _dot(tri, p2) + _dot(tri, p3)

    lane = lax.broadcasted_iota(jnp.int32, (1, 128), 1)
    m0 = (lane < HD).astype(f32)
    m1 = 1.0 - m0
    rid = lax.broadcasted_iota(jnp.int32, (128, 128), 0)
    cid = lax.broadcasted_iota(jnp.int32, (128, 128), 1)
    eye = (rid == cid).astype(f32)

    n_pairs = RW_HEADS // 2
    nb = 2 * n_pairs

    def per_head_rows(x):
        return jnp.concatenate([x * m0, x * m1], axis=0)

    ars_l, bk_l, bkh_l, v2_l, dec_l = [], [], [], [], []
    for d, drefs in enumerate((fwd_refs, bwd_refs)):
        r_ref, v_ref, kk_ref, lw_ref, a_ref, k_ref = drefs
        end_row = c - 1 if d == 0 else 0
        for p in range(n_pairs):
            ln = slice(p * 128, (p + 1) * 128)
            cl = cl_scr[d, :, ln]
            kk = kk_ref[:, ln]
            tot = cl[end_row:end_row + 1]
            e_inv = jnp.exp(-cl)
            e_end = jnp.exp(tot - cl)
            a = a_ref[:, ln]
            kka = kk * a
            kd = _in_context_key(k_ref[:, ln], a, ka_ref[:, ln])
            at = per_head_rows(-kk * jnp.exp(cl - lw_ref[:, ln]))
            rt = per_head_rows(r_ref[:, ln] * jnp.exp(cl))
            ars_l.append(jnp.concatenate([at, rt], axis=0).astype(bf16))
            bk_l.append(jnp.concatenate([per_head_rows(kka * e_inv), per_head_rows(kd * e_inv)],
                                        axis=0).astype(bf16))
            bkh_l.append(jnp.concatenate([per_head_rows(kka * e_end), per_head_rows(kd * e_end)],
                                         axis=0).astype(bf16))
            v2_l.append(per_head_rows(v_ref[:, ln]).astype(bf16))
            dec_l.append(jnp.exp(tot))
    ar = jnp.stack(ars_l)
    bk = jnp.stack(bk_l)
    bkh = jnp.stack(bkh_l)
    v2 = jnp.stack(v2_l)
    dec = jnp.stack(dec_l)

    st = st_scr[...].reshape(nb, 128, 128)
    g2 = _bdot_nt(ar, bk).reshape(2, n_pairs, 256, 256) * mask_ref[...][:, None]
    g2 = g2.reshape(nb, 256, 256)
    ars = _bdot_nt(ar, st.astype(bf16))
    gv = _bdot(g2[:, :, 128:].astype(bf16), v2)
    l_bd = g2[:, :128, :128]
    l_bf = l_bd.astype(bf16)
    pk = _bdot(l_bf, l_bf)
    q = eye[None] + l_bd
    for it in range(5):
        pkb = pk.astype(bf16)
        if it < 4:
            res = _bdot(jnp.concatenate([q, pk], axis=1).astype(bf16), pkb)
            q = q + res[:, :128]
            pk = res[:, 128:]
        else:
            q = q + _bdot(q.astype(bf16), pkb)
    rhs = ars[:, :128] + gv[:, :128]
    u2b = _bdot(q.astype(bf16), rhs.astype(bf16)).astype(bf16)
    y2 = ars[:, 128:] + gv[:, 128:] + _bdot(g2[:, 128:, :128].astype(bf16), u2b)
    y = y2[:, :c] + y2[:, c:]
    for d, y_ref in enumerate((yf_ref, yb_ref)):
        for p in range(n_pairs):
            y_ref[:, p * 128:(p + 1) * 128] = y[d * n_pairs + p]
    uv = jnp.concatenate([u2b, v2], axis=1)
    st_scr[...] = (st * dec + _bdot_tn(uv, bkh)).reshape(2, n_pairs, 128, 128)

    if emit_state:
        @pl.when(s == nc - 1)
        def _():
            for d in range(2):
                for p in range(n_pairs):
                    m = st_scr[d, p]
                    sout_ref[0, d, 2 * p] = m[:HD, :HD]
                    sout_ref[0, d, 2 * p + 1] = pltpu.roll(m[HD:, :], HD, 1)[:, :HD]


def _rwkv_scan(streams_f, streams_b, k_a, tri, mask, s0_bd, *, n_seq, seq_len, row0, emit_state):
    nc = seq_len // CHUNK
    blk0 = row0 // CHUNK
    fwd_spec = pl.BlockSpec((CHUNK, D), lambda b, s: (blk0 + b * nc + s, 0))
    bwd_spec = pl.BlockSpec((CHUNK, D), lambda b, s: (blk0 + b * nc + nc - 1 - s, 0))
    in_specs = [fwd_spec] * 6 + [bwd_spec] * 6 + [
        pl.BlockSpec((1, D), lambda b, s: (0, 0)),
        pl.BlockSpec((2, CHUNK, CHUNK), lambda b, s: (0, 0, 0)),
        pl.BlockSpec((2, 256, 256), lambda b, s: (0, 0, 0)),
    ]
    args = list(streams_f) + list(streams_b) + [k_a, tri, mask]
    state_block = (1, 2, RW_HEADS // 2, 128, 128)
    if s0_bd is not None:
        in_specs.append(pl.BlockSpec(state_block, lambda b, s: (b, 0, 0, 0, 0)))
        args.append(s0_bd)
    out_specs = [pl.BlockSpec((CHUNK, D), lambda b, s: (b * nc + s, 0)),
                 pl.BlockSpec((CHUNK, D), lambda b, s: (b * nc + nc - 1 - s, 0))]
    out_shape = [jax.ShapeDtypeStruct((n_seq * seq_len, D), f32)] * 2
    if emit_state:
        out_specs.append(pl.BlockSpec((1, 2, RW_HEADS, HD, HD), lambda b, s: (b, 0, 0, 0, 0)))
        out_shape.append(jax.ShapeDtypeStruct((n_seq, 2, RW_HEADS, HD, HD), f32))
    return pl.pallas_call(
        functools.partial(_scan_kernel, nc=nc, has_init=s0_bd is not None, emit_state=emit_state),
        grid=(n_seq, nc),
        in_specs=in_specs,
        out_specs=out_specs,
        out_shape=out_shape,
        scratch_shapes=[pltpu.VMEM(state_block[1:], f32), pltpu.VMEM((2, CHUNK, D), f32)],
        compiler_params=pltpu.CompilerParams(
            dimension_semantics=("parallel", "arbitrary"), vmem_limit_bytes=VMEM_LIMIT),
        name="rwkv_scan_prompt" if emit_state else "rwkv_scan_sample",
    )(*args)


def _scan_constants():
    t = jnp.arange(CHUNK)
    lower = (t[:, None] >= t[None, :])
    tri = jnp.stack([lower, lower.T]).astype(bf16)
    masks = []
    for d in range(2):
        strict = (t[:, None] > t[None, :]) if d == 0 else (t[:, None] < t[None, :])
        incl = lower if d == 0 else lower.T
        blocks = []
        for m in (strict, incl):
            bd = jnp.kron(jnp.eye(2, dtype=f32), m.astype(f32))
            blocks.append(jnp.concatenate([bd, bd], axis=1))
        masks.append(jnp.concatenate(blocks, axis=0))
    return tri, jnp.stack(masks)


def _state_to_blockdiag(s):
    n = s.shape[0]
    s = s.reshape(n, 2, RW_HEADS // 2, 2, HD, HD)
    z = jnp.zeros_like(s[:, :, :, 0])
    top = jnp.concatenate([s[:, :, :, 0], z], axis=-1)
    bot = jnp.concatenate([z, s[:, :, :, 1]], axis=-1)
    return jnp.concatenate([top, bot], axis=-2)


def _rope_tables():
    t = jnp.arange(SAMPLE_LEN)
    rows = (t // GRID_W).astype(f32)
    cols = (t % GRID_W).astype(f32)
    nf = HD // 4
    inv = 10000.0 ** (-jnp.arange(nf, dtype=f32) / nf)
    ang = jnp.concatenate([rows[:, None] * inv, cols[:, None] * inv], axis=-1)
    cos = jnp.repeat(jnp.cos(ang), 2, axis=-1)
    sin = jnp.repeat(jnp.sin(ang), 2, axis=-1) * jnp.tile(jnp.array([-1.0, 1.0], f32), HD // 2)
    return jnp.tile(cos, (1, 2)), jnp.tile(sin, (1, 2))


def kernel(x_prompt, x_sample, cache_diff_k, cache_diff_v, cache_na_k, cache_na_v, state_rwkv, c, c_ctx, w_ada, b_ada, g_mix, g_ffn, w_in, w_out, diff_q_g, diff_k_g, diff_lam_q1, diff_lam_k1, diff_lam_q2, diff_lam_k2, diff_subln_g, na_q_g, na_k_g, na_rpb, ffn_w1, ffn_w3, ffn_w2, rw_mu, rw_wr, rw_wk, rw_wv, rw_wo, rw_w0, rw_w1, rw_w2, rw_a0, rw_a1, rw_a2, rw_g1, rw_g2, rw_k_k, rw_k_a, rw_r_k, rw_ln_g, rw_ln_b, moe_router, moe_router_b, moe_w1, moe_w3, moe_w2):
    xp = x_prompt.reshape(NP, D)
    xs = x_sample.reshape(NS, D)
    cond8 = jnp.concatenate([c_ctx[None, :], c, jnp.zeros((3, D), f32)], axis=0)
    mod = _ada_table(cond8, w_ada, b_ada)
    gmat = jnp.kron(jnp.eye(4, dtype=f32), jnp.ones((HD, HD), f32)).astype(bf16)

    lam_init = 0.8 - 0.6 * math.exp(-0.3 * 0)
    ones_seg = jnp.ones((SEG,), f32)
    tile8 = lambda gvec: jnp.tile(gvec, SEG // HD)
    gains = jnp.stack([tile8(diff_q_g[0]), tile8(diff_k_g[0]), ones_seg,
                       tile8(na_q_g[0]), tile8(na_k_g[0]), ones_seg]).reshape(6, 1, SEG)
    cos_t, sin_t = _rope_tables()
    proj, dk_p, dv_p, nk_p, nv_p = _in_projection(
        xp, xs, mod[0][0], mod[0][1], g_mix[0][None, :], w_in[0].astype(bf16), gains, gmat, cos_t, sin_t)
    lamp = jnp.stack([diff_lam_q1[0], diff_lam_k1[0], diff_lam_q2[0], diff_lam_k2[0]])
    subg = diff_subln_g[0][None, :]
    o_prompt = _prompt_attention(proj, lamp, subg, lam_init)
    o_diff = _latent_diff_attention(proj, cache_diff_k[:, 0].reshape(N_SAMPLE_SEQ, PAST, SEG),
                                    cache_diff_v[:, 0].reshape(N_SAMPLE_SEQ, PAST, SEG), lamp, subg, lam_init)
    o_na = _latent_na_attention(proj, cache_na_k[:, 0].reshape(N_SAMPLE_SEQ, PAST, SEG),
                                cache_na_v[:, 0].reshape(N_SAMPLE_SEQ, PAST, SEG), _rpb_table(na_rpb[0]))
    x = _out_projection(o_prompt, o_diff, o_na, w_out[0].astype(bf16), xp, xs, mod[0][2])
    x = _dense_ffn(x, mod[0][3], mod[0][4], mod[0][5], g_ffn[0][None, :],
                   ffn_w1[0].astype(bf16), ffn_w3[0].astype(bf16), ffn_w2[0].astype(bf16))

    pad_g = GATE_LORA_PAD - GATE_LORA
    g1 = jnp.pad(rw_g1[0], ((0, 0), (0, pad_g))).astype(bf16)
    g2 = jnp.pad(rw_g2[0], ((0, pad_g), (0, 0))).astype(bf16)
    w1cat = jnp.concatenate([rw_w1[0, 0], rw_w1[0, 1]], axis=1).astype(bf16)
    w2cat = jnp.concatenate([rw_w2[0, 0], rw_w2[0, 1]], axis=0).astype(bf16)
    a1cat = jnp.concatenate([rw_a1[0, 0], rw_a1[0, 1]], axis=1).astype(bf16)
    a2cat = jnp.concatenate([rw_a2[0, 0], rw_a2[0, 1]], axis=0).astype(bf16)
    k_a = rw_k_a[0][None, :]
    (r, v, kkn, bonus, gate_lora, lwf, lwb, af, ab, k) = _rwkv_projection(
        x, mod[1][0], mod[1][1], g_mix[1][None, :], rw_mu[0],
        rw_wr[0].astype(bf16), rw_wk[0].astype(bf16), rw_wv[0].astype(bf16), g1, g2,
        w1cat, w2cat, a1cat, a2cat, rw_w0[0], rw_a0[0],
        rw_k_k[0][None, :], k_a, rw_r_k[0].reshape(1, D), gmat)
    tri, mask = _scan_constants()
    streams_f = (r, v, kkn, lwf, af, k)
    streams_b = (r, v, kkn, lwb, ab, k)
    yf_p, yb_p, st_p = _rwkv_scan(streams_f, streams_b, k_a, tri, mask, None,
                                  n_seq=N_PROMPT_SEQ, seq_len=PROMPT_LEN, row0=0, emit_state=True)
    yf_s, yb_s = _rwkv_scan(streams_f, streams_b, k_a, tri, mask, _state_to_blockdiag(state_rwkv[:, 0]),
                            n_seq=N_SAMPLE_SEQ, seq_len=SAMPLE_LEN, row0=NP, emit_state=False)
    x = _rwkv_out_projection(yf_p, yb_p, yf_s, yb_s, bonus, gate_lora, rw_ln_g[0][None, :], rw_ln_b[0][None, :], gmat,
                             rw_wo[0].astype(bf16), x, mod[1][2])
    w_router = jnp.pad(moe_router[0], ((0, 0), (0, 128 - N_EXPERTS)))
    b_router = jnp.pad(moe_router_b[0], (0, 128 - N_EXPERTS))[None, :]
    h_bf, gates, rank, rank_t, cnt = _moe_route(x, mod[1][3], mod[1][4], g_ffn[1][None, :], w_router, b_router)
    x = _moe_experts(cnt[:, 0, :N_EXPERTS].reshape(-1), h_bf, rank, rank_t, gates, x, mod[1][5],
                     moe_w1[0].astype(bf16), moe_w3[0].astype(bf16), moe_w2[0].astype(bf16))

    new_dk = dk_p.reshape(N_PROMPT_SEQ, 1, PROMPT_LEN, DIFF_HEADS, 2 * HD)
    new_dv = dv_p.reshape(N_PROMPT_SEQ, 1, PROMPT_LEN, DIFF_HEADS, 2 * HD)
    new_nk = nk_p.reshape(N_PROMPT_SEQ, 1, PROMPT_LEN, NA_HEADS, HD)
    new_nv = nv_p.reshape(N_PROMPT_SEQ, 1, PROMPT_LEN, NA_HEADS, HD)
    new_state = st_p[:, None]
    return (x[:NP].reshape(N_PROMPT_SEQ, PROMPT_LEN, D), x[NP:].reshape(N_SAMPLE_SEQ, SAMPLE_LEN, D),
            new_dk, new_dv, new_nk, new_nv, new_state)
```

```python
import functools
import math

import jax
import jax.numpy as jnp
from jax import lax
from jax.experimental import pallas as pl
from jax.experimental.pallas import tpu as pltpu

f32 = jnp.float32
bf16 = jnp.bfloat16

D = 1024
N_PROMPT_SEQ, PROMPT_LEN = 32, 256
N_SAMPLE_SEQ, SAMPLE_LEN = 4, 2048
NP = N_PROMPT_SEQ * PROMPT_LEN
NS = N_SAMPLE_SEQ * SAMPLE_LEN
NT = NP + NS
PAST = 256
GRID_W = 64
GRID_R = SAMPLE_LEN // GRID_W
HD = 64
DIFF_HEADS = 4
NA_HEADS = 8
NA_KH = 8
NA_KW = 16
SEG = 512
IN_COLS = 6 * SEG
FFN_DIM = 2816
N_EXPERTS = 8
EXPERT_DIM = 3584
RW_HEADS = 16
LORA = 64
GATE_LORA = 160
GATE_LORA_PAD = 256
EPS = 1e-6
GN_EPS = 64e-5
NEG_BIG = -1e30
DIFF_HEADS_PER_STEP = 2
NA_ROWS_PER_STEP = 8
CHUNK = 64
VMEM_LIMIT = 56 * 1024 * 1024

ADA_COLS = 1536
IN_PROJ_ROWS = 1024
DIFF_Q_ROWS = 256
OUT_PROJ_ROWS = 512
FFN_ROWS = 512
FFN_COLS = FFN_DIM // 2
RWKV_PROJ_ROWS = 256
RWKV_OUT_ROWS = 512


def _cond_idx(i, tm):
    return jnp.maximum((i * tm) // SAMPLE_LEN - (NP // SAMPLE_LEN - 1), 0)


def _dot(a, b):
    return jnp.dot(a, b, preferred_element_type=f32)


def _dot_nt(a, b):
    return lax.dot_general(a, b, (((1,), (1,)), ((), ())), preferred_element_type=f32)


def _dot_tn(a, b):
    return lax.dot_general(a, b, (((0,), (0,)), ((), ())), preferred_element_type=f32)


def _bdot(a, b):
    return lax.dot_general(a, b, (((2,), (1,)), ((0,), (0,))), preferred_element_type=f32)


def _bdot_nt(a, b):
    return lax.dot_general(a, b, (((2,), (2,)), ((0,), (0,))), preferred_element_type=f32)


def _bdot_tn(a, b):
    return lax.dot_general(a, b, (((1,), (1,)), ((0,), (0,))), preferred_element_type=f32)


def _sigmoid(x):
    return 1.0 / (1.0 + jnp.exp(-x))


def _norm_mod(x, g, sh, sc):
    ms = jnp.mean(x * x, axis=-1, keepdims=True)
    return (x * lax.rsqrt(ms + EPS) * g) * (1.0 + sc) + sh


def _group_sum(x, gmat):
    xb = x.astype(bf16)
    cols = [_dot(xb[:, c * 256:(c + 1) * 256], gmat) for c in range(x.shape[1] // 256)]
    return cols[0] if len(cols) == 1 else jnp.concatenate(cols, axis=1)


def _softmax_parts(parts):
    m = parts[0].max(axis=-1, keepdims=True)
    for p in parts[1:]:
        m = jnp.maximum(m, p.max(axis=-1, keepdims=True))
    es = [jnp.exp(p - m) for p in parts]
    l = es[0].sum(axis=-1, keepdims=True)
    for e in es[1:]:
        l = l + e.sum(axis=-1, keepdims=True)
    inv = 1.0 / l
    return [e * inv for e in es]


def _lane_lo(n=128):
    return lax.broadcasted_iota(jnp.int32, (1, n), 1) < HD


def _stack_halves(q):
    lo = _lane_lo()
    return jnp.concatenate([jnp.where(lo, q, 0.0), jnp.where(lo, 0.0, q)], axis=0)


def _ada_kernel(cond_ref, w_ref, b_ref, o_ref):
    x = cond_ref[...]
    s = x * _sigmoid(x)
    o_ref[0] = _dot(s.astype(bf16), w_ref[0].astype(bf16)) + b_ref[0]


def _ada_table(cond8, w_ada, b_ada):
    depth = w_ada.shape[0]
    tn = ADA_COLS
    out = pl.pallas_call(
        _ada_kernel,
        grid=(depth, 6 * D // tn),
        in_specs=[
            pl.BlockSpec((8, D), lambda l, n: (0, 0)),
            pl.BlockSpec((1, D, tn), lambda l, n: (l, 0, n)),
            pl.BlockSpec((1, 1, tn), lambda l, n: (l, 0, n)),
        ],
        out_specs=pl.BlockSpec((1, 8, tn), lambda l, n: (l, 0, n)),
        out_shape=jax.ShapeDtypeStruct((depth, 8, 6 * D), f32),
        compiler_params=pltpu.CompilerParams(vmem_limit_bytes=VMEM_LIMIT),
        name="ada_table",
    )(cond8, w_ada, b_ada.reshape(depth, 1, 6 * D))
    out = out.reshape(depth, 8, 6, D)
    return [[out[l, :, k, :].reshape(8, 1, D) for k in range(6)] for l in range(depth)]


def _qk_norm(y, gain, gmat):
    ss = _group_sum(y * y, gmat) * (1.0 / HD)
    return y * lax.rsqrt(ss + EPS) * gain


def _rope(y, cos, sin):
    even = (lax.broadcasted_iota(jnp.int32, (1, 128), 1) % 2) == 0
    outs = []
    for c in range(y.shape[1] // 128):
        yc = y[:, c * 128:(c + 1) * 128]
        swapped = jnp.where(even, pltpu.roll(yc, 127, 1), pltpu.roll(yc, 1, 1))
        outs.append(yc * cos + swapped * sin)
    return jnp.concatenate(outs, axis=1)


def _inproj_kernel(xp_ref, xs_ref, sh_ref, sc_ref, g_ref, w_ref, gain_ref, gmat_ref, cos_ref, sin_ref,
                   o_ref, dk_ref, dv_ref, nk_ref, nv_ref, h_scr, *, tm):
    i = pl.program_id(0)
    j = pl.program_id(1)

    def emit(val):
        o_ref[...] = val.astype(bf16)
        for seg, cache_ref in ((1, dk_ref), (2, dv_ref), (4, nk_ref), (5, nv_ref)):
            @pl.when(jnp.logical_and(j == seg, i < NP // tm))
            def _():
                cache_ref[...] = val

    for x_ref, active in ((xp_ref, i < NP // tm), (xs_ref, i >= NP // tm)):
        @pl.when(jnp.logical_and(j == 0, active))
        def _():
            h_scr[...] = _norm_mod(x_ref[...], g_ref[...], sh_ref[0], sc_ref[0]).astype(bf16)

    y = _dot(h_scr[...], w_ref[...])
    is_norm = jnp.logical_and(j != 2, j != 5)
    is_rope = jnp.logical_and(j < 2, i >= NP // tm)

    @pl.when(jnp.logical_not(is_norm))
    def _():
        emit(y)

    @pl.when(jnp.logical_and(is_norm, jnp.logical_not(is_rope)))
    def _():
        emit(_qk_norm(y, gain_ref[0], gmat_ref[...]))

    @pl.when(is_rope)
    def _():
        o_ref[...] = _rope(_qk_norm(y, gain_ref[0], gmat_ref[...]), cos_ref[...], sin_ref[...]).astype(bf16)


def _in_projection(xp, xs, sh, sc, g, w_bf, gains, gmat, cos_t, sin_t):
    tm = IN_PROJ_ROWS
    n_prompt_tiles = NP // tm
    tiles_per_seq = SAMPLE_LEN // tm
    cidx = functools.partial(_cond_idx, tm=tm)
    rope_idx = lambda i, j: (jnp.maximum(i - n_prompt_tiles, 0) % tiles_per_seq, 0)
    return pl.pallas_call(
        functools.partial(_inproj_kernel, tm=tm),
        grid=(NT // tm, IN_COLS // SEG),
        in_specs=[
            pl.BlockSpec((tm, D), lambda i, j: (jnp.minimum(i, n_prompt_tiles - 1), 0)),
            pl.BlockSpec((tm, D), lambda i, j: (jnp.maximum(i - n_prompt_tiles, 0), 0)),
            pl.BlockSpec((1, 1, D), lambda i, j: (cidx(i), 0, 0)),
            pl.BlockSpec((1, 1, D), lambda i, j: (cidx(i), 0, 0)),
            pl.BlockSpec((1, D), lambda i, j: (0, 0)),
            pl.BlockSpec((D, SEG), lambda i, j: (0, j)),
            pl.BlockSpec((1, 1, SEG), lambda i, j: (j, 0, 0)),
            pl.BlockSpec((256, 256), lambda i, j: (0, 0)),
            pl.BlockSpec((tm, 128), rope_idx),
            pl.BlockSpec((tm, 128), rope_idx),
        ],
        out_specs=[pl.BlockSpec((tm, SEG), lambda i, j: (i, j))] + [
            pl.BlockSpec((tm, SEG), lambda i, j: (jnp.minimum(i, n_prompt_tiles - 1), 0))] * 4,
        out_shape=[jax.ShapeDtypeStruct((NT, IN_COLS), bf16)] + [jax.ShapeDtypeStruct((NP, SEG), f32)] * 4,
        scratch_shapes=[pltpu.VMEM((tm, D), bf16)],
        compiler_params=pltpu.CompilerParams(
            dimension_semantics=("arbitrary", "arbitrary"), vmem_limit_bytes=VMEM_LIMIT),
        name="in_projection",
    )(xp, xs, sh, sc, g, w_bf, gains, gmat, cos_t, sin_t)


def _lambda_value(lamp_ref, lam_init):
    lp = lamp_ref[...]
    e1 = jnp.exp(jnp.sum(lp[0:1] * lp[1:2], axis=-1, keepdims=True))
    e2 = jnp.exp(jnp.sum(lp[2:3] * lp[3:4], axis=-1, keepdims=True))
    return e1 - e2 + lam_init


def _sub_ln(o, subg, lam_init):
    ms = jnp.mean(o * o, axis=-1, keepdims=True)
    return o * lax.rsqrt(ms + EPS) * subg * (1.0 - lam_init)


def _prompt_attn_kernel(p_ref, lamp_ref, subg_ref, o_ref, *, lam_init):
    lam = _lambda_value(lamp_ref, lam_init)
    t = PROMPT_LEN
    lo = _lane_lo()
    scale = HD ** -0.5
    for h in range(DIFF_HEADS):
        q = p_ref[:, h * 128:(h + 1) * 128]
        k = p_ref[:, SEG + h * 128:SEG + (h + 1) * 128].astype(bf16)
        v = p_ref[:, 2 * SEG + h * 128:2 * SEG + (h + 1) * 128].astype(bf16)
        s = _dot_nt((_stack_halves(q) * scale).astype(bf16), k)
        (p,) = _softmax_parts([s])
        pd = p[:t] - lam * p[t:]
        o = _dot(pd.astype(bf16), v)
        o_ref[:, h * 128:(h + 1) * 128] = _sub_ln(o, subg_ref[...], lam_init).astype(bf16)
    for hp in range(NA_HEADS // 2):
        q = p_ref[:, 3 * SEG + hp * 128:3 * SEG + (hp + 1) * 128]
        k = p_ref[:, 4 * SEG + hp * 128:4 * SEG + (hp + 1) * 128].astype(bf16)
        v = p_ref[:, 5 * SEG + hp * 128:5 * SEG + (hp + 1) * 128].astype(bf16)
        s = _dot_nt((_stack_halves(q) * scale).astype(bf16), k)
        (p,) = _softmax_parts([s])
        o = _dot(p.astype(bf16), v)
        o_ref[:, SEG + hp * 128:SEG + (hp + 1) * 128] = jnp.where(lo, o[:t], o[t:]).astype(bf16)


def _prompt_attention(proj, lamp, subg, lam_init):
    return pl.pallas_call(
        functools.partial(_prompt_attn_kernel, lam_init=lam_init),
        grid=(N_PROMPT_SEQ,),
        in_specs=[
            pl.BlockSpec((PROMPT_LEN, IN_COLS), lambda b: (b, 0)),
            pl.BlockSpec((4, HD), lambda b: (0, 0)),
            pl.BlockSpec((1, 128), lambda b: (0, 0)),
        ],
        out_specs=pl.BlockSpec((PROMPT_LEN, D), lambda b: (b, 0)),
        out_shape=jax.ShapeDtypeStruct((NP, D), bf16),
        compiler_params=pltpu.CompilerParams(
            dimension_semantics=("parallel",), vmem_limit_bytes=VMEM_LIMIT),
        name="prompt_attention",
    )(proj, lamp, subg)


def _latent_diff_kernel(q_ref, kn_ref, vn_ref, kc_ref, vc_ref, lamp_ref, subg_ref, o_ref,
                        *, lam_init, tq):
    lam = _lambda_value(lamp_ref, lam_init)
    scale = HD ** -0.5
    for hh in range(DIFF_HEADS_PER_STEP):
        ln = slice(hh * 128, (hh + 1) * 128)
        qq = (_stack_halves(q_ref[:, ln]) * scale).astype(bf16)
        s_c = _dot_nt(qq, kc_ref[0, :, ln].astype(bf16))
        s_n = _dot_nt(qq, kn_ref[:, ln].astype(bf16))
        p_c, p_n = _softmax_parts([s_c, s_n])
        pd_c = p_c[:tq] - lam * p_c[tq:]
        pd_n = p_n[:tq] - lam * p_n[tq:]
        o = (_dot(pd_c.astype(bf16), vc_ref[0, :, ln].astype(bf16))
             + _dot(pd_n.astype(bf16), vn_ref[:, ln].astype(bf16)))
        o_ref[:, ln] = _sub_ln(o, subg_ref[...], lam_init).astype(bf16)


def _latent_diff_attention(proj, cache_k, cache_v, lamp, subg, lam_init):
    tq = DIFF_Q_ROWS
    nqb = SAMPLE_LEN // tq
    q0 = NP // tq
    s0 = NP // SAMPLE_LEN
    hw = 128 * DIFF_HEADS_PER_STEP
    kcol = SEG // hw
    return pl.pallas_call(
        functools.partial(_latent_diff_kernel, lam_init=lam_init, tq=tq),
        grid=(N_SAMPLE_SEQ, DIFF_HEADS // DIFF_HEADS_PER_STEP, nqb),
        in_specs=[
            pl.BlockSpec((tq, hw), lambda b, h, q: (q0 + b * nqb + q, h)),
            pl.BlockSpec((SAMPLE_LEN, hw), lambda b, h, q: (s0 + b, kcol + h)),
            pl.BlockSpec((SAMPLE_LEN, hw), lambda b, h, q: (s0 + b, 2 * kcol + h)),
            pl.BlockSpec((1, PAST, hw), lambda b, h, q: (b, 0, h)),
            pl.BlockSpec((1, PAST, hw), lambda b, h, q: (b, 0, h)),
            pl.BlockSpec((4, HD), lambda b, h, q: (0, 0)),
            pl.BlockSpec((1, 128), lambda b, h, q: (0, 0)),
        ],
        out_specs=pl.BlockSpec((tq, hw), lambda b, h, q: (b * nqb + q, h)),
        out_shape=jax.ShapeDtypeStruct((NS, SEG), bf16),
        compiler_params=pltpu.CompilerParams(
            dimension_semantics=("parallel", "parallel", "arbitrary"), vmem_limit_bytes=VMEM_LIMIT),
        name="latent_diff_attention",
    )(proj, proj, proj, cache_k, cache_v, lamp, subg)


def _rpb_table_kernel(rpb_ref, o_ref):
    h = pl.program_id(0)
    wq = lax.broadcasted_iota(jnp.int32, (GRID_W, GRID_W), 0)
    wk = lax.broadcasted_iota(jnp.int32, (GRID_W, GRID_W), 1)
    col_start = jnp.clip(wq - NA_KW // 2, 0, GRID_W - NA_KW)
    col_in = jnp.logical_and(wk >= col_start, wk < col_start + NA_KW)
    col_off = jnp.clip(wk - wq, -(NA_KW - 1), NA_KW - 1) + (NA_KW - 1)
    n_dr = 2 * NA_KH - 1
    n_dc = 2 * NA_KW - 1
    for dr in range(n_dr):
        t = jnp.zeros((GRID_W, GRID_W), f32)
        for c in range(n_dc):
            t = jnp.where(col_off == c, rpb_ref[h * (n_dr * n_dc) + dr * n_dc + c], t)
        o_ref[0, dr] = jnp.where(col_in, t, NEG_BIG)


def _rpb_table(rpb):
    n_dr = 2 * NA_KH - 1
    tcol = pl.pallas_call(
        _rpb_table_kernel,
        grid=(NA_HEADS,),
        in_specs=[pl.BlockSpec(memory_space=pltpu.SMEM)],
        out_specs=pl.BlockSpec((1, n_dr, GRID_W, GRID_W), lambda h: (h, 0, 0, 0)),
        out_shape=jax.ShapeDtypeStruct((NA_HEADS, n_dr, GRID_W, GRID_W), f32),
        name="rpb_table",
    )(rpb.reshape(-1))
    return jnp.stack(
        [jnp.concatenate([tcol[:, j - s + NA_KH - 1] for j in range(NA_KH)], axis=-1) for s in range(NA_KH)],
        axis=1)


def _latent_na_kernel(q_ref, k_ref, v_ref, kc_ref, vc_ref, bias_ref, o_ref):
    scale = HD ** -0.5
    lo = _lane_lo()
    kc = kc_ref[0].astype(bf16)
    vc = vc_ref[0].astype(bf16)
    win = NA_KH * GRID_W

    nr = NA_ROWS_PER_STEP
    kcb = jnp.broadcast_to(kc[None], (nr,) + kc.shape)
    vcb = jnp.broadcast_to(vc[None], (nr,) + vc.shape)

    def rows(g, carry):
        qs, kws, vws, biases = [], [], [], []
        for t in range(nr):
            r = g * nr + t
            rs = jnp.clip(r - NA_KH // 2, 0, GRID_R - NA_KH)
            sidx = r - rs
            q = q_ref[pl.ds(pl.multiple_of(r * GRID_W, GRID_W), GRID_W), :]
            qs.append((_stack_halves(q) * scale).astype(bf16))
            k0 = pl.multiple_of(rs * GRID_W, GRID_W)
            kws.append(k_ref[pl.ds(k0, win), :].astype(bf16))
            vws.append(v_ref[pl.ds(k0, win), :].astype(bf16))
            biases.append(jnp.concatenate([bias_ref[0, sidx], bias_ref[1, sidx]], axis=0))
        qq = jnp.stack(qs)
        s_loc = _bdot_nt(qq, jnp.stack(kws)) + jnp.stack(biases)
        s_ctx = _bdot_nt(qq, kcb)
        p_loc, p_ctx = _softmax_parts([s_loc, s_ctx])
        o = _bdot(p_loc.astype(bf16), jnp.stack(vws)) + _bdot(p_ctx.astype(bf16), vcb)
        o = jnp.where(lo, o[:, :GRID_W], o[:, GRID_W:]).astype(bf16)
        row0 = pl.multiple_of(g * (nr * GRID_W), nr * GRID_W)
        o_ref[pl.ds(row0, nr * GRID_W), :] = o.reshape(nr * GRID_W, 128)
        return carry

    lax.fori_loop(0, GRID_R // nr, rows, 0)


def _latent_na_attention(proj, cache_k, cache_v, bias):
    s0 = NP // SAMPLE_LEN
    return pl.pallas_call(
        _latent_na_kernel,
        grid=(N_SAMPLE_SEQ, NA_HEADS // 2),
        in_specs=[
            pl.BlockSpec((SAMPLE_LEN, 128), lambda b, h: (s0 + b, 12 + h)),
            pl.BlockSpec((SAMPLE_LEN, 128), lambda b, h: (s0 + b, 16 + h)),
            pl.BlockSpec((SAMPLE_LEN, 128), lambda b, h: (s0 + b, 20 + h)),
            pl.BlockSpec((1, PAST, 128), lambda b, h: (b, 0, h)),
            pl.BlockSpec((1, PAST, 128), lambda b, h: (b, 0, h)),
            pl.BlockSpec((2, NA_KH, GRID_W, NA_KH * GRID_W), lambda b, h: (h, 0, 0, 0)),
        ],
        out_specs=pl.BlockSpec((SAMPLE_LEN, 128), lambda b, h: (b, h)),
        out_shape=jax.ShapeDtypeStruct((NS, SEG), bf16),
        compiler_params=pltpu.CompilerParams(
            dimension_semantics=("parallel", "parallel"), vmem_limit_bytes=VMEM_LIMIT),
        name="latent_na_attention",
    )(proj, proj, proj, cache_k, cache_v, bias)


def _out_proj_kernel(ap_ref, ad_ref, an_ref, w_ref, xp_ref, xs_ref, gt_ref, o_ref, *, tm):
    i = pl.program_id(0)

    @pl.when(i < NP // tm)
    def _():
        o_ref[...] = xp_ref[...] + gt_ref[0] * _dot(ap_ref[...], w_ref[...])

    @pl.when(i >= NP // tm)
    def _():
        mixed = _dot(ad_ref[...], w_ref[:SEG, :]) + _dot(an_ref[...], w_ref[SEG:, :])
        o_ref[...] = xs_ref[...] + gt_ref[0] * mixed


def _out_projection(a_prompt, a_diff, a_na, w_bf, xp, xs, gate):
    tm = OUT_PROJ_ROWS
    n_p = NP // tm
    cidx = functools.partial(_cond_idx, tm=tm)
    prompt_rows = lambda i: (jnp.minimum(i, n_p - 1), 0)
    sample_rows = lambda i: (jnp.maximum(i - n_p, 0), 0)
    return pl.pallas_call(
        functools.partial(_out_proj_kernel, tm=tm),
        grid=(NT // tm,),
        in_specs=[
            pl.BlockSpec((tm, D), prompt_rows),
            pl.BlockSpec((tm, SEG), sample_rows),
            pl.BlockSpec((tm, SEG), sample_rows),
            pl.BlockSpec((D, D), lambda i: (0, 0)),
            pl.BlockSpec((tm, D), prompt_rows),
            pl.BlockSpec((tm, D), sample_rows),
            pl.BlockSpec((1, 1, D), lambda i: (cidx(i), 0, 0)),
        ],
        out_specs=pl.BlockSpec((tm, D), lambda i: (i, 0)),
        out_shape=jax.ShapeDtypeStruct((NT, D), f32),
        compiler_params=pltpu.CompilerParams(
            dimension_semantics=("arbitrary",), vmem_limit_bytes=VMEM_LIMIT),
        name="out_projection",
    )(a_prompt, a_diff, a_na, w_bf, xp, xs, gate)


def _rwkv_out_kernel(yfp_ref, ybp_ref, yfs_ref, ybs_ref, bonus_ref, gate_ref, lng_ref, lnb_ref,
                     gmat_ref, w_ref, x_ref, gt_ref, o_ref, *, tm):
    i = pl.program_id(0)

    def finish(y):
        gmat = gmat_ref[...]
        mu = _group_sum(y, gmat) * (1.0 / HD)
        yc = y - mu
        var = _group_sum(yc * yc, gmat) * (1.0 / HD)
        z = yc * lax.rsqrt(var + GN_EPS) * lng_ref[...] + lnb_ref[...] + bonus_ref[...].astype(f32)
        z = (z * gate_ref[...].astype(f32)).astype(bf16)
        o_ref[...] = x_ref[...] + gt_ref[0] * _dot(z, w_ref[...])

    @pl.when(i < NP // tm)
    def _():
        finish(yfp_ref[...].astype(f32) + ybp_ref[...].astype(f32))

    @pl.when(i >= NP // tm)
    def _():
        finish(yfs_ref[...].astype(f32) + ybs_ref[...].astype(f32))


def _rwkv_out_projection(yf_p, yb_p, yf_s, yb_s, bonus, gate_lora, ln_g, ln_b, gmat, w_bf, x, gate):
    tm = RWKV_OUT_ROWS
    n_p = NP // tm
    cidx = functools.partial(_cond_idx, tm=tm)
    row = pl.BlockSpec((tm, D), lambda i: (i, 0))
    prompt_row = pl.BlockSpec((tm, D), lambda i: (jnp.minimum(i, n_p - 1), 0))
    sample_row = pl.BlockSpec((tm, D), lambda i: (jnp.maximum(i - n_p, 0), 0))
    vec = pl.BlockSpec((1, D), lambda i: (0, 0))
    return pl.pallas_call(
        functools.partial(_rwkv_out_kernel, tm=tm),
        grid=(NT // tm,),
        in_specs=[prompt_row, prompt_row, sample_row, sample_row, row, row, vec, vec,
                  pl.BlockSpec((256, 256), lambda i: (0, 0)),
                  pl.BlockSpec((D, D), lambda i: (0, 0)),
                  row,
                  pl.BlockSpec((1, 1, D), lambda i: (cidx(i), 0, 0))],
        out_specs=row,
        out_shape=jax.ShapeDtypeStruct((NT, D), f32),
        compiler_params=pltpu.CompilerParams(
            dimension_semantics=("arbitrary",), vmem_limit_bytes=VMEM_LIMIT),
        name="rwkv_out_projection",
    )(yf_p, yb_p, yf_s, yb_s, bonus, gate_lora, ln_g, ln_b, gmat, w_bf, x, gate)


def _swiglu_hidden(xb, w1, w3):
    a = _dot(xb, w1)
    return ((a * _sigmoid(a)) * _dot(xb, w3)).astype(bf16)


def _ffn_kernel(x_ref, sh_ref, sc_ref, gt_ref, g_ref, w1_ref, w3_ref, w2_ref, o_ref, h_scr, acc_scr):
    f = pl.program_id(1)

    @pl.when(f == 0)
    def _():
        h_scr[...] = _norm_mod(x_ref[...], g_ref[...], sh_ref[0], sc_ref[0]).astype(bf16)
        acc_scr[...] = jnp.zeros_like(acc_scr)

    acc_scr[...] += _dot(_swiglu_hidden(h_scr[...], w1_ref[...], w3_ref[...]), w2_ref[...])

    @pl.when(f == pl.num_programs(1) - 1)
    def _():
        o_ref[...] = x_ref[...] + gt_ref[0] * acc_scr[...]


def _dense_ffn(x, sh, sc, gt, g, w1, w3, w2):
    tm = FFN_ROWS
    tf = FFN_COLS
    cidx = functools.partial(_cond_idx, tm=tm)
    mod = pl.BlockSpec((1, 1, D), lambda i, f: (cidx(i), 0, 0))
    return pl.pallas_call(
        _ffn_kernel,
        grid=(NT // tm, FFN_DIM // tf),
        in_specs=[
            pl.BlockSpec((tm, D), lambda i, f: (i, 0)),
            mod, mod, mod,
            pl.BlockSpec((1, D), lambda i, f: (0, 0)),
            pl.BlockSpec((D, tf), lambda i, f: (0, f)),
            pl.BlockSpec((D, tf), lambda i, f: (0, f)),
            pl.BlockSpec((tf, D), lambda i, f: (f, 0)),
        ],
        out_specs=pl.BlockSpec((tm, D), lambda i, f: (i, 0)),
        out_shape=jax.ShapeDtypeStruct((NT, D), f32),
        scratch_shapes=[pltpu.VMEM((tm, D), bf16), pltpu.VMEM((tm, D), f32)],
        compiler_params=pltpu.CompilerParams(
            dimension_semantics=("parallel", "arbitrary"), vmem_limit_bytes=VMEM_LIMIT),
        name="ffn_mixer",
    )(x, sh, sc, gt, g, w1, w3, w2)


MOE_TILE = 1024
MOE_MAIN = 288
MOE_EXTRA = 128
MOE_ROWS = MOE_MAIN + -(-(MOE_TILE - MOE_MAIN) // MOE_EXTRA) * MOE_EXTRA


RANK_SEL = 4096


def _router_kernel(x_ref, sh_ref, sc_ref, g_ref, wr_ref, br_ref, tri_ref, triu_ref,
                   h_out, gates_out, rank_out, rank_t_out, cnt_out):
    h = _norm_mod(x_ref[...], g_ref[...], sh_ref[0], sc_ref[0])
    h_out[...] = h.astype(bf16)
    w = wr_ref[...]
    h_hi, w_hi = h.astype(bf16), w.astype(bf16)
    h_lo = (h - h_hi.astype(f32)).astype(bf16)
    w_lo = (w - w_hi.astype(f32)).astype(bf16)
    hw = _dot(h_hi, jnp.concatenate([w_hi, w_lo], axis=1))
    logits = hw[:, :128] + (hw[:, 128:] + _dot(h_lo, w_hi)) + br_ref[...]
    lane = lax.broadcasted_iota(jnp.int32, logits.shape, 1)
    logits = jnp.where(lane < N_EXPERTS, logits, -jnp.inf)
    m1 = logits.max(axis=-1, keepdims=True)
    i1 = jnp.min(jnp.where(logits == m1, lane, 128), axis=-1, keepdims=True)
    rest = jnp.where(lane == i1, -jnp.inf, logits)
    m2 = rest.max(axis=-1, keepdims=True)
    i2 = jnp.min(jnp.where(rest == m2, lane, 128), axis=-1, keepdims=True)
    e2 = jnp.exp(m2 - m1)
    den = 1.0 / (1.0 + e2)
    gates_out[...] = jnp.where(lane == i1, den, jnp.where(lane == i2, e2 * den, 0.0))
    sel = jnp.logical_or(lane == i1, lane == i2)
    self32 = sel.astype(f32)
    selb = self32.astype(bf16)
    rank_out[...] = jnp.where(sel, _dot(tri_ref[...], selb), -1.0)
    rank_t_out[...] = _dot_tn(selb, triu_ref[...])
    cnt_out[0] = jnp.sum(self32, axis=0, keepdims=True).astype(jnp.int32)


def _moe_route(x, sh, sc, g, w_router, b_router):
    tm = MOE_TILE
    cidx = functools.partial(_cond_idx, tm=tm)
    t = jnp.arange(tm)
    tri = (t[:, None] > t[None, :]).astype(bf16)
    triu = jnp.where(t[:, None] == t[None, :], float(RANK_SEL), tri.T.astype(f32)).astype(bf16)
    mod = pl.BlockSpec((1, 1, D), lambda i: (cidx(i), 0, 0))
    return pl.pallas_call(
        _router_kernel,
        grid=(NT // tm,),
        in_specs=[
            pl.BlockSpec((tm, D), lambda i: (i, 0)),
            mod, mod,
            pl.BlockSpec((1, D), lambda i: (0, 0)),
            pl.BlockSpec((D, 128), lambda i: (0, 0)),
            pl.BlockSpec((1, 128), lambda i: (0, 0)),
            pl.BlockSpec((tm, tm), lambda i: (0, 0)),
            pl.BlockSpec((tm, tm), lambda i: (0, 0)),
        ],
        out_specs=[
            pl.BlockSpec((tm, D), lambda i: (i, 0)),
            pl.BlockSpec((tm, 128), lambda i: (i, 0)),
            pl.BlockSpec((tm, 128), lambda i: (i, 0)),
            pl.BlockSpec((128, tm), lambda i: (0, i)),
            pl.BlockSpec((1, 1, 128), lambda i: (i, 0, 0)),
        ],
        out_shape=[
            jax.ShapeDtypeStruct((NT, D), bf16),
            jax.ShapeDtypeStruct((NT, 128), f32),
            jax.ShapeDtypeStruct((NT, 128), f32),
            jax.ShapeDtypeStruct((128, NT), f32),
            jax.ShapeDtypeStruct((NT // tm, 1, 128), jnp.int32),
        ],
        compiler_params=pltpu.CompilerParams(
            dimension_semantics=("parallel",), vmem_limit_bytes=VMEM_LIMIT),
        name="moe_router",
    )(x, sh, sc, g, w_router, b_router, tri, triu)


def _moe_kernel(cnt_ref, h_ref, rank_ref, rank_t_ref, gates_ref, x_ref, gt_ref, w1_ref, w3_ref, w2_ref,
                o_ref, xc_scr, acc_scr, rcol_scr, gcol_scr):
    i = pl.program_id(0)
    e = pl.program_id(1)
    f = pl.program_id(2)
    cnt = cnt_ref[i * N_EXPERTS + e]
    n_extra = jnp.maximum(cnt - MOE_MAIN + MOE_EXTRA - 1, 0) // MOE_EXTRA

    def for_each_block(fn):
        fn(0, MOE_MAIN)

        def body(b, carry):
            fn(pl.multiple_of(MOE_MAIN + b * MOE_EXTRA, 32), MOE_EXTRA)
            return carry

        lax.fori_loop(0, n_extra, body, 0)

    @pl.when(jnp.logical_and(e == 0, f == 0))
    def _():
        o_ref[...] = x_ref[...]

    @pl.when(f == 0)
    def _():
        lane = lax.broadcasted_iota(jnp.int32, (MOE_TILE, 128), 1)
        mine = lane == e
        rcol_scr[...] = jnp.sum(jnp.where(mine, rank_ref[...], 0.0), axis=1, keepdims=True).astype(jnp.int32)
        gcol_scr[...] = jnp.sum(jnp.where(mine, gates_ref[...], 0.0), axis=1, keepdims=True)

        rrow = rank_t_ref[pl.ds(e, 1), :].astype(jnp.int32)

        def gather(slot0, nrows):
            slot = lax.broadcasted_iota(jnp.int32, (nrows, MOE_TILE), 0) + (slot0 + RANK_SEL)
            xc_scr[pl.ds(slot0, nrows), :] = _dot((rrow == slot).astype(bf16), h_ref[...]).astype(bf16)

        for_each_block(gather)

    def expert(slot0, nrows):
        rows = pl.ds(slot0, nrows)
        part = _dot(_swiglu_hidden(xc_scr[rows, :], w1_ref[0], w3_ref[0]), w2_ref[0])

        @pl.when(f == 0)
        def _():
            acc_scr[rows, :] = part

        @pl.when(f != 0)
        def _():
            acc_scr[rows, :] += part

    for_each_block(expert)

    @pl.when(f == pl.num_programs(2) - 1)
    def _():
        def scatter(slot0, nrows):
            out = acc_scr[pl.ds(slot0, nrows), :].astype(bf16)
            for t0 in range(0, MOE_TILE, 256):
                rows = slice(t0, t0 + 256)
                slot = lax.broadcasted_iota(jnp.int32, (256, nrows), 1) + slot0
                pt = (rcol_scr[rows, :] == slot).astype(bf16)
                o_ref[rows, :] += (gcol_scr[rows, :] * gt_ref[0]) * _dot(pt, out)

        for_each_block(scatter)


def _moe_experts(cnt, h_bf, rank, rank_t, gates, x, gt, w1, w3, w2):
    tm = MOE_TILE
    tf = EXPERT_DIM // 2
    cidx = functools.partial(_cond_idx, tm=tm)
    once = pl.Buffered(1)
    grid_spec = pltpu.PrefetchScalarGridSpec(
        num_scalar_prefetch=1,
        grid=(NT // tm, N_EXPERTS, EXPERT_DIM // tf),
        in_specs=[
            pl.BlockSpec((tm, D), lambda i, e, f, c: (i, 0), pipeline_mode=once),
            pl.BlockSpec((tm, 128), lambda i, e, f, c: (i, 0), pipeline_mode=once),
            pl.BlockSpec((128, tm), lambda i, e, f, c: (0, i), pipeline_mode=once),
            pl.BlockSpec((tm, 128), lambda i, e, f, c: (i, 0), pipeline_mode=once),
            pl.BlockSpec((tm, D), lambda i, e, f, c: (i, 0), pipeline_mode=once),
            pl.BlockSpec((1, 1, D), lambda i, e, f, c: (cidx(i), 0, 0)),
            pl.BlockSpec((1, D, tf), lambda i, e, f, c: (e, 0, f)),
            pl.BlockSpec((1, D, tf), lambda i, e, f, c: (e, 0, f)),
            pl.BlockSpec((1, tf, D), lambda i, e, f, c: (e, f, 0)),
        ],
        out_specs=pl.BlockSpec((tm, D), lambda i, e, f, c: (i, 0)),
        scratch_shapes=[pltpu.VMEM((MOE_ROWS, D), bf16), pltpu.VMEM((MOE_ROWS, D), f32),
                        pltpu.VMEM((tm, 1), jnp.int32), pltpu.VMEM((tm, 1), f32)],
    )
    return pl.pallas_call(
        _moe_kernel,
        grid_spec=grid_spec,
        out_shape=jax.ShapeDtypeStruct((NT, D), f32),
        compiler_params=pltpu.CompilerParams(
            dimension_semantics=("parallel", "arbitrary", "arbitrary"), vmem_limit_bytes=VMEM_LIMIT),
        name="moe_experts",
    )(cnt, h_bf, rank, rank_t, gates, x, gt, w1, w3, w2)


def _in_context_key(k, a, k_a):
    return k * (1.0 + (a - 1.0) * k_a)


def _rwkv_proj_kernel(x_ref, xp_ref, xn_ref, sh_ref, sc_ref, g_ref, mu_ref,
                      wr_ref, wk_ref, wv_ref, g1_ref, g2_ref, w1_ref, w2_ref, a1_ref, a2_ref,
                      w0_ref, a0_ref, kk_ref, ka_ref, rk_ref, gmat_ref,
                      r_out, v_out, kkn_out, bonus_out, gate_out,
                      lwf_out, lwb_out, af_out, ab_out, k_out, *, tm):
    i = pl.program_id(0)
    g, sh, sc = g_ref[...], sh_ref[0], sc_ref[0]
    h = _norm_mod(x_ref[...], g, sh, sc)
    n_prompt_tiles = NP // tm
    tiles_per_seq = SAMPLE_LEN // tm
    pos = (i - n_prompt_tiles) % tiles_per_seq
    has_prev = jnp.logical_and(i >= n_prompt_tiles, pos != 0)
    has_next = jnp.logical_and(i >= n_prompt_tiles, pos != tiles_per_seq - 1)
    h_before = jnp.where(has_prev, _norm_mod(xp_ref[...], g, sh, sc)[7:8], 0.0)
    h_after = jnp.where(has_next, _norm_mod(xn_ref[...], g, sh, sc)[0:1], 0.0)
    rowi = lax.broadcasted_iota(jnp.int32, (tm, 1), 0)
    h_prev = jnp.where(rowi == 0, h_before, pltpu.roll(h, 1, 0))
    h_next = jnp.where(rowi == tm - 1, h_after, pltpu.roll(h, tm - 1, 0))
    xx = 0.5 * (h_prev + h_next) - h
    mix = lambda n: (h + xx * mu_ref[n:n + 1]).astype(bf16)

    r = _dot(mix(0), wr_ref[...])
    k = _dot(mix(2), wk_ref[...])
    v = _dot(mix(3), wv_ref[...])
    gate_out[...] = _dot(_sigmoid(_dot(mix(5), g1_ref[...])).astype(bf16), g2_ref[...]).astype(bf16)

    lo = _lane_lo()
    tw = jnp.tanh(_dot(mix(1), w1_ref[...]))
    ta = _dot(mix(4), a1_ref[...])
    gmat = gmat_ref[...]
    kk = k * kk_ref[...]
    kkn_out[...] = kk * lax.rsqrt(_group_sum(kk * kk, gmat) + 1e-12)
    r_out[...] = r
    v_out[...] = v
    k_out[...] = k
    kd_sum = jnp.zeros_like(k)
    for d, (lw_out, a_out) in enumerate(((lwf_out, af_out), (lwb_out, ab_out))):
        keep = lo if d == 0 else jnp.logical_not(lo)
        zw = w0_ref[d:d + 1] + _dot(jnp.where(keep, tw, 0.0).astype(bf16), w2_ref[...])
        lw_out[...] = -math.exp(-0.5) * _sigmoid(zw)
        a = _sigmoid(a0_ref[d:d + 1] + _dot(jnp.where(keep, ta, 0.0).astype(bf16), a2_ref[...]))
        a_out[...] = a
        kd_sum = kd_sum + _in_context_key(k, a, ka_ref[...])
    bonus_out[...] = (_group_sum(r * kd_sum * rk_ref[...], gmat) * v).astype(bf16)


def _rwkv_projection(x, sh, sc, g, mu, wr, wk, wv, g1, g2, w1, w2, a1, a2, w0, a0, k_k, k_a, r_k, gmat):
    tm = RWKV_PROJ_ROWS
    cidx = functools.partial(_cond_idx, tm=tm)
    hb = tm // 8
    n8 = NT // 8
    full = lambda shape: pl.BlockSpec(shape, lambda i: tuple(0 for _ in shape))
    row = pl.BlockSpec((tm, D), lambda i: (i, 0))
    return pl.pallas_call(
        functools.partial(_rwkv_proj_kernel, tm=tm),
        grid=(NT // tm,),
        in_specs=[
            row,
            pl.BlockSpec((8, D), lambda i: (jnp.maximum(i * hb - 1, 0), 0)),
            pl.BlockSpec((8, D), lambda i: (jnp.minimum((i + 1) * hb, n8 - 1), 0)),
            pl.BlockSpec((1, 1, D), lambda i: (cidx(i), 0, 0)),
            pl.BlockSpec((1, 1, D), lambda i: (cidx(i), 0, 0)),
            full((1, D)), full((6, D)),
            full((D, D)), full((D, D)), full((D, D)),
            full((D, GATE_LORA_PAD)), full((GATE_LORA_PAD, D)),
            full((D, 2 * LORA)), full((2 * LORA, D)), full((D, 2 * LORA)), full((2 * LORA, D)),
            full((2, D)), full((2, D)), full((1, D)), full((1, D)), full((1, D)),
            full((256, 256)),
        ],
        out_specs=[row] * 10,
        out_shape=[jax.ShapeDtypeStruct((NT, D), dt) for dt in (f32, f32, f32, bf16, bf16) + (f32,) * 5],
        compiler_params=pltpu.CompilerParams(
            dimension_semantics=("parallel",), vmem_limit_bytes=VMEM_LIMIT),
        name="rwkv_projection",
    )(x, x, x, sh, sc, g, mu, wr, wk, wv, g1, g2, w1, w2, a1, a2, w0, a0, k_k, k_a, r_k, gmat)


def _split3(x):
    x1 = x.astype(bf16)
    r1 = x - x1.astype(f32)
    x2 = r1.astype(bf16)
    x3 = (r1 - x2.astype(f32)).astype(bf16)
    return x1, x2, x3


def _scan_kernel(*refs, nc, has_init, emit_state):
    fwd_refs = refs[0:6]
    bwd_refs = refs[6:12]
    ka_ref, tri_ref, mask_ref = refs[12:15]
    pos = 15
    s0_ref = None
    if has_init:
        s0_ref = refs[pos]
        pos += 1
    yf_ref, yb_ref = refs[pos:pos + 2]
    pos += 2
    sout_ref = None
    if emit_state:
        sout_ref = refs[pos]
        pos += 1
    st_scr, cl_scr = refs[pos:pos + 2]

    s = pl.program_id(1)

    @pl.when(s == 0)
    def _():
        if has_init:
            st_scr[...] = s0_ref[0]
        else:
            st_scr[...] = jnp.zeros_like(st_scr)

    c = CHUNK
    for d, drefs in enumerate((fwd_refs, bwd_refs)):
        lw = drefs[3][...]
        tri = tri_ref[d]
        p1, p2, p3 = _split3(lw)
        cl_scr[d] = _dot(tri, p1) + _dot(tri, p2) + _dot(tri, p3)

    lane = lax.broadcasted_iota(jnp.int32, (1, 128), 1)
    m0 = (lane < HD).astype(f32)
    m1 = 1.0 - m0
    rid = lax.broadcasted_iota(jnp.int32, (128, 128), 0)
    cid = lax.broadcasted_iota(jnp.int32, (128, 128), 1)
    eye = (rid == cid).astype(f32)

    n_pairs = RW_HEADS // 2
    nb = 2 * n_pairs

    def per_head_rows(x):
        return jnp.concatenate([x * m0, x * m1], axis=0)

    ars_l, bk_l, bkh_l, v2_l, dec_l = [], [], [], [], []
    for d, drefs in enumerate((fwd_refs, bwd_refs)):
        r_ref, v_ref, kk_ref, lw_ref, a_ref, k_ref = drefs
        end_row = c - 1 if d == 0 else 0
        for p in range(n_pairs):
            ln = slice(p * 128, (p + 1) * 128)
            cl = cl_scr[d, :, ln]
            kk = kk_ref[:, ln]
            tot = cl[end_row:end_row + 1]
            e_inv = jnp.exp(-cl)
            e_end = jnp.exp(tot - cl)
            a = a_ref[:, ln]
            kka = kk * a
            kd = _in_context_key(k_ref[:, ln], a, ka_ref[:, ln])
            at = per_head_rows(-kk * jnp.exp(cl - lw_ref[:, ln]))
            rt = per_head_rows(r_ref[:, ln] * jnp.exp(cl))
            ars_l.append(jnp.concatenate([at, rt], axis=0).astype(bf16))
            bk_l.append(jnp.concatenate([per_head_rows(kka * e_inv), per_head_rows(kd * e_inv)],
                                        axis=0).astype(bf16))
            bkh_l.append(jnp.concatenate([per_head_rows(kka * e_end), per_head_rows(kd * e_end)],
                                         axis=0).astype(bf16))
            v2_l.append(per_head_rows(v_ref[:, ln]).astype(bf16))
            dec_l.append(jnp.exp(tot))
    ar = jnp.stack(ars_l)
    bk = jnp.stack(bk_l)
    bkh = jnp.stack(bkh_l)
    v2 = jnp.stack(v2_l)
    dec = jnp.stack(dec_l)

    st = st_scr[...].reshape(nb, 128, 128)
    g2 = _bdot_nt(ar, bk).reshape(2, n_pairs, 256, 256) * mask_ref[...][:, None]
    g2 = g2.reshape(nb, 256, 256)
    ars = _bdot_nt(ar, st.astype(bf16))
    gv = _bdot(g2[:, :, 128:].astype(bf16), v2)
    l_bd = g2[:, :128, :128]
    l_bf = l_bd.astype(bf16)
    pk = _bdot(l_bf, l_bf)
    q = eye[None] + l_bd
    for it in range(5):
        pkb = pk.astype(bf16)
        if it < 4:
            res = _bdot(jnp.concatenate([q, pk], axis=1).astype(bf16), pkb)
            q = q + res[:, :128]
            pk = res[:, 128:]
        else:
            q = q + _bdot(q.astype(bf16), pkb)
    rhs = ars[:, :128] + gv[:, :128]
    u2b = _bdot(q.astype(bf16), rhs.astype(bf16)).astype(bf16)
    y2 = ars[:, 128:] + gv[:, 128:] + _bdot(g2[:, 128:, :128].astype(bf16), u2b)
    y = y2[:, :c] + y2[:, c:]
    for d, y_ref in enumerate((yf_ref, yb_ref)):
        for p in range(n_pairs):
            y_ref[:, p * 128:(p + 1) * 128] = y[d * n_pairs + p].astype(y_ref.dtype)
    uv = jnp.concatenate([u2b, v2], axis=1)
    st_scr[...] = (st * dec + _bdot_tn(uv, bkh)).reshape(2, n_pairs, 128, 128)

    if emit_state:
        @pl.when(s == nc - 1)
        def _():
            for d in range(2):
                for p in range(n_pairs):
                    m = st_scr[d, p]
                    sout_ref[0, d, 2 * p] = m[:HD, :HD]
                    sout_ref[0, d, 2 * p + 1] = pltpu.roll(m[HD:, :], HD, 1)[:, :HD]


def _rwkv_scan(streams_f, streams_b, k_a, tri, mask, s0_bd, *, n_seq, seq_len, row0, emit_state):
    nc = seq_len // CHUNK
    blk0 = row0 // CHUNK
    fwd_spec = pl.BlockSpec((CHUNK, D), lambda b, s: (blk0 + b * nc + s, 0))
    bwd_spec = pl.BlockSpec((CHUNK, D), lambda b, s: (blk0 + b * nc + nc - 1 - s, 0))
    in_specs = [fwd_spec] * 6 + [bwd_spec] * 6 + [
        pl.BlockSpec((1, D), lambda b, s: (0, 0)),
        pl.BlockSpec((2, CHUNK, CHUNK), lambda b, s: (0, 0, 0)),
        pl.BlockSpec((2, 256, 256), lambda b, s: (0, 0, 0)),
    ]
    args = list(streams_f) + list(streams_b) + [k_a, tri, mask]
    state_block = (1, 2, RW_HEADS // 2, 128, 128)
    if s0_bd is not None:
        in_specs.append(pl.BlockSpec(state_block, lambda b, s: (b, 0, 0, 0, 0)))
        args.append(s0_bd)
    out_specs = [pl.BlockSpec((CHUNK, D), lambda b, s: (b * nc + s, 0)),
                 pl.BlockSpec((CHUNK, D), lambda b, s: (b * nc + nc - 1 - s, 0))]
    out_shape = [jax.ShapeDtypeStruct((n_seq * seq_len, D), bf16)] * 2
    if emit_state:
        out_specs.append(pl.BlockSpec((1, 2, RW_HEADS, HD, HD), lambda b, s: (b, 0, 0, 0, 0)))
        out_shape.append(jax.ShapeDtypeStruct((n_seq, 2, RW_HEADS, HD, HD), f32))
    return pl.pallas_call(
        functools.partial(_scan_kernel, nc=nc, has_init=s0_bd is not None, emit_state=emit_state),
        grid=(n_seq, nc),
        in_specs=in_specs,
        out_specs=out_specs,
        out_shape=out_shape,
        scratch_shapes=[pltpu.VMEM(state_block[1:], f32), pltpu.VMEM((2, CHUNK, D), f32)],
        compiler_params=pltpu.CompilerParams(
            dimension_semantics=("parallel", "arbitrary"), vmem_limit_bytes=VMEM_LIMIT),
        name="rwkv_scan_prompt" if emit_state else "rwkv_scan_sample",
    )(*args)


def _scan_constants():
    t = jnp.arange(CHUNK)
    lower = (t[:, None] >= t[None, :])
    tri = jnp.stack([lower, lower.T]).astype(bf16)
    masks = []
    for d in range(2):
        strict = (t[:, None] > t[None, :]) if d == 0 else (t[:, None] < t[None, :])
        incl = lower if d == 0 else lower.T
        blocks = []
        for m in (strict, incl):
            bd = jnp.kron(jnp.eye(2, dtype=f32), m.astype(f32))
            blocks.append(jnp.concatenate([bd, bd], axis=1))
        masks.append(jnp.concatenate(blocks, axis=0))
    return tri, jnp.stack(masks)


def _state_to_blockdiag(s):
    n = s.shape[0]
    s = s.reshape(n, 2, RW_HEADS // 2, 2, HD, HD)
    z = jnp.zeros_like(s[:, :, :, 0])
    top = jnp.concatenate([s[:, :, :, 0], z], axis=-1)
    bot = jnp.concatenate([z, s[:, :, :, 1]], axis=-1)
    return jnp.concatenate([top, bot], axis=-2)


def _rope_tables():
    t = jnp.arange(SAMPLE_LEN)
    rows = (t // GRID_W).astype(f32)
    cols = (t % GRID_W).astype(f32)
    nf = HD // 4
    inv = 10000.0 ** (-jnp.arange(nf, dtype=f32) / nf)
    ang = jnp.concatenate([rows[:, None] * inv, cols[:, None] * inv], axis=-1)
    cos = jnp.repeat(jnp.cos(ang), 2, axis=-1)
    sin = jnp.repeat(jnp.sin(ang), 2, axis=-1) * jnp.tile(jnp.array([-1.0, 1.0], f32), HD // 2)
    return jnp.tile(cos, (1, 2)), jnp.tile(sin, (1, 2))


def kernel(x_prompt, x_sample, cache_diff_k, cache_diff_v, cache_na_k, cache_na_v, state_rwkv, c, c_ctx, w_ada, b_ada, g_mix, g_ffn, w_in, w_out, diff_q_g, diff_k_g, diff_lam_q1, diff_lam_k1, diff_lam_q2, diff_lam_k2, diff_subln_g, na_q_g, na_k_g, na_rpb, ffn_w1, ffn_w3, ffn_w2, rw_mu, rw_wr, rw_wk, rw_wv, rw_wo, rw_w0, rw_w1, rw_w2, rw_a0, rw_a1, rw_a2, rw_g1, rw_g2, rw_k_k, rw_k_a, rw_r_k, rw_ln_g, rw_ln_b, moe_router, moe_router_b, moe_w1, moe_w3, moe_w2):
    xp = x_prompt.reshape(NP, D)
    xs = x_sample.reshape(NS, D)
    cond8 = jnp.concatenate([c_ctx[None, :], c, jnp.zeros((3, D), f32)], axis=0)
    mod = _ada_table(cond8, w_ada, b_ada)
    gmat = jnp.kron(jnp.eye(4, dtype=f32), jnp.ones((HD, HD), f32)).astype(bf16)

    lam_init = 0.8 - 0.6 * math.exp(-0.3 * 0)
    ones_seg = jnp.ones((SEG,), f32)
    tile8 = lambda gvec: jnp.tile(gvec, SEG // HD)
    gains = jnp.stack([tile8(diff_q_g[0]), tile8(diff_k_g[0]), ones_seg,
                       tile8(na_q_g[0]), tile8(na_k_g[0]), ones_seg]).reshape(6, 1, SEG)
    cos_t, sin_t = _rope_tables()
    proj, dk_p, dv_p, nk_p, nv_p = _in_projection(
        xp, xs, mod[0][0], mod[0][1], g_mix[0][None, :], w_in[0].astype(bf16), gains, gmat, cos_t, sin_t)
    lamp = jnp.stack([diff_lam_q1[0], diff_lam_k1[0], diff_lam_q2[0], diff_lam_k2[0]])
    subg = diff_subln_g[0][None, :]
    o_prompt = _prompt_attention(proj, lamp, subg, lam_init)
    o_diff = _latent_diff_attention(proj, cache_diff_k[:, 0].reshape(N_SAMPLE_SEQ, PAST, SEG),
                                    cache_diff_v[:, 0].reshape(N_SAMPLE_SEQ, PAST, SEG), lamp, subg, lam_init)
    o_na = _latent_na_attention(proj, cache_na_k[:, 0].reshape(N_SAMPLE_SEQ, PAST, SEG),
                                cache_na_v[:, 0].reshape(N_SAMPLE_SEQ, PAST, SEG), _rpb_table(na_rpb[0]))
    x = _out_projection(o_prompt, o_diff, o_na, w_out[0].astype(bf16), xp, xs, mod[0][2])
    x = _dense_ffn(x, mod[0][3], mod[0][4], mod[0][5], g_ffn[0][None, :],
                   ffn_w1[0].astype(bf16), ffn_w3[0].astype(bf16), ffn_w2[0].astype(bf16))

    pad_g = GATE_LORA_PAD - GATE_LORA
    g1 = jnp.pad(rw_g1[0], ((0, 0), (0, pad_g))).astype(bf16)
    g2 = jnp.pad(rw_g2[0], ((0, pad_g), (0, 0))).astype(bf16)
    w1cat = jnp.concatenate([rw_w1[0, 0], rw_w1[0, 1]], axis=1).astype(bf16)
    w2cat = jnp.concatenate([rw_w2[0, 0], rw_w2[0, 1]], axis=0).astype(bf16)
    a1cat = jnp.concatenate([rw_a1[0, 0], rw_a1[0, 1]], axis=1).astype(bf16)
    a2cat = jnp.concatenate([rw_a2[0, 0], rw_a2[0, 1]], axis=0).astype(bf16)
    k_a = rw_k_a[0][None, :]
    (r, v, kkn, bonus, gate_lora, lwf, lwb, af, ab, k) = _rwkv_projection(
        x, mod[1][0], mod[1][1], g_mix[1][None, :], rw_mu[0],
        rw_wr[0].astype(bf16), rw_wk[0].astype(bf16), rw_wv[0].astype(bf16), g1, g2,
        w1cat, w2cat, a1cat, a2cat, rw_w0[0], rw_a0[0],
        rw_k_k[0][None, :], k_a, rw_r_k[0].reshape(1, D), gmat)
    tri, mask = _scan_constants()
    streams_f = (r, v, kkn, lwf, af, k)
    streams_b = (r, v, kkn, lwb, ab, k)
    yf_p, yb_p, st_p = _rwkv_scan(streams_f, streams_b, k_a, tri, mask, None,
                                  n_seq=N_PROMPT_SEQ, seq_len=PROMPT_LEN, row0=0, emit_state=True)
    yf_s, yb_s = _rwkv_scan(streams_f, streams_b, k_a, tri, mask, _state_to_blockdiag(state_rwkv[:, 0]),
                            n_seq=N_SAMPLE_SEQ, seq_len=SAMPLE_LEN, row0=NP, emit_state=False)
    x = _rwkv_out_projection(yf_p, yb_p, yf_s, yb_s, bonus, gate_lora, rw_ln_g[0][None, :], rw_ln_b[0][None, :], gmat,
                             rw_wo[0].astype(bf16), x, mod[1][2])
    w_router = jnp.pad(moe_router[0], ((0, 0), (0, 128 - N_EXPERTS)))
    b_router = jnp.pad(moe_router_b[0], (0, 128 - N_EXPERTS))[None, :]
    h_bf, gates, rank, rank_t, cnt = _moe_route(x, mod[1][3], mod[1][4], g_ffn[1][None, :], w_router, b_router)
    x = _moe_experts(cnt[:, 0, :N_EXPERTS].reshape(-1), h_bf, rank, rank_t, gates, x, mod[1][5],
                     moe_w1[0].astype(bf16), moe_w3[0].astype(bf16), moe_w2[0].astype(bf16))

    new_dk = dk_p.reshape(N_PROMPT_SEQ, 1, PROMPT_LEN, DIFF_HEADS, 2 * HD)
    new_dv = dv_p.reshape(N_PROMPT_SEQ, 1, PROMPT_LEN, DIFF_HEADS, 2 * HD)
    new_nk = nk_p.reshape(N_PROMPT_SEQ, 1, PROMPT_LEN, NA_HEADS, HD)
    new_nv = nv_p.reshape(N_PROMPT_SEQ, 1, PROMPT_LEN, NA_HEADS, HD)
    new_state = st_p[:, None]
    return (x[:NP].reshape(N_PROMPT_SEQ, PROMPT_LEN, D), x[NP:].reshape(N_SAMPLE_SEQ, SAMPLE_LEN, D),
            new_dk, new_dv, new_nk, new_nv, new_state)
```

```python
import functools
import math

import jax
import jax.numpy as jnp
from jax import lax
from jax.experimental import pallas as pl
from jax.experimental.pallas import tpu as pltpu

f32 = jnp.float32
bf16 = jnp.bfloat16

D = 1024
N_PROMPT_SEQ, PROMPT_LEN = 32, 256
N_SAMPLE_SEQ, SAMPLE_LEN = 4, 2048
NP = N_PROMPT_SEQ * PROMPT_LEN
NS = N_SAMPLE_SEQ * SAMPLE_LEN
NT = NP + NS
PAST = 256
GRID_W = 64
GRID_R = SAMPLE_LEN // GRID_W
HD = 64
DIFF_HEADS = 4
NA_HEADS = 8
NA_KH = 8
NA_KW = 16
SEG = 512
IN_COLS = 6 * SEG
FFN_DIM = 2816
N_EXPERTS = 8
EXPERT_DIM = 3584
RW_HEADS = 16
LORA = 64
GATE_LORA = 160
GATE_LORA_PAD = 256
EPS = 1e-6
GN_EPS = 64e-5
NEG_BIG = -1e30
DIFF_HEADS_PER_STEP = 2
NA_ROWS_PER_STEP = 8
CHUNK = 64
VMEM_LIMIT = 56 * 1024 * 1024

ADA_COLS = 1536
IN_PROJ_ROWS = 1024
DIFF_Q_ROWS = 256
OUT_PROJ_ROWS = 512
FFN_ROWS = 512
FFN_COLS = FFN_DIM // 2
RWKV_PROJ_ROWS = 256
RWKV_OUT_ROWS = 512


def _cond_idx(i, tm):
    return jnp.maximum((i * tm) // SAMPLE_LEN - (NP // SAMPLE_LEN - 1), 0)


def _dot(a, b):
    return jnp.dot(a, b, preferred_element_type=f32)


def _dot_nt(a, b):
    return lax.dot_general(a, b, (((1,), (1,)), ((), ())), preferred_element_type=f32)


def _dot_tn(a, b):
    return lax.dot_general(a, b, (((0,), (0,)), ((), ())), preferred_element_type=f32)


def _bdot(a, b):
    return lax.dot_general(a, b, (((2,), (1,)), ((0,), (0,))), preferred_element_type=f32)


def _bdot_nt(a, b):
    return lax.dot_general(a, b, (((2,), (2,)), ((0,), (0,))), preferred_element_type=f32)


def _bdot_tn(a, b):
    return lax.dot_general(a, b, (((1,), (1,)), ((0,), (0,))), preferred_element_type=f32)


def _sigmoid(x):
    return 1.0 / (1.0 + jnp.exp(-x))


def _norm_mod(x, g, sh, sc):
    ms = jnp.mean(x * x, axis=-1, keepdims=True)
    return (x * lax.rsqrt(ms + EPS) * g) * (1.0 + sc) + sh


def _group_sum(x, gmat):
    xb = x.astype(bf16)
    cols = [_dot(xb[:, c * 256:(c + 1) * 256], gmat) for c in range(x.shape[1] // 256)]
    return cols[0] if len(cols) == 1 else jnp.concatenate(cols, axis=1)


def _softmax_parts(parts):
    m = parts[0].max(axis=-1, keepdims=True)
    for p in parts[1:]:
        m = jnp.maximum(m, p.max(axis=-1, keepdims=True))
    es = [jnp.exp(p - m) for p in parts]
    l = es[0].sum(axis=-1, keepdims=True)
    for e in es[1:]:
        l = l + e.sum(axis=-1, keepdims=True)
    inv = 1.0 / l
    return [e * inv for e in es]


def _lane_lo(n=128):
    return lax.broadcasted_iota(jnp.int32, (1, n), 1) < HD


def _stack_halves(q):
    lo = _lane_lo()
    return jnp.concatenate([jnp.where(lo, q, 0.0), jnp.where(lo, 0.0, q)], axis=0)


def _ada_kernel(cond_ref, w_ref, b_ref, o_ref):
    x = cond_ref[...]
    s = x * _sigmoid(x)
    o_ref[0] = _dot(s.astype(bf16), w_ref[0].astype(bf16)) + b_ref[0]


def _ada_table(cond8, w_ada, b_ada):
    depth = w_ada.shape[0]
    tn = ADA_COLS
    out = pl.pallas_call(
        _ada_kernel,
        grid=(depth, 6 * D // tn),
        in_specs=[
            pl.BlockSpec((8, D), lambda l, n: (0, 0)),
            pl.BlockSpec((1, D, tn), lambda l, n: (l, 0, n)),
            pl.BlockSpec((1, 1, tn), lambda l, n: (l, 0, n)),
        ],
        out_specs=pl.BlockSpec((1, 8, tn), lambda l, n: (l, 0, n)),
        out_shape=jax.ShapeDtypeStruct((depth, 8, 6 * D), f32),
        compiler_params=pltpu.CompilerParams(vmem_limit_bytes=VMEM_LIMIT),
        name="ada_table",
    )(cond8, w_ada, b_ada.reshape(depth, 1, 6 * D))
    out = out.reshape(depth, 8, 6, D)
    return [[out[l, :, k, :].reshape(8, 1, D) for k in range(6)] for l in range(depth)]


def _qk_norm(y, gain, gmat):
    ss = _group_sum(y * y, gmat) * (1.0 / HD)
    return y * lax.rsqrt(ss + EPS) * gain


def _rope(y, cos, sin):
    even = (lax.broadcasted_iota(jnp.int32, (1, 128), 1) % 2) == 0
    outs = []
    for c in range(y.shape[1] // 128):
        yc = y[:, c * 128:(c + 1) * 128]
        swapped = jnp.where(even, pltpu.roll(yc, 127, 1), pltpu.roll(yc, 1, 1))
        outs.append(yc * cos + swapped * sin)
    return jnp.concatenate(outs, axis=1)


def _inproj_kernel(xp_ref, xs_ref, sh_ref, sc_ref, g_ref, w_ref, gain_ref, gmat_ref, cos_ref, sin_ref,
                   f1_ref, f3_ref, f2_ref,
                   o_ref, dk_ref, dv_ref, nk_ref, nv_ref, f1_out, f3_out, f2_out, h_scr, *, tm):
    i = pl.program_id(0)
    j = pl.program_id(1)

    @pl.when(j == 0)
    def _():
        for src, dst in ((f1_ref, f1_out), (f3_ref, f3_out), (f2_ref, f2_out)):
            dst[...] = src[...].astype(bf16)

    def emit(val):
        o_ref[...] = val.astype(bf16)
        for seg, cache_ref in ((1, dk_ref), (2, dv_ref), (4, nk_ref), (5, nv_ref)):
            @pl.when(jnp.logical_and(j == seg, i < NP // tm))
            def _():
                cache_ref[...] = val

    for x_ref, active in ((xp_ref, i < NP // tm), (xs_ref, i >= NP // tm)):
        @pl.when(jnp.logical_and(j == 0, active))
        def _():
            h_scr[...] = _norm_mod(x_ref[...], g_ref[...], sh_ref[0], sc_ref[0]).astype(bf16)

    project = lambda: _dot(h_scr[...], w_ref[...])
    is_norm = jnp.logical_and(j != 2, j != 5)
    is_rope = jnp.logical_and(j < 2, i >= NP // tm)

    @pl.when(jnp.logical_not(is_norm))
    def _():
        emit(project())

    @pl.when(jnp.logical_and(is_norm, jnp.logical_not(is_rope)))
    def _():
        emit(_qk_norm(project(), gain_ref[0], gmat_ref[...]))

    @pl.when(is_rope)
    def _():
        o_ref[...] = _rope(_qk_norm(project(), gain_ref[0], gmat_ref[...]),
                           cos_ref[...], sin_ref[...]).astype(bf16)


def _in_projection(xp, xs, sh, sc, g, w_bf, gains, gmat, cos_t, sin_t, ffn_weights):
    tm = IN_PROJ_ROWS
    n_tiles = NT // tm
    n_prompt_tiles = NP // tm
    tiles_per_seq = SAMPLE_LEN // tm
    cidx = functools.partial(_cond_idx, tm=tm)
    rope_idx = lambda i, j: (jnp.maximum(i - n_prompt_tiles, 0) % tiles_per_seq, 0)
    cast_specs = [pl.BlockSpec((w.shape[0] // n_tiles, w.shape[1]), lambda i, j: (i, 0)) for w in ffn_weights]
    return pl.pallas_call(
        functools.partial(_inproj_kernel, tm=tm),
        grid=(NT // tm, IN_COLS // SEG),
        in_specs=[
            pl.BlockSpec((tm, D), lambda i, j: (jnp.minimum(i, n_prompt_tiles - 1), 0)),
            pl.BlockSpec((tm, D), lambda i, j: (jnp.maximum(i - n_prompt_tiles, 0), 0)),
            pl.BlockSpec((1, 1, D), lambda i, j: (cidx(i), 0, 0)),
            pl.BlockSpec((1, 1, D), lambda i, j: (cidx(i), 0, 0)),
            pl.BlockSpec((1, D), lambda i, j: (0, 0)),
            pl.BlockSpec((D, SEG), lambda i, j: (0, j)),
            pl.BlockSpec((1, 1, SEG), lambda i, j: (j, 0, 0)),
            pl.BlockSpec((256, 256), lambda i, j: (0, 0)),
            pl.BlockSpec((tm, 128), rope_idx),
            pl.BlockSpec((tm, 128), rope_idx),
        ] + cast_specs,
        out_specs=[pl.BlockSpec((tm, SEG), lambda i, j: (i, j))] + [
            pl.BlockSpec((tm, SEG), lambda i, j: (jnp.minimum(i, n_prompt_tiles - 1), 0))] * 4 + cast_specs,
        out_shape=[jax.ShapeDtypeStruct((NT, IN_COLS), bf16)] + [jax.ShapeDtypeStruct((NP, SEG), f32)] * 4
        + [jax.ShapeDtypeStruct(w.shape, bf16) for w in ffn_weights],
        scratch_shapes=[pltpu.VMEM((tm, D), bf16)],
        compiler_params=pltpu.CompilerParams(
            dimension_semantics=("arbitrary", "arbitrary"), vmem_limit_bytes=VMEM_LIMIT),
        name="in_projection",
    )(xp, xs, sh, sc, g, w_bf, gains, gmat, cos_t, sin_t, *ffn_weights)


def _lambda_value(lamp_ref, lam_init):
    lp = lamp_ref[...]
    e1 = jnp.exp(jnp.sum(lp[0:1] * lp[1:2], axis=-1, keepdims=True))
    e2 = jnp.exp(jnp.sum(lp[2:3] * lp[3:4], axis=-1, keepdims=True))
    return e1 - e2 + lam_init


def _sub_ln(o, subg, lam_init):
    ms = jnp.mean(o * o, axis=-1, keepdims=True)
    return o * lax.rsqrt(ms + EPS) * subg * (1.0 - lam_init)


def _prompt_attn_kernel(p_ref, lamp_ref, subg_ref, o_ref, *, lam_init):
    lam = _lambda_value(lamp_ref, lam_init)
    t = PROMPT_LEN
    lo = _lane_lo()
    scale = HD ** -0.5
    for h in range(DIFF_HEADS):
        q = p_ref[:, h * 128:(h + 1) * 128]
        k = p_ref[:, SEG + h * 128:SEG + (h + 1) * 128].astype(bf16)
        v = p_ref[:, 2 * SEG + h * 128:2 * SEG + (h + 1) * 128].astype(bf16)
        s = _dot_nt((_stack_halves(q) * scale).astype(bf16), k)
        (p,) = _softmax_parts([s])
        pd = p[:t] - lam * p[t:]
        o = _dot(pd.astype(bf16), v)
        o_ref[:, h * 128:(h + 1) * 128] = _sub_ln(o, subg_ref[...], lam_init).astype(bf16)
    for hp in range(NA_HEADS // 2):
        q = p_ref[:, 3 * SEG + hp * 128:3 * SEG + (hp + 1) * 128]
        k = p_ref[:, 4 * SEG + hp * 128:4 * SEG + (hp + 1) * 128].astype(bf16)
        v = p_ref[:, 5 * SEG + hp * 128:5 * SEG + (hp + 1) * 128].astype(bf16)
        s = _dot_nt((_stack_halves(q) * scale).astype(bf16), k)
        (p,) = _softmax_parts([s])
        o = _dot(p.astype(bf16), v)
        o_ref[:, SEG + hp * 128:SEG + (hp + 1) * 128] = jnp.where(lo, o[:t], o[t:]).astype(bf16)


def _prompt_attention(proj, lamp, subg, lam_init):
    return pl.pallas_call(
        functools.partial(_prompt_attn_kernel, lam_init=lam_init),
        grid=(N_PROMPT_SEQ,),
        in_specs=[
            pl.BlockSpec((PROMPT_LEN, IN_COLS), lambda b: (b, 0)),
            pl.BlockSpec((4, HD), lambda b: (0, 0)),
            pl.BlockSpec((1, 128), lambda b: (0, 0)),
        ],
        out_specs=pl.BlockSpec((PROMPT_LEN, D), lambda b: (b, 0)),
        out_shape=jax.ShapeDtypeStruct((NP, D), bf16),
        compiler_params=pltpu.CompilerParams(
            dimension_semantics=("parallel",), vmem_limit_bytes=VMEM_LIMIT),
        name="prompt_attention",
    )(proj, lamp, subg)


def _latent_diff_kernel(q_ref, kn_ref, vn_ref, kc_ref, vc_ref, lamp_ref, subg_ref, o_ref,
                        *, lam_init, tq):
    lam = _lambda_value(lamp_ref, lam_init)
    scale = HD ** -0.5
    for hh in range(DIFF_HEADS_PER_STEP):
        ln = slice(hh * 128, (hh + 1) * 128)
        qq = (_stack_halves(q_ref[:, ln]) * scale).astype(bf16)
        s_c = _dot_nt(qq, kc_ref[0, :, ln].astype(bf16))
        s_n = _dot_nt(qq, kn_ref[:, ln].astype(bf16))
        p_c, p_n = _softmax_parts([s_c, s_n])
        pd_c = p_c[:tq] - lam * p_c[tq:]
        pd_n = p_n[:tq] - lam * p_n[tq:]
        o = (_dot(pd_c.astype(bf16), vc_ref[0, :, ln].astype(bf16))
             + _dot(pd_n.astype(bf16), vn_ref[:, ln].astype(bf16)))
        o_ref[:, ln] = _sub_ln(o, subg_ref[...], lam_init).astype(bf16)


def _latent_diff_attention(proj, cache_k, cache_v, lamp, subg, lam_init):
    tq = DIFF_Q_ROWS
    nqb = SAMPLE_LEN // tq
    q0 = NP // tq
    s0 = NP // SAMPLE_LEN
    hw = 128 * DIFF_HEADS_PER_STEP
    kcol = SEG // hw
    return pl.pallas_call(
        functools.partial(_latent_diff_kernel, lam_init=lam_init, tq=tq),
        grid=(N_SAMPLE_SEQ, DIFF_HEADS // DIFF_HEADS_PER_STEP, nqb),
        in_specs=[
            pl.BlockSpec((tq, hw), lambda b, h, q: (q0 + b * nqb + q, h)),
            pl.BlockSpec((SAMPLE_LEN, hw), lambda b, h, q: (s0 + b, kcol + h)),
            pl.BlockSpec((SAMPLE_LEN, hw), lambda b, h, q: (s0 + b, 2 * kcol + h)),
            pl.BlockSpec((1, PAST, hw), lambda b, h, q: (b, 0, h)),
            pl.BlockSpec((1, PAST, hw), lambda b, h, q: (b, 0, h)),
            pl.BlockSpec((4, HD), lambda b, h, q: (0, 0)),
            pl.BlockSpec((1, 128), lambda b, h, q: (0, 0)),
        ],
        out_specs=pl.BlockSpec((tq, hw), lambda b, h, q: (b * nqb + q, h)),
        out_shape=jax.ShapeDtypeStruct((NS, SEG), bf16),
        compiler_params=pltpu.CompilerParams(
            dimension_semantics=("parallel", "parallel", "arbitrary"), vmem_limit_bytes=VMEM_LIMIT),
        name="latent_diff_attention",
    )(proj, proj, proj, cache_k, cache_v, lamp, subg)


def _rpb_table_kernel(rpb_ref, o_ref):
    h = pl.program_id(0)
    wq = lax.broadcasted_iota(jnp.int32, (GRID_W, GRID_W), 0)
    wk = lax.broadcasted_iota(jnp.int32, (GRID_W, GRID_W), 1)
    col_start = jnp.clip(wq - NA_KW // 2, 0, GRID_W - NA_KW)
    col_in = jnp.logical_and(wk >= col_start, wk < col_start + NA_KW)
    col_off = jnp.clip(wk - wq, -(NA_KW - 1), NA_KW - 1) + (NA_KW - 1)
    n_dr = 2 * NA_KH - 1
    n_dc = 2 * NA_KW - 1
    for dr in range(n_dr):
        t = jnp.zeros((GRID_W, GRID_W), f32)
        for c in range(n_dc):
            t = jnp.where(col_off == c, rpb_ref[h * (n_dr * n_dc) + dr * n_dc + c], t)
        o_ref[0, dr] = jnp.where(col_in, t, NEG_BIG)


def _rpb_table(rpb):
    n_dr = 2 * NA_KH - 1
    tcol = pl.pallas_call(
        _rpb_table_kernel,
        grid=(NA_HEADS,),
        in_specs=[pl.BlockSpec(memory_space=pltpu.SMEM)],
        out_specs=pl.BlockSpec((1, n_dr, GRID_W, GRID_W), lambda h: (h, 0, 0, 0)),
        out_shape=jax.ShapeDtypeStruct((NA_HEADS, n_dr, GRID_W, GRID_W), f32),
        name="rpb_table",
    )(rpb.reshape(-1))
    return jnp.stack(
        [jnp.concatenate([tcol[:, j - s + NA_KH - 1] for j in range(NA_KH)], axis=-1) for s in range(NA_KH)],
        axis=1)


def _latent_na_kernel(q_ref, k_ref, v_ref, kc_ref, vc_ref, bias_ref, o_ref):
    scale = HD ** -0.5
    lo = _lane_lo()
    kc = kc_ref[0].astype(bf16)
    vc = vc_ref[0].astype(bf16)
    win = NA_KH * GRID_W

    nr = NA_ROWS_PER_STEP
    kcb = jnp.broadcast_to(kc[None], (nr,) + kc.shape)
    vcb = jnp.broadcast_to(vc[None], (nr,) + vc.shape)

    def rows(g, carry):
        qs, kws, vws, biases = [], [], [], []
        for t in range(nr):
            r = g * nr + t
            rs = jnp.clip(r - NA_KH // 2, 0, GRID_R - NA_KH)
            sidx = r - rs
            q = q_ref[pl.ds(pl.multiple_of(r * GRID_W, GRID_W), GRID_W), :]
            qs.append((_stack_halves(q) * scale).astype(bf16))
            k0 = pl.multiple_of(rs * GRID_W, GRID_W)
            kws.append(k_ref[pl.ds(k0, win), :].astype(bf16))
            vws.append(v_ref[pl.ds(k0, win), :].astype(bf16))
            biases.append(jnp.concatenate([bias_ref[0, sidx], bias_ref[1, sidx]], axis=0))
        qq = jnp.stack(qs)
        s_loc = _bdot_nt(qq, jnp.stack(kws)) + jnp.stack(biases)
        s_ctx = _bdot_nt(qq, kcb)
        p_loc, p_ctx = _softmax_parts([s_loc, s_ctx])
        o = _bdot(p_loc.astype(bf16), jnp.stack(vws)) + _bdot(p_ctx.astype(bf16), vcb)
        o = jnp.where(lo, o[:, :GRID_W], o[:, GRID_W:]).astype(bf16)
        row0 = pl.multiple_of(g * (nr * GRID_W), nr * GRID_W)
        o_ref[pl.ds(row0, nr * GRID_W), :] = o.reshape(nr * GRID_W, 128)
        return carry

    lax.fori_loop(0, GRID_R // nr, rows, 0)


def _latent_na_attention(proj, cache_k, cache_v, bias):
    s0 = NP // SAMPLE_LEN
    return pl.pallas_call(
        _latent_na_kernel,
        grid=(N_SAMPLE_SEQ, NA_HEADS // 2),
        in_specs=[
            pl.BlockSpec((SAMPLE_LEN, 128), lambda b, h: (s0 + b, 12 + h)),
            pl.BlockSpec((SAMPLE_LEN, 128), lambda b, h: (s0 + b, 16 + h)),
            pl.BlockSpec((SAMPLE_LEN, 128), lambda b, h: (s0 + b, 20 + h)),
            pl.BlockSpec((1, PAST, 128), lambda b, h: (b, 0, h)),
            pl.BlockSpec((1, PAST, 128), lambda b, h: (b, 0, h)),
            pl.BlockSpec((2, NA_KH, GRID_W, NA_KH * GRID_W), lambda b, h: (h, 0, 0, 0)),
        ],
        out_specs=pl.BlockSpec((SAMPLE_LEN, 128), lambda b, h: (b, h)),
        out_shape=jax.ShapeDtypeStruct((NS, SEG), bf16),
        compiler_params=pltpu.CompilerParams(
            dimension_semantics=("parallel", "parallel"), vmem_limit_bytes=VMEM_LIMIT),
        name="latent_na_attention",
    )(proj, proj, proj, cache_k, cache_v, bias)


def _out_proj_kernel(ap_ref, ad_ref, an_ref, w_ref, xp_ref, xs_ref, gt_ref, o_ref, *, tm):
    i = pl.program_id(0)

    @pl.when(i < NP // tm)
    def _():
        o_ref[...] = xp_ref[...] + gt_ref[0] * _dot(ap_ref[...], w_ref[...])

    @pl.when(i >= NP // tm)
    def _():
        mixed = _dot(ad_ref[...], w_ref[:SEG, :]) + _dot(an_ref[...], w_ref[SEG:, :])
        o_ref[...] = xs_ref[...] + gt_ref[0] * mixed


def _out_projection(a_prompt, a_diff, a_na, w_bf, xp, xs, gate):
    tm = OUT_PROJ_ROWS
    n_p = NP // tm
    cidx = functools.partial(_cond_idx, tm=tm)
    prompt_rows = lambda i: (jnp.minimum(i, n_p - 1), 0)
    sample_rows = lambda i: (jnp.maximum(i - n_p, 0), 0)
    return pl.pallas_call(
        functools.partial(_out_proj_kernel, tm=tm),
        grid=(NT // tm,),
        in_specs=[
            pl.BlockSpec((tm, D), prompt_rows),
            pl.BlockSpec((tm, SEG), sample_rows),
            pl.BlockSpec((tm, SEG), sample_rows),
            pl.BlockSpec((D, D), lambda i: (0, 0)),
            pl.BlockSpec((tm, D), prompt_rows),
            pl.BlockSpec((tm, D), sample_rows),
            pl.BlockSpec((1, 1, D), lambda i: (cidx(i), 0, 0)),
        ],
        out_specs=pl.BlockSpec((tm, D), lambda i: (i, 0)),
        out_shape=jax.ShapeDtypeStruct((NT, D), f32),
        compiler_params=pltpu.CompilerParams(
            dimension_semantics=("arbitrary",), vmem_limit_bytes=VMEM_LIMIT),
        name="out_projection",
    )(a_prompt, a_diff, a_na, w_bf, xp, xs, gate)


def _rwkv_out_kernel(yfp_ref, ybp_ref, yfs_ref, ybs_ref, bonus_ref, gate_ref, lng_ref, lnb_ref,
                     gmat_ref, w_ref, x_ref, gt_ref, o_ref, *, tm):
    i = pl.program_id(0)

    def finish(y):
        gmat = gmat_ref[...]
        mu = _group_sum(y, gmat) * (1.0 / HD)
        yc = y - mu
        var = _group_sum(yc * yc, gmat) * (1.0 / HD)
        z = yc * lax.rsqrt(var + GN_EPS) * lng_ref[...] + lnb_ref[...] + bonus_ref[...].astype(f32)
        z = (z * gate_ref[...].astype(f32)).astype(bf16)
        o_ref[...] = x_ref[...] + gt_ref[0] * _dot(z, w_ref[...])

    @pl.when(i < NP // tm)
    def _():
        finish(yfp_ref[...].astype(f32) + ybp_ref[...].astype(f32))

    @pl.when(i >= NP // tm)
    def _():
        finish(yfs_ref[...].astype(f32) + ybs_ref[...].astype(f32))


def _rwkv_out_projection(yf_p, yb_p, yf_s, yb_s, bonus, gate_lora, ln_g, ln_b, gmat, w_bf, x, gate):
    tm = RWKV_OUT_ROWS
    n_p = NP // tm
    cidx = functools.partial(_cond_idx, tm=tm)
    row = pl.BlockSpec((tm, D), lambda i: (i, 0))
    prompt_row = pl.BlockSpec((tm, D), lambda i: (jnp.minimum(i, n_p - 1), 0))
    sample_row = pl.BlockSpec((tm, D), lambda i: (jnp.maximum(i - n_p, 0), 0))
    vec = pl.BlockSpec((1, D), lambda i: (0, 0))
    return pl.pallas_call(
        functools.partial(_rwkv_out_kernel, tm=tm),
        grid=(NT // tm,),
        in_specs=[prompt_row, prompt_row, sample_row, sample_row, row, row, vec, vec,
                  pl.BlockSpec((256, 256), lambda i: (0, 0)),
                  pl.BlockSpec((D, D), lambda i: (0, 0)),
                  row,
                  pl.BlockSpec((1, 1, D), lambda i: (cidx(i), 0, 0))],
        out_specs=row,
        out_shape=jax.ShapeDtypeStruct((NT, D), f32),
        compiler_params=pltpu.CompilerParams(
            dimension_semantics=("arbitrary",), vmem_limit_bytes=VMEM_LIMIT),
        name="rwkv_out_projection",
    )(yf_p, yb_p, yf_s, yb_s, bonus, gate_lora, ln_g, ln_b, gmat, w_bf, x, gate)


def _swiglu_hidden(xb, w1, w3):
    a = _dot(xb, w1)
    return ((a * _sigmoid(a)) * _dot(xb, w3)).astype(bf16)


def _ffn_kernel(x_ref, sh_ref, sc_ref, gt_ref, g_ref, w1_ref, w3_ref, w2_ref, o_ref, h_scr, acc_scr):
    f = pl.program_id(1)

    @pl.when(f == 0)
    def _():
        h_scr[...] = _norm_mod(x_ref[...], g_ref[...], sh_ref[0], sc_ref[0]).astype(bf16)
        acc_scr[...] = jnp.zeros_like(acc_scr)

    acc_scr[...] += _dot(_swiglu_hidden(h_scr[...], w1_ref[...], w3_ref[...]), w2_ref[...])

    @pl.when(f == pl.num_programs(1) - 1)
    def _():
        o_ref[...] = x_ref[...] + gt_ref[0] * acc_scr[...]


def _dense_ffn(x, sh, sc, gt, g, w1, w3, w2):
    tm = FFN_ROWS
    tf = FFN_COLS
    cidx = functools.partial(_cond_idx, tm=tm)
    mod = pl.BlockSpec((1, 1, D), lambda i, f: (cidx(i), 0, 0))
    return pl.pallas_call(
        _ffn_kernel,
        grid=(NT // tm, FFN_DIM // tf),
        in_specs=[
            pl.BlockSpec((tm, D), lambda i, f: (i, 0)),
            mod, mod, mod,
            pl.BlockSpec((1, D), lambda i, f: (0, 0)),
            pl.BlockSpec((D, tf), lambda i, f: (0, f)),
            pl.BlockSpec((D, tf), lambda i, f: (0, f)),
            pl.BlockSpec((tf, D), lambda i, f: (f, 0)),
        ],
        out_specs=pl.BlockSpec((tm, D), lambda i, f: (i, 0)),
        out_shape=jax.ShapeDtypeStruct((NT, D), f32),
        scratch_shapes=[pltpu.VMEM((tm, D), bf16), pltpu.VMEM((tm, D), f32)],
        compiler_params=pltpu.CompilerParams(
            dimension_semantics=("parallel", "arbitrary"), vmem_limit_bytes=VMEM_LIMIT),
        name="ffn_mixer",
    )(x, sh, sc, gt, g, w1, w3, w2)


MOE_TILE = 1024
MOE_MAIN = 288
MOE_EXTRA = 128
MOE_ROWS = MOE_MAIN + -(-(MOE_TILE - MOE_MAIN) // MOE_EXTRA) * MOE_EXTRA


RANK_SEL = 4096


def _router_kernel(x_ref, sh_ref, sc_ref, g_ref, wr_ref, br_ref, tri_ref, triu_ref,
                   h_out, gates_out, rank_out, rank_t_out, cnt_out):
    h = _norm_mod(x_ref[...], g_ref[...], sh_ref[0], sc_ref[0])
    h_out[...] = h.astype(bf16)
    w = wr_ref[...]
    h_hi, w_hi = h.astype(bf16), w.astype(bf16)
    h_lo = (h - h_hi.astype(f32)).astype(bf16)
    w_lo = (w - w_hi.astype(f32)).astype(bf16)
    hw = _dot(h_hi, jnp.concatenate([w_hi, w_lo], axis=1))
    logits = hw[:, :128] + (hw[:, 128:] + _dot(h_lo, w_hi)) + br_ref[...]
    lane = lax.broadcasted_iota(jnp.int32, logits.shape, 1)
    logits = jnp.where(lane < N_EXPERTS, logits, -jnp.inf)
    m1 = logits.max(axis=-1, keepdims=True)
    i1 = jnp.min(jnp.where(logits == m1, lane, 128), axis=-1, keepdims=True)
    rest = jnp.where(lane == i1, -jnp.inf, logits)
    m2 = rest.max(axis=-1, keepdims=True)
    i2 = jnp.min(jnp.where(rest == m2, lane, 128), axis=-1, keepdims=True)
    e2 = jnp.exp(m2 - m1)
    den = 1.0 / (1.0 + e2)
    gates_out[...] = jnp.where(lane == i1, den, jnp.where(lane == i2, e2 * den, 0.0))
    sel = jnp.logical_or(lane == i1, lane == i2)
    self32 = sel.astype(f32)
    selb = self32.astype(bf16)
    rank_out[...] = jnp.where(sel, _dot(tri_ref[...], selb), -1.0)
    rank_t_out[...] = _dot_tn(selb, triu_ref[...])
    cnt_out[0] = jnp.sum(self32, axis=0, keepdims=True).astype(jnp.int32)


def _moe_route(x, sh, sc, g, w_router, b_router):
    tm = MOE_TILE
    cidx = functools.partial(_cond_idx, tm=tm)
    t = jnp.arange(tm)
    tri = (t[:, None] > t[None, :]).astype(bf16)
    triu = jnp.where(t[:, None] == t[None, :], float(RANK_SEL), tri.T.astype(f32)).astype(bf16)
    mod = pl.BlockSpec((1, 1, D), lambda i: (cidx(i), 0, 0))
    return pl.pallas_call(
        _router_kernel,
        grid=(NT // tm,),
        in_specs=[
            pl.BlockSpec((tm, D), lambda i: (i, 0)),
            mod, mod,
            pl.BlockSpec((1, D), lambda i: (0, 0)),
            pl.BlockSpec((D, 128), lambda i: (0, 0)),
            pl.BlockSpec((1, 128), lambda i: (0, 0)),
            pl.BlockSpec((tm, tm), lambda i: (0, 0)),
            pl.BlockSpec((tm, tm), lambda i: (0, 0)),
        ],
        out_specs=[
            pl.BlockSpec((tm, D), lambda i: (i, 0)),
            pl.BlockSpec((tm, 128), lambda i: (i, 0)),
            pl.BlockSpec((tm, 128), lambda i: (i, 0)),
            pl.BlockSpec((128, tm), lambda i: (0, i)),
            pl.BlockSpec((1, 1, 128), lambda i: (i, 0, 0)),
        ],
        out_shape=[
            jax.ShapeDtypeStruct((NT, D), bf16),
            jax.ShapeDtypeStruct((NT, 128), f32),
            jax.ShapeDtypeStruct((NT, 128), f32),
            jax.ShapeDtypeStruct((128, NT), f32),
            jax.ShapeDtypeStruct((NT // tm, 1, 128), jnp.int32),
        ],
        compiler_params=pltpu.CompilerParams(
            dimension_semantics=("parallel",), vmem_limit_bytes=VMEM_LIMIT),
        name="moe_router",
    )(x, sh, sc, g, w_router, b_router, tri, triu)


def _moe_kernel(cnt_ref, h_ref, rank_ref, rank_t_ref, gates_ref, x_ref, gt_ref, w1_ref, w3_ref, w2_ref,
                o_ref, xc_scr, acc_scr, rcol_scr, gcol_scr):
    i = pl.program_id(0)
    e = pl.program_id(1)
    f = pl.program_id(2)
    cnt = cnt_ref[i * N_EXPERTS + e]
    n_extra = jnp.maximum(cnt - MOE_MAIN + MOE_EXTRA - 1, 0) // MOE_EXTRA

    def for_each_block(fn):
        fn(0, MOE_MAIN)

        def body(b, carry):
            fn(pl.multiple_of(MOE_MAIN + b * MOE_EXTRA, 32), MOE_EXTRA)
            return carry

        lax.fori_loop(0, n_extra, body, 0)

    @pl.when(jnp.logical_and(e == 0, f == 0))
    def _():
        o_ref[...] = x_ref[...]

    @pl.when(f == 0)
    def _():
        lane = lax.broadcasted_iota(jnp.int32, (MOE_TILE, 128), 1)
        mine = lane == e
        rcol_scr[...] = jnp.sum(jnp.where(mine, rank_ref[...], 0.0), axis=1, keepdims=True).astype(jnp.int32)
        gcol_scr[...] = jnp.sum(jnp.where(mine, gates_ref[...], 0.0), axis=1, keepdims=True)

        rrow = rank_t_ref[pl.ds(e, 1), :].astype(jnp.int32)

        def gather(slot0, nrows):
            slot = lax.broadcasted_iota(jnp.int32, (nrows, MOE_TILE), 0) + (slot0 + RANK_SEL)
            xc_scr[pl.ds(slot0, nrows), :] = _dot((rrow == slot).astype(bf16), h_ref[...]).astype(bf16)

        for_each_block(gather)

    def expert(slot0, nrows):
        rows = pl.ds(slot0, nrows)
        part = _dot(_swiglu_hidden(xc_scr[rows, :], w1_ref[0], w3_ref[0]), w2_ref[0])

        @pl.when(f == 0)
        def _():
            acc_scr[rows, :] = part

        @pl.when(f != 0)
        def _():
            acc_scr[rows, :] += part

    for_each_block(expert)

    @pl.when(f == pl.num_programs(2) - 1)
    def _():
        def scatter(slot0, nrows):
            out = acc_scr[pl.ds(slot0, nrows), :].astype(bf16)
            for t0 in range(0, MOE_TILE, 256):
                rows = slice(t0, t0 + 256)
                slot = lax.broadcasted_iota(jnp.int32, (256, nrows), 1) + slot0
                pt = (rcol_scr[rows, :] == slot).astype(bf16)
                o_ref[rows, :] += (gcol_scr[rows, :] * gt_ref[0]) * _dot(pt, out)

        for_each_block(scatter)


def _moe_experts(cnt, h_bf, rank, rank_t, gates, x, gt, w1, w3, w2):
    tm = MOE_TILE
    tf = EXPERT_DIM // 2
    cidx = functools.partial(_cond_idx, tm=tm)
    once = pl.Buffered(1)
    grid_spec = pltpu.PrefetchScalarGridSpec(
        num_scalar_prefetch=1,
        grid=(NT // tm, N_EXPERTS, EXPERT_DIM // tf),
        in_specs=[
            pl.BlockSpec((tm, D), lambda i, e, f, c: (i, 0), pipeline_mode=once),
            pl.BlockSpec((tm, 128), lambda i, e, f, c: (i, 0), pipeline_mode=once),
            pl.BlockSpec((128, tm), lambda i, e, f, c: (0, i), pipeline_mode=once),
            pl.BlockSpec((tm, 128), lambda i, e, f, c: (i, 0), pipeline_mode=once),
            pl.BlockSpec((tm, D), lambda i, e, f, c: (i, 0), pipeline_mode=once),
            pl.BlockSpec((1, 1, D), lambda i, e, f, c: (cidx(i), 0, 0)),
            pl.BlockSpec((1, D, tf), lambda i, e, f, c: (e, 0, f)),
            pl.BlockSpec((1, D, tf), lambda i, e, f, c: (e, 0, f)),
            pl.BlockSpec((1, tf, D), lambda i, e, f, c: (e, f, 0)),
        ],
        out_specs=pl.BlockSpec((tm, D), lambda i, e, f, c: (i, 0)),
        scratch_shapes=[pltpu.VMEM((MOE_ROWS, D), bf16), pltpu.VMEM((MOE_ROWS, D), f32),
                        pltpu.VMEM((tm, 1), jnp.int32), pltpu.VMEM((tm, 1), f32)],
    )
    return pl.pallas_call(
        _moe_kernel,
        grid_spec=grid_spec,
        out_shape=jax.ShapeDtypeStruct((NT, D), f32),
        compiler_params=pltpu.CompilerParams(
            dimension_semantics=("parallel", "arbitrary", "arbitrary"), vmem_limit_bytes=VMEM_LIMIT),
        name="moe_experts",
    )(cnt, h_bf, rank, rank_t, gates, x, gt, w1, w3, w2)


def _in_context_key(k, a, k_a):
    return k * (1.0 + (a - 1.0) * k_a)


def _rwkv_proj_kernel(x_ref, xp_ref, xn_ref, sh_ref, sc_ref, g_ref, mu_ref,
                      wr_ref, wk_ref, wv_ref, g1_ref, g2_ref, w1_ref, w2_ref, a1_ref, a2_ref,
                      w0_ref, a0_ref, kk_ref, ka_ref, rk_ref, gmat_ref,
                      r_out, v_out, kkn_out, bonus_out, gate_out,
                      lwf_out, lwb_out, af_out, ab_out, k_out, *, tm):
    i = pl.program_id(0)
    g, sh, sc = g_ref[...], sh_ref[0], sc_ref[0]
    h = _norm_mod(x_ref[...], g, sh, sc)
    n_prompt_tiles = NP // tm
    tiles_per_seq = SAMPLE_LEN // tm
    pos = (i - n_prompt_tiles) % tiles_per_seq
    has_prev = jnp.logical_and(i >= n_prompt_tiles, pos != 0)
    has_next = jnp.logical_and(i >= n_prompt_tiles, pos != tiles_per_seq - 1)
    h_before = jnp.where(has_prev, _norm_mod(xp_ref[...], g, sh, sc)[7:8], 0.0)
    h_after = jnp.where(has_next, _norm_mod(xn_ref[...], g, sh, sc)[0:1], 0.0)
    rowi = lax.broadcasted_iota(jnp.int32, (tm, 1), 0)
    h_prev = jnp.where(rowi == 0, h_before, pltpu.roll(h, 1, 0))
    h_next = jnp.where(rowi == tm - 1, h_after, pltpu.roll(h, tm - 1, 0))
    xx = 0.5 * (h_prev + h_next) - h
    mix = lambda n: (h + xx * mu_ref[n:n + 1]).astype(bf16)

    r = _dot(mix(0), wr_ref[...])
    k = _dot(mix(2), wk_ref[...])
    v = _dot(mix(3), wv_ref[...])
    gate_out[...] = _dot(_sigmoid(_dot(mix(5), g1_ref[...])).astype(bf16), g2_ref[...]).astype(bf16)

    lo = _lane_lo()
    tw = jnp.tanh(_dot(mix(1), w1_ref[...]))
    ta = _dot(mix(4), a1_ref[...])
    gmat = gmat_ref[...]
    kk = k * kk_ref[...]
    kkn_out[...] = kk * lax.rsqrt(_group_sum(kk * kk, gmat) + 1e-12)
    r_out[...] = r
    v_out[...] = v
    k_out[...] = k
    kd_sum = jnp.zeros_like(k)
    for d, (lw_out, a_out) in enumerate(((lwf_out, af_out), (lwb_out, ab_out))):
        keep = lo if d == 0 else jnp.logical_not(lo)
        zw = w0_ref[d:d + 1] + _dot(jnp.where(keep, tw, 0.0).astype(bf16), w2_ref[...])
        lw_out[...] = -math.exp(-0.5) * _sigmoid(zw)
        a = _sigmoid(a0_ref[d:d + 1] + _dot(jnp.where(keep, ta, 0.0).astype(bf16), a2_ref[...]))
        a_out[...] = a
        kd_sum = kd_sum + _in_context_key(k, a, ka_ref[...])
    bonus_out[...] = (_group_sum(r * kd_sum * rk_ref[...], gmat) * v).astype(bf16)


def _rwkv_projection(x, sh, sc, g, mu, wr, wk, wv, g1, g2, w1, w2, a1, a2, w0, a0, k_k, k_a, r_k, gmat):
    tm = RWKV_PROJ_ROWS
    cidx = functools.partial(_cond_idx, tm=tm)
    hb = tm // 8
    n8 = NT // 8
    full = lambda shape: pl.BlockSpec(shape, lambda i: tuple(0 for _ in shape))
    row = pl.BlockSpec((tm, D), lambda i: (i, 0))
    return pl.pallas_call(
        functools.partial(_rwkv_proj_kernel, tm=tm),
        grid=(NT // tm,),
        in_specs=[
            row,
            pl.BlockSpec((8, D), lambda i: (jnp.maximum(i * hb - 1, 0), 0)),
            pl.BlockSpec((8, D), lambda i: (jnp.minimum((i + 1) * hb, n8 - 1), 0)),
            pl.BlockSpec((1, 1, D), lambda i: (cidx(i), 0, 0)),
            pl.BlockSpec((1, 1, D), lambda i: (cidx(i), 0, 0)),
            full((1, D)), full((6, D)),
            full((D, D)), full((D, D)), full((D, D)),
            full((D, GATE_LORA_PAD)), full((GATE_LORA_PAD, D)),
            full((D, 2 * LORA)), full((2 * LORA, D)), full((D, 2 * LORA)), full((2 * LORA, D)),
            full((2, D)), full((2, D)), full((1, D)), full((1, D)), full((1, D)),
            full((256, 256)),
        ],
        out_specs=[row] * 10,
        out_shape=[jax.ShapeDtypeStruct((NT, D), dt) for dt in (f32, f32, f32, bf16, bf16) + (f32,) * 5],
        compiler_params=pltpu.CompilerParams(
            dimension_semantics=("parallel",), vmem_limit_bytes=VMEM_LIMIT),
        name="rwkv_projection",
    )(x, x, x, sh, sc, g, mu, wr, wk, wv, g1, g2, w1, w2, a1, a2, w0, a0, k_k, k_a, r_k, gmat)


def _split3(x):
    x1 = x.astype(bf16)
    r1 = x - x1.astype(f32)
    x2 = r1.astype(bf16)
    x3 = (r1 - x2.astype(f32)).astype(bf16)
    return x1, x2, x3


def _scan_kernel(*refs, nc, has_init, emit_state, n_casts):
    fwd_refs = refs[0:6]
    bwd_refs = refs[6:12]
    ka_ref, tri_ref, mask_ref = refs[12:15]
    pos = 15
    s0_ref = None
    if has_init:
        s0_ref = refs[pos]
        pos += 1
    cast_in = refs[pos:pos + n_casts]
    pos += n_casts
    yf_ref, yb_ref = refs[pos:pos + 2]
    pos += 2
    sout_ref = None
    if emit_state:
        sout_ref = refs[pos]
        pos += 1
    cast_out = refs[pos:pos + n_casts]
    pos += n_casts
    st_scr, cl_scr = refs[pos:pos + 2]

    for src, dst in zip(cast_in, cast_out):
        dst[...] = src[...].astype(bf16)

    s = pl.program_id(1)

    @pl.when(s == 0)
    def _():
        if has_init:
            st_scr[...] = s0_ref[0]
        else:
            st_scr[...] = jnp.zeros_like(st_scr)

    c = CHUNK
    for d, drefs in enumerate((fwd_refs, bwd_refs)):
        lw = drefs[3][...]
        tri = tri_ref[d]
        p1, p2, p3 = _split3(lw)
        cl_scr[d] = _dot(tri, p1) + _dot(tri, p2) + _dot(tri, p3)

    lane = lax.broadcasted_iota(jnp.int32, (1, 128), 1)
    m0 = (lane < HD).astype(f32)
    m1 = 1.0 - m0
    rid = lax.broadcasted_iota(jnp.int32, (128, 128), 0)
    cid = lax.broadcasted_iota(jnp.int32, (128, 128), 1)
    eye = (rid == cid).astype(f32)

    n_pairs = RW_HEADS // 2
    nb = 2 * n_pairs

    def per_head_rows(x):
        return jnp.concatenate([x * m0, x * m1], axis=0)

    ars_l, bk_l, bkh_l, v2_l, dec_l = [], [], [], [], []
    for d, drefs in enumerate((fwd_refs, bwd_refs)):
        r_ref, v_ref, kk_ref, lw_ref, a_ref, k_ref = drefs
        end_row = c - 1 if d == 0 else 0
        for p in range(n_pairs):
            ln = slice(p * 128, (p + 1) * 128)
            cl = cl_scr[d, :, ln]
            kk = kk_ref[:, ln]
            tot = cl[end_row:end_row + 1]
            e_inv = jnp.exp(-cl)
            e_end = jnp.exp(tot - cl)
            a = a_ref[:, ln]
            kka = kk * a
            kd = _in_context_key(k_ref[:, ln], a, ka_ref[:, ln])
            at = per_head_rows(-kk * jnp.exp(cl - lw_ref[:, ln]))
            rt = per_head_rows(r_ref[:, ln] * jnp.exp(cl))
            ars_l.append(jnp.concatenate([at, rt], axis=0).astype(bf16))
            bk_l.append(jnp.concatenate([per_head_rows(kka * e_inv), per_head_rows(kd * e_inv)],
                                        axis=0).astype(bf16))
            bkh_l.append(jnp.concatenate([per_head_rows(kka * e_end), per_head_rows(kd * e_end)],
                                         axis=0).astype(bf16))
            v2_l.append(per_head_rows(v_ref[:, ln]).astype(bf16))
            dec_l.append(jnp.exp(tot))
    ar = jnp.stack(ars_l)
    bk = jnp.stack(bk_l)
    bkh = jnp.stack(bkh_l)
    v2 = jnp.stack(v2_l)
    dec = jnp.stack(dec_l)

    st = st_scr[...].reshape(nb, 128, 128)
    g2 = _bdot_nt(ar, bk).reshape(2, n_pairs, 256, 256) * mask_ref[...][:, None]
    g2 = g2.reshape(nb, 256, 256)
    ars = _bdot_nt(ar, st.astype(bf16))
    gv = _bdot(g2[:, :, 128:].astype(bf16), v2)
    l_bd = g2[:, :128, :128]
    l_bf = l_bd.astype(bf16)
    pk = _bdot(l_bf, l_bf)
    q = eye[None] + l_bd
    for it in range(5):
        pkb = pk.astype(bf16)
        if it < 4:
            res = _bdot(jnp.concatenate([q, pk], axis=1).astype(bf16), pkb)
            q = q + res[:, :128]
            pk = res[:, 128:]
        else:
            q = q + _bdot(q.astype(bf16), pkb)
    rhs = ars[:, :128] + gv[:, :128]
    u2b = _bdot(q.astype(bf16), rhs.astype(bf16)).astype(bf16)
    y2 = ars[:, 128:] + gv[:, 128:] + _bdot(g2[:, 128:, :128].astype(bf16), u2b)
    y = y2[:, :c] + y2[:, c:]
    for d, y_ref in enumerate((yf_ref, yb_ref)):
        for p in range(n_pairs):
            y_ref[:, p * 128:(p + 1) * 128] = y[d * n_pairs + p].astype(y_ref.dtype)
    uv = jnp.concatenate([u2b, v2], axis=1)
    st_scr[...] = (st * dec + _bdot_tn(uv, bkh)).reshape(2, n_pairs, 128, 128)

    if emit_state:
        @pl.when(s == nc - 1)
        def _():
            for d in range(2):
                for p in range(n_pairs):
                    m = st_scr[d, p]
                    sout_ref[0, d, 2 * p] = m[:HD, :HD]
                    sout_ref[0, d, 2 * p + 1] = pltpu.roll(m[HD:, :], HD, 1)[:, :HD]


def _rwkv_scan(streams_f, streams_b, k_a, tri, mask, s0_bd, casts, *, n_seq, seq_len, row0, emit_state):
    nc = seq_len // CHUNK
    blk0 = row0 // CHUNK
    fwd_spec = pl.BlockSpec((CHUNK, D), lambda b, s: (blk0 + b * nc + s, 0))
    bwd_spec = pl.BlockSpec((CHUNK, D), lambda b, s: (blk0 + b * nc + nc - 1 - s, 0))
    in_specs = [fwd_spec] * 6 + [bwd_spec] * 6 + [
        pl.BlockSpec((1, D), lambda b, s: (0, 0)),
        pl.BlockSpec((2, CHUNK, CHUNK), lambda b, s: (0, 0, 0)),
        pl.BlockSpec((2, 256, 256), lambda b, s: (0, 0, 0)),
    ]
    args = list(streams_f) + list(streams_b) + [k_a, tri, mask]
    state_block = (1, 2, RW_HEADS // 2, 128, 128)
    if s0_bd is not None:
        in_specs.append(pl.BlockSpec(state_block, lambda b, s: (b, 0, 0, 0, 0)))
        args.append(s0_bd)
    out_specs = [pl.BlockSpec((CHUNK, D), lambda b, s: (b * nc + s, 0)),
                 pl.BlockSpec((CHUNK, D), lambda b, s: (b * nc + nc - 1 - s, 0))]
    out_shape = [jax.ShapeDtypeStruct((n_seq * seq_len, D), bf16)] * 2
    if emit_state:
        out_specs.append(pl.BlockSpec((1, 2, RW_HEADS, HD, HD), lambda b, s: (b, 0, 0, 0, 0)))
        out_shape.append(jax.ShapeDtypeStruct((n_seq, 2, RW_HEADS, HD, HD), f32))
    for w in casts:
        per_expert = (n_seq * nc) // w.shape[0]
        spec = pl.BlockSpec((1, w.shape[1] // per_expert, w.shape[2]),
                            lambda b, s, per_expert=per_expert: ((b * nc + s) // per_expert, (b * nc + s) % per_expert, 0))
        in_specs.append(spec)
        args.append(w)
        out_specs.append(spec)
        out_shape.append(jax.ShapeDtypeStruct(w.shape, bf16))
    return pl.pallas_call(
        functools.partial(_scan_kernel, nc=nc, has_init=s0_bd is not None, emit_state=emit_state,
                          n_casts=len(casts)),
        grid=(n_seq, nc),
        in_specs=in_specs,
        out_specs=out_specs,
        out_shape=out_shape,
        scratch_shapes=[pltpu.VMEM(state_block[1:], f32), pltpu.VMEM((2, CHUNK, D), f32)],
        compiler_params=pltpu.CompilerParams(
            dimension_semantics=("parallel", "arbitrary"), vmem_limit_bytes=VMEM_LIMIT),
        name="rwkv_scan_prompt" if emit_state else "rwkv_scan_sample",
    )(*args)


def _scan_constants():
    t = jnp.arange(CHUNK)
    lower = (t[:, None] >= t[None, :])
    tri = jnp.stack([lower, lower.T]).astype(bf16)
    masks = []
    for d in range(2):
        strict = (t[:, None] > t[None, :]) if d == 0 else (t[:, None] < t[None, :])
        incl = lower if d == 0 else lower.T
        blocks = []
        for m in (strict, incl):
            bd = jnp.kron(jnp.eye(2, dtype=f32), m.astype(f32))
            blocks.append(jnp.concatenate([bd, bd], axis=1))
        masks.append(jnp.concatenate(blocks, axis=0))
    return tri, jnp.stack(masks)


def _state_to_blockdiag(s):
    n = s.shape[0]
    s = s.reshape(n, 2, RW_HEADS // 2, 2, HD, HD)
    z = jnp.zeros_like(s[:, :, :, 0])
    top = jnp.concatenate([s[:, :, :, 0], z], axis=-1)
    bot = jnp.concatenate([z, s[:, :, :, 1]], axis=-1)
    return jnp.concatenate([top, bot], axis=-2)


def _rope_tables():
    t = jnp.arange(SAMPLE_LEN)
    rows = (t // GRID_W).astype(f32)
    cols = (t % GRID_W).astype(f32)
    nf = HD // 4
    inv = 10000.0 ** (-jnp.arange(nf, dtype=f32) / nf)
    ang = jnp.concatenate([rows[:, None] * inv, cols[:, None] * inv], axis=-1)
    cos = jnp.repeat(jnp.cos(ang), 2, axis=-1)
    sin = jnp.repeat(jnp.sin(ang), 2, axis=-1) * jnp.tile(jnp.array([-1.0, 1.0], f32), HD // 2)
    return jnp.tile(cos, (1, 2)), jnp.tile(sin, (1, 2))


def kernel(x_prompt, x_sample, cache_diff_k, cache_diff_v, cache_na_k, cache_na_v, state_rwkv, c, c_ctx, w_ada, b_ada, g_mix, g_ffn, w_in, w_out, diff_q_g, diff_k_g, diff_lam_q1, diff_lam_k1, diff_lam_q2, diff_lam_k2, diff_subln_g, na_q_g, na_k_g, na_rpb, ffn_w1, ffn_w3, ffn_w2, rw_mu, rw_wr, rw_wk, rw_wv, rw_wo, rw_w0, rw_w1, rw_w2, rw_a0, rw_a1, rw_a2, rw_g1, rw_g2, rw_k_k, rw_k_a, rw_r_k, rw_ln_g, rw_ln_b, moe_router, moe_router_b, moe_w1, moe_w3, moe_w2):
    xp = x_prompt.reshape(NP, D)
    xs = x_sample.reshape(NS, D)
    cond8 = jnp.concatenate([c_ctx[None, :], c, jnp.zeros((3, D), f32)], axis=0)
    mod = _ada_table(cond8, w_ada, b_ada)
    gmat = jnp.kron(jnp.eye(4, dtype=f32), jnp.ones((HD, HD), f32)).astype(bf16)

    lam_init = 0.8 - 0.6 * math.exp(-0.3 * 0)
    ones_seg = jnp.ones((SEG,), f32)
    tile8 = lambda gvec: jnp.tile(gvec, SEG // HD)
    gains = jnp.stack([tile8(diff_q_g[0]), tile8(diff_k_g[0]), ones_seg,
                       tile8(na_q_g[0]), tile8(na_k_g[0]), ones_seg]).reshape(6, 1, SEG)
    cos_t, sin_t = _rope_tables()
    proj, dk_p, dv_p, nk_p, nv_p, ffn_w1_bf, ffn_w3_bf, ffn_w2_bf = _in_projection(
        xp, xs, mod[0][0], mod[0][1], g_mix[0][None, :], w_in[0].astype(bf16), gains, gmat, cos_t, sin_t,
        (ffn_w1[0], ffn_w3[0], ffn_w2[0]))
    lamp = jnp.stack([diff_lam_q1[0], diff_lam_k1[0], diff_lam_q2[0], diff_lam_k2[0]])
    subg = diff_subln_g[0][None, :]
    o_prompt = _prompt_attention(proj, lamp, subg, lam_init)
    o_diff = _latent_diff_attention(proj, cache_diff_k[:, 0].reshape(N_SAMPLE_SEQ, PAST, SEG),
                                    cache_diff_v[:, 0].reshape(N_SAMPLE_SEQ, PAST, SEG), lamp, subg, lam_init)
    o_na = _latent_na_attention(proj, cache_na_k[:, 0].reshape(N_SAMPLE_SEQ, PAST, SEG),
                                cache_na_v[:, 0].reshape(N_SAMPLE_SEQ, PAST, SEG), _rpb_table(na_rpb[0]))
    x = _out_projection(o_prompt, o_diff, o_na, w_out[0].astype(bf16), xp, xs, mod[0][2])
    x = _dense_ffn(x, mod[0][3], mod[0][4], mod[0][5], g_ffn[0][None, :],
                   ffn_w1_bf, ffn_w3_bf, ffn_w2_bf)

    pad_g = GATE_LORA_PAD - GATE_LORA
    g1 = jnp.pad(rw_g1[0], ((0, 0), (0, pad_g))).astype(bf16)
    g2 = jnp.pad(rw_g2[0], ((0, pad_g), (0, 0))).astype(bf16)
    w1cat = jnp.concatenate([rw_w1[0, 0], rw_w1[0, 1]], axis=1).astype(bf16)
    w2cat = jnp.concatenate([rw_w2[0, 0], rw_w2[0, 1]], axis=0).astype(bf16)
    a1cat = jnp.concatenate([rw_a1[0, 0], rw_a1[0, 1]], axis=1).astype(bf16)
    a2cat = jnp.concatenate([rw_a2[0, 0], rw_a2[0, 1]], axis=0).astype(bf16)
    k_a = rw_k_a[0][None, :]
    (r, v, kkn, bonus, gate_lora, lwf, lwb, af, ab, k) = _rwkv_projection(
        x, mod[1][0], mod[1][1], g_mix[1][None, :], rw_mu[0],
        rw_wr[0].astype(bf16), rw_wk[0].astype(bf16), rw_wv[0].astype(bf16), g1, g2,
        w1cat, w2cat, a1cat, a2cat, rw_w0[0], rw_a0[0],
        rw_k_k[0][None, :], k_a, rw_r_k[0].reshape(1, D), gmat)
    tri, mask = _scan_constants()
    streams_f = (r, v, kkn, lwf, af, k)
    streams_b = (r, v, kkn, lwb, ab, k)
    yf_p, yb_p, st_p, moe_w1_bf, moe_w3_bf = _rwkv_scan(
        streams_f, streams_b, k_a, tri, mask, None, (moe_w1[0], moe_w3[0]),
        n_seq=N_PROMPT_SEQ, seq_len=PROMPT_LEN, row0=0, emit_state=True)
    yf_s, yb_s, moe_w2_bf = _rwkv_scan(
        streams_f, streams_b, k_a, tri, mask, _state_to_blockdiag(state_rwkv[:, 0]), (moe_w2[0],),
        n_seq=N_SAMPLE_SEQ, seq_len=SAMPLE_LEN, row0=NP, emit_state=False)
    x = _rwkv_out_projection(yf_p, yb_p, yf_s, yb_s, bonus, gate_lora, rw_ln_g[0][None, :], rw_ln_b[0][None, :], gmat,
                             rw_wo[0].astype(bf16), x, mod[1][2])
    w_router = jnp.pad(moe_router[0], ((0, 0), (0, 128 - N_EXPERTS)))
    b_router = jnp.pad(moe_router_b[0], (0, 128 - N_EXPERTS))[None, :]
    h_bf, gates, rank, rank_t, cnt = _moe_route(x, mod[1][3], mod[1][4], g_ffn[1][None, :], w_router, b_router)
    x = _moe_experts(cnt[:, 0, :N_EXPERTS].reshape(-1), h_bf, rank, rank_t, gates, x, mod[1][5],
                     moe_w1_bf, moe_w3_bf, moe_w2_bf)

    new_dk = dk_p.reshape(N_PROMPT_SEQ, 1, PROMPT_LEN, DIFF_HEADS, 2 * HD)
    new_dv = dv_p.reshape(N_PROMPT_SEQ, 1, PROMPT_LEN, DIFF_HEADS, 2 * HD)
    new_nk = nk_p.reshape(N_PROMPT_SEQ, 1, PROMPT_LEN, NA_HEADS, HD)
    new_nv = nv_p.reshape(N_PROMPT_SEQ, 1, PROMPT_LEN, NA_HEADS, HD)
    new_state = st_p[:, None]
    return (x[:NP].reshape(N_PROMPT_SEQ, PROMPT_LEN, D), x[NP:].reshape(N_SAMPLE_SEQ, SAMPLE_LEN, D),
            new_dk, new_dv, new_nk, new_nv, new_state)
```

```python
import functools
import math

import jax
import jax.numpy as jnp
from jax import lax
from jax.experimental import pallas as pl
from jax.experimental.pallas import tpu as pltpu

f32 = jnp.float32
bf16 = jnp.bfloat16

D = 1024
N_PROMPT_SEQ, PROMPT_LEN = 32, 256
N_SAMPLE_SEQ, SAMPLE_LEN = 4, 2048
NP = N_PROMPT_SEQ * PROMPT_LEN
NS = N_SAMPLE_SEQ * SAMPLE_LEN
NT = NP + NS
PAST = 256
GRID_W = 64
GRID_R = SAMPLE_LEN // GRID_W
HD = 64
DIFF_HEADS = 4
NA_HEADS = 8
NA_KH = 8
NA_KW = 16
SEG = 512
IN_COLS = 6 * SEG
FFN_DIM = 2816
N_EXPERTS = 8
EXPERT_DIM = 3584
RW_HEADS = 16
LORA = 64
GATE_LORA = 160
GATE_LORA_PAD = 256
EPS = 1e-6
GN_EPS = 64e-5
NEG_BIG = -1e30
DIFF_HEADS_PER_STEP = 2
NA_ROWS_PER_STEP = 8
CHUNK = 64
VMEM_LIMIT = 56 * 1024 * 1024

ADA_COLS = 1536
IN_PROJ_ROWS = 1024
DIFF_Q_ROWS = 256
OUT_PROJ_ROWS = 512
FFN_ROWS = 512
FFN_COLS = FFN_DIM // 2
RWKV_PROJ_ROWS = 256
RWKV_OUT_ROWS = 512


def _cond_idx(i, tm):
    return jnp.maximum((i * tm) // SAMPLE_LEN - (NP // SAMPLE_LEN - 1), 0)


def _dot(a, b):
    return jnp.dot(a, b, preferred_element_type=f32)


def _dot_nt(a, b):
    return lax.dot_general(a, b, (((1,), (1,)), ((), ())), preferred_element_type=f32)


def _dot_tn(a, b):
    return lax.dot_general(a, b, (((0,), (0,)), ((), ())), preferred_element_type=f32)


def _bdot(a, b):
    return lax.dot_general(a, b, (((2,), (1,)), ((0,), (0,))), preferred_element_type=f32)


def _bdot_nt(a, b):
    return lax.dot_general(a, b, (((2,), (2,)), ((0,), (0,))), preferred_element_type=f32)


def _bdot_tn(a, b):
    return lax.dot_general(a, b, (((1,), (1,)), ((0,), (0,))), preferred_element_type=f32)


def _sigmoid(x):
    return 1.0 / (1.0 + jnp.exp(-x))


def _norm_mod(x, g, sh, sc):
    ms = jnp.mean(x * x, axis=-1, keepdims=True)
    return (x * lax.rsqrt(ms + EPS) * g) * (1.0 + sc) + sh


def _group_sum(x, gmat):
    xb = x.astype(bf16)
    cols = [_dot(xb[:, c * 256:(c + 1) * 256], gmat) for c in range(x.shape[1] // 256)]
    return cols[0] if len(cols) == 1 else jnp.concatenate(cols, axis=1)


def _softmax_parts(parts):
    m = parts[0].max(axis=-1, keepdims=True)
    for p in parts[1:]:
        m = jnp.maximum(m, p.max(axis=-1, keepdims=True))
    es = [jnp.exp(p - m) for p in parts]
    l = es[0].sum(axis=-1, keepdims=True)
    for e in es[1:]:
        l = l + e.sum(axis=-1, keepdims=True)
    inv = 1.0 / l
    return [e * inv for e in es]


def _lane_lo(n=128):
    return lax.broadcasted_iota(jnp.int32, (1, n), 1) < HD


def _stack_halves(q):
    lo = _lane_lo()
    return jnp.concatenate([jnp.where(lo, q, 0.0), jnp.where(lo, 0.0, q)], axis=0)


def _ada_kernel(cond_ref, w_ref, b_ref, o_ref):
    x = cond_ref[...]
    s = x * _sigmoid(x)
    o_ref[0] = _dot(s.astype(bf16), w_ref[0].astype(bf16)) + b_ref[0]


def _ada_table(cond8, w_ada, b_ada):
    depth = w_ada.shape[0]
    tn = ADA_COLS
    out = pl.pallas_call(
        _ada_kernel,
        grid=(depth, 6 * D // tn),
        in_specs=[
            pl.BlockSpec((8, D), lambda l, n: (0, 0)),
            pl.BlockSpec((1, D, tn), lambda l, n: (l, 0, n)),
            pl.BlockSpec((1, 1, tn), lambda l, n: (l, 0, n)),
        ],
        out_specs=pl.BlockSpec((1, 8, tn), lambda l, n: (l, 0, n)),
        out_shape=jax.ShapeDtypeStruct((depth, 8, 6 * D), f32),
        compiler_params=pltpu.CompilerParams(vmem_limit_bytes=VMEM_LIMIT),
        name="ada_table",
    )(cond8, w_ada, b_ada.reshape(depth, 1, 6 * D))
    out = out.reshape(depth, 8, 6, D)
    return [[out[l, :, k, :].reshape(8, 1, D) for k in range(6)] for l in range(depth)]


def _qk_norm(y, gain, gmat):
    ss = _group_sum(y * y, gmat) * (1.0 / HD)
    return y * lax.rsqrt(ss + EPS) * gain


def _rope(y, cos, sin):
    even = (lax.broadcasted_iota(jnp.int32, (1, 128), 1) % 2) == 0
    outs = []
    for c in range(y.shape[1] // 128):
        yc = y[:, c * 128:(c + 1) * 128]
        swapped = jnp.where(even, pltpu.roll(yc, 127, 1), pltpu.roll(yc, 1, 1))
        outs.append(yc * cos + swapped * sin)
    return jnp.concatenate(outs, axis=1)


def _inproj_kernel(xp_ref, xs_ref, sh_ref, sc_ref, g_ref, w_ref, gain_ref, gmat_ref, cos_ref, sin_ref,
                   f1_ref, f3_ref, f2_ref,
                   o_ref, dk_ref, dv_ref, nk_ref, nv_ref, f1_out, f3_out, f2_out, h_scr, *, tm):
    i = pl.program_id(0)
    j = pl.program_id(1)

    @pl.when(j == 0)
    def _():
        for src, dst in ((f1_ref, f1_out), (f3_ref, f3_out), (f2_ref, f2_out)):
            dst[...] = src[...].astype(bf16)

    def emit(val):
        o_ref[...] = val.astype(bf16)
        for seg, cache_ref in ((1, dk_ref), (2, dv_ref), (4, nk_ref), (5, nv_ref)):
            @pl.when(jnp.logical_and(j == seg, i < NP // tm))
            def _():
                cache_ref[...] = val

    for x_ref, active in ((xp_ref, i < NP // tm), (xs_ref, i >= NP // tm)):
        @pl.when(jnp.logical_and(j == 0, active))
        def _():
            h_scr[...] = _norm_mod(x_ref[...], g_ref[...], sh_ref[0], sc_ref[0]).astype(bf16)

    project = lambda: _dot(h_scr[...], w_ref[...])
    is_norm = jnp.logical_and(j != 2, j != 5)
    is_rope = jnp.logical_and(j < 2, i >= NP // tm)

    @pl.when(jnp.logical_not(is_norm))
    def _():
        emit(project())

    @pl.when(jnp.logical_and(is_norm, jnp.logical_not(is_rope)))
    def _():
        emit(_qk_norm(project(), gain_ref[0], gmat_ref[...]))

    @pl.when(is_rope)
    def _():
        o_ref[...] = _rope(_qk_norm(project(), gain_ref[0], gmat_ref[...]),
                           cos_ref[...], sin_ref[...]).astype(bf16)


def _in_projection(xp, xs, sh, sc, g, w_bf, gains, gmat, cos_t, sin_t, ffn_weights):
    tm = IN_PROJ_ROWS
    n_tiles = NT // tm
    n_prompt_tiles = NP // tm
    tiles_per_seq = SAMPLE_LEN // tm
    cidx = functools.partial(_cond_idx, tm=tm)
    rope_idx = lambda i, j: (jnp.maximum(i - n_prompt_tiles, 0) % tiles_per_seq, 0)
    cast_specs = [pl.BlockSpec((w.shape[0] // n_tiles, w.shape[1]), lambda i, j: (i, 0)) for w in ffn_weights]
    return pl.pallas_call(
        functools.partial(_inproj_kernel, tm=tm),
        grid=(NT // tm, IN_COLS // SEG),
        in_specs=[
            pl.BlockSpec((tm, D), lambda i, j: (jnp.minimum(i, n_prompt_tiles - 1), 0)),
            pl.BlockSpec((tm, D), lambda i, j: (jnp.maximum(i - n_prompt_tiles, 0), 0)),
            pl.BlockSpec((1, 1, D), lambda i, j: (cidx(i), 0, 0)),
            pl.BlockSpec((1, 1, D), lambda i, j: (cidx(i), 0, 0)),
            pl.BlockSpec((1, D), lambda i, j: (0, 0)),
            pl.BlockSpec((D, SEG), lambda i, j: (0, j)),
            pl.BlockSpec((1, 1, SEG), lambda i, j: (j, 0, 0)),
            pl.BlockSpec((256, 256), lambda i, j: (0, 0)),
            pl.BlockSpec((tm, 128), rope_idx),
            pl.BlockSpec((tm, 128), rope_idx),
        ] + cast_specs,
        out_specs=[pl.BlockSpec((tm, SEG), lambda i, j: (i, j))] + [
            pl.BlockSpec((tm, SEG), lambda i, j: (jnp.minimum(i, n_prompt_tiles - 1), 0))] * 4 + cast_specs,
        out_shape=[jax.ShapeDtypeStruct((NT, IN_COLS), bf16)] + [jax.ShapeDtypeStruct((NP, SEG), f32)] * 4
        + [jax.ShapeDtypeStruct(w.shape, bf16) for w in ffn_weights],
        scratch_shapes=[pltpu.VMEM((tm, D), bf16)],
        compiler_params=pltpu.CompilerParams(
            dimension_semantics=("arbitrary", "arbitrary"), vmem_limit_bytes=VMEM_LIMIT),
        name="in_projection",
    )(xp, xs, sh, sc, g, w_bf, gains, gmat, cos_t, sin_t, *ffn_weights)


def _lambda_value(lamp_ref, lam_init):
    lp = lamp_ref[...]
    e1 = jnp.exp(jnp.sum(lp[0:1] * lp[1:2], axis=-1, keepdims=True))
    e2 = jnp.exp(jnp.sum(lp[2:3] * lp[3:4], axis=-1, keepdims=True))
    return e1 - e2 + lam_init


def _sub_ln(o, subg, lam_init):
    ms = jnp.mean(o * o, axis=-1, keepdims=True)
    return o * lax.rsqrt(ms + EPS) * subg * (1.0 - lam_init)


def _prompt_attn_kernel(p_ref, lamp_ref, subg_ref, o_ref, *, lam_init):
    lam = _lambda_value(lamp_ref, lam_init)
    t = PROMPT_LEN
    lo = _lane_lo()
    scale = HD ** -0.5
    for h in range(DIFF_HEADS):
        q = p_ref[:, h * 128:(h + 1) * 128]
        k = p_ref[:, SEG + h * 128:SEG + (h + 1) * 128].astype(bf16)
        v = p_ref[:, 2 * SEG + h * 128:2 * SEG + (h + 1) * 128].astype(bf16)
        s = _dot_nt((_stack_halves(q) * scale).astype(bf16), k)
        (p,) = _softmax_parts([s])
        pd = p[:t] - lam * p[t:]
        o = _dot(pd.astype(bf16), v)
        o_ref[:, h * 128:(h + 1) * 128] = _sub_ln(o, subg_ref[...], lam_init).astype(bf16)
    for hp in range(NA_HEADS // 2):
        q = p_ref[:, 3 * SEG + hp * 128:3 * SEG + (hp + 1) * 128]
        k = p_ref[:, 4 * SEG + hp * 128:4 * SEG + (hp + 1) * 128].astype(bf16)
        v = p_ref[:, 5 * SEG + hp * 128:5 * SEG + (hp + 1) * 128].astype(bf16)
        s = _dot_nt((_stack_halves(q) * scale).astype(bf16), k)
        (p,) = _softmax_parts([s])
        o = _dot(p.astype(bf16), v)
        o_ref[:, SEG + hp * 128:SEG + (hp + 1) * 128] = jnp.where(lo, o[:t], o[t:]).astype(bf16)


def _prompt_attention(proj, lamp, subg, lam_init):
    return pl.pallas_call(
        functools.partial(_prompt_attn_kernel, lam_init=lam_init),
        grid=(N_PROMPT_SEQ,),
        in_specs=[
            pl.BlockSpec((PROMPT_LEN, IN_COLS), lambda b: (b, 0)),
            pl.BlockSpec((4, HD), lambda b: (0, 0)),
            pl.BlockSpec((1, 128), lambda b: (0, 0)),
        ],
        out_specs=pl.BlockSpec((PROMPT_LEN, D), lambda b: (b, 0)),
        out_shape=jax.ShapeDtypeStruct((NP, D), bf16),
        compiler_params=pltpu.CompilerParams(
            dimension_semantics=("parallel",), vmem_limit_bytes=VMEM_LIMIT),
        name="prompt_attention",
    )(proj, lamp, subg)


def _latent_diff_kernel(q_ref, kn_ref, vn_ref, kc_ref, vc_ref, lamp_ref, subg_ref, o_ref,
                        *, lam_init, tq):
    lam = _lambda_value(lamp_ref, lam_init)
    scale = HD ** -0.5
    for hh in range(DIFF_HEADS_PER_STEP):
        ln = slice(hh * 128, (hh + 1) * 128)
        qq = (_stack_halves(q_ref[:, ln]) * scale).astype(bf16)
        s_c = _dot_nt(qq, kc_ref[0, :, ln].astype(bf16))
        s_n = _dot_nt(qq, kn_ref[:, ln].astype(bf16))
        p_c, p_n = _softmax_parts([s_c, s_n])
        pd_c = p_c[:tq] - lam * p_c[tq:]
        pd_n = p_n[:tq] - lam * p_n[tq:]
        o = (_dot(pd_c.astype(bf16), vc_ref[0, :, ln].astype(bf16))
             + _dot(pd_n.astype(bf16), vn_ref[:, ln].astype(bf16)))
        o_ref[:, ln] = _sub_ln(o, subg_ref[...], lam_init).astype(bf16)


def _latent_diff_attention(proj, cache_k, cache_v, lamp, subg, lam_init):
    tq = DIFF_Q_ROWS
    nqb = SAMPLE_LEN // tq
    q0 = NP // tq
    s0 = NP // SAMPLE_LEN
    hw = 128 * DIFF_HEADS_PER_STEP
    kcol = SEG // hw
    return pl.pallas_call(
        functools.partial(_latent_diff_kernel, lam_init=lam_init, tq=tq),
        grid=(N_SAMPLE_SEQ, DIFF_HEADS // DIFF_HEADS_PER_STEP, nqb),
        in_specs=[
            pl.BlockSpec((tq, hw), lambda b, h, q: (q0 + b * nqb + q, h)),
            pl.BlockSpec((SAMPLE_LEN, hw), lambda b, h, q: (s0 + b, kcol + h)),
            pl.BlockSpec((SAMPLE_LEN, hw), lambda b, h, q: (s0 + b, 2 * kcol + h)),
            pl.BlockSpec((1, PAST, hw), lambda b, h, q: (b, 0, h)),
            pl.BlockSpec((1, PAST, hw), lambda b, h, q: (b, 0, h)),
            pl.BlockSpec((4, HD), lambda b, h, q: (0, 0)),
            pl.BlockSpec((1, 128), lambda b, h, q: (0, 0)),
        ],
        out_specs=pl.BlockSpec((tq, hw), lambda b, h, q: (b * nqb + q, h)),
        out_shape=jax.ShapeDtypeStruct((NS, SEG), bf16),
        compiler_params=pltpu.CompilerParams(
            dimension_semantics=("parallel", "parallel", "arbitrary"), vmem_limit_bytes=VMEM_LIMIT),
        name="latent_diff_attention",
    )(proj, proj, proj, cache_k, cache_v, lamp, subg)


def _rpb_table_kernel(rpb_ref, o_ref):
    h = pl.program_id(0)
    wq = lax.broadcasted_iota(jnp.int32, (GRID_W, GRID_W), 0)
    wk = lax.broadcasted_iota(jnp.int32, (GRID_W, GRID_W), 1)
    col_start = jnp.clip(wq - NA_KW // 2, 0, GRID_W - NA_KW)
    col_in = jnp.logical_and(wk >= col_start, wk < col_start + NA_KW)
    col_off = jnp.clip(wk - wq, -(NA_KW - 1), NA_KW - 1) + (NA_KW - 1)
    n_dr = 2 * NA_KH - 1
    n_dc = 2 * NA_KW - 1
    for dr in range(n_dr):
        t = jnp.zeros((GRID_W, GRID_W), f32)
        for c in range(n_dc):
            t = jnp.where(col_off == c, rpb_ref[h * (n_dr * n_dc) + dr * n_dc + c], t)
        o_ref[0, dr] = jnp.where(col_in, t, NEG_BIG)


def _rpb_table(rpb):
    n_dr = 2 * NA_KH - 1
    tcol = pl.pallas_call(
        _rpb_table_kernel,
        grid=(NA_HEADS,),
        in_specs=[pl.BlockSpec(memory_space=pltpu.SMEM)],
        out_specs=pl.BlockSpec((1, n_dr, GRID_W, GRID_W), lambda h: (h, 0, 0, 0)),
        out_shape=jax.ShapeDtypeStruct((NA_HEADS, n_dr, GRID_W, GRID_W), f32),
        name="rpb_table",
    )(rpb.reshape(-1))
    return jnp.stack(
        [jnp.concatenate([tcol[:, j - s + NA_KH - 1] for j in range(NA_KH)], axis=-1) for s in range(NA_KH)],
        axis=1)


def _latent_na_kernel(q_ref, k_ref, v_ref, kc_ref, vc_ref, bias_ref, o_ref):
    scale = HD ** -0.5
    lo = _lane_lo()
    kc = kc_ref[0].astype(bf16)
    vc = vc_ref[0].astype(bf16)
    win = NA_KH * GRID_W

    nr = NA_ROWS_PER_STEP
    kcb = jnp.broadcast_to(kc[None], (nr,) + kc.shape)
    vcb = jnp.broadcast_to(vc[None], (nr,) + vc.shape)

    def rows(g, carry):
        qs, kws, vws, biases = [], [], [], []
        for t in range(nr):
            r = g * nr + t
            rs = jnp.clip(r - NA_KH // 2, 0, GRID_R - NA_KH)
            sidx = r - rs
            q = q_ref[pl.ds(pl.multiple_of(r * GRID_W, GRID_W), GRID_W), :]
            qs.append((_stack_halves(q) * scale).astype(bf16))
            k0 = pl.multiple_of(rs * GRID_W, GRID_W)
            kws.append(k_ref[pl.ds(k0, win), :].astype(bf16))
            vws.append(v_ref[pl.ds(k0, win), :].astype(bf16))
            biases.append(jnp.concatenate([bias_ref[0, sidx], bias_ref[1, sidx]], axis=0))
        qq = jnp.stack(qs)
        s_loc = _bdot_nt(qq, jnp.stack(kws)) + jnp.stack(biases)
        s_ctx = _bdot_nt(qq, kcb)
        p_loc, p_ctx = _softmax_parts([s_loc, s_ctx])
        o = _bdot(p_loc.astype(bf16), jnp.stack(vws)) + _bdot(p_ctx.astype(bf16), vcb)
        o = jnp.where(lo, o[:, :GRID_W], o[:, GRID_W:]).astype(bf16)
        row0 = pl.multiple_of(g * (nr * GRID_W), nr * GRID_W)
        o_ref[pl.ds(row0, nr * GRID_W), :] = o.reshape(nr * GRID_W, 128)
        return carry

    lax.fori_loop(0, GRID_R // nr, rows, 0)


def _latent_na_attention(proj, cache_k, cache_v, bias):
    s0 = NP // SAMPLE_LEN
    return pl.pallas_call(
        _latent_na_kernel,
        grid=(N_SAMPLE_SEQ, NA_HEADS // 2),
        in_specs=[
            pl.BlockSpec((SAMPLE_LEN, 128), lambda b, h: (s0 + b, 12 + h)),
            pl.BlockSpec((SAMPLE_LEN, 128), lambda b, h: (s0 + b, 16 + h)),
            pl.BlockSpec((SAMPLE_LEN, 128), lambda b, h: (s0 + b, 20 + h)),
            pl.BlockSpec((1, PAST, 128), lambda b, h: (b, 0, h)),
            pl.BlockSpec((1, PAST, 128), lambda b, h: (b, 0, h)),
            pl.BlockSpec((2, NA_KH, GRID_W, NA_KH * GRID_W), lambda b, h: (h, 0, 0, 0)),
        ],
        out_specs=pl.BlockSpec((SAMPLE_LEN, 128), lambda b, h: (b, h)),
        out_shape=jax.ShapeDtypeStruct((NS, SEG), bf16),
        compiler_params=pltpu.CompilerParams(
            dimension_semantics=("parallel", "parallel"), vmem_limit_bytes=VMEM_LIMIT),
        name="latent_na_attention",
    )(proj, proj, proj, cache_k, cache_v, bias)


def _out_proj_kernel(ap_ref, ad_ref, an_ref, w_ref, xp_ref, xs_ref, gt_ref, o_ref, *, tm):
    i = pl.program_id(0)

    @pl.when(i < NP // tm)
    def _():
        o_ref[...] = xp_ref[...] + gt_ref[0] * _dot(ap_ref[...], w_ref[...])

    @pl.when(i >= NP // tm)
    def _():
        mixed = _dot(ad_ref[...], w_ref[:SEG, :]) + _dot(an_ref[...], w_ref[SEG:, :])
        o_ref[...] = xs_ref[...] + gt_ref[0] * mixed


def _out_projection(a_prompt, a_diff, a_na, w_bf, xp, xs, gate):
    tm = OUT_PROJ_ROWS
    n_p = NP // tm
    cidx = functools.partial(_cond_idx, tm=tm)
    prompt_rows = lambda i: (jnp.minimum(i, n_p - 1), 0)
    sample_rows = lambda i: (jnp.maximum(i - n_p, 0), 0)
    return pl.pallas_call(
        functools.partial(_out_proj_kernel, tm=tm),
        grid=(NT // tm,),
        in_specs=[
            pl.BlockSpec((tm, D), prompt_rows),
            pl.BlockSpec((tm, SEG), sample_rows),
            pl.BlockSpec((tm, SEG), sample_rows),
            pl.BlockSpec((D, D), lambda i: (0, 0)),
            pl.BlockSpec((tm, D), prompt_rows),
            pl.BlockSpec((tm, D), sample_rows),
            pl.BlockSpec((1, 1, D), lambda i: (cidx(i), 0, 0)),
        ],
        out_specs=pl.BlockSpec((tm, D), lambda i: (i, 0)),
        out_shape=jax.ShapeDtypeStruct((NT, D), f32),
        compiler_params=pltpu.CompilerParams(
            dimension_semantics=("arbitrary",), vmem_limit_bytes=VMEM_LIMIT),
        name="out_projection",
    )(a_prompt, a_diff, a_na, w_bf, xp, xs, gate)


def _rwkv_out_kernel(yfp_ref, ybp_ref, yfs_ref, ybs_ref, bonus_ref, gate_ref, lng_ref, lnb_ref,
                     gmat_ref, w_ref, x_ref, gt_ref, o_ref, *, tm):
    i = pl.program_id(0)

    def finish(y):
        gmat = gmat_ref[...]
        mu = _group_sum(y, gmat) * (1.0 / HD)
        yc = y - mu
        var = _group_sum(yc * yc, gmat) * (1.0 / HD)
        z = yc * lax.rsqrt(var + GN_EPS) * lng_ref[...] + lnb_ref[...] + bonus_ref[...].astype(f32)
        z = (z * gate_ref[...].astype(f32)).astype(bf16)
        o_ref[...] = x_ref[...] + gt_ref[0] * _dot(z, w_ref[...])

    @pl.when(i < NP // tm)
    def _():
        finish(yfp_ref[...].astype(f32) + ybp_ref[...].astype(f32))

    @pl.when(i >= NP // tm)
    def _():
        finish(yfs_ref[...].astype(f32) + ybs_ref[...].astype(f32))


def _rwkv_out_projection(yf_p, yb_p, yf_s, yb_s, bonus, gate_lora, ln_g, ln_b, gmat, w_bf, x, gate):
    tm = RWKV_OUT_ROWS
    n_p = NP // tm
    cidx = functools.partial(_cond_idx, tm=tm)
    row = pl.BlockSpec((tm, D), lambda i: (i, 0))
    prompt_row = pl.BlockSpec((tm, D), lambda i: (jnp.minimum(i, n_p - 1), 0))
    sample_row = pl.BlockSpec((tm, D), lambda i: (jnp.maximum(i - n_p, 0), 0))
    vec = pl.BlockSpec((1, D), lambda i: (0, 0))
    return pl.pallas_call(
        functools.partial(_rwkv_out_kernel, tm=tm),
        grid=(NT // tm,),
        in_specs=[prompt_row, prompt_row, sample_row, sample_row, row, row, vec, vec,
                  pl.BlockSpec((256, 256), lambda i: (0, 0)),
                  pl.BlockSpec((D, D), lambda i: (0, 0)),
                  row,
                  pl.BlockSpec((1, 1, D), lambda i: (cidx(i), 0, 0))],
        out_specs=row,
        out_shape=jax.ShapeDtypeStruct((NT, D), f32),
        compiler_params=pltpu.CompilerParams(
            dimension_semantics=("arbitrary",), vmem_limit_bytes=VMEM_LIMIT),
        name="rwkv_out_projection",
    )(yf_p, yb_p, yf_s, yb_s, bonus, gate_lora, ln_g, ln_b, gmat, w_bf, x, gate)


def _swiglu_hidden(xb, w1, w3):
    a = _dot(xb, w1)
    return ((a * _sigmoid(a)) * _dot(xb, w3)).astype(bf16)


def _ffn_kernel(x_ref, sh_ref, sc_ref, gt_ref, g_ref, w1_ref, w3_ref, w2_ref, o_ref, h_scr, acc_scr):
    f = pl.program_id(1)

    @pl.when(f == 0)
    def _():
        h_scr[...] = _norm_mod(x_ref[...], g_ref[...], sh_ref[0], sc_ref[0]).astype(bf16)
        acc_scr[...] = jnp.zeros_like(acc_scr)

    acc_scr[...] += _dot(_swiglu_hidden(h_scr[...], w1_ref[...], w3_ref[...]), w2_ref[...])

    @pl.when(f == pl.num_programs(1) - 1)
    def _():
        o_ref[...] = x_ref[...] + gt_ref[0] * acc_scr[...]


def _dense_ffn(x, sh, sc, gt, g, w1, w3, w2):
    tm = FFN_ROWS
    tf = FFN_COLS
    cidx = functools.partial(_cond_idx, tm=tm)
    mod = pl.BlockSpec((1, 1, D), lambda i, f: (cidx(i), 0, 0))
    return pl.pallas_call(
        _ffn_kernel,
        grid=(NT // tm, FFN_DIM // tf),
        in_specs=[
            pl.BlockSpec((tm, D), lambda i, f: (i, 0)),
            mod, mod, mod,
            pl.BlockSpec((1, D), lambda i, f: (0, 0)),
            pl.BlockSpec((D, tf), lambda i, f: (0, f)),
            pl.BlockSpec((D, tf), lambda i, f: (0, f)),
            pl.BlockSpec((tf, D), lambda i, f: (f, 0)),
        ],
        out_specs=pl.BlockSpec((tm, D), lambda i, f: (i, 0)),
        out_shape=jax.ShapeDtypeStruct((NT, D), f32),
        scratch_shapes=[pltpu.VMEM((tm, D), bf16), pltpu.VMEM((tm, D), f32)],
        compiler_params=pltpu.CompilerParams(
            dimension_semantics=("parallel", "arbitrary"), vmem_limit_bytes=VMEM_LIMIT),
        name="ffn_mixer",
    )(x, sh, sc, gt, g, w1, w3, w2)


MOE_TILE = 1024
MOE_MAIN = 288
MOE_EXTRA = 128
MOE_ROWS = MOE_MAIN + -(-(MOE_TILE - MOE_MAIN) // MOE_EXTRA) * MOE_EXTRA


RANK_SEL = 4096


def _router_kernel(x_ref, sh_ref, sc_ref, g_ref, wr_ref, br_ref, tri_ref, triu_ref,
                   h_out, gates_out, rank_out, rank_t_out, cnt_out):
    h = _norm_mod(x_ref[...], g_ref[...], sh_ref[0], sc_ref[0])
    h_out[...] = h.astype(bf16)
    w = wr_ref[...]
    h_hi, w_hi = h.astype(bf16), w.astype(bf16)
    h_lo = (h - h_hi.astype(f32)).astype(bf16)
    w_lo = (w - w_hi.astype(f32)).astype(bf16)
    hw = _dot(h_hi, jnp.concatenate([w_hi, w_lo], axis=1))
    logits = hw[:, :128] + (hw[:, 128:] + _dot(h_lo, w_hi)) + br_ref[...]
    lane = lax.broadcasted_iota(jnp.int32, logits.shape, 1)
    logits = jnp.where(lane < N_EXPERTS, logits, -jnp.inf)
    m1 = logits.max(axis=-1, keepdims=True)
    i1 = jnp.min(jnp.where(logits == m1, lane, 128), axis=-1, keepdims=True)
    rest = jnp.where(lane == i1, -jnp.inf, logits)
    m2 = rest.max(axis=-1, keepdims=True)
    i2 = jnp.min(jnp.where(rest == m2, lane, 128), axis=-1, keepdims=True)
    e2 = jnp.exp(m2 - m1)
    den = 1.0 / (1.0 + e2)
    gates_out[...] = jnp.where(lane == i1, den, jnp.where(lane == i2, e2 * den, 0.0))
    sel = jnp.logical_or(lane == i1, lane == i2)
    self32 = sel.astype(f32)
    selb = self32.astype(bf16)
    rank_out[...] = jnp.where(sel, _dot(tri_ref[...], selb), -1.0)
    rank_t_out[...] = _dot_tn(selb, triu_ref[...])
    cnt_out[0] = jnp.sum(self32, axis=0, keepdims=True).astype(jnp.int32)


def _moe_route(x, sh, sc, g, w_router, b_router):
    tm = MOE_TILE
    cidx = functools.partial(_cond_idx, tm=tm)
    t = jnp.arange(tm)
    tri = (t[:, None] > t[None, :]).astype(bf16)
    triu = jnp.where(t[:, None] == t[None, :], float(RANK_SEL), tri.T.astype(f32)).astype(bf16)
    mod = pl.BlockSpec((1, 1, D), lambda i: (cidx(i), 0, 0))
    return pl.pallas_call(
        _router_kernel,
        grid=(NT // tm,),
        in_specs=[
            pl.BlockSpec((tm, D), lambda i: (i, 0)),
            mod, mod,
            pl.BlockSpec((1, D), lambda i: (0, 0)),
            pl.BlockSpec((D, 128), lambda i: (0, 0)),
            pl.BlockSpec((1, 128), lambda i: (0, 0)),
            pl.BlockSpec((tm, tm), lambda i: (0, 0)),
            pl.BlockSpec((tm, tm), lambda i: (0, 0)),
        ],
        out_specs=[
            pl.BlockSpec((tm, D), lambda i: (i, 0)),
            pl.BlockSpec((tm, 128), lambda i: (i, 0)),
            pl.BlockSpec((tm, 128), lambda i: (i, 0)),
            pl.BlockSpec((128, tm), lambda i: (0, i)),
            pl.BlockSpec((1, 1, 128), lambda i: (i, 0, 0)),
        ],
        out_shape=[
            jax.ShapeDtypeStruct((NT, D), bf16),
            jax.ShapeDtypeStruct((NT, 128), f32),
            jax.ShapeDtypeStruct((NT, 128), f32),
            jax.ShapeDtypeStruct((128, NT), f32),
            jax.ShapeDtypeStruct((NT // tm, 1, 128), jnp.int32),
        ],
        compiler_params=pltpu.CompilerParams(
            dimension_semantics=("parallel",), vmem_limit_bytes=VMEM_LIMIT),
        name="moe_router",
    )(x, sh, sc, g, w_router, b_router, tri, triu)


def _moe_kernel(cnt_ref, h_ref, rank_ref, rank_t_ref, gates_ref, x_ref, gt_ref, w1_ref, w3_ref, w2_ref,
                op_ref, os_ref, o_ref, xc_scr, acc_scr, rcol_scr, gcol_scr):
    i = pl.program_id(0)
    e = pl.program_id(1)
    f = pl.program_id(2)
    cnt = cnt_ref[i * N_EXPERTS + e]
    n_extra = jnp.maximum(cnt - MOE_MAIN + MOE_EXTRA - 1, 0) // MOE_EXTRA

    def for_each_block(fn):
        fn(0, MOE_MAIN)

        def body(b, carry):
            fn(pl.multiple_of(MOE_MAIN + b * MOE_EXTRA, 32), MOE_EXTRA)
            return carry

        lax.fori_loop(0, n_extra, body, 0)

    @pl.when(jnp.logical_and(e == 0, f == 0))
    def _():
        o_ref[...] = x_ref[...]

    @pl.when(f == 0)
    def _():
        lane = lax.broadcasted_iota(jnp.int32, (MOE_TILE, 128), 1)
        mine = lane == e
        rcol_scr[...] = jnp.sum(jnp.where(mine, rank_ref[...], 0.0), axis=1, keepdims=True).astype(jnp.int32)
        gcol_scr[...] = jnp.sum(jnp.where(mine, gates_ref[...], 0.0), axis=1, keepdims=True)

        rrow = rank_t_ref[pl.ds(e, 1), :].astype(jnp.int32)

        def gather(slot0, nrows):
            slot = lax.broadcasted_iota(jnp.int32, (nrows, MOE_TILE), 0) + (slot0 + RANK_SEL)
            xc_scr[pl.ds(slot0, nrows), :] = _dot((rrow == slot).astype(bf16), h_ref[...]).astype(bf16)

        for_each_block(gather)

    def expert(slot0, nrows):
        rows = pl.ds(slot0, nrows)
        part = _dot(_swiglu_hidden(xc_scr[rows, :], w1_ref[0], w3_ref[0]), w2_ref[0])

        @pl.when(f == 0)
        def _():
            acc_scr[rows, :] = part

        @pl.when(f != 0)
        def _():
            acc_scr[rows, :] += part

    for_each_block(expert)

    @pl.when(f == pl.num_programs(2) - 1)
    def _():
        def scatter(slot0, nrows):
            out = acc_scr[pl.ds(slot0, nrows), :].astype(bf16)
            for t0 in range(0, MOE_TILE, 256):
                rows = slice(t0, t0 + 256)
                slot = lax.broadcasted_iota(jnp.int32, (256, nrows), 1) + slot0
                pt = (rcol_scr[rows, :] == slot).astype(bf16)
                o_ref[rows, :] += (gcol_scr[rows, :] * gt_ref[0]) * _dot(pt, out)

        for_each_block(scatter)

    for dst, mine in ((op_ref, i < NP // MOE_TILE), (os_ref, i >= NP // MOE_TILE)):
        @pl.when(jnp.logical_and(mine, jnp.logical_and(e == pl.num_programs(1) - 1, f == pl.num_programs(2) - 1)))
        def _():
            dst[...] = o_ref[...]


def _moe_experts(cnt, h_bf, rank, rank_t, gates, x, gt, w1, w3, w2):
    tm = MOE_TILE
    tf = EXPERT_DIM // 2
    n_p = NP // tm
    cidx = functools.partial(_cond_idx, tm=tm)
    once = pl.Buffered(1)
    grid_spec = pltpu.PrefetchScalarGridSpec(
        num_scalar_prefetch=1,
        grid=(NT // tm, N_EXPERTS, EXPERT_DIM // tf),
        in_specs=[
            pl.BlockSpec((tm, D), lambda i, e, f, c: (i, 0), pipeline_mode=once),
            pl.BlockSpec((tm, 128), lambda i, e, f, c: (i, 0), pipeline_mode=once),
            pl.BlockSpec((128, tm), lambda i, e, f, c: (0, i), pipeline_mode=once),
            pl.BlockSpec((tm, 128), lambda i, e, f, c: (i, 0), pipeline_mode=once),
            pl.BlockSpec((tm, D), lambda i, e, f, c: (i, 0), pipeline_mode=once),
            pl.BlockSpec((1, 1, D), lambda i, e, f, c: (cidx(i), 0, 0)),
            pl.BlockSpec((1, D, tf), lambda i, e, f, c: (e, 0, f)),
            pl.BlockSpec((1, D, tf), lambda i, e, f, c: (e, 0, f)),
            pl.BlockSpec((1, tf, D), lambda i, e, f, c: (e, f, 0)),
        ],
        out_specs=[
            pl.BlockSpec((tm, D), lambda i, e, f, c: (jnp.minimum(i, n_p - 1), 0), pipeline_mode=once),
            pl.BlockSpec((tm, D), lambda i, e, f, c: (jnp.maximum(i - n_p, 0), 0), pipeline_mode=once),
        ],
        scratch_shapes=[pltpu.VMEM((tm, D), f32),
                        pltpu.VMEM((MOE_ROWS, D), bf16), pltpu.VMEM((MOE_ROWS, D), f32),
                        pltpu.VMEM((tm, 1), jnp.int32), pltpu.VMEM((tm, 1), f32)],
    )
    return pl.pallas_call(
        _moe_kernel,
        grid_spec=grid_spec,
        out_shape=[jax.ShapeDtypeStruct((NP, D), f32), jax.ShapeDtypeStruct((NS, D), f32)],
        compiler_params=pltpu.CompilerParams(
            dimension_semantics=("arbitrary", "arbitrary", "arbitrary"), vmem_limit_bytes=VMEM_LIMIT),
        name="moe_experts",
    )(cnt, h_bf, rank, rank_t, gates, x, gt, w1, w3, w2)


def _in_context_key(k, a, k_a):
    return k * (1.0 + (a - 1.0) * k_a)


def _rwkv_proj_kernel(x_ref, xp_ref, xn_ref, sh_ref, sc_ref, g_ref, mu_ref,
                      wr_ref, wk_ref, wv_ref, g1_ref, g2_ref, w1_ref, w2_ref, a1_ref, a2_ref,
                      w0_ref, a0_ref, kk_ref, ka_ref, rk_ref, gmat_ref,
                      r_out, v_out, kkn_out, bonus_out, gate_out,
                      lwf_out, lwb_out, af_out, ab_out, k_out, *, tm):
    i = pl.program_id(0)
    g, sh, sc = g_ref[...], sh_ref[0], sc_ref[0]
    h = _norm_mod(x_ref[...], g, sh, sc)
    n_prompt_tiles = NP // tm
    tiles_per_seq = SAMPLE_LEN // tm
    pos = (i - n_prompt_tiles) % tiles_per_seq
    has_prev = jnp.logical_and(i >= n_prompt_tiles, pos != 0)
    has_next = jnp.logical_and(i >= n_prompt_tiles, pos != tiles_per_seq - 1)
    h_before = jnp.where(has_prev, _norm_mod(xp_ref[...], g, sh, sc)[7:8], 0.0)
    h_after = jnp.where(has_next, _norm_mod(xn_ref[...], g, sh, sc)[0:1], 0.0)
    rowi = lax.broadcasted_iota(jnp.int32, (tm, 1), 0)
    h_prev = jnp.where(rowi == 0, h_before, pltpu.roll(h, 1, 0))
    h_next = jnp.where(rowi == tm - 1, h_after, pltpu.roll(h, tm - 1, 0))
    xx = 0.5 * (h_prev + h_next) - h
    mix = lambda n: (h + xx * mu_ref[n:n + 1]).astype(bf16)

    r = _dot(mix(0), wr_ref[...])
    k = _dot(mix(2), wk_ref[...])
    v = _dot(mix(3), wv_ref[...])
    gate_out[...] = _dot(_sigmoid(_dot(mix(5), g1_ref[...])).astype(bf16), g2_ref[...]).astype(bf16)

    lo = _lane_lo()
    tw = jnp.tanh(_dot(mix(1), w1_ref[...]))
    ta = _dot(mix(4), a1_ref[...])
    gmat = gmat_ref[...]
    kk = k * kk_ref[...]
    kkn_out[...] = kk * lax.rsqrt(_group_sum(kk * kk, gmat) + 1e-12)
    r_out[...] = r
    v_out[...] = v
    k_out[...] = k
    kd_sum = jnp.zeros_like(k)
    for d, (lw_out, a_out) in enumerate(((lwf_out, af_out), (lwb_out, ab_out))):
        keep = lo if d == 0 else jnp.logical_not(lo)
        zw = w0_ref[d:d + 1] + _dot(jnp.where(keep, tw, 0.0).astype(bf16), w2_ref[...])
        lw_out[...] = -math.exp(-0.5) * _sigmoid(zw)
        a = _sigmoid(a0_ref[d:d + 1] + _dot(jnp.where(keep, ta, 0.0).astype(bf16), a2_ref[...]))
        a_out[...] = a
        kd_sum = kd_sum + _in_context_key(k, a, ka_ref[...])
    bonus_out[...] = (_group_sum(r * kd_sum * rk_ref[...], gmat) * v).astype(bf16)


def _rwkv_projection(x, sh, sc, g, mu, wr, wk, wv, g1, g2, w1, w2, a1, a2, w0, a0, k_k, k_a, r_k, gmat):
    tm = RWKV_PROJ_ROWS
    cidx = functools.partial(_cond_idx, tm=tm)
    hb = tm // 8
    n8 = NT // 8
    full = lambda shape: pl.BlockSpec(shape, lambda i: tuple(0 for _ in shape))
    row = pl.BlockSpec((tm, D), lambda i: (i, 0))
    return pl.pallas_call(
        functools.partial(_rwkv_proj_kernel, tm=tm),
        grid=(NT // tm,),
        in_specs=[
            row,
            pl.BlockSpec((8, D), lambda i: (jnp.maximum(i * hb - 1, 0), 0)),
            pl.BlockSpec((8, D), lambda i: (jnp.minimum((i + 1) * hb, n8 - 1), 0)),
            pl.BlockSpec((1, 1, D), lambda i: (cidx(i), 0, 0)),
            pl.BlockSpec((1, 1, D), lambda i: (cidx(i), 0, 0)),
            full((1, D)), full((6, D)),
            full((D, D)), full((D, D)), full((D, D)),
            full((D, GATE_LORA_PAD)), full((GATE_LORA_PAD, D)),
            full((D, 2 * LORA)), full((2 * LORA, D)), full((D, 2 * LORA)), full((2 * LORA, D)),
            full((2, D)), full((2, D)), full((1, D)), full((1, D)), full((1, D)),
            full((256, 256)),
        ],
        out_specs=[row] * 10,
        out_shape=[jax.ShapeDtypeStruct((NT, D), dt) for dt in (f32, f32, f32, bf16, bf16) + (f32,) * 5],
        compiler_params=pltpu.CompilerParams(
            dimension_semantics=("parallel",), vmem_limit_bytes=VMEM_LIMIT),
        name="rwkv_projection",
    )(x, x, x, sh, sc, g, mu, wr, wk, wv, g1, g2, w1, w2, a1, a2, w0, a0, k_k, k_a, r_k, gmat)


def _split3(x):
    x1 = x.astype(bf16)
    r1 = x - x1.astype(f32)
    x2 = r1.astype(bf16)
    x3 = (r1 - x2.astype(f32)).astype(bf16)
    return x1, x2, x3


def _scan_kernel(*refs, nc, has_init, emit_state, n_casts):
    fwd_refs = refs[0:6]
    bwd_refs = refs[6:12]
    ka_ref, tri_ref, mask_ref = refs[12:15]
    pos = 15
    s0_ref = None
    if has_init:
        s0_ref = refs[pos]
        pos += 1
    cast_in = refs[pos:pos + n_casts]
    pos += n_casts
    yf_ref, yb_ref = refs[pos:pos + 2]
    pos += 2
    sout_ref = None
    if emit_state:
        sout_ref = refs[pos]
        pos += 1
    cast_out = refs[pos:pos + n_casts]
    pos += n_casts
    st_scr, cl_scr = refs[pos:pos + 2]

    for src, dst in zip(cast_in, cast_out):
        dst[...] = src[...].astype(bf16)

    s = pl.program_id(1)

    @pl.when(s == 0)
    def _():
        if has_init:
            st_scr[...] = s0_ref[0]
        else:
            st_scr[...] = jnp.zeros_like(st_scr)

    c = CHUNK
    for d, drefs in enumerate((fwd_refs, bwd_refs)):
        lw = drefs[3][...]
        tri = tri_ref[d]
        p1, p2, p3 = _split3(lw)
        cl_scr[d] = _dot(tri, p1) + _dot(tri, p2) + _dot(tri, p3)

    lane = lax.broadcasted_iota(jnp.int32, (1, 128), 1)
    m0 = (lane < HD).astype(f32)
    m1 = 1.0 - m0
    rid = lax.broadcasted_iota(jnp.int32, (128, 128), 0)
    cid = lax.broadcasted_iota(jnp.int32, (128, 128), 1)
    eye = (rid == cid).astype(f32)

    n_pairs = RW_HEADS // 2
    nb = 2 * n_pairs

    def per_head_rows(x):
        return jnp.concatenate([x * m0, x * m1], axis=0)

    ars_l, bk_l, v2_l, dec_l = [], [], [], []
    for d, drefs in enumerate((fwd_refs, bwd_refs)):
        r_ref, v_ref, kk_ref, lw_ref, a_ref, k_ref = drefs
        end_row = c - 1 if d == 0 else 0
        for p in range(n_pairs):
            ln = slice(p * 128, (p + 1) * 128)
            cl = cl_scr[d, :, ln]
            kk = kk_ref[:, ln]
            tot = cl[end_row:end_row + 1]
            e_inv = jnp.exp(-cl)
            a = a_ref[:, ln]
            kka = kk * a
            kd = _in_context_key(k_ref[:, ln], a, ka_ref[:, ln])
            at = per_head_rows(-kk * jnp.exp(cl - lw_ref[:, ln]))
            rt = per_head_rows(r_ref[:, ln] * jnp.exp(cl))
            ars_l.append(jnp.concatenate([at, rt], axis=0).astype(bf16))
            bk_l.append(jnp.concatenate([per_head_rows(kka * e_inv), per_head_rows(kd * e_inv)],
                                        axis=0).astype(bf16))
            v2_l.append(per_head_rows(v_ref[:, ln]).astype(bf16))
            dec_l.append(jnp.exp(tot))
    ar = jnp.stack(ars_l)
    bk = jnp.stack(bk_l)
    v2 = jnp.stack(v2_l)
    dec = jnp.stack(dec_l)

    st = st_scr[...].reshape(nb, 128, 128)
    g2 = _bdot_nt(ar, bk).reshape(2, n_pairs, 256, 256) * mask_ref[...][:, None]
    g2 = g2.reshape(nb, 256, 256)
    ars = _bdot_nt(ar, st.astype(bf16))
    gv = _bdot(g2[:, :, 128:].astype(bf16), v2)
    l_bd = g2[:, :128, :128]
    l_bf = l_bd.astype(bf16)
    pk = _bdot(l_bf, l_bf)
    q = eye[None] + l_bd
    for it in range(5):
        pkb = pk.astype(bf16)
        if it < 4:
            res = _bdot(jnp.concatenate([q, pk], axis=1).astype(bf16), pkb)
            q = q + res[:, :128]
            pk = res[:, 128:]
        else:
            q = q + _bdot(q.astype(bf16), pkb)
    rhs = ars[:, :128] + gv[:, :128]
    u2b = _bdot(q.astype(bf16), rhs.astype(bf16)).astype(bf16)
    y2 = ars[:, 128:] + gv[:, 128:] + _bdot(g2[:, 128:, :128].astype(bf16), u2b)
    y = y2[:, :c] + y2[:, c:]
    for d, y_ref in enumerate((yf_ref, yb_ref)):
        for p in range(n_pairs):
            y_ref[:, p * 128:(p + 1) * 128] = y[d * n_pairs + p].astype(y_ref.dtype)
    uv = jnp.concatenate([u2b, v2], axis=1)
    st_scr[...] = ((st + _bdot_tn(uv, bk)) * dec).reshape(2, n_pairs, 128, 128)

    if emit_state:
        @pl.when(s == nc - 1)
        def _():
            for d in range(2):
                for p in range(n_pairs):
                    m = st_scr[d, p]
                    sout_ref[0, d, 2 * p] = m[:HD, :HD]
                    sout_ref[0, d, 2 * p + 1] = pltpu.roll(m[HD:, :], HD, 1)[:, :HD]


def _rwkv_scan(streams_f, streams_b, k_a, tri, mask, s0_bd, casts, *, n_seq, seq_len, row0, emit_state):
    nc = seq_len // CHUNK
    blk0 = row0 // CHUNK
    fwd_spec = pl.BlockSpec((CHUNK, D), lambda b, s: (blk0 + b * nc + s, 0))
    bwd_spec = pl.BlockSpec((CHUNK, D), lambda b, s: (blk0 + b * nc + nc - 1 - s, 0))
    in_specs = [fwd_spec] * 6 + [bwd_spec] * 6 + [
        pl.BlockSpec((1, D), lambda b, s: (0, 0)),
        pl.BlockSpec((2, CHUNK, CHUNK), lambda b, s: (0, 0, 0)),
        pl.BlockSpec((2, 256, 256), lambda b, s: (0, 0, 0)),
    ]
    args = list(streams_f) + list(streams_b) + [k_a, tri, mask]
    state_block = (1, 2, RW_HEADS // 2, 128, 128)
    if s0_bd is not None:
        in_specs.append(pl.BlockSpec(state_block, lambda b, s: (b, 0, 0, 0, 0)))
        args.append(s0_bd)
    out_specs = [pl.BlockSpec((CHUNK, D), lambda b, s: (b * nc + s, 0)),
                 pl.BlockSpec((CHUNK, D), lambda b, s: (b * nc + nc - 1 - s, 0))]
    out_shape = [jax.ShapeDtypeStruct((n_seq * seq_len, D), bf16)] * 2
    if emit_state:
        out_specs.append(pl.BlockSpec((1, 2, RW_HEADS, HD, HD), lambda b, s: (b, 0, 0, 0, 0)))
        out_shape.append(jax.ShapeDtypeStruct((n_seq, 2, RW_HEADS, HD, HD), f32))
    for w in casts:
        per_expert = (n_seq * nc) // w.shape[0]
        spec = pl.BlockSpec((1, w.shape[1] // per_expert, w.shape[2]),
                            lambda b, s, per_expert=per_expert: ((b * nc + s) // per_expert, (b * nc + s) % per_expert, 0))
        in_specs.append(spec)
        args.append(w)
        out_specs.append(spec)
        out_shape.append(jax.ShapeDtypeStruct(w.shape, bf16))
    return pl.pallas_call(
        functools.partial(_scan_kernel, nc=nc, has_init=s0_bd is not None, emit_state=emit_state,
                          n_casts=len(casts)),
        grid=(n_seq, nc),
        in_specs=in_specs,
        out_specs=out_specs,
        out_shape=out_shape,
        scratch_shapes=[pltpu.VMEM(state_block[1:], f32), pltpu.VMEM((2, CHUNK, D), f32)],
        compiler_params=pltpu.CompilerParams(
            dimension_semantics=("parallel", "arbitrary"), vmem_limit_bytes=VMEM_LIMIT),
        name="rwkv_scan_prompt" if emit_state else "rwkv_scan_sample",
    )(*args)


def _scan_constants():
    t = jnp.arange(CHUNK)
    lower = (t[:, None] >= t[None, :])
    tri = jnp.stack([lower, lower.T]).astype(bf16)
    masks = []
    for d in range(2):
        strict = (t[:, None] > t[None, :]) if d == 0 else (t[:, None] < t[None, :])
        incl = lower if d == 0 else lower.T
        blocks = []
        for m in (strict, incl):
            bd = jnp.kron(jnp.eye(2, dtype=f32), m.astype(f32))
            blocks.append(jnp.concatenate([bd, bd], axis=1))
        masks.append(jnp.concatenate(blocks, axis=0))
    return tri, jnp.stack(masks)


def _state_to_blockdiag(s):
    n = s.shape[0]
    s = s.reshape(n, 2, RW_HEADS // 2, 2, HD, HD)
    z = jnp.zeros_like(s[:, :, :, 0])
    top = jnp.concatenate([s[:, :, :, 0], z], axis=-1)
    bot = jnp.concatenate([z, s[:, :, :, 1]], axis=-1)
    return jnp.concatenate([top, bot], axis=-2)


def _rope_tables():
    t = jnp.arange(SAMPLE_LEN)
    rows = (t // GRID_W).astype(f32)
    cols = (t % GRID_W).astype(f32)
    nf = HD // 4
    inv = 10000.0 ** (-jnp.arange(nf, dtype=f32) / nf)
    ang = jnp.concatenate([rows[:, None] * inv, cols[:, None] * inv], axis=-1)
    cos = jnp.repeat(jnp.cos(ang), 2, axis=-1)
    sin = jnp.repeat(jnp.sin(ang), 2, axis=-1) * jnp.tile(jnp.array([-1.0, 1.0], f32), HD // 2)
    return jnp.tile(cos, (1, 2)), jnp.tile(sin, (1, 2))


def kernel(x_prompt, x_sample, cache_diff_k, cache_diff_v, cache_na_k, cache_na_v, state_rwkv, c, c_ctx, w_ada, b_ada, g_mix, g_ffn, w_in, w_out, diff_q_g, diff_k_g, diff_lam_q1, diff_lam_k1, diff_lam_q2, diff_lam_k2, diff_subln_g, na_q_g, na_k_g, na_rpb, ffn_w1, ffn_w3, ffn_w2, rw_mu, rw_wr, rw_wk, rw_wv, rw_wo, rw_w0, rw_w1, rw_w2, rw_a0, rw_a1, rw_a2, rw_g1, rw_g2, rw_k_k, rw_k_a, rw_r_k, rw_ln_g, rw_ln_b, moe_router, moe_router_b, moe_w1, moe_w3, moe_w2):
    xp = x_prompt.reshape(NP, D)
    xs = x_sample.reshape(NS, D)
    cond8 = jnp.concatenate([c_ctx[None, :], c, jnp.zeros((3, D), f32)], axis=0)
    mod = _ada_table(cond8, w_ada, b_ada)
    gmat = jnp.kron(jnp.eye(4, dtype=f32), jnp.ones((HD, HD), f32)).astype(bf16)

    lam_init = 0.8 - 0.6 * math.exp(-0.3 * 0)
    ones_seg = jnp.ones((SEG,), f32)
    tile8 = lambda gvec: jnp.tile(gvec, SEG // HD)
    gains = jnp.stack([tile8(diff_q_g[0]), tile8(diff_k_g[0]), ones_seg,
                       tile8(na_q_g[0]), tile8(na_k_g[0]), ones_seg]).reshape(6, 1, SEG)
    cos_t, sin_t = _rope_tables()
    proj, dk_p, dv_p, nk_p, nv_p, ffn_w1_bf, ffn_w3_bf, ffn_w2_bf = _in_projection(
        xp, xs, mod[0][0], mod[0][1], g_mix[0][None, :], w_in[0].astype(bf16), gains, gmat, cos_t, sin_t,
        (ffn_w1[0], ffn_w3[0], ffn_w2[0]))
    lamp = jnp.stack([diff_lam_q1[0], diff_lam_k1[0], diff_lam_q2[0], diff_lam_k2[0]])
    subg = diff_subln_g[0][None, :]
    o_prompt = _prompt_attention(proj, lamp, subg, lam_init)
    o_diff = _latent_diff_attention(proj, cache_diff_k[:, 0].reshape(N_SAMPLE_SEQ, PAST, SEG),
                                    cache_diff_v[:, 0].reshape(N_SAMPLE_SEQ, PAST, SEG), lamp, subg, lam_init)
    o_na = _latent_na_attention(proj, cache_na_k[:, 0].reshape(N_SAMPLE_SEQ, PAST, SEG),
                                cache_na_v[:, 0].reshape(N_SAMPLE_SEQ, PAST, SEG), _rpb_table(na_rpb[0]))
    x = _out_projection(o_prompt, o_diff, o_na, w_out[0].astype(bf16), xp, xs, mod[0][2])
    x = _dense_ffn(x, mod[0][3], mod[0][4], mod[0][5], g_ffn[0][None, :],
                   ffn_w1_bf, ffn_w3_bf, ffn_w2_bf)

    pad_g = GATE_LORA_PAD - GATE_LORA
    g1 = jnp.pad(rw_g1[0], ((0, 0), (0, pad_g))).astype(bf16)
    g2 = jnp.pad(rw_g2[0], ((0, pad_g), (0, 0))).astype(bf16)
    w1cat = jnp.concatenate([rw_w1[0, 0], rw_w1[0, 1]], axis=1).astype(bf16)
    w2cat = jnp.concatenate([rw_w2[0, 0], rw_w2[0, 1]], axis=0).astype(bf16)
    a1cat = jnp.concatenate([rw_a1[0, 0], rw_a1[0, 1]], axis=1).astype(bf16)
    a2cat = jnp.concatenate([rw_a2[0, 0], rw_a2[0, 1]], axis=0).astype(bf16)
    k_a = rw_k_a[0][None, :]
    (r, v, kkn, bonus, gate_lora, lwf, lwb, af, ab, k) = _rwkv_projection(
        x, mod[1][0], mod[1][1], g_mix[1][None, :], rw_mu[0],
        rw_wr[0].astype(bf16), rw_wk[0].astype(bf16), rw_wv[0].astype(bf16), g1, g2,
        w1cat, w2cat, a1cat, a2cat, rw_w0[0], rw_a0[0],
        rw_k_k[0][None, :], k_a, rw_r_k[0].reshape(1, D), gmat)
    tri, mask = _scan_constants()
    streams_f = (r, v, kkn, lwf, af, k)
    streams_b = (r, v, kkn, lwb, ab, k)
    yf_p, yb_p, st_p, moe_w1_bf, moe_w3_bf = _rwkv_scan(
        streams_f, streams_b, k_a, tri, mask, None, (moe_w1[0], moe_w3[0]),
        n_seq=N_PROMPT_SEQ, seq_len=PROMPT_LEN, row0=0, emit_state=True)
    yf_s, yb_s, moe_w2_bf = _rwkv_scan(
        streams_f, streams_b, k_a, tri, mask, _state_to_blockdiag(state_rwkv[:, 0]), (moe_w2[0],),
        n_seq=N_SAMPLE_SEQ, seq_len=SAMPLE_LEN, row0=NP, emit_state=False)
    x = _rwkv_out_projection(yf_p, yb_p, yf_s, yb_s, bonus, gate_lora, rw_ln_g[0][None, :], rw_ln_b[0][None, :], gmat,
                             rw_wo[0].astype(bf16), x, mod[1][2])
    w_router = jnp.pad(moe_router[0], ((0, 0), (0, 128 - N_EXPERTS)))
    b_router = jnp.pad(moe_router_b[0], (0, 128 - N_EXPERTS))[None, :]
    h_bf, gates, rank, rank_t, cnt = _moe_route(x, mod[1][3], mod[1][4], g_ffn[1][None, :], w_router, b_router)
    y_p, y_s = _moe_experts(cnt[:, 0, :N_EXPERTS].reshape(-1), h_bf, rank, rank_t, gates, x, mod[1][5],
                            moe_w1_bf, moe_w3_bf, moe_w2_bf)

    new_dk = dk_p.reshape(N_PROMPT_SEQ, 1, PROMPT_LEN, DIFF_HEADS, 2 * HD)
    new_dv = dv_p.reshape(N_PROMPT_SEQ, 1, PROMPT_LEN, DIFF_HEADS, 2 * HD)
    new_nk = nk_p.reshape(N_PROMPT_SEQ, 1, PROMPT_LEN, NA_HEADS, HD)
    new_nv = nv_p.reshape(N_PROMPT_SEQ, 1, PROMPT_LEN, NA_HEADS, HD)
    new_state = st_p[:, None]
    return (y_p.reshape(N_PROMPT_SEQ, PROMPT_LEN, D), y_s.reshape(N_SAMPLE_SEQ, SAMPLE_LEN, D),
            new_dk, new_dv, new_nk, new_nv, new_state)
```

```python
import functools
import math

import jax
import jax.numpy as jnp
from jax import lax
from jax.experimental import pallas as pl
from jax.experimental.pallas import tpu as pltpu

f32 = jnp.float32
bf16 = jnp.bfloat16

D = 1024
N_PROMPT_SEQ, PROMPT_LEN = 32, 256
N_SAMPLE_SEQ, SAMPLE_LEN = 4, 2048
NP = N_PROMPT_SEQ * PROMPT_LEN
NS = N_SAMPLE_SEQ * SAMPLE_LEN
NT = NP + NS
PAST = 256
GRID_W = 64
GRID_R = SAMPLE_LEN // GRID_W
HD = 64
DIFF_HEADS = 4
NA_HEADS = 8
NA_KH = 8
NA_KW = 16
SEG = 512
IN_COLS = 6 * SEG
FFN_DIM = 2816
N_EXPERTS = 8
EXPERT_DIM = 3584
RW_HEADS = 16
LORA = 64
GATE_LORA = 160
GATE_LORA_PAD = 256
EPS = 1e-6
GN_EPS = 64e-5
NEG_BIG = -1e30
DIFF_HEADS_PER_STEP = 2
NA_ROWS_PER_STEP = 16
CHUNK = 64
VMEM_LIMIT = 56 * 1024 * 1024

ADA_COLS = 1536
IN_PROJ_ROWS = 1024
DIFF_Q_ROWS = 256
OUT_PROJ_ROWS = 512
FFN_ROWS = 512
FFN_COLS = FFN_DIM // 2
RWKV_PROJ_ROWS = 256
RWKV_OUT_ROWS = 512


def _cond_idx(i, tm):
    return jnp.maximum((i * tm) // SAMPLE_LEN - (NP // SAMPLE_LEN - 1), 0)


def _dot(a, b):
    return jnp.dot(a, b, preferred_element_type=f32)


def _dot_nt(a, b):
    return lax.dot_general(a, b, (((1,), (1,)), ((), ())), preferred_element_type=f32)


def _dot_tn(a, b):
    return lax.dot_general(a, b, (((0,), (0,)), ((), ())), preferred_element_type=f32)


def _bdot(a, b):
    return lax.dot_general(a, b, (((2,), (1,)), ((0,), (0,))), preferred_element_type=f32)


def _bdot_nt(a, b):
    return lax.dot_general(a, b, (((2,), (2,)), ((0,), (0,))), preferred_element_type=f32)


def _bdot_tn(a, b):
    return lax.dot_general(a, b, (((1,), (1,)), ((0,), (0,))), preferred_element_type=f32)


def _sigmoid(x):
    return 0.5 * jnp.tanh(0.5 * x) + 0.5


def _norm_mod(x, g, sh, sc):
    ms = jnp.mean(x * x, axis=-1, keepdims=True)
    return (x * lax.rsqrt(ms + EPS) * g) * (1.0 + sc) + sh


def _group_sum(x, gmat):
    xb = x.astype(bf16)
    cols = [_dot(xb[:, c * 256:(c + 1) * 256], gmat) for c in range(x.shape[1] // 256)]
    return cols[0] if len(cols) == 1 else jnp.concatenate(cols, axis=1)


def _softmax_parts(parts):
    m = parts[0].max(axis=-1, keepdims=True)
    for p in parts[1:]:
        m = jnp.maximum(m, p.max(axis=-1, keepdims=True))
    es = [jnp.exp(p - m) for p in parts]
    l = es[0].sum(axis=-1, keepdims=True)
    for e in es[1:]:
        l = l + e.sum(axis=-1, keepdims=True)
    inv = 1.0 / l
    return [e * inv for e in es]


def _lane_lo(n=128):
    return lax.broadcasted_iota(jnp.int32, (1, n), 1) < HD


def _stack_halves(q):
    lo = _lane_lo()
    return jnp.concatenate([jnp.where(lo, q, 0.0), jnp.where(lo, 0.0, q)], axis=0)


def _ada_kernel(cond_ref, w_ref, b_ref, o_ref):
    x = cond_ref[...]
    s = x * _sigmoid(x)
    o_ref[0] = _dot(s.astype(bf16), w_ref[0].astype(bf16)) + b_ref[0]


def _ada_table(cond8, w_ada, b_ada):
    depth = w_ada.shape[0]
    tn = ADA_COLS
    out = pl.pallas_call(
        _ada_kernel,
        grid=(depth, 6 * D // tn),
        in_specs=[
            pl.BlockSpec((8, D), lambda l, n: (0, 0)),
            pl.BlockSpec((1, D, tn), lambda l, n: (l, 0, n)),
            pl.BlockSpec((1, 1, tn), lambda l, n: (l, 0, n)),
        ],
        out_specs=pl.BlockSpec((1, 8, tn), lambda l, n: (l, 0, n)),
        out_shape=jax.ShapeDtypeStruct((depth, 8, 6 * D), f32),
        compiler_params=pltpu.CompilerParams(vmem_limit_bytes=VMEM_LIMIT),
        name="ada_table",
    )(cond8, w_ada, b_ada.reshape(depth, 1, 6 * D))
    out = out.reshape(depth, 8, 6, D)
    return [[out[l, :, k, :].reshape(8, 1, D) for k in range(6)] for l in range(depth)]


def _qk_norm(y, gain, gmat):
    ss = _group_sum(y * y, gmat) * (1.0 / HD)
    return y * lax.rsqrt(ss + EPS) * gain


def _rope(y, cos, sin):
    even = (lax.broadcasted_iota(jnp.int32, (1, 128), 1) % 2) == 0
    outs = []
    for c in range(y.shape[1] // 128):
        yc = y[:, c * 128:(c + 1) * 128]
        swapped = jnp.where(even, pltpu.roll(yc, 127, 1), pltpu.roll(yc, 1, 1))
        outs.append(yc * cos + swapped * sin)
    return jnp.concatenate(outs, axis=1)


def _inproj_kernel(xp_ref, xs_ref, sh_ref, sc_ref, g_ref, w_ref, gain_ref, gmat_ref, cos_ref, sin_ref,
                   f1_ref, f3_ref, f2_ref,
                   o_ref, dk_ref, dv_ref, nk_ref, nv_ref, f1_out, f3_out, f2_out, h_scr, *, tm):
    i = pl.program_id(0)
    j = pl.program_id(1)

    @pl.when(j == 0)
    def _():
        for src, dst in ((f1_ref, f1_out), (f3_ref, f3_out), (f2_ref, f2_out)):
            dst[...] = src[...].astype(bf16)

    def emit(val):
        o_ref[...] = val.astype(bf16)
        for seg, cache_ref in ((1, dk_ref), (2, dv_ref), (4, nk_ref), (5, nv_ref)):
            @pl.when(jnp.logical_and(j == seg, i < NP // tm))
            def _():
                cache_ref[...] = val

    for x_ref, active in ((xp_ref, i < NP // tm), (xs_ref, i >= NP // tm)):
        @pl.when(jnp.logical_and(j == 0, active))
        def _():
            h_scr[...] = _norm_mod(x_ref[...], g_ref[...], sh_ref[0], sc_ref[0]).astype(bf16)

    project = lambda: _dot(h_scr[...], w_ref[...])
    is_norm = jnp.logical_and(j != 2, j != 5)
    is_rope = jnp.logical_and(j < 2, i >= NP // tm)

    @pl.when(jnp.logical_not(is_norm))
    def _():
        emit(project())

    @pl.when(jnp.logical_and(is_norm, jnp.logical_not(is_rope)))
    def _():
        emit(_qk_norm(project(), gain_ref[0], gmat_ref[...]))

    @pl.when(is_rope)
    def _():
        o_ref[...] = _rope(_qk_norm(project(), gain_ref[0], gmat_ref[...]),
                           cos_ref[...], sin_ref[...]).astype(bf16)


def _in_projection(xp, xs, sh, sc, g, w_bf, gains, gmat, cos_t, sin_t, ffn_weights):
    tm = IN_PROJ_ROWS
    n_tiles = NT // tm
    n_prompt_tiles = NP // tm
    tiles_per_seq = SAMPLE_LEN // tm
    cidx = functools.partial(_cond_idx, tm=tm)
    rope_idx = lambda i, j: (jnp.maximum(i - n_prompt_tiles, 0) % tiles_per_seq, 0)
    cast_specs = [pl.BlockSpec((w.shape[0] // n_tiles, w.shape[1]), lambda i, j: (i, 0)) for w in ffn_weights]
    return pl.pallas_call(
        functools.partial(_inproj_kernel, tm=tm),
        grid=(NT // tm, IN_COLS // SEG),
        in_specs=[
            pl.BlockSpec((tm, D), lambda i, j: (jnp.minimum(i, n_prompt_tiles - 1), 0)),
            pl.BlockSpec((tm, D), lambda i, j: (jnp.maximum(i - n_prompt_tiles, 0), 0)),
            pl.BlockSpec((1, 1, D), lambda i, j: (cidx(i), 0, 0)),
            pl.BlockSpec((1, 1, D), lambda i, j: (cidx(i), 0, 0)),
            pl.BlockSpec((1, D), lambda i, j: (0, 0)),
            pl.BlockSpec((D, SEG), lambda i, j: (0, j)),
            pl.BlockSpec((1, 1, SEG), lambda i, j: (j, 0, 0)),
            pl.BlockSpec((256, 256), lambda i, j: (0, 0)),
            pl.BlockSpec((tm, 128), rope_idx),
            pl.BlockSpec((tm, 128), rope_idx),
        ] + cast_specs,
        out_specs=[pl.BlockSpec((tm, SEG), lambda i, j: (i, j))] + [
            pl.BlockSpec((tm, SEG), lambda i, j: (jnp.minimum(i, n_prompt_tiles - 1), 0))] * 4 + cast_specs,
        out_shape=[jax.ShapeDtypeStruct((NT, IN_COLS), bf16)] + [jax.ShapeDtypeStruct((NP, SEG), f32)] * 4
        + [jax.ShapeDtypeStruct(w.shape, bf16) for w in ffn_weights],
        scratch_shapes=[pltpu.VMEM((tm, D), bf16)],
        compiler_params=pltpu.CompilerParams(
            dimension_semantics=("arbitrary", "arbitrary"), vmem_limit_bytes=VMEM_LIMIT),
        name="in_projection",
    )(xp, xs, sh, sc, g, w_bf, gains, gmat, cos_t, sin_t, *ffn_weights)


def _lambda_value(lamp_ref, lam_init):
    lp = lamp_ref[...]
    e1 = jnp.exp(jnp.sum(lp[0:1] * lp[1:2], axis=-1, keepdims=True))
    e2 = jnp.exp(jnp.sum(lp[2:3] * lp[3:4], axis=-1, keepdims=True))
    return e1 - e2 + lam_init


def _sub_ln(o, subg, lam_init):
    ms = jnp.mean(o * o, axis=-1, keepdims=True)
    return o * lax.rsqrt(ms + EPS) * subg * (1.0 - lam_init)


def _prompt_attn_kernel(p_ref, lamp_ref, subg_ref, o_ref, *, lam_init):
    lam = _lambda_value(lamp_ref, lam_init)
    t = PROMPT_LEN
    lo = _lane_lo()
    scale = HD ** -0.5
    for h in range(DIFF_HEADS):
        q = p_ref[:, h * 128:(h + 1) * 128]
        k = p_ref[:, SEG + h * 128:SEG + (h + 1) * 128].astype(bf16)
        v = p_ref[:, 2 * SEG + h * 128:2 * SEG + (h + 1) * 128].astype(bf16)
        s = _dot_nt((_stack_halves(q) * scale).astype(bf16), k)
        (p,) = _softmax_parts([s])
        pd = p[:t] - lam * p[t:]
        o = _dot(pd.astype(bf16), v)
        o_ref[:, h * 128:(h + 1) * 128] = _sub_ln(o, subg_ref[...], lam_init).astype(bf16)
    for hp in range(NA_HEADS // 2):
        q = p_ref[:, 3 * SEG + hp * 128:3 * SEG + (hp + 1) * 128]
        k = p_ref[:, 4 * SEG + hp * 128:4 * SEG + (hp + 1) * 128].astype(bf16)
        v = p_ref[:, 5 * SEG + hp * 128:5 * SEG + (hp + 1) * 128].astype(bf16)
        s = _dot_nt((_stack_halves(q) * scale).astype(bf16), k)
        (p,) = _softmax_parts([s])
        o = _dot(p.astype(bf16), v)
        o_ref[:, SEG + hp * 128:SEG + (hp + 1) * 128] = jnp.where(lo, o[:t], o[t:]).astype(bf16)


def _prompt_attention(proj, lamp, subg, lam_init):
    return pl.pallas_call(
        functools.partial(_prompt_attn_kernel, lam_init=lam_init),
        grid=(N_PROMPT_SEQ,),
        in_specs=[
            pl.BlockSpec((PROMPT_LEN, IN_COLS), lambda b: (b, 0)),
            pl.BlockSpec((4, HD), lambda b: (0, 0)),
            pl.BlockSpec((1, 128), lambda b: (0, 0)),
        ],
        out_specs=pl.BlockSpec((PROMPT_LEN, D), lambda b: (b, 0)),
        out_shape=jax.ShapeDtypeStruct((NP, D), bf16),
        compiler_params=pltpu.CompilerParams(
            dimension_semantics=("parallel",), vmem_limit_bytes=VMEM_LIMIT),
        name="prompt_attention",
    )(proj, lamp, subg)


def _latent_diff_kernel(q_ref, kn_ref, vn_ref, kc_ref, vc_ref, lamp_ref, subg_ref, o_ref,
                        *, lam_init, tq):
    lam = _lambda_value(lamp_ref, lam_init)
    scale = HD ** -0.5
    for hh in range(DIFF_HEADS_PER_STEP):
        ln = slice(hh * 128, (hh + 1) * 128)
        qq = (_stack_halves(q_ref[:, ln]) * scale).astype(bf16)
        s_c = _dot_nt(qq, kc_ref[0, :, ln].astype(bf16))
        s_n = _dot_nt(qq, kn_ref[:, ln].astype(bf16))
        p_c, p_n = _softmax_parts([s_c, s_n])
        pd_c = p_c[:tq] - lam * p_c[tq:]
        pd_n = p_n[:tq] - lam * p_n[tq:]
        o = (_dot(pd_c.astype(bf16), vc_ref[0, :, ln].astype(bf16))
             + _dot(pd_n.astype(bf16), vn_ref[:, ln].astype(bf16)))
        o_ref[:, ln] = _sub_ln(o, subg_ref[...], lam_init).astype(bf16)


def _latent_diff_attention(proj, cache_k, cache_v, lamp, subg, lam_init):
    tq = DIFF_Q_ROWS
    nqb = SAMPLE_LEN // tq
    q0 = NP // tq
    s0 = NP // SAMPLE_LEN
    hw = 128 * DIFF_HEADS_PER_STEP
    kcol = SEG // hw
    return pl.pallas_call(
        functools.partial(_latent_diff_kernel, lam_init=lam_init, tq=tq),
        grid=(N_SAMPLE_SEQ, DIFF_HEADS // DIFF_HEADS_PER_STEP, nqb),
        in_specs=[
            pl.BlockSpec((tq, hw), lambda b, h, q: (q0 + b * nqb + q, h)),
            pl.BlockSpec((SAMPLE_LEN, hw), lambda b, h, q: (s0 + b, kcol + h)),
            pl.BlockSpec((SAMPLE_LEN, hw), lambda b, h, q: (s0 + b, 2 * kcol + h)),
            pl.BlockSpec((1, PAST, hw), lambda b, h, q: (b, 0, h)),
            pl.BlockSpec((1, PAST, hw), lambda b, h, q: (b, 0, h)),
            pl.BlockSpec((4, HD), lambda b, h, q: (0, 0)),
            pl.BlockSpec((1, 128), lambda b, h, q: (0, 0)),
        ],
        out_specs=pl.BlockSpec((tq, hw), lambda b, h, q: (b * nqb + q, h)),
        out_shape=jax.ShapeDtypeStruct((NS, SEG), bf16),
        compiler_params=pltpu.CompilerParams(
            dimension_semantics=("parallel", "parallel", "arbitrary"), vmem_limit_bytes=VMEM_LIMIT),
        name="latent_diff_attention",
    )(proj, proj, proj, cache_k, cache_v, lamp, subg)


def _rpb_table_kernel(rpb_ref, o_ref):
    h = pl.program_id(0)
    wq = lax.broadcasted_iota(jnp.int32, (GRID_W, GRID_W), 0)
    wk = lax.broadcasted_iota(jnp.int32, (GRID_W, GRID_W), 1)
    col_start = jnp.clip(wq - NA_KW // 2, 0, GRID_W - NA_KW)
    col_in = jnp.logical_and(wk >= col_start, wk < col_start + NA_KW)
    col_off = jnp.clip(wk - wq, -(NA_KW - 1), NA_KW - 1) + (NA_KW - 1)
    n_dr = 2 * NA_KH - 1
    n_dc = 2 * NA_KW - 1
    for dr in range(n_dr):
        t = jnp.zeros((GRID_W, GRID_W), f32)
        for c in range(n_dc):
            t = jnp.where(col_off == c, rpb_ref[h * (n_dr * n_dc) + dr * n_dc + c], t)
        o_ref[0, dr] = jnp.where(col_in, t, NEG_BIG)


def _rpb_table(rpb):
    n_dr = 2 * NA_KH - 1
    tcol = pl.pallas_call(
        _rpb_table_kernel,
        grid=(NA_HEADS,),
        in_specs=[pl.BlockSpec(memory_space=pltpu.SMEM)],
        out_specs=pl.BlockSpec((1, n_dr, GRID_W, GRID_W), lambda h: (h, 0, 0, 0)),
        out_shape=jax.ShapeDtypeStruct((NA_HEADS, n_dr, GRID_W, GRID_W), f32),
        name="rpb_table",
    )(rpb.reshape(-1))
    return jnp.stack(
        [jnp.concatenate([tcol[:, j - s + NA_KH - 1] for j in range(NA_KH)], axis=-1) for s in range(NA_KH)],
        axis=1)


def _latent_na_kernel(q_ref, k_ref, v_ref, kc_ref, vc_ref, bias_ref, o_ref):
    scale = HD ** -0.5
    lo = _lane_lo()
    kc = kc_ref[0].astype(bf16)
    vc = vc_ref[0].astype(bf16)
    win = NA_KH * GRID_W

    nr = NA_ROWS_PER_STEP
    kcb = jnp.broadcast_to(kc[None], (nr,) + kc.shape)
    vcb = jnp.broadcast_to(vc[None], (nr,) + vc.shape)

    def rows(g, carry):
        qs, kws, vws, biases = [], [], [], []
        for t in range(nr):
            r = g * nr + t
            rs = jnp.clip(r - NA_KH // 2, 0, GRID_R - NA_KH)
            sidx = r - rs
            q = q_ref[pl.ds(pl.multiple_of(r * GRID_W, GRID_W), GRID_W), :]
            qs.append((_stack_halves(q) * scale).astype(bf16))
            k0 = pl.multiple_of(rs * GRID_W, GRID_W)
            kws.append(k_ref[pl.ds(k0, win), :].astype(bf16))
            vws.append(v_ref[pl.ds(k0, win), :].astype(bf16))
            biases.append(jnp.concatenate([bias_ref[0, sidx], bias_ref[1, sidx]], axis=0))
        qq = jnp.stack(qs)
        s_loc = _bdot_nt(qq, jnp.stack(kws)) + jnp.stack(biases)
        s_ctx = _bdot_nt(qq, kcb)
        p_loc, p_ctx = _softmax_parts([s_loc, s_ctx])
        o = _bdot(p_loc.astype(bf16), jnp.stack(vws)) + _bdot(p_ctx.astype(bf16), vcb)
        o = jnp.where(lo, o[:, :GRID_W], o[:, GRID_W:]).astype(bf16)
        row0 = pl.multiple_of(g * (nr * GRID_W), nr * GRID_W)
        o_ref[pl.ds(row0, nr * GRID_W), :] = o.reshape(nr * GRID_W, 128)
        return carry

    lax.fori_loop(0, GRID_R // nr, rows, 0)


def _latent_na_attention(proj, cache_k, cache_v, bias):
    s0 = NP // SAMPLE_LEN
    return pl.pallas_call(
        _latent_na_kernel,
        grid=(N_SAMPLE_SEQ, NA_HEADS // 2),
        in_specs=[
            pl.BlockSpec((SAMPLE_LEN, 128), lambda b, h: (s0 + b, 12 + h)),
            pl.BlockSpec((SAMPLE_LEN, 128), lambda b, h: (s0 + b, 16 + h)),
            pl.BlockSpec((SAMPLE_LEN, 128), lambda b, h: (s0 + b, 20 + h)),
            pl.BlockSpec((1, PAST, 128), lambda b, h: (b, 0, h)),
            pl.BlockSpec((1, PAST, 128), lambda b, h: (b, 0, h)),
            pl.BlockSpec((2, NA_KH, GRID_W, NA_KH * GRID_W), lambda b, h: (h, 0, 0, 0)),
        ],
        out_specs=pl.BlockSpec((SAMPLE_LEN, 128), lambda b, h: (b, h)),
        out_shape=jax.ShapeDtypeStruct((NS, SEG), bf16),
        compiler_params=pltpu.CompilerParams(
            dimension_semantics=("parallel", "parallel"), vmem_limit_bytes=VMEM_LIMIT),
        name="latent_na_attention",
    )(proj, proj, proj, cache_k, cache_v, bias)


def _out_proj_kernel(ap_ref, ad_ref, an_ref, w_ref, xp_ref, xs_ref, gt_ref, o_ref, *, tm):
    i = pl.program_id(0)

    @pl.when(i < NP // tm)
    def _():
        o_ref[...] = xp_ref[...] + gt_ref[0] * _dot(ap_ref[...], w_ref[...])

    @pl.when(i >= NP // tm)
    def _():
        mixed = _dot(ad_ref[...], w_ref[:SEG, :]) + _dot(an_ref[...], w_ref[SEG:, :])
        o_ref[...] = xs_ref[...] + gt_ref[0] * mixed


def _out_projection(a_prompt, a_diff, a_na, w_bf, xp, xs, gate):
    tm = OUT_PROJ_ROWS
    n_p = NP // tm
    cidx = functools.partial(_cond_idx, tm=tm)
    prompt_rows = lambda i: (jnp.minimum(i, n_p - 1), 0)
    sample_rows = lambda i: (jnp.maximum(i - n_p, 0), 0)
    return pl.pallas_call(
        functools.partial(_out_proj_kernel, tm=tm),
        grid=(NT // tm,),
        in_specs=[
            pl.BlockSpec((tm, D), prompt_rows),
            pl.BlockSpec((tm, SEG), sample_rows),
            pl.BlockSpec((tm, SEG), sample_rows),
            pl.BlockSpec((D, D), lambda i: (0, 0)),
            pl.BlockSpec((tm, D), prompt_rows),
            pl.BlockSpec((tm, D), sample_rows),
            pl.BlockSpec((1, 1, D), lambda i: (cidx(i), 0, 0)),
        ],
        out_specs=pl.BlockSpec((tm, D), lambda i: (i, 0)),
        out_shape=jax.ShapeDtypeStruct((NT, D), f32),
        compiler_params=pltpu.CompilerParams(
            dimension_semantics=("arbitrary",), vmem_limit_bytes=VMEM_LIMIT),
        name="out_projection",
    )(a_prompt, a_diff, a_na, w_bf, xp, xs, gate)


def _rwkv_out_kernel(yfp_ref, ybp_ref, yfs_ref, ybs_ref, bonus_ref, gate_ref, lng_ref, lnb_ref,
                     gmat_ref, w_ref, x_ref, gt_ref, o_ref, *, tm):
    i = pl.program_id(0)

    def finish(y):
        gmat = gmat_ref[...]
        mu = _group_sum(y, gmat) * (1.0 / HD)
        yc = y - mu
        var = _group_sum(yc * yc, gmat) * (1.0 / HD)
        z = yc * lax.rsqrt(var + GN_EPS) * lng_ref[...] + lnb_ref[...] + bonus_ref[...].astype(f32)
        z = (z * gate_ref[...].astype(f32)).astype(bf16)
        o_ref[...] = x_ref[...] + gt_ref[0] * _dot(z, w_ref[...])

    @pl.when(i < NP // tm)
    def _():
        finish(yfp_ref[...].astype(f32) + ybp_ref[...].astype(f32))

    @pl.when(i >= NP // tm)
    def _():
        finish(yfs_ref[...].astype(f32) + ybs_ref[...].astype(f32))


def _rwkv_out_projection(yf_p, yb_p, yf_s, yb_s, bonus, gate_lora, ln_g, ln_b, gmat, w_bf, x, gate):
    tm = RWKV_OUT_ROWS
    n_p = NP // tm
    cidx = functools.partial(_cond_idx, tm=tm)
    row = pl.BlockSpec((tm, D), lambda i: (i, 0))
    prompt_row = pl.BlockSpec((tm, D), lambda i: (jnp.minimum(i, n_p - 1), 0))
    sample_row = pl.BlockSpec((tm, D), lambda i: (jnp.maximum(i - n_p, 0), 0))
    vec = pl.BlockSpec((1, D), lambda i: (0, 0))
    return pl.pallas_call(
        functools.partial(_rwkv_out_kernel, tm=tm),
        grid=(NT // tm,),
        in_specs=[prompt_row, prompt_row, sample_row, sample_row, row, row, vec, vec,
                  pl.BlockSpec((256, 256), lambda i: (0, 0)),
                  pl.BlockSpec((D, D), lambda i: (0, 0)),
                  row,
                  pl.BlockSpec((1, 1, D), lambda i: (cidx(i), 0, 0))],
        out_specs=row,
        out_shape=jax.ShapeDtypeStruct((NT, D), f32),
        compiler_params=pltpu.CompilerParams(
            dimension_semantics=("arbitrary",), vmem_limit_bytes=VMEM_LIMIT),
        name="rwkv_out_projection",
    )(yf_p, yb_p, yf_s, yb_s, bonus, gate_lora, ln_g, ln_b, gmat, w_bf, x, gate)


def _swiglu_hidden(xb, w1, w3):
    a = _dot(xb, w1)
    return ((a * _sigmoid(a)) * _dot(xb, w3)).astype(bf16)


def _ffn_kernel(x_ref, sh_ref, sc_ref, gt_ref, g_ref, w1_ref, w3_ref, w2_ref, o_ref, h_scr, acc_scr):
    f = pl.program_id(1)

    @pl.when(f == 0)
    def _():
        h_scr[...] = _norm_mod(x_ref[...], g_ref[...], sh_ref[0], sc_ref[0]).astype(bf16)
        acc_scr[...] = jnp.zeros_like(acc_scr)

    acc_scr[...] += _dot(_swiglu_hidden(h_scr[...], w1_ref[...], w3_ref[...]), w2_ref[...])

    @pl.when(f == pl.num_programs(1) - 1)
    def _():
        o_ref[...] = x_ref[...] + gt_ref[0] * acc_scr[...]


def _dense_ffn(x, sh, sc, gt, g, w1, w3, w2):
    tm = FFN_ROWS
    tf = FFN_COLS
    cidx = functools.partial(_cond_idx, tm=tm)
    mod = pl.BlockSpec((1, 1, D), lambda i, f: (cidx(i), 0, 0))
    return pl.pallas_call(
        _ffn_kernel,
        grid=(NT // tm, FFN_DIM // tf),
        in_specs=[
            pl.BlockSpec((tm, D), lambda i, f: (i, 0)),
            mod, mod, mod,
            pl.BlockSpec((1, D), lambda i, f: (0, 0)),
            pl.BlockSpec((D, tf), lambda i, f: (0, f)),
            pl.BlockSpec((D, tf), lambda i, f: (0, f)),
            pl.BlockSpec((tf, D), lambda i, f: (f, 0)),
        ],
        out_specs=pl.BlockSpec((tm, D), lambda i, f: (i, 0)),
        out_shape=jax.ShapeDtypeStruct((NT, D), f32),
        scratch_shapes=[pltpu.VMEM((tm, D), bf16), pltpu.VMEM((tm, D), f32)],
        compiler_params=pltpu.CompilerParams(
            dimension_semantics=("parallel", "arbitrary"), vmem_limit_bytes=VMEM_LIMIT),
        name="ffn_mixer",
    )(x, sh, sc, gt, g, w1, w3, w2)


MOE_TILE = 1024
MOE_MAIN = 288
MOE_EXTRA = 128
MOE_ROWS = MOE_MAIN + -(-(MOE_TILE - MOE_MAIN) // MOE_EXTRA) * MOE_EXTRA


RANK_SEL = 4096


def _router_kernel(x_ref, sh_ref, sc_ref, g_ref, wr_ref, br_ref, tri_ref, triu_ref,
                   h_out, gates_out, rank_out, rank_t_out, cnt_out):
    h = _norm_mod(x_ref[...], g_ref[...], sh_ref[0], sc_ref[0])
    h_out[...] = h.astype(bf16)
    w = wr_ref[...]
    h_hi, w_hi = h.astype(bf16), w.astype(bf16)
    h_lo = (h - h_hi.astype(f32)).astype(bf16)
    w_lo = (w - w_hi.astype(f32)).astype(bf16)
    hw = _dot(h_hi, jnp.concatenate([w_hi, w_lo], axis=1))
    logits = hw[:, :128] + (hw[:, 128:] + _dot(h_lo, w_hi)) + br_ref[...]
    lane = lax.broadcasted_iota(jnp.int32, logits.shape, 1)
    logits = jnp.where(lane < N_EXPERTS, logits, -jnp.inf)
    m1 = logits.max(axis=-1, keepdims=True)
    i1 = jnp.min(jnp.where(logits == m1, lane, 128), axis=-1, keepdims=True)
    rest = jnp.where(lane == i1, -jnp.inf, logits)
    m2 = rest.max(axis=-1, keepdims=True)
    i2 = jnp.min(jnp.where(rest == m2, lane, 128), axis=-1, keepdims=True)
    e2 = jnp.exp(m2 - m1)
    den = 1.0 / (1.0 + e2)
    gates_out[...] = jnp.where(lane == i1, den, jnp.where(lane == i2, e2 * den, 0.0))
    sel = jnp.logical_or(lane == i1, lane == i2)
    self32 = sel.astype(f32)
    selb = self32.astype(bf16)
    rank_out[...] = jnp.where(sel, _dot(tri_ref[...], selb), -1.0)
    rank_t_out[...] = _dot_tn(selb, triu_ref[...])
    cnt_out[0] = jnp.sum(self32, axis=0, keepdims=True).astype(jnp.int32)


def _moe_route(x, sh, sc, g, w_router, b_router):
    tm = MOE_TILE
    cidx = functools.partial(_cond_idx, tm=tm)
    t = jnp.arange(tm)
    tri = (t[:, None] > t[None, :]).astype(bf16)
    triu = jnp.where(t[:, None] == t[None, :], float(RANK_SEL), tri.T.astype(f32)).astype(bf16)
    mod = pl.BlockSpec((1, 1, D), lambda i: (cidx(i), 0, 0))
    return pl.pallas_call(
        _router_kernel,
        grid=(NT // tm,),
        in_specs=[
            pl.BlockSpec((tm, D), lambda i: (i, 0)),
            mod, mod,
            pl.BlockSpec((1, D), lambda i: (0, 0)),
            pl.BlockSpec((D, 128), lambda i: (0, 0)),
            pl.BlockSpec((1, 128), lambda i: (0, 0)),
            pl.BlockSpec((tm, tm), lambda i: (0, 0)),
            pl.BlockSpec((tm, tm), lambda i: (0, 0)),
        ],
        out_specs=[
            pl.BlockSpec((tm, D), lambda i: (i, 0)),
            pl.BlockSpec((tm, 128), lambda i: (i, 0)),
            pl.BlockSpec((tm, 128), lambda i: (i, 0)),
            pl.BlockSpec((128, tm), lambda i: (0, i)),
            pl.BlockSpec((1, 1, 128), lambda i: (i, 0, 0)),
        ],
        out_shape=[
            jax.ShapeDtypeStruct((NT, D), bf16),
            jax.ShapeDtypeStruct((NT, 128), f32),
            jax.ShapeDtypeStruct((NT, 128), f32),
            jax.ShapeDtypeStruct((128, NT), f32),
            jax.ShapeDtypeStruct((NT // tm, 1, 128), jnp.int32),
        ],
        compiler_params=pltpu.CompilerParams(
            dimension_semantics=("parallel",), vmem_limit_bytes=VMEM_LIMIT),
        name="moe_router",
    )(x, sh, sc, g, w_router, b_router, tri, triu)


def _moe_kernel(cnt_ref, h_ref, rank_ref, rank_t_ref, gates_ref, x_ref, gt_ref, w1_ref, w3_ref, w2_ref,
                op_ref, os_ref, o_ref, xc_scr, acc_scr, rcol_scr, gcol_scr):
    i = pl.program_id(0)
    e = pl.program_id(1)
    f = pl.program_id(2)
    cnt = cnt_ref[i * N_EXPERTS + e]
    n_extra = jnp.maximum(cnt - MOE_MAIN + MOE_EXTRA - 1, 0) // MOE_EXTRA

    def for_each_block(fn):
        fn(0, MOE_MAIN)

        def body(b, carry):
            fn(pl.multiple_of(MOE_MAIN + b * MOE_EXTRA, 32), MOE_EXTRA)
            return carry

        lax.fori_loop(0, n_extra, body, 0)

    @pl.when(jnp.logical_and(e == 0, f == 0))
    def _():
        o_ref[...] = x_ref[...]

    @pl.when(f == 0)
    def _():
        lane = lax.broadcasted_iota(jnp.int32, (MOE_TILE, 128), 1)
        mine = lane == e
        rcol_scr[...] = jnp.sum(jnp.where(mine, rank_ref[...], 0.0), axis=1, keepdims=True).astype(jnp.int32)
        gcol_scr[...] = jnp.sum(jnp.where(mine, gates_ref[...], 0.0), axis=1, keepdims=True)

        rrow = rank_t_ref[pl.ds(e, 1), :].astype(jnp.int32)

        def gather(slot0, nrows):
            slot = lax.broadcasted_iota(jnp.int32, (nrows, MOE_TILE), 0) + (slot0 + RANK_SEL)
            xc_scr[pl.ds(slot0, nrows), :] = _dot((rrow == slot).astype(bf16), h_ref[...]).astype(bf16)

        for_each_block(gather)

    def expert(slot0, nrows):
        rows = pl.ds(slot0, nrows)
        part = _dot(_swiglu_hidden(xc_scr[rows, :], w1_ref[0], w3_ref[0]), w2_ref[0])

        @pl.when(f == 0)
        def _():
            acc_scr[rows, :] = part

        @pl.when(f != 0)
        def _():
            acc_scr[rows, :] += part

    for_each_block(expert)

    @pl.when(f == pl.num_programs(2) - 1)
    def _():
        def scatter(slot0, nrows):
            out = acc_scr[pl.ds(slot0, nrows), :].astype(bf16)
            for t0 in range(0, MOE_TILE, 256):
                rows = slice(t0, t0 + 256)
                slot = lax.broadcasted_iota(jnp.int32, (256, nrows), 1) + slot0
                pt = (rcol_scr[rows, :] == slot).astype(bf16)
                o_ref[rows, :] += (gcol_scr[rows, :] * gt_ref[0]) * _dot(pt, out)

        for_each_block(scatter)

    for dst, mine in ((op_ref, i < NP // MOE_TILE), (os_ref, i >= NP // MOE_TILE)):
        @pl.when(jnp.logical_and(mine, jnp.logical_and(e == pl.num_programs(1) - 1, f == pl.num_programs(2) - 1)))
        def _():
            dst[...] = o_ref[...]


def _moe_experts(cnt, h_bf, rank, rank_t, gates, x, gt, w1, w3, w2):
    tm = MOE_TILE
    tf = EXPERT_DIM // 2
    n_p = NP // tm
    cidx = functools.partial(_cond_idx, tm=tm)
    once = pl.Buffered(1)
    grid_spec = pltpu.PrefetchScalarGridSpec(
        num_scalar_prefetch=1,
        grid=(NT // tm, N_EXPERTS, EXPERT_DIM // tf),
        in_specs=[
            pl.BlockSpec((tm, D), lambda i, e, f, c: (i, 0), pipeline_mode=once),
            pl.BlockSpec((tm, 128), lambda i, e, f, c: (i, 0), pipeline_mode=once),
            pl.BlockSpec((128, tm), lambda i, e, f, c: (0, i), pipeline_mode=once),
            pl.BlockSpec((tm, 128), lambda i, e, f, c: (i, 0), pipeline_mode=once),
            pl.BlockSpec((tm, D), lambda i, e, f, c: (i, 0), pipeline_mode=once),
            pl.BlockSpec((1, 1, D), lambda i, e, f, c: (cidx(i), 0, 0)),
            pl.BlockSpec((1, D, tf), lambda i, e, f, c: (e, 0, f)),
            pl.BlockSpec((1, D, tf), lambda i, e, f, c: (e, 0, f)),
            pl.BlockSpec((1, tf, D), lambda i, e, f, c: (e, f, 0)),
        ],
        out_specs=[
            pl.BlockSpec((tm, D), lambda i, e, f, c: (jnp.minimum(i, n_p - 1), 0), pipeline_mode=once),
            pl.BlockSpec((tm, D), lambda i, e, f, c: (jnp.maximum(i - n_p, 0), 0), pipeline_mode=once),
        ],
        scratch_shapes=[pltpu.VMEM((tm, D), f32),
                        pltpu.VMEM((MOE_ROWS, D), bf16), pltpu.VMEM((MOE_ROWS, D), f32),
                        pltpu.VMEM((tm, 1), jnp.int32), pltpu.VMEM((tm, 1), f32)],
    )
    return pl.pallas_call(
        _moe_kernel,
        grid_spec=grid_spec,
        out_shape=[jax.ShapeDtypeStruct((NP, D), f32), jax.ShapeDtypeStruct((NS, D), f32)],
        compiler_params=pltpu.CompilerParams(
            dimension_semantics=("arbitrary", "arbitrary", "arbitrary"), vmem_limit_bytes=VMEM_LIMIT),
        name="moe_experts",
    )(cnt, h_bf, rank, rank_t, gates, x, gt, w1, w3, w2)


def _in_context_key(k, a, k_a):
    return k * (1.0 + (a - 1.0) * k_a)


def _rwkv_proj_kernel(x_ref, xp_ref, xn_ref, sh_ref, sc_ref, g_ref, mu_ref,
                      wr_ref, wk_ref, wv_ref, g1_ref, g2_ref, w1_ref, w2_ref, a1_ref, a2_ref,
                      w0_ref, a0_ref, kk_ref, ka_ref, rk_ref, gmat_ref,
                      r_out, v_out, kkn_out, bonus_out, gate_out,
                      lwf_out, lwb_out, af_out, ab_out, k_out, *, tm):
    i = pl.program_id(0)
    g, sh, sc = g_ref[...], sh_ref[0], sc_ref[0]
    h = _norm_mod(x_ref[...], g, sh, sc)
    n_prompt_tiles = NP // tm
    tiles_per_seq = SAMPLE_LEN // tm
    pos = (i - n_prompt_tiles) % tiles_per_seq
    has_prev = jnp.logical_and(i >= n_prompt_tiles, pos != 0)
    has_next = jnp.logical_and(i >= n_prompt_tiles, pos != tiles_per_seq - 1)
    h_before = jnp.where(has_prev, _norm_mod(xp_ref[...], g, sh, sc)[7:8], 0.0)
    h_after = jnp.where(has_next, _norm_mod(xn_ref[...], g, sh, sc)[0:1], 0.0)
    rowi = lax.broadcasted_iota(jnp.int32, (tm, 1), 0)
    h_prev = jnp.where(rowi == 0, h_before, pltpu.roll(h, 1, 0))
    h_next = jnp.where(rowi == tm - 1, h_after, pltpu.roll(h, tm - 1, 0))
    xx = 0.5 * (h_prev + h_next) - h
    mix = lambda n: (h + xx * mu_ref[n:n + 1]).astype(bf16)

    r = _dot(mix(0), wr_ref[...])
    k = _dot(mix(2), wk_ref[...])
    v = _dot(mix(3), wv_ref[...])
    gate_out[...] = _dot(_sigmoid(_dot(mix(5), g1_ref[...])).astype(bf16), g2_ref[...]).astype(bf16)

    lo = _lane_lo()
    tw = jnp.tanh(_dot(mix(1), w1_ref[...]))
    ta = _dot(mix(4), a1_ref[...])
    gmat = gmat_ref[...]
    kk = k * kk_ref[...]
    kkn_out[...] = kk * lax.rsqrt(_group_sum(kk * kk, gmat) + 1e-12)
    r_out[...] = r
    v_out[...] = v
    k_out[...] = k
    kd_sum = jnp.zeros_like(k)
    for d, (lw_out, a_out) in enumerate(((lwf_out, af_out), (lwb_out, ab_out))):
        keep = lo if d == 0 else jnp.logical_not(lo)
        zw = w0_ref[d:d + 1] + _dot(jnp.where(keep, tw, 0.0).astype(bf16), w2_ref[...])
        lw_out[...] = -math.exp(-0.5) * _sigmoid(zw)
        a = _sigmoid(a0_ref[d:d + 1] + _dot(jnp.where(keep, ta, 0.0).astype(bf16), a2_ref[...]))
        a_out[...] = a
        kd_sum = kd_sum + _in_context_key(k, a, ka_ref[...])
    bonus_out[...] = (_group_sum(r * kd_sum * rk_ref[...], gmat) * v).astype(bf16)


def _rwkv_projection(x, sh, sc, g, mu, wr, wk, wv, g1, g2, w1, w2, a1, a2, w0, a0, k_k, k_a, r_k, gmat):
    tm = RWKV_PROJ_ROWS
    cidx = functools.partial(_cond_idx, tm=tm)
    hb = tm // 8
    n8 = NT // 8
    full = lambda shape: pl.BlockSpec(shape, lambda i: tuple(0 for _ in shape))
    row = pl.BlockSpec((tm, D), lambda i: (i, 0))
    return pl.pallas_call(
        functools.partial(_rwkv_proj_kernel, tm=tm),
        grid=(NT // tm,),
        in_specs=[
            row,
            pl.BlockSpec((8, D), lambda i: (jnp.maximum(i * hb - 1, 0), 0)),
            pl.BlockSpec((8, D), lambda i: (jnp.minimum((i + 1) * hb, n8 - 1), 0)),
            pl.BlockSpec((1, 1, D), lambda i: (cidx(i), 0, 0)),
            pl.BlockSpec((1, 1, D), lambda i: (cidx(i), 0, 0)),
            full((1, D)), full((6, D)),
            full((D, D)), full((D, D)), full((D, D)),
            full((D, GATE_LORA_PAD)), full((GATE_LORA_PAD, D)),
            full((D, 2 * LORA)), full((2 * LORA, D)), full((D, 2 * LORA)), full((2 * LORA, D)),
            full((2, D)), full((2, D)), full((1, D)), full((1, D)), full((1, D)),
            full((256, 256)),
        ],
        out_specs=[row] * 10,
        out_shape=[jax.ShapeDtypeStruct((NT, D), dt) for dt in (f32, f32, f32, bf16, bf16) + (f32,) * 5],
        compiler_params=pltpu.CompilerParams(
            dimension_semantics=("parallel",), vmem_limit_bytes=VMEM_LIMIT),
        name="rwkv_projection",
    )(x, x, x, sh, sc, g, mu, wr, wk, wv, g1, g2, w1, w2, a1, a2, w0, a0, k_k, k_a, r_k, gmat)


def _split3(x):
    x1 = x.astype(bf16)
    r1 = x - x1.astype(f32)
    x2 = r1.astype(bf16)
    x3 = (r1 - x2.astype(f32)).astype(bf16)
    return x1, x2, x3


def _scan_kernel(*refs, nc, has_init, emit_state, n_casts):
    fwd_refs = refs[0:6]
    bwd_refs = refs[6:12]
    ka_ref, tri_ref, mask_ref = refs[12:15]
    pos = 15
    s0_ref = None
    if has_init:
        s0_ref = refs[pos]
        pos += 1
    cast_in = refs[pos:pos + n_casts]
    pos += n_casts
    yf_ref, yb_ref = refs[pos:pos + 2]
    pos += 2
    sout_ref = None
    if emit_state:
        sout_ref = refs[pos]
        pos += 1
    cast_out = refs[pos:pos + n_casts]
    pos += n_casts
    st_scr, cl_scr = refs[pos:pos + 2]

    for src, dst in zip(cast_in, cast_out):
        dst[...] = src[...].astype(bf16)

    s = pl.program_id(1)

    @pl.when(s == 0)
    def _():
        if has_init:
            st_scr[...] = s0_ref[0]
        else:
            st_scr[...] = jnp.zeros_like(st_scr)

    c = CHUNK
    for d, drefs in enumerate((fwd_refs, bwd_refs)):
        lw = drefs[3][...]
        tri = tri_ref[d]
        p1, p2, p3 = _split3(lw)
        cl_scr[d] = _dot(tri, p1) + _dot(tri, p2) + _dot(tri, p3)

    lane = lax.broadcasted_iota(jnp.int32, (1, 128), 1)
    m0 = (lane < HD).astype(f32)
    m1 = 1.0 - m0
    rid = lax.broadcasted_iota(jnp.int32, (128, 128), 0)
    cid = lax.broadcasted_iota(jnp.int32, (128, 128), 1)
    eye = (rid == cid).astype(f32)

    n_pairs = RW_HEADS // 2
    nb = 2 * n_pairs

    def per_head_rows(x):
        return jnp.concatenate([x * m0, x * m1], axis=0)

    ars_l, bk_l, v2_l, dec_l = [], [], [], []
    for d, drefs in enumerate((fwd_refs, bwd_refs)):
        r_ref, v_ref, kk_ref, lw_ref, a_ref, k_ref = drefs
        end_row = c - 1 if d == 0 else 0
        for p in range(n_pairs):
            ln = slice(p * 128, (p + 1) * 128)
            cl = cl_scr[d, :, ln]
            kk = kk_ref[:, ln]
            tot = cl[end_row:end_row + 1]
            e_inv = jnp.exp(-cl)
            a = a_ref[:, ln]
            kka = kk * a
            kd = _in_context_key(k_ref[:, ln], a, ka_ref[:, ln])
            at = per_head_rows(-kk * jnp.exp(cl - lw_ref[:, ln]))
            rt = per_head_rows(r_ref[:, ln] * jnp.exp(cl))
            ars_l.append(jnp.concatenate([at, rt], axis=0).astype(bf16))
            bk_l.append(jnp.concatenate([per_head_rows(kka * e_inv), per_head_rows(kd * e_inv)],
                                        axis=0).astype(bf16))
            v2_l.append(per_head_rows(v_ref[:, ln]).astype(bf16))
            dec_l.append(jnp.exp(tot))
    ar = jnp.stack(ars_l)
    bk = jnp.stack(bk_l)
    v2 = jnp.stack(v2_l)
    dec = jnp.stack(dec_l)

    st = st_scr[...].reshape(nb, 128, 128)
    g2 = _bdot_nt(ar, bk).reshape(2, n_pairs, 256, 256) * mask_ref[...][:, None]
    g2 = g2.reshape(nb, 256, 256)
    ars = _bdot_nt(ar, st.astype(bf16))
    gv = _bdot(g2[:, :, 128:].astype(bf16), v2)
    l_bd = g2[:, :128, :128]
    l_bf = l_bd.astype(bf16)
    pk = _bdot(l_bf, l_bf)
    q = eye[None] + l_bd
    for it in range(5):
        pkb = pk.astype(bf16)
        if it < 4:
            res = _bdot(jnp.concatenate([q, pk], axis=1).astype(bf16), pkb)
            q = q + res[:, :128]
            pk = res[:, 128:]
        else:
            q = q + _bdot(q.astype(bf16), pkb)
    rhs = ars[:, :128] + gv[:, :128]
    u2b = _bdot(q.astype(bf16), rhs.astype(bf16)).astype(bf16)
    y2 = ars[:, 128:] + gv[:, 128:] + _bdot(g2[:, 128:, :128].astype(bf16), u2b)
    y = y2[:, :c] + y2[:, c:]
    for d, y_ref in enumerate((yf_ref, yb_ref)):
        for p in range(n_pairs):
            y_ref[:, p * 128:(p + 1) * 128] = y[d * n_pairs + p].astype(y_ref.dtype)
    uv = jnp.concatenate([u2b, v2], axis=1)
    st_scr[...] = ((st + _bdot_tn(uv, bk)) * dec).reshape(2, n_pairs, 128, 128)

    if emit_state:
        @pl.when(s == nc - 1)
        def _():
            for d in range(2):
                for p in range(n_pairs):
                    m = st_scr[d, p]
                    sout_ref[0, d, 2 * p] = m[:HD, :HD]
                    sout_ref[0, d, 2 * p + 1] = pltpu.roll(m[HD:, :], HD, 1)[:, :HD]


def _rwkv_scan(streams_f, streams_b, k_a, tri, mask, s0_bd, casts, *, n_seq, seq_len, row0, emit_state):
    nc = seq_len // CHUNK
    blk0 = row0 // CHUNK
    fwd_spec = pl.BlockSpec((CHUNK, D), lambda b, s: (blk0 + b * nc + s, 0))
    bwd_spec = pl.BlockSpec((CHUNK, D), lambda b, s: (blk0 + b * nc + nc - 1 - s, 0))
    in_specs = [fwd_spec] * 6 + [bwd_spec] * 6 + [
        pl.BlockSpec((1, D), lambda b, s: (0, 0)),
        pl.BlockSpec((2, CHUNK, CHUNK), lambda b, s: (0, 0, 0)),
        pl.BlockSpec((2, 256, 256), lambda b, s: (0, 0, 0)),
    ]
    args = list(streams_f) + list(streams_b) + [k_a, tri, mask]
    state_block = (1, 2, RW_HEADS // 2, 128, 128)
    if s0_bd is not None:
        in_specs.append(pl.BlockSpec(state_block, lambda b, s: (b, 0, 0, 0, 0)))
        args.append(s0_bd)
    out_specs = [pl.BlockSpec((CHUNK, D), lambda b, s: (b * nc + s, 0)),
                 pl.BlockSpec((CHUNK, D), lambda b, s: (b * nc + nc - 1 - s, 0))]
    out_shape = [jax.ShapeDtypeStruct((n_seq * seq_len, D), bf16)] * 2
    if emit_state:
        out_specs.append(pl.BlockSpec((1, 2, RW_HEADS, HD, HD), lambda b, s: (b, 0, 0, 0, 0)))
        out_shape.append(jax.ShapeDtypeStruct((n_seq, 2, RW_HEADS, HD, HD), f32))
    for w in casts:
        per_expert = (n_seq * nc) // w.shape[0]
        spec = pl.BlockSpec((1, w.shape[1] // per_expert, w.shape[2]),
                            lambda b, s, per_expert=per_expert: ((b * nc + s) // per_expert, (b * nc + s) % per_expert, 0))
        in_specs.append(spec)
        args.append(w)
        out_specs.append(spec)
        out_shape.append(jax.ShapeDtypeStruct(w.shape, bf16))
    return pl.pallas_call(
        functools.partial(_scan_kernel, nc=nc, has_init=s0_bd is not None, emit_state=emit_state,
                          n_casts=len(casts)),
        grid=(n_seq, nc),
        in_specs=in_specs,
        out_specs=out_specs,
        out_shape=out_shape,
        scratch_shapes=[pltpu.VMEM(state_block[1:], f32), pltpu.VMEM((2, CHUNK, D), f32)],
        compiler_params=pltpu.CompilerParams(
            dimension_semantics=("parallel", "arbitrary"), vmem_limit_bytes=VMEM_LIMIT),
        name="rwkv_scan_prompt" if emit_state else "rwkv_scan_sample",
    )(*args)


def _scan_constants():
    t = jnp.arange(CHUNK)
    lower = (t[:, None] >= t[None, :])
    tri = jnp.stack([lower, lower.T]).astype(bf16)
    masks = []
    for d in range(2):
        strict = (t[:, None] > t[None, :]) if d == 0 else (t[:, None] < t[None, :])
        incl = lower if d == 0 else lower.T
        blocks = []
        for m in (strict, incl):
            bd = jnp.kron(jnp.eye(2, dtype=f32), m.astype(f32))
            blocks.append(jnp.concatenate([bd, bd], axis=1))
        masks.append(jnp.concatenate(blocks, axis=0))
    return tri, jnp.stack(masks)


def _state_to_blockdiag(s):
    n = s.shape[0]
    s = s.reshape(n, 2, RW_HEADS // 2, 2, HD, HD)
    z = jnp.zeros_like(s[:, :, :, 0])
    top = jnp.concatenate([s[:, :, :, 0], z], axis=-1)
    bot = jnp.concatenate([z, s[:, :, :, 1]], axis=-1)
    return jnp.concatenate([top, bot], axis=-2)


def _rope_tables():
    t = jnp.arange(SAMPLE_LEN)
    rows = (t // GRID_W).astype(f32)
    cols = (t % GRID_W).astype(f32)
    nf = HD // 4
    inv = 10000.0 ** (-jnp.arange(nf, dtype=f32) / nf)
    ang = jnp.concatenate([rows[:, None] * inv, cols[:, None] * inv], axis=-1)
    cos = jnp.repeat(jnp.cos(ang), 2, axis=-1)
    sin = jnp.repeat(jnp.sin(ang), 2, axis=-1) * jnp.tile(jnp.array([-1.0, 1.0], f32), HD // 2)
    return jnp.tile(cos, (1, 2)), jnp.tile(sin, (1, 2))


def kernel(x_prompt, x_sample, cache_diff_k, cache_diff_v, cache_na_k, cache_na_v, state_rwkv, c, c_ctx, w_ada, b_ada, g_mix, g_ffn, w_in, w_out, diff_q_g, diff_k_g, diff_lam_q1, diff_lam_k1, diff_lam_q2, diff_lam_k2, diff_subln_g, na_q_g, na_k_g, na_rpb, ffn_w1, ffn_w3, ffn_w2, rw_mu, rw_wr, rw_wk, rw_wv, rw_wo, rw_w0, rw_w1, rw_w2, rw_a0, rw_a1, rw_a2, rw_g1, rw_g2, rw_k_k, rw_k_a, rw_r_k, rw_ln_g, rw_ln_b, moe_router, moe_router_b, moe_w1, moe_w3, moe_w2):
    xp = x_prompt.reshape(NP, D)
    xs = x_sample.reshape(NS, D)
    cond8 = jnp.concatenate([c_ctx[None, :], c, jnp.zeros((3, D), f32)], axis=0)
    mod = _ada_table(cond8, w_ada, b_ada)
    gmat = jnp.kron(jnp.eye(4, dtype=f32), jnp.ones((HD, HD), f32)).astype(bf16)

    lam_init = 0.8 - 0.6 * math.exp(-0.3 * 0)
    ones_seg = jnp.ones((SEG,), f32)
    tile8 = lambda gvec: jnp.tile(gvec, SEG // HD)
    gains = jnp.stack([tile8(diff_q_g[0]), tile8(diff_k_g[0]), ones_seg,
                       tile8(na_q_g[0]), tile8(na_k_g[0]), ones_seg]).reshape(6, 1, SEG)
    cos_t, sin_t = _rope_tables()
    proj, dk_p, dv_p, nk_p, nv_p, ffn_w1_bf, ffn_w3_bf, ffn_w2_bf = _in_projection(
        xp, xs, mod[0][0], mod[0][1], g_mix[0][None, :], w_in[0].astype(bf16), gains, gmat, cos_t, sin_t,
        (ffn_w1[0], ffn_w3[0], ffn_w2[0]))
    lamp = jnp.stack([diff_lam_q1[0], diff_lam_k1[0], diff_lam_q2[0], diff_lam_k2[0]])
    subg = diff_subln_g[0][None, :]
    o_prompt = _prompt_attention(proj, lamp, subg, lam_init)
    o_diff = _latent_diff_attention(proj, cache_diff_k[:, 0].reshape(N_SAMPLE_SEQ, PAST, SEG),
                                    cache_diff_v[:, 0].reshape(N_SAMPLE_SEQ, PAST, SEG), lamp, subg, lam_init)
    o_na = _latent_na_attention(proj, cache_na_k[:, 0].reshape(N_SAMPLE_SEQ, PAST, SEG),
                                cache_na_v[:, 0].reshape(N_SAMPLE_SEQ, PAST, SEG), _rpb_table(na_rpb[0]))
    x = _out_projection(o_prompt, o_diff, o_na, w_out[0].astype(bf16), xp, xs, mod[0][2])
    x = _dense_ffn(x, mod[0][3], mod[0][4], mod[0][5], g_ffn[0][None, :],
                   ffn_w1_bf, ffn_w3_bf, ffn_w2_bf)

    pad_g = GATE_LORA_PAD - GATE_LORA
    g1 = jnp.pad(rw_g1[0], ((0, 0), (0, pad_g))).astype(bf16)
    g2 = jnp.pad(rw_g2[0], ((0, pad_g), (0, 0))).astype(bf16)
    w1cat = jnp.concatenate([rw_w1[0, 0], rw_w1[0, 1]], axis=1).astype(bf16)
    w2cat = jnp.concatenate([rw_w2[0, 0], rw_w2[0, 1]], axis=0).astype(bf16)
    a1cat = jnp.concatenate([rw_a1[0, 0], rw_a1[0, 1]], axis=1).astype(bf16)
    a2cat = jnp.concatenate([rw_a2[0, 0], rw_a2[0, 1]], axis=0).astype(bf16)
    k_a = rw_k_a[0][None, :]
    (r, v, kkn, bonus, gate_lora, lwf, lwb, af, ab, k) = _rwkv_projection(
        x, mod[1][0], mod[1][1], g_mix[1][None, :], rw_mu[0],
        rw_wr[0].astype(bf16), rw_wk[0].astype(bf16), rw_wv[0].astype(bf16), g1, g2,
        w1cat, w2cat, a1cat, a2cat, rw_w0[0], rw_a0[0],
        rw_k_k[0][None, :], k_a, rw_r_k[0].reshape(1, D), gmat)
    tri, mask = _scan_constants()
    streams_f = (r, v, kkn, lwf, af, k)
    streams_b = (r, v, kkn, lwb, ab, k)
    yf_p, yb_p, st_p, moe_w1_bf, moe_w3_bf = _rwkv_scan(
        streams_f, streams_b, k_a, tri, mask, None, (moe_w1[0], moe_w3[0]),
        n_seq=N_PROMPT_SEQ, seq_len=PROMPT_LEN, row0=0, emit_state=True)
    yf_s, yb_s, moe_w2_bf = _rwkv_scan(
        streams_f, streams_b, k_a, tri, mask, _state_to_blockdiag(state_rwkv[:, 0]), (moe_w2[0],),
        n_seq=N_SAMPLE_SEQ, seq_len=SAMPLE_LEN, row0=NP, emit_state=False)
    x = _rwkv_out_projection(yf_p, yb_p, yf_s, yb_s, bonus, gate_lora, rw_ln_g[0][None, :], rw_ln_b[0][None, :], gmat,
                             rw_wo[0].astype(bf16), x, mod[1][2])
    w_router = jnp.pad(moe_router[0], ((0, 0), (0, 128 - N_EXPERTS)))
    b_router = jnp.pad(moe_router_b[0], (0, 128 - N_EXPERTS))[None, :]
    h_bf, gates, rank, rank_t, cnt = _moe_route(x, mod[1][3], mod[1][4], g_ffn[1][None, :], w_router, b_router)
    y_p, y_s = _moe_experts(cnt[:, 0, :N_EXPERTS].reshape(-1), h_bf, rank, rank_t, gates, x, mod[1][5],
                            moe_w1_bf, moe_w3_bf, moe_w2_bf)

    new_dk = dk_p.reshape(N_PROMPT_SEQ, 1, PROMPT_LEN, DIFF_HEADS, 2 * HD)
    new_dv = dv_p.reshape(N_PROMPT_SEQ, 1, PROMPT_LEN, DIFF_HEADS, 2 * HD)
    new_nk = nk_p.reshape(N_PROMPT_SEQ, 1, PROMPT_LEN, NA_HEADS, HD)
    new_nv = nv_p.reshape(N_PROMPT_SEQ, 1, PROMPT_LEN, NA_HEADS, HD)
    new_state = st_p[:, None]
    return (y_p.reshape(N_PROMPT_SEQ, PROMPT_LEN, D), y_s.reshape(N_SAMPLE_SEQ, SAMPLE_LEN, D),
            new_dk, new_dv, new_nk, new_nv, new_state)
```

```python
import functools
import math

import jax
import jax.numpy as jnp
from jax import lax
from jax.experimental import pallas as pl
from jax.experimental.pallas import tpu as pltpu

f32 = jnp.float32
bf16 = jnp.bfloat16

D = 1024
N_PROMPT_SEQ, PROMPT_LEN = 32, 256
N_SAMPLE_SEQ, SAMPLE_LEN = 4, 2048
NP = N_PROMPT_SEQ * PROMPT_LEN
NS = N_SAMPLE_SEQ * SAMPLE_LEN
NT = NP + NS
PAST = 256
GRID_W = 64
GRID_R = SAMPLE_LEN // GRID_W
HD = 64
DIFF_HEADS = 4
NA_HEADS = 8
NA_KH = 8
NA_KW = 16
SEG = 512
IN_COLS = 6 * SEG
FFN_DIM = 2816
N_EXPERTS = 8
EXPERT_DIM = 3584
RW_HEADS = 16
LORA = 64
GATE_LORA = 160
GATE_LORA_PAD = 256
EPS = 1e-6
GN_EPS = 64e-5
NEG_BIG = -1e30
DIFF_HEADS_PER_STEP = 2
NA_ROWS_PER_STEP = 16
CHUNK = 64
VMEM_LIMIT = 56 * 1024 * 1024

ADA_COLS = 1536
IN_PROJ_ROWS = 1024
DIFF_Q_ROWS = 256
OUT_PROJ_ROWS = 512
FFN_ROWS = 512
FFN_COLS = FFN_DIM // 2
RWKV_PROJ_ROWS = 256
RWKV_OUT_ROWS = 512


def _cond_idx(i, tm):
    return jnp.maximum((i * tm) // SAMPLE_LEN - (NP // SAMPLE_LEN - 1), 0)


def _dot(a, b):
    return jnp.dot(a, b, preferred_element_type=f32)


def _dot_nt(a, b):
    return lax.dot_general(a, b, (((1,), (1,)), ((), ())), preferred_element_type=f32)


def _dot_tn(a, b):
    return lax.dot_general(a, b, (((0,), (0,)), ((), ())), preferred_element_type=f32)


def _bdot(a, b):
    return lax.dot_general(a, b, (((2,), (1,)), ((0,), (0,))), preferred_element_type=f32)


def _bdot_nt(a, b):
    return lax.dot_general(a, b, (((2,), (2,)), ((0,), (0,))), preferred_element_type=f32)


def _bdot_tn(a, b):
    return lax.dot_general(a, b, (((1,), (1,)), ((0,), (0,))), preferred_element_type=f32)


def _sigmoid(x):
    return 0.5 * jnp.tanh(0.5 * x) + 0.5


def _norm_mod(x, g, sh, sc):
    ms = jnp.mean(x * x, axis=-1, keepdims=True)
    return (x * lax.rsqrt(ms + EPS) * g) * (1.0 + sc) + sh


def _group_sum(x, gmat):
    xb = x.astype(bf16)
    cols = [_dot(xb[:, c * 256:(c + 1) * 256], gmat) for c in range(x.shape[1] // 256)]
    return cols[0] if len(cols) == 1 else jnp.concatenate(cols, axis=1)


def _softmax_parts(parts):
    m = parts[0].max(axis=-1, keepdims=True)
    for p in parts[1:]:
        m = jnp.maximum(m, p.max(axis=-1, keepdims=True))
    es = [jnp.exp(p - m) for p in parts]
    l = es[0].sum(axis=-1, keepdims=True)
    for e in es[1:]:
        l = l + e.sum(axis=-1, keepdims=True)
    inv = 1.0 / l
    return [e * inv for e in es]


def _lane_lo(n=128):
    return lax.broadcasted_iota(jnp.int32, (1, n), 1) < HD


def _stack_halves(q):
    lo = _lane_lo()
    return jnp.concatenate([jnp.where(lo, q, 0.0), jnp.where(lo, 0.0, q)], axis=0)


def _ada_kernel(cond_ref, w_ref, b_ref, o_ref):
    x = cond_ref[...]
    s = x * _sigmoid(x)
    o_ref[0] = _dot(s.astype(bf16), w_ref[0].astype(bf16)) + b_ref[0]


def _ada_table(cond8, w_ada, b_ada):
    depth = w_ada.shape[0]
    tn = ADA_COLS
    out = pl.pallas_call(
        _ada_kernel,
        grid=(depth, 6 * D // tn),
        in_specs=[
            pl.BlockSpec((8, D), lambda l, n: (0, 0)),
            pl.BlockSpec((1, D, tn), lambda l, n: (l, 0, n)),
            pl.BlockSpec((1, 1, tn), lambda l, n: (l, 0, n)),
        ],
        out_specs=pl.BlockSpec((1, 8, tn), lambda l, n: (l, 0, n)),
        out_shape=jax.ShapeDtypeStruct((depth, 8, 6 * D), f32),
        compiler_params=pltpu.CompilerParams(vmem_limit_bytes=VMEM_LIMIT),
        name="ada_table",
    )(cond8, w_ada, b_ada.reshape(depth, 1, 6 * D))
    out = out.reshape(depth, 8, 6, D)
    return [[out[l, :, k, :].reshape(8, 1, D) for k in range(6)] for l in range(depth)]


def _qk_norm(y, gain, gmat):
    ss = _group_sum(y * y, gmat) * (1.0 / HD)
    return y * lax.rsqrt(ss + EPS) * gain


def _rope(y, cos, sin):
    even = (lax.broadcasted_iota(jnp.int32, (1, 128), 1) % 2) == 0
    outs = []
    for c in range(y.shape[1] // 128):
        yc = y[:, c * 128:(c + 1) * 128]
        swapped = jnp.where(even, pltpu.roll(yc, 127, 1), pltpu.roll(yc, 1, 1))
        outs.append(yc * cos + swapped * sin)
    return jnp.concatenate(outs, axis=1)


def _inproj_kernel(xp_ref, xs_ref, sh_ref, sc_ref, g_ref, w_ref, gain_ref, gmat_ref, cos_ref, sin_ref,
                   f1_ref, f3_ref, f2_ref,
                   o_ref, dk_ref, dv_ref, nk_ref, nv_ref, f1_out, f3_out, f2_out, h_scr, *, tm):
    i = pl.program_id(0)
    j = pl.program_id(1)

    @pl.when(j == 0)
    def _():
        for src, dst in ((f1_ref, f1_out), (f3_ref, f3_out), (f2_ref, f2_out)):
            dst[...] = src[...].astype(bf16)

    def emit(val):
        o_ref[...] = val.astype(bf16)
        for seg, cache_ref in ((1, dk_ref), (2, dv_ref), (4, nk_ref), (5, nv_ref)):
            @pl.when(jnp.logical_and(j == seg, i < NP // tm))
            def _():
                cache_ref[...] = val

    for x_ref, active in ((xp_ref, i < NP // tm), (xs_ref, i >= NP // tm)):
        @pl.when(jnp.logical_and(j == 0, active))
        def _():
            h_scr[...] = _norm_mod(x_ref[...], g_ref[...], sh_ref[0], sc_ref[0]).astype(bf16)

    project = lambda: _dot(h_scr[...], w_ref[...])
    is_norm = jnp.logical_and(j != 2, j != 5)
    is_rope = jnp.logical_and(j < 2, i >= NP // tm)

    @pl.when(jnp.logical_not(is_norm))
    def _():
        emit(project())

    @pl.when(jnp.logical_and(is_norm, jnp.logical_not(is_rope)))
    def _():
        emit(_qk_norm(project(), gain_ref[0], gmat_ref[...]))

    @pl.when(is_rope)
    def _():
        o_ref[...] = _rope(_qk_norm(project(), gain_ref[0], gmat_ref[...]),
                           cos_ref[...], sin_ref[...]).astype(bf16)


def _in_projection(xp, xs, sh, sc, g, w_bf, gains, gmat, cos_t, sin_t, ffn_weights):
    tm = IN_PROJ_ROWS
    n_tiles = NT // tm
    n_prompt_tiles = NP // tm
    tiles_per_seq = SAMPLE_LEN // tm
    cidx = functools.partial(_cond_idx, tm=tm)
    rope_idx = lambda i, j: (jnp.maximum(i - n_prompt_tiles, 0) % tiles_per_seq, 0)
    cast_specs = [pl.BlockSpec((w.shape[0] // n_tiles, w.shape[1]), lambda i, j: (i, 0)) for w in ffn_weights]
    return pl.pallas_call(
        functools.partial(_inproj_kernel, tm=tm),
        grid=(NT // tm, IN_COLS // SEG),
        in_specs=[
            pl.BlockSpec((tm, D), lambda i, j: (jnp.minimum(i, n_prompt_tiles - 1), 0)),
            pl.BlockSpec((tm, D), lambda i, j: (jnp.maximum(i - n_prompt_tiles, 0), 0)),
            pl.BlockSpec((1, 1, D), lambda i, j: (cidx(i), 0, 0)),
            pl.BlockSpec((1, 1, D), lambda i, j: (cidx(i), 0, 0)),
            pl.BlockSpec((1, D), lambda i, j: (0, 0)),
            pl.BlockSpec((D, SEG), lambda i, j: (0, j)),
            pl.BlockSpec((1, 1, SEG), lambda i, j: (j, 0, 0)),
            pl.BlockSpec((256, 256), lambda i, j: (0, 0)),
            pl.BlockSpec((tm, 128), rope_idx),
            pl.BlockSpec((tm, 128), rope_idx),
        ] + cast_specs,
        out_specs=[pl.BlockSpec((tm, SEG), lambda i, j: (i, j))] + [
            pl.BlockSpec((tm, SEG), lambda i, j: (jnp.minimum(i, n_prompt_tiles - 1), 0))] * 4 + cast_specs,
        out_shape=[jax.ShapeDtypeStruct((NT, IN_COLS), bf16)] + [jax.ShapeDtypeStruct((NP, SEG), f32)] * 4
        + [jax.ShapeDtypeStruct(w.shape, bf16) for w in ffn_weights],
        scratch_shapes=[pltpu.VMEM((tm, D), bf16)],
        compiler_params=pltpu.CompilerParams(
            dimension_semantics=("arbitrary", "arbitrary"), vmem_limit_bytes=VMEM_LIMIT),
        name="in_projection",
    )(xp, xs, sh, sc, g, w_bf, gains, gmat, cos_t, sin_t, *ffn_weights)


def _lambda_value(lamp_ref, lam_init):
    lp = lamp_ref[...]
    e1 = jnp.exp(jnp.sum(lp[0:1] * lp[1:2], axis=-1, keepdims=True))
    e2 = jnp.exp(jnp.sum(lp[2:3] * lp[3:4], axis=-1, keepdims=True))
    return e1 - e2 + lam_init


def _sub_ln(o, subg, lam_init):
    ms = jnp.mean(o * o, axis=-1, keepdims=True)
    return o * lax.rsqrt(ms + EPS) * subg * (1.0 - lam_init)


def _prompt_attn_kernel(p_ref, lamp_ref, subg_ref, o_ref, *, lam_init):
    lam = _lambda_value(lamp_ref, lam_init)
    t = PROMPT_LEN
    lo = _lane_lo()
    scale = HD ** -0.5
    def stacked(seg, n):
        return [p_ref[:, seg * SEG + g * 128:seg * SEG + (g + 1) * 128] for g in range(n)]

    def scores(seg_q, seg_k, n):
        q = jnp.stack([(_stack_halves(x) * scale).astype(bf16) for x in stacked(seg_q, n)])
        k = jnp.stack([x.astype(bf16) for x in stacked(seg_k, n)])
        (p,) = _softmax_parts([_bdot_nt(q, k)])
        return p

    p = scores(0, 1, DIFF_HEADS)
    v = jnp.stack([x.astype(bf16) for x in stacked(2, DIFF_HEADS)])
    o = _bdot((p[:, :t] - lam * p[:, t:]).astype(bf16), v)
    o = _sub_ln(o, subg_ref[...], lam_init).astype(bf16)
    for h in range(DIFF_HEADS):
        o_ref[:, h * 128:(h + 1) * 128] = o[h]
    p = scores(3, 4, NA_HEADS // 2)
    v = jnp.stack([x.astype(bf16) for x in stacked(5, NA_HEADS // 2)])
    o = _bdot(p.astype(bf16), v)
    o = jnp.where(lo, o[:, :t], o[:, t:]).astype(bf16)
    for hp in range(NA_HEADS // 2):
        o_ref[:, SEG + hp * 128:SEG + (hp + 1) * 128] = o[hp]


def _prompt_attention(proj, lamp, subg, lam_init):
    return pl.pallas_call(
        functools.partial(_prompt_attn_kernel, lam_init=lam_init),
        grid=(N_PROMPT_SEQ,),
        in_specs=[
            pl.BlockSpec((PROMPT_LEN, IN_COLS), lambda b: (b, 0)),
            pl.BlockSpec((4, HD), lambda b: (0, 0)),
            pl.BlockSpec((1, 128), lambda b: (0, 0)),
        ],
        out_specs=pl.BlockSpec((PROMPT_LEN, D), lambda b: (b, 0)),
        out_shape=jax.ShapeDtypeStruct((NP, D), bf16),
        compiler_params=pltpu.CompilerParams(
            dimension_semantics=("parallel",), vmem_limit_bytes=VMEM_LIMIT),
        name="prompt_attention",
    )(proj, lamp, subg)


def _latent_diff_kernel(q_ref, kn_ref, vn_ref, kc_ref, vc_ref, lamp_ref, subg_ref, o_ref,
                        *, lam_init, tq):
    lam = _lambda_value(lamp_ref, lam_init)
    scale = HD ** -0.5
    for hh in range(DIFF_HEADS_PER_STEP):
        ln = slice(hh * 128, (hh + 1) * 128)
        qq = (_stack_halves(q_ref[:, ln]) * scale).astype(bf16)
        s_c = _dot_nt(qq, kc_ref[0, :, ln].astype(bf16))
        s_n = _dot_nt(qq, kn_ref[:, ln].astype(bf16))
        p_c, p_n = _softmax_parts([s_c, s_n])
        pd_c = p_c[:tq] - lam * p_c[tq:]
        pd_n = p_n[:tq] - lam * p_n[tq:]
        o = (_dot(pd_c.astype(bf16), vc_ref[0, :, ln].astype(bf16))
             + _dot(pd_n.astype(bf16), vn_ref[:, ln].astype(bf16)))
        o_ref[:, ln] = _sub_ln(o, subg_ref[...], lam_init).astype(bf16)


def _latent_diff_attention(proj, cache_k, cache_v, lamp, subg, lam_init):
    tq = DIFF_Q_ROWS
    nqb = SAMPLE_LEN // tq
    q0 = NP // tq
    s0 = NP // SAMPLE_LEN
    hw = 128 * DIFF_HEADS_PER_STEP
    kcol = SEG // hw
    return pl.pallas_call(
        functools.partial(_latent_diff_kernel, lam_init=lam_init, tq=tq),
        grid=(N_SAMPLE_SEQ, DIFF_HEADS // DIFF_HEADS_PER_STEP, nqb),
        in_specs=[
            pl.BlockSpec((tq, hw), lambda b, h, q: (q0 + b * nqb + q, h)),
            pl.BlockSpec((SAMPLE_LEN, hw), lambda b, h, q: (s0 + b, kcol + h)),
            pl.BlockSpec((SAMPLE_LEN, hw), lambda b, h, q: (s0 + b, 2 * kcol + h)),
            pl.BlockSpec((1, PAST, hw), lambda b, h, q: (b, 0, h)),
            pl.BlockSpec((1, PAST, hw), lambda b, h, q: (b, 0, h)),
            pl.BlockSpec((4, HD), lambda b, h, q: (0, 0)),
            pl.BlockSpec((1, 128), lambda b, h, q: (0, 0)),
        ],
        out_specs=pl.BlockSpec((tq, hw), lambda b, h, q: (b * nqb + q, h)),
        out_shape=jax.ShapeDtypeStruct((NS, SEG), bf16),
        compiler_params=pltpu.CompilerParams(
            dimension_semantics=("parallel", "parallel", "arbitrary"), vmem_limit_bytes=VMEM_LIMIT),
        name="latent_diff_attention",
    )(proj, proj, proj, cache_k, cache_v, lamp, subg)


def _rpb_table_kernel(rpb_ref, o_ref):
    h = pl.program_id(0)
    wq = lax.broadcasted_iota(jnp.int32, (GRID_W, GRID_W), 0)
    wk = lax.broadcasted_iota(jnp.int32, (GRID_W, GRID_W), 1)
    col_start = jnp.clip(wq - NA_KW // 2, 0, GRID_W - NA_KW)
    col_in = jnp.logical_and(wk >= col_start, wk < col_start + NA_KW)
    col_off = jnp.clip(wk - wq, -(NA_KW - 1), NA_KW - 1) + (NA_KW - 1)
    n_dr = 2 * NA_KH - 1
    n_dc = 2 * NA_KW - 1
    for dr in range(n_dr):
        t = jnp.zeros((GRID_W, GRID_W), f32)
        for c in range(n_dc):
            t = jnp.where(col_off == c, rpb_ref[h * (n_dr * n_dc) + dr * n_dc + c], t)
        o_ref[0, dr] = jnp.where(col_in, t, NEG_BIG)


def _rpb_table(rpb):
    n_dr = 2 * NA_KH - 1
    tcol = pl.pallas_call(
        _rpb_table_kernel,
        grid=(NA_HEADS,),
        in_specs=[pl.BlockSpec(memory_space=pltpu.SMEM)],
        out_specs=pl.BlockSpec((1, n_dr, GRID_W, GRID_W), lambda h: (h, 0, 0, 0)),
        out_shape=jax.ShapeDtypeStruct((NA_HEADS, n_dr, GRID_W, GRID_W), f32),
        name="rpb_table",
    )(rpb.reshape(-1))
    return jnp.stack(
        [jnp.concatenate([tcol[:, j - s + NA_KH - 1] for j in range(NA_KH)], axis=-1) for s in range(NA_KH)],
        axis=1)


def _latent_na_kernel(q_ref, k_ref, v_ref, kc_ref, vc_ref, bias_ref, o_ref):
    scale = HD ** -0.5
    lo = _lane_lo()
    kc = kc_ref[0].astype(bf16)
    vc = vc_ref[0].astype(bf16)
    win = NA_KH * GRID_W

    nr = NA_ROWS_PER_STEP
    kcb = jnp.broadcast_to(kc[None], (nr,) + kc.shape)
    vcb = jnp.broadcast_to(vc[None], (nr,) + vc.shape)

    def rows(g, carry):
        qs, kws, vws, biases = [], [], [], []
        for t in range(nr):
            r = g * nr + t
            rs = jnp.clip(r - NA_KH // 2, 0, GRID_R - NA_KH)
            sidx = r - rs
            q = q_ref[pl.ds(pl.multiple_of(r * GRID_W, GRID_W), GRID_W), :]
            qs.append((_stack_halves(q) * scale).astype(bf16))
            k0 = pl.multiple_of(rs * GRID_W, GRID_W)
            kws.append(k_ref[pl.ds(k0, win), :].astype(bf16))
            vws.append(v_ref[pl.ds(k0, win), :].astype(bf16))
            biases.append(jnp.concatenate([bias_ref[0, sidx], bias_ref[1, sidx]], axis=0))
        qq = jnp.stack(qs)
        s_loc = _bdot_nt(qq, jnp.stack(kws)) + jnp.stack(biases)
        s_ctx = _bdot_nt(qq, kcb)
        p_loc, p_ctx = _softmax_parts([s_loc, s_ctx])
        o = _bdot(p_loc.astype(bf16), jnp.stack(vws)) + _bdot(p_ctx.astype(bf16), vcb)
        o = jnp.where(lo, o[:, :GRID_W], o[:, GRID_W:]).astype(bf16)
        row0 = pl.multiple_of(g * (nr * GRID_W), nr * GRID_W)
        o_ref[pl.ds(row0, nr * GRID_W), :] = o.reshape(nr * GRID_W, 128)
        return carry

    lax.fori_loop(0, GRID_R // nr, rows, 0)


def _latent_na_attention(proj, cache_k, cache_v, bias):
    s0 = NP // SAMPLE_LEN
    return pl.pallas_call(
        _latent_na_kernel,
        grid=(N_SAMPLE_SEQ, NA_HEADS // 2),
        in_specs=[
            pl.BlockSpec((SAMPLE_LEN, 128), lambda b, h: (s0 + b, 12 + h)),
            pl.BlockSpec((SAMPLE_LEN, 128), lambda b, h: (s0 + b, 16 + h)),
            pl.BlockSpec((SAMPLE_LEN, 128), lambda b, h: (s0 + b, 20 + h)),
            pl.BlockSpec((1, PAST, 128), lambda b, h: (b, 0, h)),
            pl.BlockSpec((1, PAST, 128), lambda b, h: (b, 0, h)),
            pl.BlockSpec((2, NA_KH, GRID_W, NA_KH * GRID_W), lambda b, h: (h, 0, 0, 0)),
        ],
        out_specs=pl.BlockSpec((SAMPLE_LEN, 128), lambda b, h: (b, h)),
        out_shape=jax.ShapeDtypeStruct((NS, SEG), bf16),
        compiler_params=pltpu.CompilerParams(
            dimension_semantics=("parallel", "parallel"), vmem_limit_bytes=VMEM_LIMIT),
        name="latent_na_attention",
    )(proj, proj, proj, cache_k, cache_v, bias)


def _out_proj_kernel(ap_ref, ad_ref, an_ref, w_ref, xp_ref, xs_ref, gt_ref, o_ref, *, tm):
    i = pl.program_id(0)

    @pl.when(i < NP // tm)
    def _():
        o_ref[...] = xp_ref[...] + gt_ref[0] * _dot(ap_ref[...], w_ref[...])

    @pl.when(i >= NP // tm)
    def _():
        mixed = _dot(ad_ref[...], w_ref[:SEG, :]) + _dot(an_ref[...], w_ref[SEG:, :])
        o_ref[...] = xs_ref[...] + gt_ref[0] * mixed


def _out_projection(a_prompt, a_diff, a_na, w_bf, xp, xs, gate):
    tm = OUT_PROJ_ROWS
    n_p = NP // tm
    cidx = functools.partial(_cond_idx, tm=tm)
    prompt_rows = lambda i: (jnp.minimum(i, n_p - 1), 0)
    sample_rows = lambda i: (jnp.maximum(i - n_p, 0), 0)
    return pl.pallas_call(
        functools.partial(_out_proj_kernel, tm=tm),
        grid=(NT // tm,),
        in_specs=[
            pl.BlockSpec((tm, D), prompt_rows),
            pl.BlockSpec((tm, SEG), sample_rows),
            pl.BlockSpec((tm, SEG), sample_rows),
            pl.BlockSpec((D, D), lambda i: (0, 0)),
            pl.BlockSpec((tm, D), prompt_rows),
            pl.BlockSpec((tm, D), sample_rows),
            pl.BlockSpec((1, 1, D), lambda i: (cidx(i), 0, 0)),
        ],
        out_specs=pl.BlockSpec((tm, D), lambda i: (i, 0)),
        out_shape=jax.ShapeDtypeStruct((NT, D), f32),
        compiler_params=pltpu.CompilerParams(
            dimension_semantics=("arbitrary",), vmem_limit_bytes=VMEM_LIMIT),
        name="out_projection",
    )(a_prompt, a_diff, a_na, w_bf, xp, xs, gate)


def _rwkv_out_kernel(yfp_ref, ybp_ref, yfs_ref, ybs_ref, bonus_ref, gate_ref, lng_ref, lnb_ref,
                     gmat_ref, w_ref, x_ref, gt_ref, o_ref, *, tm):
    i = pl.program_id(0)

    def finish(y):
        gmat = gmat_ref[...]
        mu = _group_sum(y, gmat) * (1.0 / HD)
        yc = y - mu
        var = _group_sum(yc * yc, gmat) * (1.0 / HD)
        z = yc * lax.rsqrt(var + GN_EPS) * lng_ref[...] + lnb_ref[...] + bonus_ref[...].astype(f32)
        z = (z * gate_ref[...].astype(f32)).astype(bf16)
        o_ref[...] = x_ref[...] + gt_ref[0] * _dot(z, w_ref[...])

    @pl.when(i < NP // tm)
    def _():
        finish(yfp_ref[...].astype(f32) + ybp_ref[...].astype(f32))

    @pl.when(i >= NP // tm)
    def _():
        finish(yfs_ref[...].astype(f32) + ybs_ref[...].astype(f32))


def _rwkv_out_projection(yf_p, yb_p, yf_s, yb_s, bonus, gate_lora, ln_g, ln_b, gmat, w_bf, x, gate):
    tm = RWKV_OUT_ROWS
    n_p = NP // tm
    cidx = functools.partial(_cond_idx, tm=tm)
    row = pl.BlockSpec((tm, D), lambda i: (i, 0))
    prompt_row = pl.BlockSpec((tm, D), lambda i: (jnp.minimum(i, n_p - 1), 0))
    sample_row = pl.BlockSpec((tm, D), lambda i: (jnp.maximum(i - n_p, 0), 0))
    vec = pl.BlockSpec((1, D), lambda i: (0, 0))
    return pl.pallas_call(
        functools.partial(_rwkv_out_kernel, tm=tm),
        grid=(NT // tm,),
        in_specs=[prompt_row, prompt_row, sample_row, sample_row, row, row, vec, vec,
                  pl.BlockSpec((256, 256), lambda i: (0, 0)),
                  pl.BlockSpec((D, D), lambda i: (0, 0)),
                  row,
                  pl.BlockSpec((1, 1, D), lambda i: (cidx(i), 0, 0))],
        out_specs=row,
        out_shape=jax.ShapeDtypeStruct((NT, D), f32),
        compiler_params=pltpu.CompilerParams(
            dimension_semantics=("arbitrary",), vmem_limit_bytes=VMEM_LIMIT),
        name="rwkv_out_projection",
    )(yf_p, yb_p, yf_s, yb_s, bonus, gate_lora, ln_g, ln_b, gmat, w_bf, x, gate)


def _swiglu_hidden(xb, w1, w3):
    a = _dot(xb, w1)
    return ((a * _sigmoid(a)) * _dot(xb, w3)).astype(bf16)


def _ffn_kernel(x_ref, sh_ref, sc_ref, gt_ref, g_ref, w1_ref, w3_ref, w2_ref, o_ref, h_scr, acc_scr):
    f = pl.program_id(1)

    @pl.when(f == 0)
    def _():
        h_scr[...] = _norm_mod(x_ref[...], g_ref[...], sh_ref[0], sc_ref[0]).astype(bf16)
        acc_scr[...] = jnp.zeros_like(acc_scr)

    acc_scr[...] += _dot(_swiglu_hidden(h_scr[...], w1_ref[...], w3_ref[...]), w2_ref[...])

    @pl.when(f == pl.num_programs(1) - 1)
    def _():
        o_ref[...] = x_ref[...] + gt_ref[0] * acc_scr[...]


def _dense_ffn(x, sh, sc, gt, g, w1, w3, w2):
    tm = FFN_ROWS
    tf = FFN_COLS
    cidx = functools.partial(_cond_idx, tm=tm)
    mod = pl.BlockSpec((1, 1, D), lambda i, f: (cidx(i), 0, 0))
    return pl.pallas_call(
        _ffn_kernel,
        grid=(NT // tm, FFN_DIM // tf),
        in_specs=[
            pl.BlockSpec((tm, D), lambda i, f: (i, 0)),
            mod, mod, mod,
            pl.BlockSpec((1, D), lambda i, f: (0, 0)),
            pl.BlockSpec((D, tf), lambda i, f: (0, f)),
            pl.BlockSpec((D, tf), lambda i, f: (0, f)),
            pl.BlockSpec((tf, D), lambda i, f: (f, 0)),
        ],
        out_specs=pl.BlockSpec((tm, D), lambda i, f: (i, 0)),
        out_shape=jax.ShapeDtypeStruct((NT, D), f32),
        scratch_shapes=[pltpu.VMEM((tm, D), bf16), pltpu.VMEM((tm, D), f32)],
        compiler_params=pltpu.CompilerParams(
            dimension_semantics=("parallel", "arbitrary"), vmem_limit_bytes=VMEM_LIMIT),
        name="ffn_mixer",
    )(x, sh, sc, gt, g, w1, w3, w2)


MOE_TILE = 1024
MOE_MAIN = 288
MOE_EXTRA = 128
MOE_ROWS = MOE_MAIN + -(-(MOE_TILE - MOE_MAIN) // MOE_EXTRA) * MOE_EXTRA


RANK_SEL = 4096


def _router_kernel(x_ref, sh_ref, sc_ref, g_ref, wr_ref, br_ref, tri_ref, triu_ref,
                   h_out, gates_out, rank_out, rank_t_out, cnt_out):
    h = _norm_mod(x_ref[...], g_ref[...], sh_ref[0], sc_ref[0])
    h_out[...] = h.astype(bf16)
    w = wr_ref[...]
    h_hi, w_hi = h.astype(bf16), w.astype(bf16)
    h_lo = (h - h_hi.astype(f32)).astype(bf16)
    w_lo = (w - w_hi.astype(f32)).astype(bf16)
    hw = _dot(h_hi, jnp.concatenate([w_hi, w_lo], axis=1))
    logits = hw[:, :128] + (hw[:, 128:] + _dot(h_lo, w_hi)) + br_ref[...]
    lane = lax.broadcasted_iota(jnp.int32, logits.shape, 1)
    logits = jnp.where(lane < N_EXPERTS, logits, -jnp.inf)
    m1 = logits.max(axis=-1, keepdims=True)
    i1 = jnp.min(jnp.where(logits == m1, lane, 128), axis=-1, keepdims=True)
    rest = jnp.where(lane == i1, -jnp.inf, logits)
    m2 = rest.max(axis=-1, keepdims=True)
    i2 = jnp.min(jnp.where(rest == m2, lane, 128), axis=-1, keepdims=True)
    e2 = jnp.exp(m2 - m1)
    den = 1.0 / (1.0 + e2)
    gates_out[...] = jnp.where(lane == i1, den, jnp.where(lane == i2, e2 * den, 0.0))
    sel = jnp.logical_or(lane == i1, lane == i2)
    self32 = sel.astype(f32)
    selb = self32.astype(bf16)
    rank_out[...] = jnp.where(sel, _dot(tri_ref[...], selb), -1.0)
    rank_t_out[...] = _dot_tn(selb, triu_ref[...])
    cnt_out[0] = jnp.sum(self32, axis=0, keepdims=True).astype(jnp.int32)


def _moe_route(x, sh, sc, g, w_router, b_router):
    tm = MOE_TILE
    cidx = functools.partial(_cond_idx, tm=tm)
    t = jnp.arange(tm)
    tri = (t[:, None] > t[None, :]).astype(bf16)
    triu = jnp.where(t[:, None] == t[None, :], float(RANK_SEL), tri.T.astype(f32)).astype(bf16)
    mod = pl.BlockSpec((1, 1, D), lambda i: (cidx(i), 0, 0))
    return pl.pallas_call(
        _router_kernel,
        grid=(NT // tm,),
        in_specs=[
            pl.BlockSpec((tm, D), lambda i: (i, 0)),
            mod, mod,
            pl.BlockSpec((1, D), lambda i: (0, 0)),
            pl.BlockSpec((D, 128), lambda i: (0, 0)),
            pl.BlockSpec((1, 128), lambda i: (0, 0)),
            pl.BlockSpec((tm, tm), lambda i: (0, 0)),
            pl.BlockSpec((tm, tm), lambda i: (0, 0)),
        ],
        out_specs=[
            pl.BlockSpec((tm, D), lambda i: (i, 0)),
            pl.BlockSpec((tm, 128), lambda i: (i, 0)),
            pl.BlockSpec((tm, 128), lambda i: (i, 0)),
            pl.BlockSpec((128, tm), lambda i: (0, i)),
            pl.BlockSpec((1, 1, 128), lambda i: (i, 0, 0)),
        ],
        out_shape=[
            jax.ShapeDtypeStruct((NT, D), bf16),
            jax.ShapeDtypeStruct((NT, 128), f32),
            jax.ShapeDtypeStruct((NT, 128), f32),
            jax.ShapeDtypeStruct((128, NT), f32),
            jax.ShapeDtypeStruct((NT // tm, 1, 128), jnp.int32),
        ],
        compiler_params=pltpu.CompilerParams(
            dimension_semantics=("parallel",), vmem_limit_bytes=VMEM_LIMIT),
        name="moe_router",
    )(x, sh, sc, g, w_router, b_router, tri, triu)


def _moe_kernel(cnt_ref, h_ref, rank_ref, rank_t_ref, gates_ref, x_ref, gt_ref, w1_ref, w3_ref, w2_ref,
                op_ref, os_ref, o_ref, xc_scr, acc_scr, rcol_scr, gcol_scr):
    i = pl.program_id(0)
    e = pl.program_id(1)
    f = pl.program_id(2)
    cnt = cnt_ref[i * N_EXPERTS + e]
    n_extra = jnp.maximum(cnt - MOE_MAIN + MOE_EXTRA - 1, 0) // MOE_EXTRA

    def for_each_block(fn):
        fn(0, MOE_MAIN)

        def body(b, carry):
            fn(pl.multiple_of(MOE_MAIN + b * MOE_EXTRA, 32), MOE_EXTRA)
            return carry

        lax.fori_loop(0, n_extra, body, 0)

    @pl.when(jnp.logical_and(e == 0, f == 0))
    def _():
        o_ref[...] = x_ref[...]

    @pl.when(f == 0)
    def _():
        lane = lax.broadcasted_iota(jnp.int32, (MOE_TILE, 128), 1)
        mine = lane == e
        rcol_scr[...] = jnp.sum(jnp.where(mine, rank_ref[...], 0.0), axis=1, keepdims=True).astype(jnp.int32)
        gcol_scr[...] = jnp.sum(jnp.where(mine, gates_ref[...], 0.0), axis=1, keepdims=True)

        rrow = rank_t_ref[pl.ds(e, 1), :].astype(jnp.int32)

        def gather(slot0, nrows):
            slot = lax.broadcasted_iota(jnp.int32, (nrows, MOE_TILE), 0) + (slot0 + RANK_SEL)
            xc_scr[pl.ds(slot0, nrows), :] = _dot((rrow == slot).astype(bf16), h_ref[...]).astype(bf16)

        for_each_block(gather)

    def expert(slot0, nrows):
        rows = pl.ds(slot0, nrows)
        part = _dot(_swiglu_hidden(xc_scr[rows, :], w1_ref[0], w3_ref[0]), w2_ref[0])

        @pl.when(f == 0)
        def _():
            acc_scr[rows, :] = part

        @pl.when(f != 0)
        def _():
            acc_scr[rows, :] += part

    for_each_block(expert)

    @pl.when(f == pl.num_programs(2) - 1)
    def _():
        def scatter(slot0, nrows):
            out = acc_scr[pl.ds(slot0, nrows), :].astype(bf16)
            for t0 in range(0, MOE_TILE, 256):
                rows = slice(t0, t0 + 256)
                slot = lax.broadcasted_iota(jnp.int32, (256, nrows), 1) + slot0
                pt = (rcol_scr[rows, :] == slot).astype(bf16)
                o_ref[rows, :] += (gcol_scr[rows, :] * gt_ref[0]) * _dot(pt, out)

        for_each_block(scatter)

    for dst, mine in ((op_ref, i < NP // MOE_TILE), (os_ref, i >= NP // MOE_TILE)):
        @pl.when(jnp.logical_and(mine, jnp.logical_and(e == pl.num_programs(1) - 1, f == pl.num_programs(2) - 1)))
        def _():
            dst[...] = o_ref[...]


def _moe_experts(cnt, h_bf, rank, rank_t, gates, x, gt, w1, w3, w2):
    tm = MOE_TILE
    tf = EXPERT_DIM // 2
    n_p = NP // tm
    cidx = functools.partial(_cond_idx, tm=tm)
    once = pl.Buffered(1)
    grid_spec = pltpu.PrefetchScalarGridSpec(
        num_scalar_prefetch=1,
        grid=(NT // tm, N_EXPERTS, EXPERT_DIM // tf),
        in_specs=[
            pl.BlockSpec((tm, D), lambda i, e, f, c: (i, 0), pipeline_mode=once),
            pl.BlockSpec((tm, 128), lambda i, e, f, c: (i, 0), pipeline_mode=once),
            pl.BlockSpec((128, tm), lambda i, e, f, c: (0, i), pipeline_mode=once),
            pl.BlockSpec((tm, 128), lambda i, e, f, c: (i, 0), pipeline_mode=once),
            pl.BlockSpec((tm, D), lambda i, e, f, c: (i, 0), pipeline_mode=once),
            pl.BlockSpec((1, 1, D), lambda i, e, f, c: (cidx(i), 0, 0)),
            pl.BlockSpec((1, D, tf), lambda i, e, f, c: (e, 0, f)),
            pl.BlockSpec((1, D, tf), lambda i, e, f, c: (e, 0, f)),
            pl.BlockSpec((1, tf, D), lambda i, e, f, c: (e, f, 0)),
        ],
        out_specs=[
            pl.BlockSpec((tm, D), lambda i, e, f, c: (jnp.minimum(i, n_p - 1), 0), pipeline_mode=once),
            pl.BlockSpec((tm, D), lambda i, e, f, c: (jnp.maximum(i - n_p, 0), 0), pipeline_mode=once),
        ],
        scratch_shapes=[pltpu.VMEM((tm, D), f32),
                        pltpu.VMEM((MOE_ROWS, D), bf16), pltpu.VMEM((MOE_ROWS, D), f32),
                        pltpu.VMEM((tm, 1), jnp.int32), pltpu.VMEM((tm, 1), f32)],
    )
    return pl.pallas_call(
        _moe_kernel,
        grid_spec=grid_spec,
        out_shape=[jax.ShapeDtypeStruct((NP, D), f32), jax.ShapeDtypeStruct((NS, D), f32)],
        compiler_params=pltpu.CompilerParams(
            dimension_semantics=("arbitrary", "arbitrary", "arbitrary"), vmem_limit_bytes=VMEM_LIMIT),
        name="moe_experts",
    )(cnt, h_bf, rank, rank_t, gates, x, gt, w1, w3, w2)


def _in_context_key(k, a, k_a):
    return k * (1.0 + (a - 1.0) * k_a)


def _rwkv_proj_kernel(x_ref, xp_ref, xn_ref, sh_ref, sc_ref, g_ref, mu_ref,
                      wr_ref, wk_ref, wv_ref, g1_ref, g2_ref, w1_ref, w2_ref, a1_ref, a2_ref,
                      w0_ref, a0_ref, kk_ref, ka_ref, rk_ref, gmat_ref,
                      r_out, v_out, kkn_out, bonus_out, gate_out,
                      lwf_out, lwb_out, af_out, ab_out, k_out, *, tm):
    i = pl.program_id(0)
    g, sh, sc = g_ref[...], sh_ref[0], sc_ref[0]
    h = _norm_mod(x_ref[...], g, sh, sc)
    n_prompt_tiles = NP // tm
    tiles_per_seq = SAMPLE_LEN // tm
    pos = (i - n_prompt_tiles) % tiles_per_seq
    has_prev = jnp.logical_and(i >= n_prompt_tiles, pos != 0)
    has_next = jnp.logical_and(i >= n_prompt_tiles, pos != tiles_per_seq - 1)
    h_before = jnp.where(has_prev, _norm_mod(xp_ref[...], g, sh, sc)[7:8], 0.0)
    h_after = jnp.where(has_next, _norm_mod(xn_ref[...], g, sh, sc)[0:1], 0.0)
    rowi = lax.broadcasted_iota(jnp.int32, (tm, 1), 0)
    h_prev = jnp.where(rowi == 0, h_before, pltpu.roll(h, 1, 0))
    h_next = jnp.where(rowi == tm - 1, h_after, pltpu.roll(h, tm - 1, 0))
    xx = 0.5 * (h_prev + h_next) - h
    mix = lambda n: (h + xx * mu_ref[n:n + 1]).astype(bf16)

    r = _dot(mix(0), wr_ref[...])
    k = _dot(mix(2), wk_ref[...])
    v = _dot(mix(3), wv_ref[...])
    gate_out[...] = _dot(_sigmoid(_dot(mix(5), g1_ref[...])).astype(bf16), g2_ref[...]).astype(bf16)

    lo = _lane_lo()
    tw = jnp.tanh(_dot(mix(1), w1_ref[...]))
    ta = _dot(mix(4), a1_ref[...])
    gmat = gmat_ref[...]
    kk = k * kk_ref[...]
    kkn_out[...] = kk * lax.rsqrt(_group_sum(kk * kk, gmat) + 1e-12)
    r_out[...] = r
    v_out[...] = v
    k_out[...] = k
    kd_sum = jnp.zeros_like(k)
    for d, (lw_out, a_out) in enumerate(((lwf_out, af_out), (lwb_out, ab_out))):
        keep = lo if d == 0 else jnp.logical_not(lo)
        zw = w0_ref[d:d + 1] + _dot(jnp.where(keep, tw, 0.0).astype(bf16), w2_ref[...])
        lw_out[...] = -math.exp(-0.5) * _sigmoid(zw)
        a = _sigmoid(a0_ref[d:d + 1] + _dot(jnp.where(keep, ta, 0.0).astype(bf16), a2_ref[...]))
        a_out[...] = a
        kd_sum = kd_sum + _in_context_key(k, a, ka_ref[...])
    bonus_out[...] = (_group_sum(r * kd_sum * rk_ref[...], gmat) * v).astype(bf16)


def _rwkv_projection(x, sh, sc, g, mu, wr, wk, wv, g1, g2, w1, w2, a1, a2, w0, a0, k_k, k_a, r_k, gmat):
    tm = RWKV_PROJ_ROWS
    cidx = functools.partial(_cond_idx, tm=tm)
    hb = tm // 8
    n8 = NT // 8
    full = lambda shape: pl.BlockSpec(shape, lambda i: tuple(0 for _ in shape))
    row = pl.BlockSpec((tm, D), lambda i: (i, 0))
    return pl.pallas_call(
        functools.partial(_rwkv_proj_kernel, tm=tm),
        grid=(NT // tm,),
        in_specs=[
            row,
            pl.BlockSpec((8, D), lambda i: (jnp.maximum(i * hb - 1, 0), 0)),
            pl.BlockSpec((8, D), lambda i: (jnp.minimum((i + 1) * hb, n8 - 1), 0)),
            pl.BlockSpec((1, 1, D), lambda i: (cidx(i), 0, 0)),
            pl.BlockSpec((1, 1, D), lambda i: (cidx(i), 0, 0)),
            full((1, D)), full((6, D)),
            full((D, D)), full((D, D)), full((D, D)),
            full((D, GATE_LORA_PAD)), full((GATE_LORA_PAD, D)),
            full((D, 2 * LORA)), full((2 * LORA, D)), full((D, 2 * LORA)), full((2 * LORA, D)),
            full((2, D)), full((2, D)), full((1, D)), full((1, D)), full((1, D)),
            full((256, 256)),
        ],
        out_specs=[row] * 10,
        out_shape=[jax.ShapeDtypeStruct((NT, D), dt) for dt in (f32, f32, f32, bf16, bf16) + (f32,) * 5],
        compiler_params=pltpu.CompilerParams(
            dimension_semantics=("parallel",), vmem_limit_bytes=VMEM_LIMIT),
        name="rwkv_projection",
    )(x, x, x, sh, sc, g, mu, wr, wk, wv, g1, g2, w1, w2, a1, a2, w0, a0, k_k, k_a, r_k, gmat)


def _split3(x):
    x1 = x.astype(bf16)
    r1 = x - x1.astype(f32)
    x2 = r1.astype(bf16)
    x3 = (r1 - x2.astype(f32)).astype(bf16)
    return x1, x2, x3


def _scan_kernel(*refs, nc, has_init, emit_state, n_casts):
    fwd_refs = refs[0:6]
    bwd_refs = refs[6:12]
    ka_ref, tri_ref, mask_ref = refs[12:15]
    pos = 15
    s0_ref = None
    if has_init:
        s0_ref = refs[pos]
        pos += 1
    cast_in = refs[pos:pos + n_casts]
    pos += n_casts
    yf_ref, yb_ref = refs[pos:pos + 2]
    pos += 2
    sout_ref = None
    if emit_state:
        sout_ref = refs[pos]
        pos += 1
    cast_out = refs[pos:pos + n_casts]
    pos += n_casts
    st_scr, cl_scr = refs[pos:pos + 2]

    for src, dst in zip(cast_in, cast_out):
        dst[...] = src[...].astype(bf16)

    s = pl.program_id(1)

    @pl.when(s == 0)
    def _():
        if has_init:
            st_scr[...] = s0_ref[0]
        else:
            st_scr[...] = jnp.zeros_like(st_scr)

    c = CHUNK
    for d, drefs in enumerate((fwd_refs, bwd_refs)):
        lw = drefs[3][...]
        tri = tri_ref[d]
        p1, p2, p3 = _split3(lw)
        cl_scr[d] = _dot(tri, p1) + _dot(tri, p2) + _dot(tri, p3)

    lane = lax.broadcasted_iota(jnp.int32, (1, 128), 1)
    m0 = (lane < HD).astype(f32)
    m1 = 1.0 - m0
    rid = lax.broadcasted_iota(jnp.int32, (128, 128), 0)
    cid = lax.broadcasted_iota(jnp.int32, (128, 128), 1)
    eye = (rid == cid).astype(f32)

    n_pairs = RW_HEADS // 2
    nb = 2 * n_pairs

    def per_head_rows(x):
        return jnp.concatenate([x * m0, x * m1], axis=0)

    ars_l, bk_l, v2_l, dec_l = [], [], [], []
    for d, drefs in enumerate((fwd_refs, bwd_refs)):
        r_ref, v_ref, kk_ref, lw_ref, a_ref, k_ref = drefs
        end_row = c - 1 if d == 0 else 0
        for p in range(n_pairs):
            ln = slice(p * 128, (p + 1) * 128)
            cl = cl_scr[d, :, ln]
            kk = kk_ref[:, ln]
            tot = cl[end_row:end_row + 1]
            e_inv = jnp.exp(-cl)
            a = a_ref[:, ln]
            kka = kk * a
            kd = _in_context_key(k_ref[:, ln], a, ka_ref[:, ln])
            at = per_head_rows(-kk * jnp.exp(cl - lw_ref[:, ln]))
            rt = per_head_rows(r_ref[:, ln] * jnp.exp(cl))
            ars_l.append(jnp.concatenate([at, rt], axis=0).astype(bf16))
            bk_l.append(jnp.concatenate([per_head_rows(kka * e_inv), per_head_rows(kd * e_inv)],
                                        axis=0).astype(bf16))
            v2_l.append(per_head_rows(v_ref[:, ln]).astype(bf16))
            dec_l.append(jnp.exp(tot))
    ar = jnp.stack(ars_l)
    bk = jnp.stack(bk_l)
    v2 = jnp.stack(v2_l)
    dec = jnp.stack(dec_l)

    st = st_scr[...].reshape(nb, 128, 128)
    g2 = _bdot_nt(ar, bk).reshape(2, n_pairs, 256, 256) * mask_ref[...][:, None]
    g2 = g2.reshape(nb, 256, 256)
    ars = _bdot_nt(ar, st.astype(bf16))
    gv = _bdot(g2[:, :, 128:].astype(bf16), v2)
    l_bd = g2[:, :128, :128]
    l_bf = l_bd.astype(bf16)
    pk = _bdot(l_bf, l_bf)
    q = eye[None] + l_bd
    for it in range(5):
        pkb = pk.astype(bf16)
        if it < 4:
            res = _bdot(jnp.concatenate([q, pk], axis=1).astype(bf16), pkb)
            q = q + res[:, :128]
            pk = res[:, 128:]
        else:
            q = q + _bdot(q.astype(bf16), pkb)
    rhs = ars[:, :128] + gv[:, :128]
    u2b = _bdot(q.astype(bf16), rhs.astype(bf16)).astype(bf16)
    y2 = ars[:, 128:] + gv[:, 128:] + _bdot(g2[:, 128:, :128].astype(bf16), u2b)
    y = y2[:, :c] + y2[:, c:]
    for d, y_ref in enumerate((yf_ref, yb_ref)):
        for p in range(n_pairs):
            y_ref[:, p * 128:(p + 1) * 128] = y[d * n_pairs + p].astype(y_ref.dtype)
    uv = jnp.concatenate([u2b, v2], axis=1)
    st_scr[...] = ((st + _bdot_tn(uv, bk)) * dec).reshape(2, n_pairs, 128, 128)

    if emit_state:
        @pl.when(s == nc - 1)
        def _():
            for d in range(2):
                for p in range(n_pairs):
                    m = st_scr[d, p]
                    sout_ref[0, d, 2 * p] = m[:HD, :HD]
                    sout_ref[0, d, 2 * p + 1] = pltpu.roll(m[HD:, :], HD, 1)[:, :HD]


def _rwkv_scan(streams_f, streams_b, k_a, tri, mask, s0_bd, casts, *, n_seq, seq_len, row0, emit_state):
    nc = seq_len // CHUNK
    blk0 = row0 // CHUNK
    fwd_spec = pl.BlockSpec((CHUNK, D), lambda b, s: (blk0 + b * nc + s, 0))
    bwd_spec = pl.BlockSpec((CHUNK, D), lambda b, s: (blk0 + b * nc + nc - 1 - s, 0))
    in_specs = [fwd_spec] * 6 + [bwd_spec] * 6 + [
        pl.BlockSpec((1, D), lambda b, s: (0, 0)),
        pl.BlockSpec((2, CHUNK, CHUNK), lambda b, s: (0, 0, 0)),
        pl.BlockSpec((2, 256, 256), lambda b, s: (0, 0, 0)),
    ]
    args = list(streams_f) + list(streams_b) + [k_a, tri, mask]
    state_block = (1, 2, RW_HEADS // 2, 128, 128)
    if s0_bd is not None:
        in_specs.append(pl.BlockSpec(state_block, lambda b, s: (b, 0, 0, 0, 0)))
        args.append(s0_bd)
    out_specs = [pl.BlockSpec((CHUNK, D), lambda b, s: (b * nc + s, 0)),
                 pl.BlockSpec((CHUNK, D), lambda b, s: (b * nc + nc - 1 - s, 0))]
    out_shape = [jax.ShapeDtypeStruct((n_seq * seq_len, D), bf16)] * 2
    if emit_state:
        out_specs.append(pl.BlockSpec((1, 2, RW_HEADS, HD, HD), lambda b, s: (b, 0, 0, 0, 0)))
        out_shape.append(jax.ShapeDtypeStruct((n_seq, 2, RW_HEADS, HD, HD), f32))
    for w in casts:
        per_expert = (n_seq * nc) // w.shape[0]
        spec = pl.BlockSpec((1, w.shape[1] // per_expert, w.shape[2]),
                            lambda b, s, per_expert=per_expert: ((b * nc + s) // per_expert, (b * nc + s) % per_expert, 0))
        in_specs.append(spec)
        args.append(w)
        out_specs.append(spec)
        out_shape.append(jax.ShapeDtypeStruct(w.shape, bf16))
    return pl.pallas_call(
        functools.partial(_scan_kernel, nc=nc, has_init=s0_bd is not None, emit_state=emit_state,
                          n_casts=len(casts)),
        grid=(n_seq, nc),
        in_specs=in_specs,
        out_specs=out_specs,
        out_shape=out_shape,
        scratch_shapes=[pltpu.VMEM(state_block[1:], f32), pltpu.VMEM((2, CHUNK, D), f32)],
        compiler_params=pltpu.CompilerParams(
            dimension_semantics=("parallel", "arbitrary"), vmem_limit_bytes=VMEM_LIMIT),
        name="rwkv_scan_prompt" if emit_state else "rwkv_scan_sample",
    )(*args)


def _scan_constants():
    t = jnp.arange(CHUNK)
    lower = (t[:, None] >= t[None, :])
    tri = jnp.stack([lower, lower.T]).astype(bf16)
    masks = []
    for d in range(2):
        strict = (t[:, None] > t[None, :]) if d == 0 else (t[:, None] < t[None, :])
        incl = lower if d == 0 else lower.T
        blocks = []
        for m in (strict, incl):
            bd = jnp.kron(jnp.eye(2, dtype=f32), m.astype(f32))
            blocks.append(jnp.concatenate([bd, bd], axis=1))
        masks.append(jnp.concatenate(blocks, axis=0))
    return tri, jnp.stack(masks)


def _state_to_blockdiag(s):
    n = s.shape[0]
    s = s.reshape(n, 2, RW_HEADS // 2, 2, HD, HD)
    z = jnp.zeros_like(s[:, :, :, 0])
    top = jnp.concatenate([s[:, :, :, 0], z], axis=-1)
    bot = jnp.concatenate([z, s[:, :, :, 1]], axis=-1)
    return jnp.concatenate([top, bot], axis=-2)


def _rope_tables():
    t = jnp.arange(SAMPLE_LEN)
    rows = (t // GRID_W).astype(f32)
    cols = (t % GRID_W).astype(f32)
    nf = HD // 4
    inv = 10000.0 ** (-jnp.arange(nf, dtype=f32) / nf)
    ang = jnp.concatenate([rows[:, None] * inv, cols[:, None] * inv], axis=-1)
    cos = jnp.repeat(jnp.cos(ang), 2, axis=-1)
    sin = jnp.repeat(jnp.sin(ang), 2, axis=-1) * jnp.tile(jnp.array([-1.0, 1.0], f32), HD // 2)
    return jnp.tile(cos, (1, 2)), jnp.tile(sin, (1, 2))


def kernel(x_prompt, x_sample, cache_diff_k, cache_diff_v, cache_na_k, cache_na_v, state_rwkv, c, c_ctx, w_ada, b_ada, g_mix, g_ffn, w_in, w_out, diff_q_g, diff_k_g, diff_lam_q1, diff_lam_k1, diff_lam_q2, diff_lam_k2, diff_subln_g, na_q_g, na_k_g, na_rpb, ffn_w1, ffn_w3, ffn_w2, rw_mu, rw_wr, rw_wk, rw_wv, rw_wo, rw_w0, rw_w1, rw_w2, rw_a0, rw_a1, rw_a2, rw_g1, rw_g2, rw_k_k, rw_k_a, rw_r_k, rw_ln_g, rw_ln_b, moe_router, moe_router_b, moe_w1, moe_w3, moe_w2):
    xp = x_prompt.reshape(NP, D)
    xs = x_sample.reshape(NS, D)
    cond8 = jnp.concatenate([c_ctx[None, :], c, jnp.zeros((3, D), f32)], axis=0)
    mod = _ada_table(cond8, w_ada, b_ada)
    gmat = jnp.kron(jnp.eye(4, dtype=f32), jnp.ones((HD, HD), f32)).astype(bf16)

    lam_init = 0.8 - 0.6 * math.exp(-0.3 * 0)
    ones_seg = jnp.ones((SEG,), f32)
    tile8 = lambda gvec: jnp.tile(gvec, SEG // HD)
    gains = jnp.stack([tile8(diff_q_g[0]), tile8(diff_k_g[0]), ones_seg,
                       tile8(na_q_g[0]), tile8(na_k_g[0]), ones_seg]).reshape(6, 1, SEG)
    cos_t, sin_t = _rope_tables()
    proj, dk_p, dv_p, nk_p, nv_p, ffn_w1_bf, ffn_w3_bf, ffn_w2_bf = _in_projection(
        xp, xs, mod[0][0], mod[0][1], g_mix[0][None, :], w_in[0].astype(bf16), gains, gmat, cos_t, sin_t,
        (ffn_w1[0], ffn_w3[0], ffn_w2[0]))
    lamp = jnp.stack([diff_lam_q1[0], diff_lam_k1[0], diff_lam_q2[0], diff_lam_k2[0]])
    subg = diff_subln_g[0][None, :]
    o_prompt = _prompt_attention(proj, lamp, subg, lam_init)
    o_diff = _latent_diff_attention(proj, cache_diff_k[:, 0].reshape(N_SAMPLE_SEQ, PAST, SEG),
                                    cache_diff_v[:, 0].reshape(N_SAMPLE_SEQ, PAST, SEG), lamp, subg, lam_init)
    o_na = _latent_na_attention(proj, cache_na_k[:, 0].reshape(N_SAMPLE_SEQ, PAST, SEG),
                                cache_na_v[:, 0].reshape(N_SAMPLE_SEQ, PAST, SEG), _rpb_table(na_rpb[0]))
    x = _out_projection(o_prompt, o_diff, o_na, w_out[0].astype(bf16), xp, xs, mod[0][2])
    x = _dense_ffn(x, mod[0][3], mod[0][4], mod[0][5], g_ffn[0][None, :],
                   ffn_w1_bf, ffn_w3_bf, ffn_w2_bf)

    pad_g = GATE_LORA_PAD - GATE_LORA
    g1 = jnp.pad(rw_g1[0], ((0, 0), (0, pad_g))).astype(bf16)
    g2 = jnp.pad(rw_g2[0], ((0, pad_g), (0, 0))).astype(bf16)
    w1cat = jnp.concatenate([rw_w1[0, 0], rw_w1[0, 1]], axis=1).astype(bf16)
    w2cat = jnp.concatenate([rw_w2[0, 0], rw_w2[0, 1]], axis=0).astype(bf16)
    a1cat = jnp.concatenate([rw_a1[0, 0], rw_a1[0, 1]], axis=1).astype(bf16)
    a2cat = jnp.concatenate([rw_a2[0, 0], rw_a2[0, 1]], axis=0).astype(bf16)
    k_a = rw_k_a[0][None, :]
    (r, v, kkn, bonus, gate_lora, lwf, lwb, af, ab, k) = _rwkv_projection(
        x, mod[1][0], mod[1][1], g_mix[1][None, :], rw_mu[0],
        rw_wr[0].astype(bf16), rw_wk[0].astype(bf16), rw_wv[0].astype(bf16), g1, g2,
        w1cat, w2cat, a1cat, a2cat, rw_w0[0], rw_a0[0],
        rw_k_k[0][None, :], k_a, rw_r_k[0].reshape(1, D), gmat)
    tri, mask = _scan_constants()
    streams_f = (r, v, kkn, lwf, af, k)
    streams_b = (r, v, kkn, lwb, ab, k)
    yf_p, yb_p, st_p, moe_w1_bf, moe_w3_bf = _rwkv_scan(
        streams_f, streams_b, k_a, tri, mask, None, (moe_w1[0], moe_w3[0]),
        n_seq=N_PROMPT_SEQ, seq_len=PROMPT_LEN, row0=0, emit_state=True)
    yf_s, yb_s, moe_w2_bf = _rwkv_scan(
        streams_f, streams_b, k_a, tri, mask, _state_to_blockdiag(state_rwkv[:, 0]), (moe_w2[0],),
        n_seq=N_SAMPLE_SEQ, seq_len=SAMPLE_LEN, row0=NP, emit_state=False)
    x = _rwkv_out_projection(yf_p, yb_p, yf_s, yb_s, bonus, gate_lora, rw_ln_g[0][None, :], rw_ln_b[0][None, :], gmat,
                             rw_wo[0].astype(bf16), x, mod[1][2])
    w_router = jnp.pad(moe_router[0], ((0, 0), (0, 128 - N_EXPERTS)))
    b_router = jnp.pad(moe_router_b[0], (0, 128 - N_EXPERTS))[None, :]
    h_bf, gates, rank, rank_t, cnt = _moe_route(x, mod[1][3], mod[1][4], g_ffn[1][None, :], w_router, b_router)
    y_p, y_s = _moe_experts(cnt[:, 0, :N_EXPERTS].reshape(-1), h_bf, rank, rank_t, gates, x, mod[1][5],
                            moe_w1_bf, moe_w3_bf, moe_w2_bf)

    new_dk = dk_p.reshape(N_PROMPT_SEQ, 1, PROMPT_LEN, DIFF_HEADS, 2 * HD)
    new_dv = dv_p.reshape(N_PROMPT_SEQ, 1, PROMPT_LEN, DIFF_HEADS, 2 * HD)
    new_nk = nk_p.reshape(N_PROMPT_SEQ, 1, PROMPT_LEN, NA_HEADS, HD)
    new_nv = nv_p.reshape(N_PROMPT_SEQ, 1, PROMPT_LEN, NA_HEADS, HD)
    new_state = st_p[:, None]
    return (y_p.reshape(N_PROMPT_SEQ, PROMPT_LEN, D), y_s.reshape(N_SAMPLE_SEQ, SAMPLE_LEN, D),
            new_dk, new_dv, new_nk, new_nv, new_state)
```

```python
import functools
import math

import jax
import jax.numpy as jnp
from jax import lax
from jax.experimental import pallas as pl
from jax.experimental.pallas import tpu as pltpu

f32 = jnp.float32
bf16 = jnp.bfloat16

D = 1024
N_PROMPT_SEQ, PROMPT_LEN = 32, 256
N_SAMPLE_SEQ, SAMPLE_LEN = 4, 2048
NP = N_PROMPT_SEQ * PROMPT_LEN
NS = N_SAMPLE_SEQ * SAMPLE_LEN
NT = NP + NS
PAST = 256
GRID_W = 64
GRID_R = SAMPLE_LEN // GRID_W
HD = 64
DIFF_HEADS = 4
NA_HEADS = 8
NA_KH = 8
NA_KW = 16
SEG = 512
IN_COLS = 6 * SEG
FFN_DIM = 2816
N_EXPERTS = 8
EXPERT_DIM = 3584
RW_HEADS = 16
LORA = 64
GATE_LORA = 160
GATE_LORA_PAD = 256
EPS = 1e-6
GN_EPS = 64e-5
NEG_BIG = -1e30
DIFF_HEADS_PER_STEP = 2
NA_ROWS_PER_STEP = 16
CHUNK = 64
VMEM_LIMIT = 56 * 1024 * 1024

ADA_COLS = 1536
IN_PROJ_ROWS = 1024
DIFF_Q_ROWS = 256
OUT_PROJ_ROWS = 512
FFN_ROWS = 512
FFN_COLS = FFN_DIM // 2
RWKV_PROJ_ROWS = 256
RWKV_OUT_ROWS = 512


def _cond_idx(i, tm):
    return jnp.maximum((i * tm) // SAMPLE_LEN - (NP // SAMPLE_LEN - 1), 0)


def _dot(a, b):
    return jnp.dot(a, b, preferred_element_type=f32)


def _dot_nt(a, b):
    return lax.dot_general(a, b, (((1,), (1,)), ((), ())), preferred_element_type=f32)


def _dot_tn(a, b):
    return lax.dot_general(a, b, (((0,), (0,)), ((), ())), preferred_element_type=f32)


def _bdot(a, b):
    return lax.dot_general(a, b, (((2,), (1,)), ((0,), (0,))), preferred_element_type=f32)


def _bdot_nt(a, b):
    return lax.dot_general(a, b, (((2,), (2,)), ((0,), (0,))), preferred_element_type=f32)


def _bdot_tn(a, b):
    return lax.dot_general(a, b, (((1,), (1,)), ((0,), (0,))), preferred_element_type=f32)


def _sigmoid(x):
    return 0.5 * jnp.tanh(0.5 * x) + 0.5


def _norm_mod(x, g, sh, sc):
    ms = jnp.mean(x * x, axis=-1, keepdims=True)
    return (x * lax.rsqrt(ms + EPS) * g) * (1.0 + sc) + sh


def _group_sum(x, gmat):
    xb = x.astype(bf16)
    cols = [_dot(xb[:, c * 256:(c + 1) * 256], gmat) for c in range(x.shape[1] // 256)]
    return cols[0] if len(cols) == 1 else jnp.concatenate(cols, axis=1)


def _softmax_parts(parts):
    m = parts[0].max(axis=-1, keepdims=True)
    for p in parts[1:]:
        m = jnp.maximum(m, p.max(axis=-1, keepdims=True))
    es = [jnp.exp(p - m) for p in parts]
    l = es[0].sum(axis=-1, keepdims=True)
    for e in es[1:]:
        l = l + e.sum(axis=-1, keepdims=True)
    inv = 1.0 / l
    return [e * inv for e in es]


def _lane_lo(n=128):
    return lax.broadcasted_iota(jnp.int32, (1, n), 1) < HD


def _stack_halves(q):
    lo = _lane_lo()
    return jnp.concatenate([jnp.where(lo, q, 0.0), jnp.where(lo, 0.0, q)], axis=0)


def _ada_kernel(cond_ref, w_ref, b_ref, o_ref):
    x = cond_ref[...]
    s = x * _sigmoid(x)
    o_ref[0] = _dot(s.astype(bf16), w_ref[0].astype(bf16)) + b_ref[0]


def _ada_table(cond8, w_ada, b_ada):
    depth = w_ada.shape[0]
    tn = ADA_COLS
    out = pl.pallas_call(
        _ada_kernel,
        grid=(depth, 6 * D // tn),
        in_specs=[
            pl.BlockSpec((8, D), lambda l, n: (0, 0)),
            pl.BlockSpec((1, D, tn), lambda l, n: (l, 0, n)),
            pl.BlockSpec((1, 1, tn), lambda l, n: (l, 0, n)),
        ],
        out_specs=pl.BlockSpec((1, 8, tn), lambda l, n: (l, 0, n)),
        out_shape=jax.ShapeDtypeStruct((depth, 8, 6 * D), f32),
        compiler_params=pltpu.CompilerParams(vmem_limit_bytes=VMEM_LIMIT),
        name="ada_table",
    )(cond8, w_ada, b_ada.reshape(depth, 1, 6 * D))
    out = out.reshape(depth, 8, 6, D)
    return [[out[l, :, k, :].reshape(8, 1, D) for k in range(6)] for l in range(depth)]


def _qk_norm(y, gain, gmat):
    ss = _group_sum(y * y, gmat) * (1.0 / HD)
    return y * lax.rsqrt(ss + EPS) * gain


def _rope(y, cos, sin):
    even = (lax.broadcasted_iota(jnp.int32, (1, 128), 1) % 2) == 0
    outs = []
    for c in range(y.shape[1] // 128):
        yc = y[:, c * 128:(c + 1) * 128]
        swapped = jnp.where(even, pltpu.roll(yc, 127, 1), pltpu.roll(yc, 1, 1))
        outs.append(yc * cos + swapped * sin)
    return jnp.concatenate(outs, axis=1)


def _inproj_kernel(xp_ref, xs_ref, sh_ref, sc_ref, g_ref, w_ref, gain_ref, gmat_ref, cos_ref, sin_ref,
                   f1_ref, f3_ref, f2_ref,
                   o_ref, dk_ref, dv_ref, nk_ref, nv_ref, f1_out, f3_out, f2_out, h_scr, *, tm):
    i = pl.program_id(0)
    j = pl.program_id(1)

    @pl.when(j == 0)
    def _():
        for src, dst in ((f1_ref, f1_out), (f3_ref, f3_out), (f2_ref, f2_out)):
            dst[...] = src[...].astype(bf16)

    def emit(val):
        o_ref[...] = val.astype(bf16)
        for seg, cache_ref in ((1, dk_ref), (2, dv_ref), (4, nk_ref), (5, nv_ref)):
            @pl.when(jnp.logical_and(j == seg, i < NP // tm))
            def _():
                cache_ref[...] = val

    for x_ref, active in ((xp_ref, i < NP // tm), (xs_ref, i >= NP // tm)):
        @pl.when(jnp.logical_and(j == 0, active))
        def _():
            h_scr[...] = _norm_mod(x_ref[...], g_ref[...], sh_ref[0], sc_ref[0]).astype(bf16)

    project = lambda: _dot(h_scr[...], w_ref[...])
    is_norm = jnp.logical_and(j != 2, j != 5)
    is_rope = jnp.logical_and(j < 2, i >= NP // tm)

    @pl.when(jnp.logical_not(is_norm))
    def _():
        emit(project())

    @pl.when(jnp.logical_and(is_norm, jnp.logical_not(is_rope)))
    def _():
        emit(_qk_norm(project(), gain_ref[0], gmat_ref[...]))

    @pl.when(is_rope)
    def _():
        o_ref[...] = _rope(_qk_norm(project(), gain_ref[0], gmat_ref[...]),
                           cos_ref[...], sin_ref[...]).astype(bf16)


def _in_projection(xp, xs, sh, sc, g, w_bf, gains, gmat, cos_t, sin_t, ffn_weights):
    tm = IN_PROJ_ROWS
    n_tiles = NT // tm
    n_prompt_tiles = NP // tm
    tiles_per_seq = SAMPLE_LEN // tm
    cidx = functools.partial(_cond_idx, tm=tm)
    rope_idx = lambda i, j: (jnp.maximum(i - n_prompt_tiles, 0) % tiles_per_seq, 0)
    cast_specs = [pl.BlockSpec((w.shape[0] // n_tiles, w.shape[1]), lambda i, j: (i, 0)) for w in ffn_weights]
    return pl.pallas_call(
        functools.partial(_inproj_kernel, tm=tm),
        grid=(NT // tm, IN_COLS // SEG),
        in_specs=[
            pl.BlockSpec((tm, D), lambda i, j: (jnp.minimum(i, n_prompt_tiles - 1), 0)),
            pl.BlockSpec((tm, D), lambda i, j: (jnp.maximum(i - n_prompt_tiles, 0), 0)),
            pl.BlockSpec((1, 1, D), lambda i, j: (cidx(i), 0, 0)),
            pl.BlockSpec((1, 1, D), lambda i, j: (cidx(i), 0, 0)),
            pl.BlockSpec((1, D), lambda i, j: (0, 0)),
            pl.BlockSpec((D, SEG), lambda i, j: (0, j)),
            pl.BlockSpec((1, 1, SEG), lambda i, j: (j, 0, 0)),
            pl.BlockSpec((256, 256), lambda i, j: (0, 0)),
            pl.BlockSpec((tm, 128), rope_idx),
            pl.BlockSpec((tm, 128), rope_idx),
        ] + cast_specs,
        out_specs=[pl.BlockSpec((tm, SEG), lambda i, j: (i, j))] + [
            pl.BlockSpec((tm, SEG), lambda i, j: (jnp.minimum(i, n_prompt_tiles - 1), 0))] * 4 + cast_specs,
        out_shape=[jax.ShapeDtypeStruct((NT, IN_COLS), bf16)] + [jax.ShapeDtypeStruct((NP, SEG), f32)] * 4
        + [jax.ShapeDtypeStruct(w.shape, bf16) for w in ffn_weights],
        scratch_shapes=[pltpu.VMEM((tm, D), bf16)],
        compiler_params=pltpu.CompilerParams(
            dimension_semantics=("arbitrary", "arbitrary"), vmem_limit_bytes=VMEM_LIMIT),
        name="in_projection",
    )(xp, xs, sh, sc, g, w_bf, gains, gmat, cos_t, sin_t, *ffn_weights)


def _lambda_value(lamp_ref, lam_init):
    lp = lamp_ref[...]
    e1 = jnp.exp(jnp.sum(lp[0:1] * lp[1:2], axis=-1, keepdims=True))
    e2 = jnp.exp(jnp.sum(lp[2:3] * lp[3:4], axis=-1, keepdims=True))
    return e1 - e2 + lam_init


def _sub_ln(o, subg, lam_init):
    ms = jnp.mean(o * o, axis=-1, keepdims=True)
    return o * lax.rsqrt(ms + EPS) * subg * (1.0 - lam_init)


def _prompt_attn_kernel(p_ref, lamp_ref, subg_ref, o_ref, *, lam_init):
    lam = _lambda_value(lamp_ref, lam_init)
    t = PROMPT_LEN
    lo = _lane_lo()
    scale = HD ** -0.5
    def stacked(seg, n):
        return [p_ref[:, seg * SEG + g * 128:seg * SEG + (g + 1) * 128] for g in range(n)]

    def scores(seg_q, seg_k, n):
        q = jnp.stack([(_stack_halves(x) * scale).astype(bf16) for x in stacked(seg_q, n)])
        k = jnp.stack([x.astype(bf16) for x in stacked(seg_k, n)])
        (p,) = _softmax_parts([_bdot_nt(q, k)])
        return p

    p = scores(0, 1, DIFF_HEADS)
    v = jnp.stack([x.astype(bf16) for x in stacked(2, DIFF_HEADS)])
    o = _bdot((p[:, :t] - lam * p[:, t:]).astype(bf16), v)
    o = _sub_ln(o, subg_ref[...], lam_init).astype(bf16)
    for h in range(DIFF_HEADS):
        o_ref[:, h * 128:(h + 1) * 128] = o[h]
    p = scores(3, 4, NA_HEADS // 2)
    v = jnp.stack([x.astype(bf16) for x in stacked(5, NA_HEADS // 2)])
    o = _bdot(p.astype(bf16), v)
    o = jnp.where(lo, o[:, :t], o[:, t:]).astype(bf16)
    for hp in range(NA_HEADS // 2):
        o_ref[:, SEG + hp * 128:SEG + (hp + 1) * 128] = o[hp]


def _prompt_attention(proj, lamp, subg, lam_init):
    return pl.pallas_call(
        functools.partial(_prompt_attn_kernel, lam_init=lam_init),
        grid=(N_PROMPT_SEQ,),
        in_specs=[
            pl.BlockSpec((PROMPT_LEN, IN_COLS), lambda b: (b, 0)),
            pl.BlockSpec((4, HD), lambda b: (0, 0)),
            pl.BlockSpec((1, 128), lambda b: (0, 0)),
        ],
        out_specs=pl.BlockSpec((PROMPT_LEN, D), lambda b: (b, 0)),
        out_shape=jax.ShapeDtypeStruct((NP, D), bf16),
        compiler_params=pltpu.CompilerParams(
            dimension_semantics=("parallel",), vmem_limit_bytes=VMEM_LIMIT),
        name="prompt_attention",
    )(proj, lamp, subg)


def _latent_diff_kernel(q_ref, kn_ref, vn_ref, kc_ref, vc_ref, lamp_ref, subg_ref, o_ref,
                        *, lam_init, tq):
    lam = _lambda_value(lamp_ref, lam_init)
    scale = HD ** -0.5
    lanes = [slice(hh * 128, (hh + 1) * 128) for hh in range(DIFF_HEADS_PER_STEP)]
    heads = lambda ref2d: jnp.stack([ref2d[:, ln].astype(bf16) for ln in lanes])
    qq = jnp.stack([(_stack_halves(q_ref[:, ln]) * scale).astype(bf16) for ln in lanes])
    s_c = _bdot_nt(qq, heads(kc_ref.at[0]))
    s_n = _bdot_nt(qq, heads(kn_ref))
    p_c, p_n = _softmax_parts([s_c, s_n])
    pd_c = p_c[:, :tq] - lam * p_c[:, tq:]
    pd_n = p_n[:, :tq] - lam * p_n[:, tq:]
    o = _bdot(pd_c.astype(bf16), heads(vc_ref.at[0])) + _bdot(pd_n.astype(bf16), heads(vn_ref))
    o = _sub_ln(o, subg_ref[...], lam_init).astype(bf16)
    for hh, ln in enumerate(lanes):
        o_ref[:, ln] = o[hh]


def _latent_diff_attention(proj, cache_k, cache_v, lamp, subg, lam_init):
    tq = DIFF_Q_ROWS
    nqb = SAMPLE_LEN // tq
    q0 = NP // tq
    s0 = NP // SAMPLE_LEN
    hw = 128 * DIFF_HEADS_PER_STEP
    kcol = SEG // hw
    return pl.pallas_call(
        functools.partial(_latent_diff_kernel, lam_init=lam_init, tq=tq),
        grid=(N_SAMPLE_SEQ, DIFF_HEADS // DIFF_HEADS_PER_STEP, nqb),
        in_specs=[
            pl.BlockSpec((tq, hw), lambda b, h, q: (q0 + b * nqb + q, h)),
            pl.BlockSpec((SAMPLE_LEN, hw), lambda b, h, q: (s0 + b, kcol + h)),
            pl.BlockSpec((SAMPLE_LEN, hw), lambda b, h, q: (s0 + b, 2 * kcol + h)),
            pl.BlockSpec((1, PAST, hw), lambda b, h, q: (b, 0, h)),
            pl.BlockSpec((1, PAST, hw), lambda b, h, q: (b, 0, h)),
            pl.BlockSpec((4, HD), lambda b, h, q: (0, 0)),
            pl.BlockSpec((1, 128), lambda b, h, q: (0, 0)),
        ],
        out_specs=pl.BlockSpec((tq, hw), lambda b, h, q: (b * nqb + q, h)),
        out_shape=jax.ShapeDtypeStruct((NS, SEG), bf16),
        compiler_params=pltpu.CompilerParams(
            dimension_semantics=("parallel", "parallel", "arbitrary"), vmem_limit_bytes=VMEM_LIMIT),
        name="latent_diff_attention",
    )(proj, proj, proj, cache_k, cache_v, lamp, subg)


def _rpb_table_kernel(rpb_ref, o_ref):
    h = pl.program_id(0)
    wq = lax.broadcasted_iota(jnp.int32, (GRID_W, GRID_W), 0)
    wk = lax.broadcasted_iota(jnp.int32, (GRID_W, GRID_W), 1)
    col_start = jnp.clip(wq - NA_KW // 2, 0, GRID_W - NA_KW)
    col_in = jnp.logical_and(wk >= col_start, wk < col_start + NA_KW)
    col_off = jnp.clip(wk - wq, -(NA_KW - 1), NA_KW - 1) + (NA_KW - 1)
    n_dr = 2 * NA_KH - 1
    n_dc = 2 * NA_KW - 1
    for dr in range(n_dr):
        t = jnp.zeros((GRID_W, GRID_W), f32)
        for c in range(n_dc):
            t = jnp.where(col_off == c, rpb_ref[h * (n_dr * n_dc) + dr * n_dc + c], t)
        o_ref[0, dr] = jnp.where(col_in, t, NEG_BIG)


def _rpb_table(rpb):
    n_dr = 2 * NA_KH - 1
    tcol = pl.pallas_call(
        _rpb_table_kernel,
        grid=(NA_HEADS,),
        in_specs=[pl.BlockSpec(memory_space=pltpu.SMEM)],
        out_specs=pl.BlockSpec((1, n_dr, GRID_W, GRID_W), lambda h: (h, 0, 0, 0)),
        out_shape=jax.ShapeDtypeStruct((NA_HEADS, n_dr, GRID_W, GRID_W), f32),
        name="rpb_table",
    )(rpb.reshape(-1))
    return jnp.stack(
        [jnp.concatenate([tcol[:, j - s + NA_KH - 1] for j in range(NA_KH)], axis=-1) for s in range(NA_KH)],
        axis=1)


def _latent_na_kernel(q_ref, k_ref, v_ref, kc_ref, vc_ref, bias_ref, o_ref):
    scale = HD ** -0.5
    lo = _lane_lo()
    kc = kc_ref[0].astype(bf16)
    vc = vc_ref[0].astype(bf16)
    win = NA_KH * GRID_W

    nr = NA_ROWS_PER_STEP
    kcb = jnp.broadcast_to(kc[None], (nr,) + kc.shape)
    vcb = jnp.broadcast_to(vc[None], (nr,) + vc.shape)

    def rows(g, carry):
        qs, kws, vws, biases = [], [], [], []
        for t in range(nr):
            r = g * nr + t
            rs = jnp.clip(r - NA_KH // 2, 0, GRID_R - NA_KH)
            sidx = r - rs
            q = q_ref[pl.ds(pl.multiple_of(r * GRID_W, GRID_W), GRID_W), :]
            qs.append((_stack_halves(q) * scale).astype(bf16))
            k0 = pl.multiple_of(rs * GRID_W, GRID_W)
            kws.append(k_ref[pl.ds(k0, win), :].astype(bf16))
            vws.append(v_ref[pl.ds(k0, win), :].astype(bf16))
            biases.append(jnp.concatenate([bias_ref[0, sidx], bias_ref[1, sidx]], axis=0))
        qq = jnp.stack(qs)
        s_loc = _bdot_nt(qq, jnp.stack(kws)) + jnp.stack(biases)
        s_ctx = _bdot_nt(qq, kcb)
        p_loc, p_ctx = _softmax_parts([s_loc, s_ctx])
        o = _bdot(p_loc.astype(bf16), jnp.stack(vws)) + _bdot(p_ctx.astype(bf16), vcb)
        o = jnp.where(lo, o[:, :GRID_W], o[:, GRID_W:]).astype(bf16)
        row0 = pl.multiple_of(g * (nr * GRID_W), nr * GRID_W)
        o_ref[pl.ds(row0, nr * GRID_W), :] = o.reshape(nr * GRID_W, 128)
        return carry

    lax.fori_loop(0, GRID_R // nr, rows, 0)


def _latent_na_attention(proj, cache_k, cache_v, bias):
    s0 = NP // SAMPLE_LEN
    return pl.pallas_call(
        _latent_na_kernel,
        grid=(N_SAMPLE_SEQ, NA_HEADS // 2),
        in_specs=[
            pl.BlockSpec((SAMPLE_LEN, 128), lambda b, h: (s0 + b, 12 + h)),
            pl.BlockSpec((SAMPLE_LEN, 128), lambda b, h: (s0 + b, 16 + h)),
            pl.BlockSpec((SAMPLE_LEN, 128), lambda b, h: (s0 + b, 20 + h)),
            pl.BlockSpec((1, PAST, 128), lambda b, h: (b, 0, h)),
            pl.BlockSpec((1, PAST, 128), lambda b, h: (b, 0, h)),
            pl.BlockSpec((2, NA_KH, GRID_W, NA_KH * GRID_W), lambda b, h: (h, 0, 0, 0)),
        ],
        out_specs=pl.BlockSpec((SAMPLE_LEN, 128), lambda b, h: (b, h)),
        out_shape=jax.ShapeDtypeStruct((NS, SEG), bf16),
        compiler_params=pltpu.CompilerParams(
            dimension_semantics=("parallel", "parallel"), vmem_limit_bytes=VMEM_LIMIT),
        name="latent_na_attention",
    )(proj, proj, proj, cache_k, cache_v, bias)


def _out_proj_kernel(ap_ref, ad_ref, an_ref, w_ref, xp_ref, xs_ref, gt_ref, o_ref, *, tm):
    i = pl.program_id(0)

    @pl.when(i < NP // tm)
    def _():
        o_ref[...] = xp_ref[...] + gt_ref[0] * _dot(ap_ref[...], w_ref[...])

    @pl.when(i >= NP // tm)
    def _():
        mixed = _dot(ad_ref[...], w_ref[:SEG, :]) + _dot(an_ref[...], w_ref[SEG:, :])
        o_ref[...] = xs_ref[...] + gt_ref[0] * mixed


def _out_projection(a_prompt, a_diff, a_na, w_bf, xp, xs, gate):
    tm = OUT_PROJ_ROWS
    n_p = NP // tm
    cidx = functools.partial(_cond_idx, tm=tm)
    prompt_rows = lambda i: (jnp.minimum(i, n_p - 1), 0)
    sample_rows = lambda i: (jnp.maximum(i - n_p, 0), 0)
    return pl.pallas_call(
        functools.partial(_out_proj_kernel, tm=tm),
        grid=(NT // tm,),
        in_specs=[
            pl.BlockSpec((tm, D), prompt_rows),
            pl.BlockSpec((tm, SEG), sample_rows),
            pl.BlockSpec((tm, SEG), sample_rows),
            pl.BlockSpec((D, D), lambda i: (0, 0)),
            pl.BlockSpec((tm, D), prompt_rows),
            pl.BlockSpec((tm, D), sample_rows),
            pl.BlockSpec((1, 1, D), lambda i: (cidx(i), 0, 0)),
        ],
        out_specs=pl.BlockSpec((tm, D), lambda i: (i, 0)),
        out_shape=jax.ShapeDtypeStruct((NT, D), f32),
        compiler_params=pltpu.CompilerParams(
            dimension_semantics=("arbitrary",), vmem_limit_bytes=VMEM_LIMIT),
        name="out_projection",
    )(a_prompt, a_diff, a_na, w_bf, xp, xs, gate)


def _rwkv_out_kernel(yfp_ref, ybp_ref, yfs_ref, ybs_ref, bonus_ref, gate_ref, lng_ref, lnb_ref,
                     gmat_ref, w_ref, x_ref, gt_ref, o_ref, *, tm):
    i = pl.program_id(0)

    def finish(y):
        gmat = gmat_ref[...]
        mu = _group_sum(y, gmat) * (1.0 / HD)
        yc = y - mu
        var = _group_sum(yc * yc, gmat) * (1.0 / HD)
        z = yc * lax.rsqrt(var + GN_EPS) * lng_ref[...] + lnb_ref[...] + bonus_ref[...].astype(f32)
        z = (z * gate_ref[...].astype(f32)).astype(bf16)
        o_ref[...] = x_ref[...] + gt_ref[0] * _dot(z, w_ref[...])

    @pl.when(i < NP // tm)
    def _():
        finish(yfp_ref[...].astype(f32) + ybp_ref[...].astype(f32))

    @pl.when(i >= NP // tm)
    def _():
        finish(yfs_ref[...].astype(f32) + ybs_ref[...].astype(f32))


def _rwkv_out_projection(yf_p, yb_p, yf_s, yb_s, bonus, gate_lora, ln_g, ln_b, gmat, w_bf, x, gate):
    tm = RWKV_OUT_ROWS
    n_p = NP // tm
    cidx = functools.partial(_cond_idx, tm=tm)
    row = pl.BlockSpec((tm, D), lambda i: (i, 0))
    prompt_row = pl.BlockSpec((tm, D), lambda i: (jnp.minimum(i, n_p - 1), 0))
    sample_row = pl.BlockSpec((tm, D), lambda i: (jnp.maximum(i - n_p, 0), 0))
    vec = pl.BlockSpec((1, D), lambda i: (0, 0))
    return pl.pallas_call(
        functools.partial(_rwkv_out_kernel, tm=tm),
        grid=(NT // tm,),
        in_specs=[prompt_row, prompt_row, sample_row, sample_row, row, row, vec, vec,
                  pl.BlockSpec((256, 256), lambda i: (0, 0)),
                  pl.BlockSpec((D, D), lambda i: (0, 0)),
                  row,
                  pl.BlockSpec((1, 1, D), lambda i: (cidx(i), 0, 0))],
        out_specs=row,
        out_shape=jax.ShapeDtypeStruct((NT, D), f32),
        compiler_params=pltpu.CompilerParams(
            dimension_semantics=("arbitrary",), vmem_limit_bytes=VMEM_LIMIT),
        name="rwkv_out_projection",
    )(yf_p, yb_p, yf_s, yb_s, bonus, gate_lora, ln_g, ln_b, gmat, w_bf, x, gate)


def _swiglu_hidden(xb, w1, w3):
    a = _dot(xb, w1)
    return ((a * _sigmoid(a)) * _dot(xb, w3)).astype(bf16)


def _ffn_kernel(x_ref, sh_ref, sc_ref, gt_ref, g_ref, w1_ref, w3_ref, w2_ref, o_ref, h_scr, acc_scr):
    f = pl.program_id(1)

    @pl.when(f == 0)
    def _():
        h_scr[...] = _norm_mod(x_ref[...], g_ref[...], sh_ref[0], sc_ref[0]).astype(bf16)
        acc_scr[...] = jnp.zeros_like(acc_scr)

    acc_scr[...] += _dot(_swiglu_hidden(h_scr[...], w1_ref[...], w3_ref[...]), w2_ref[...])

    @pl.when(f == pl.num_programs(1) - 1)
    def _():
        o_ref[...] = x_ref[...] + gt_ref[0] * acc_scr[...]


def _dense_ffn(x, sh, sc, gt, g, w1, w3, w2):
    tm = FFN_ROWS
    tf = FFN_COLS
    cidx = functools.partial(_cond_idx, tm=tm)
    mod = pl.BlockSpec((1, 1, D), lambda i, f: (cidx(i), 0, 0))
    return pl.pallas_call(
        _ffn_kernel,
        grid=(NT // tm, FFN_DIM // tf),
        in_specs=[
            pl.BlockSpec((tm, D), lambda i, f: (i, 0)),
            mod, mod, mod,
            pl.BlockSpec((1, D), lambda i, f: (0, 0)),
            pl.BlockSpec((D, tf), lambda i, f: (0, f)),
            pl.BlockSpec((D, tf), lambda i, f: (0, f)),
            pl.BlockSpec((tf, D), lambda i, f: (f, 0)),
        ],
        out_specs=pl.BlockSpec((tm, D), lambda i, f: (i, 0)),
        out_shape=jax.ShapeDtypeStruct((NT, D), f32),
        scratch_shapes=[pltpu.VMEM((tm, D), bf16), pltpu.VMEM((tm, D), f32)],
        compiler_params=pltpu.CompilerParams(
            dimension_semantics=("parallel", "arbitrary"), vmem_limit_bytes=VMEM_LIMIT),
        name="ffn_mixer",
    )(x, sh, sc, gt, g, w1, w3, w2)


MOE_TILE = 1024
MOE_MAIN = 288
MOE_EXTRA = 128
MOE_ROWS = MOE_MAIN + -(-(MOE_TILE - MOE_MAIN) // MOE_EXTRA) * MOE_EXTRA


RANK_SEL = 4096


def _router_kernel(x_ref, sh_ref, sc_ref, g_ref, wr_ref, br_ref, tri_ref, triu_ref,
                   h_out, gates_out, rank_out, rank_t_out, cnt_out):
    h = _norm_mod(x_ref[...], g_ref[...], sh_ref[0], sc_ref[0])
    h_out[...] = h.astype(bf16)
    w = wr_ref[...]
    h_hi, w_hi = h.astype(bf16), w.astype(bf16)
    h_lo = (h - h_hi.astype(f32)).astype(bf16)
    w_lo = (w - w_hi.astype(f32)).astype(bf16)
    hw = _dot(h_hi, jnp.concatenate([w_hi, w_lo], axis=1))
    logits = hw[:, :128] + (hw[:, 128:] + _dot(h_lo, w_hi)) + br_ref[...]
    lane = lax.broadcasted_iota(jnp.int32, logits.shape, 1)
    logits = jnp.where(lane < N_EXPERTS, logits, -jnp.inf)
    m1 = logits.max(axis=-1, keepdims=True)
    i1 = jnp.min(jnp.where(logits == m1, lane, 128), axis=-1, keepdims=True)
    rest = jnp.where(lane == i1, -jnp.inf, logits)
    m2 = rest.max(axis=-1, keepdims=True)
    i2 = jnp.min(jnp.where(rest == m2, lane, 128), axis=-1, keepdims=True)
    e2 = jnp.exp(m2 - m1)
    den = 1.0 / (1.0 + e2)
    gates_out[...] = jnp.where(lane == i1, den, jnp.where(lane == i2, e2 * den, 0.0))
    sel = jnp.logical_or(lane == i1, lane == i2)
    self32 = sel.astype(f32)
    selb = self32.astype(bf16)
    rank_out[...] = jnp.where(sel, _dot(tri_ref[...], selb), -1.0)
    rank_t_out[...] = _dot_tn(selb, triu_ref[...])
    cnt_out[0] = jnp.sum(self32, axis=0, keepdims=True).astype(jnp.int32)


def _moe_route(x, sh, sc, g, w_router, b_router):
    tm = MOE_TILE
    cidx = functools.partial(_cond_idx, tm=tm)
    t = jnp.arange(tm)
    tri = (t[:, None] > t[None, :]).astype(bf16)
    triu = jnp.where(t[:, None] == t[None, :], float(RANK_SEL), tri.T.astype(f32)).astype(bf16)
    mod = pl.BlockSpec((1, 1, D), lambda i: (cidx(i), 0, 0))
    return pl.pallas_call(
        _router_kernel,
        grid=(NT // tm,),
        in_specs=[
            pl.BlockSpec((tm, D), lambda i: (i, 0)),
            mod, mod,
            pl.BlockSpec((1, D), lambda i: (0, 0)),
            pl.BlockSpec((D, 128), lambda i: (0, 0)),
            pl.BlockSpec((1, 128), lambda i: (0, 0)),
            pl.BlockSpec((tm, tm), lambda i: (0, 0)),
            pl.BlockSpec((tm, tm), lambda i: (0, 0)),
        ],
        out_specs=[
            pl.BlockSpec((tm, D), lambda i: (i, 0)),
            pl.BlockSpec((tm, 128), lambda i: (i, 0)),
            pl.BlockSpec((tm, 128), lambda i: (i, 0)),
            pl.BlockSpec((128, tm), lambda i: (0, i)),
            pl.BlockSpec((1, 1, 128), lambda i: (i, 0, 0)),
        ],
        out_shape=[
            jax.ShapeDtypeStruct((NT, D), bf16),
            jax.ShapeDtypeStruct((NT, 128), f32),
            jax.ShapeDtypeStruct((NT, 128), f32),
            jax.ShapeDtypeStruct((128, NT), f32),
            jax.ShapeDtypeStruct((NT // tm, 1, 128), jnp.int32),
        ],
        compiler_params=pltpu.CompilerParams(
            dimension_semantics=("parallel",), vmem_limit_bytes=VMEM_LIMIT),
        name="moe_router",
    )(x, sh, sc, g, w_router, b_router, tri, triu)


def _moe_kernel(cnt_ref, h_ref, rank_ref, rank_t_ref, gates_ref, x_ref, gt_ref, w1_ref, w3_ref, w2_ref,
                op_ref, os_ref, o_ref, xc_scr, acc_scr, rcol_scr, gcol_scr):
    i = pl.program_id(0)
    e = pl.program_id(1)
    f = pl.program_id(2)
    cnt = cnt_ref[i * N_EXPERTS + e]
    n_extra = jnp.maximum(cnt - MOE_MAIN + MOE_EXTRA - 1, 0) // MOE_EXTRA

    def for_each_block(fn):
        fn(0, MOE_MAIN)

        def body(b, carry):
            fn(pl.multiple_of(MOE_MAIN + b * MOE_EXTRA, 32), MOE_EXTRA)
            return carry

        lax.fori_loop(0, n_extra, body, 0)

    @pl.when(jnp.logical_and(e == 0, f == 0))
    def _():
        o_ref[...] = x_ref[...]

    @pl.when(f == 0)
    def _():
        lane = lax.broadcasted_iota(jnp.int32, (MOE_TILE, 128), 1)
        mine = lane == e
        rcol_scr[...] = jnp.sum(jnp.where(mine, rank_ref[...], 0.0), axis=1, keepdims=True).astype(jnp.int32)
        gcol_scr[...] = jnp.sum(jnp.where(mine, gates_ref[...], 0.0), axis=1, keepdims=True)

        rrow = rank_t_ref[pl.ds(e, 1), :].astype(jnp.int32)

        def gather(slot0, nrows):
            slot = lax.broadcasted_iota(jnp.int32, (nrows, MOE_TILE), 0) + (slot0 + RANK_SEL)
            xc_scr[pl.ds(slot0, nrows), :] = _dot((rrow == slot).astype(bf16), h_ref[...]).astype(bf16)

        for_each_block(gather)

    def expert(slot0, nrows):
        rows = pl.ds(slot0, nrows)
        part = _dot(_swiglu_hidden(xc_scr[rows, :], w1_ref[0], w3_ref[0]), w2_ref[0])

        @pl.when(f == 0)
        def _():
            acc_scr[rows, :] = part

        @pl.when(f != 0)
        def _():
            acc_scr[rows, :] += part

    for_each_block(expert)

    @pl.when(f == pl.num_programs(2) - 1)
    def _():
        def scatter(slot0, nrows):
            out = acc_scr[pl.ds(slot0, nrows), :].astype(bf16)
            for t0 in range(0, MOE_TILE, 256):
                rows = slice(t0, t0 + 256)
                slot = lax.broadcasted_iota(jnp.int32, (256, nrows), 1) + slot0
                pt = (rcol_scr[rows, :] == slot).astype(bf16)
                o_ref[rows, :] += (gcol_scr[rows, :] * gt_ref[0]) * _dot(pt, out)

        for_each_block(scatter)

    for dst, mine in ((op_ref, i < NP // MOE_TILE), (os_ref, i >= NP // MOE_TILE)):
        @pl.when(jnp.logical_and(mine, jnp.logical_and(e == pl.num_programs(1) - 1, f == pl.num_programs(2) - 1)))
        def _():
            dst[...] = o_ref[...]


def _moe_experts(cnt, h_bf, rank, rank_t, gates, x, gt, w1, w3, w2):
    tm = MOE_TILE
    tf = EXPERT_DIM // 2
    n_p = NP // tm
    cidx = functools.partial(_cond_idx, tm=tm)
    once = pl.Buffered(1)
    grid_spec = pltpu.PrefetchScalarGridSpec(
        num_scalar_prefetch=1,
        grid=(NT // tm, N_EXPERTS, EXPERT_DIM // tf),
        in_specs=[
            pl.BlockSpec((tm, D), lambda i, e, f, c: (i, 0), pipeline_mode=once),
            pl.BlockSpec((tm, 128), lambda i, e, f, c: (i, 0), pipeline_mode=once),
            pl.BlockSpec((128, tm), lambda i, e, f, c: (0, i), pipeline_mode=once),
            pl.BlockSpec((tm, 128), lambda i, e, f, c: (i, 0), pipeline_mode=once),
            pl.BlockSpec((tm, D), lambda i, e, f, c: (i, 0), pipeline_mode=once),
            pl.BlockSpec((1, 1, D), lambda i, e, f, c: (cidx(i), 0, 0)),
            pl.BlockSpec((1, D, tf), lambda i, e, f, c: (e, 0, f)),
            pl.BlockSpec((1, D, tf), lambda i, e, f, c: (e, 0, f)),
            pl.BlockSpec((1, tf, D), lambda i, e, f, c: (e, f, 0)),
        ],
        out_specs=[
            pl.BlockSpec((tm, D), lambda i, e, f, c: (jnp.minimum(i, n_p - 1), 0), pipeline_mode=once),
            pl.BlockSpec((tm, D), lambda i, e, f, c: (jnp.maximum(i - n_p, 0), 0), pipeline_mode=once),
        ],
        scratch_shapes=[pltpu.VMEM((tm, D), f32),
                        pltpu.VMEM((MOE_ROWS, D), bf16), pltpu.VMEM((MOE_ROWS, D), f32),
                        pltpu.VMEM((tm, 1), jnp.int32), pltpu.VMEM((tm, 1), f32)],
    )
    return pl.pallas_call(
        _moe_kernel,
        grid_spec=grid_spec,
        out_shape=[jax.ShapeDtypeStruct((NP, D), f32), jax.ShapeDtypeStruct((NS, D), f32)],
        compiler_params=pltpu.CompilerParams(
            dimension_semantics=("arbitrary", "arbitrary", "arbitrary"), vmem_limit_bytes=VMEM_LIMIT),
        name="moe_experts",
    )(cnt, h_bf, rank, rank_t, gates, x, gt, w1, w3, w2)


def _in_context_key(k, a, k_a):
    return k * (1.0 + (a - 1.0) * k_a)


def _rwkv_proj_kernel(x_ref, xp_ref, xn_ref, sh_ref, sc_ref, g_ref, mu_ref,
                      wr_ref, wk_ref, wv_ref, g1_ref, g2_ref, w1_ref, w2_ref, a1_ref, a2_ref,
                      w0_ref, a0_ref, kk_ref, ka_ref, rk_ref, gmat_ref,
                      r_out, v_out, kkn_out, bonus_out, gate_out,
                      lwf_out, lwb_out, af_out, ab_out, k_out, *, tm):
    i = pl.program_id(0)
    g, sh, sc = g_ref[...], sh_ref[0], sc_ref[0]
    h = _norm_mod(x_ref[...], g, sh, sc)
    n_prompt_tiles = NP // tm
    tiles_per_seq = SAMPLE_LEN // tm
    pos = (i - n_prompt_tiles) % tiles_per_seq
    has_prev = jnp.logical_and(i >= n_prompt_tiles, pos != 0)
    has_next = jnp.logical_and(i >= n_prompt_tiles, pos != tiles_per_seq - 1)
    h_before = jnp.where(has_prev, _norm_mod(xp_ref[...], g, sh, sc)[7:8], 0.0)
    h_after = jnp.where(has_next, _norm_mod(xn_ref[...], g, sh, sc)[0:1], 0.0)
    rowi = lax.broadcasted_iota(jnp.int32, (tm, 1), 0)
    h_prev = jnp.where(rowi == 0, h_before, pltpu.roll(h, 1, 0))
    h_next = jnp.where(rowi == tm - 1, h_after, pltpu.roll(h, tm - 1, 0))
    xx = 0.5 * (h_prev + h_next) - h
    mix = lambda n: (h + xx * mu_ref[n:n + 1]).astype(bf16)

    r = _dot(mix(0), wr_ref[...])
    k = _dot(mix(2), wk_ref[...])
    v = _dot(mix(3), wv_ref[...])
    gate_out[...] = _dot(_sigmoid(_dot(mix(5), g1_ref[...])).astype(bf16), g2_ref[...]).astype(bf16)

    lo = _lane_lo()
    tw = jnp.tanh(_dot(mix(1), w1_ref[...]))
    ta = _dot(mix(4), a1_ref[...])
    gmat = gmat_ref[...]
    kk = k * kk_ref[...]
    kkn_out[...] = kk * lax.rsqrt(_group_sum(kk * kk, gmat) + 1e-12)
    r_out[...] = r
    v_out[...] = v
    k_out[...] = k
    kd_sum = jnp.zeros_like(k)
    for d, (lw_out, a_out) in enumerate(((lwf_out, af_out), (lwb_out, ab_out))):
        keep = lo if d == 0 else jnp.logical_not(lo)
        zw = w0_ref[d:d + 1] + _dot(jnp.where(keep, tw, 0.0).astype(bf16), w2_ref[...])
        lw_out[...] = -math.exp(-0.5) * _sigmoid(zw)
        a = _sigmoid(a0_ref[d:d + 1] + _dot(jnp.where(keep, ta, 0.0).astype(bf16), a2_ref[...]))
        a_out[...] = a
        kd_sum = kd_sum + _in_context_key(k, a, ka_ref[...])
    bonus_out[...] = (_group_sum(r * kd_sum * rk_ref[...], gmat) * v).astype(bf16)


def _rwkv_projection(x, sh, sc, g, mu, wr, wk, wv, g1, g2, w1, w2, a1, a2, w0, a0, k_k, k_a, r_k, gmat):
    tm = RWKV_PROJ_ROWS
    cidx = functools.partial(_cond_idx, tm=tm)
    hb = tm // 8
    n8 = NT // 8
    full = lambda shape: pl.BlockSpec(shape, lambda i: tuple(0 for _ in shape))
    row = pl.BlockSpec((tm, D), lambda i: (i, 0))
    return pl.pallas_call(
        functools.partial(_rwkv_proj_kernel, tm=tm),
        grid=(NT // tm,),
        in_specs=[
            row,
            pl.BlockSpec((8, D), lambda i: (jnp.maximum(i * hb - 1, 0), 0)),
            pl.BlockSpec((8, D), lambda i: (jnp.minimum((i + 1) * hb, n8 - 1), 0)),
            pl.BlockSpec((1, 1, D), lambda i: (cidx(i), 0, 0)),
            pl.BlockSpec((1, 1, D), lambda i: (cidx(i), 0, 0)),
            full((1, D)), full((6, D)),
            full((D, D)), full((D, D)), full((D, D)),
            full((D, GATE_LORA_PAD)), full((GATE_LORA_PAD, D)),
            full((D, 2 * LORA)), full((2 * LORA, D)), full((D, 2 * LORA)), full((2 * LORA, D)),
            full((2, D)), full((2, D)), full((1, D)), full((1, D)), full((1, D)),
            full((256, 256)),
        ],
        out_specs=[row] * 10,
        out_shape=[jax.ShapeDtypeStruct((NT, D), dt) for dt in (f32, f32, f32, bf16, bf16) + (f32,) * 5],
        compiler_params=pltpu.CompilerParams(
            dimension_semantics=("parallel",), vmem_limit_bytes=VMEM_LIMIT),
        name="rwkv_projection",
    )(x, x, x, sh, sc, g, mu, wr, wk, wv, g1, g2, w1, w2, a1, a2, w0, a0, k_k, k_a, r_k, gmat)


def _split3(x):
    x1 = x.astype(bf16)
    r1 = x - x1.astype(f32)
    x2 = r1.astype(bf16)
    x3 = (r1 - x2.astype(f32)).astype(bf16)
    return x1, x2, x3


def _scan_kernel(*refs, nc, has_init, emit_state, n_casts):
    fwd_refs = refs[0:6]
    bwd_refs = refs[6:12]
    ka_ref, tri_ref, mask_ref = refs[12:15]
    pos = 15
    s0_ref = None
    if has_init:
        s0_ref = refs[pos]
        pos += 1
    cast_in = refs[pos:pos + n_casts]
    pos += n_casts
    yf_ref, yb_ref = refs[pos:pos + 2]
    pos += 2
    sout_ref = None
    if emit_state:
        sout_ref = refs[pos]
        pos += 1
    cast_out = refs[pos:pos + n_casts]
    pos += n_casts
    st_scr, cl_scr = refs[pos:pos + 2]

    for src, dst in zip(cast_in, cast_out):
        dst[...] = src[...].astype(bf16)

    s = pl.program_id(1)

    @pl.when(s == 0)
    def _():
        if has_init:
            st_scr[...] = s0_ref[0]
        else:
            st_scr[...] = jnp.zeros_like(st_scr)

    c = CHUNK
    for d, drefs in enumerate((fwd_refs, bwd_refs)):
        lw = drefs[3][...]
        tri = tri_ref[d]
        p1, p2, p3 = _split3(lw)
        cl_scr[d] = _dot(tri, p1) + _dot(tri, p2) + _dot(tri, p3)

    lane = lax.broadcasted_iota(jnp.int32, (1, 128), 1)
    m0 = (lane < HD).astype(f32)
    m1 = 1.0 - m0
    rid = lax.broadcasted_iota(jnp.int32, (128, 128), 0)
    cid = lax.broadcasted_iota(jnp.int32, (128, 128), 1)
    eye = (rid == cid).astype(f32)

    n_pairs = RW_HEADS // 2
    nb = 2 * n_pairs

    def per_head_rows(x):
        return jnp.concatenate([x * m0, x * m1], axis=0)

    ars_l, bk_l, v2_l, dec_l = [], [], [], []
    for d, drefs in enumerate((fwd_refs, bwd_refs)):
        r_ref, v_ref, kk_ref, lw_ref, a_ref, k_ref = drefs
        end_row = c - 1 if d == 0 else 0
        for p in range(n_pairs):
            ln = slice(p * 128, (p + 1) * 128)
            cl = cl_scr[d, :, ln]
            kk = kk_ref[:, ln]
            tot = cl[end_row:end_row + 1]
            e_inv = jnp.exp(-cl)
            a = a_ref[:, ln]
            kka = kk * a
            kd = _in_context_key(k_ref[:, ln], a, ka_ref[:, ln])
            at = per_head_rows(-kk * jnp.exp(cl - lw_ref[:, ln]))
            rt = per_head_rows(r_ref[:, ln] * jnp.exp(cl))
            ars_l.append(jnp.concatenate([at, rt], axis=0).astype(bf16))
            bk_l.append(jnp.concatenate([per_head_rows(kka * e_inv), per_head_rows(kd * e_inv)],
                                        axis=0).astype(bf16))
            v2_l.append(per_head_rows(v_ref[:, ln]).astype(bf16))
            dec_l.append(jnp.exp(tot))
    ar = jnp.stack(ars_l)
    bk = jnp.stack(bk_l)
    v2 = jnp.stack(v2_l)
    dec = jnp.stack(dec_l)

    st = st_scr[...].reshape(nb, 128, 128)
    g2 = _bdot_nt(ar, bk).reshape(2, n_pairs, 256, 256) * mask_ref[...][:, None]
    g2 = g2.reshape(nb, 256, 256)
    ars = _bdot_nt(ar, st.astype(bf16))
    gv = _bdot(g2[:, :, 128:].astype(bf16), v2)
    l_bd = g2[:, :128, :128]
    l_bf = l_bd.astype(bf16)
    pk = _bdot(l_bf, l_bf)
    q = eye[None] + l_bd
    for it in range(5):
        pkb = pk.astype(bf16)
        if it < 4:
            res = _bdot(jnp.concatenate([q, pk], axis=1).astype(bf16), pkb)
            q = q + res[:, :128]
            pk = res[:, 128:]
        else:
            q = q + _bdot(q.astype(bf16), pkb)
    rhs = ars[:, :128] + gv[:, :128]
    u2b = _bdot(q.astype(bf16), rhs.astype(bf16)).astype(bf16)
    y2 = ars[:, 128:] + gv[:, 128:] + _bdot(g2[:, 128:, :128].astype(bf16), u2b)
    y = y2[:, :c] + y2[:, c:]
    for d, y_ref in enumerate((yf_ref, yb_ref)):
        for p in range(n_pairs):
            y_ref[:, p * 128:(p + 1) * 128] = y[d * n_pairs + p].astype(y_ref.dtype)
    uv = jnp.concatenate([u2b, v2], axis=1)
    st_scr[...] = ((st + _bdot_tn(uv, bk)) * dec).reshape(2, n_pairs, 128, 128)

    if emit_state:
        @pl.when(s == nc - 1)
        def _():
            for d in range(2):
                for p in range(n_pairs):
                    m = st_scr[d, p]
                    sout_ref[0, d, 2 * p] = m[:HD, :HD]
                    sout_ref[0, d, 2 * p + 1] = pltpu.roll(m[HD:, :], HD, 1)[:, :HD]


def _rwkv_scan(streams_f, streams_b, k_a, tri, mask, s0_bd, casts, *, n_seq, seq_len, row0, emit_state):
    nc = seq_len // CHUNK
    blk0 = row0 // CHUNK
    fwd_spec = pl.BlockSpec((CHUNK, D), lambda b, s: (blk0 + b * nc + s, 0))
    bwd_spec = pl.BlockSpec((CHUNK, D), lambda b, s: (blk0 + b * nc + nc - 1 - s, 0))
    in_specs = [fwd_spec] * 6 + [bwd_spec] * 6 + [
        pl.BlockSpec((1, D), lambda b, s: (0, 0)),
        pl.BlockSpec((2, CHUNK, CHUNK), lambda b, s: (0, 0, 0)),
        pl.BlockSpec((2, 256, 256), lambda b, s: (0, 0, 0)),
    ]
    args = list(streams_f) + list(streams_b) + [k_a, tri, mask]
    state_block = (1, 2, RW_HEADS // 2, 128, 128)
    if s0_bd is not None:
        in_specs.append(pl.BlockSpec(state_block, lambda b, s: (b, 0, 0, 0, 0)))
        args.append(s0_bd)
    out_specs = [pl.BlockSpec((CHUNK, D), lambda b, s: (b * nc + s, 0)),
                 pl.BlockSpec((CHUNK, D), lambda b, s: (b * nc + nc - 1 - s, 0))]
    out_shape = [jax.ShapeDtypeStruct((n_seq * seq_len, D), bf16)] * 2
    if emit_state:
        out_specs.append(pl.BlockSpec((1, 2, RW_HEADS, HD, HD), lambda b, s: (b, 0, 0, 0, 0)))
        out_shape.append(jax.ShapeDtypeStruct((n_seq, 2, RW_HEADS, HD, HD), f32))
    for w in casts:
        per_expert = (n_seq * nc) // w.shape[0]
        spec = pl.BlockSpec((1, w.shape[1] // per_expert, w.shape[2]),
                            lambda b, s, per_expert=per_expert: ((b * nc + s) // per_expert, (b * nc + s) % per_expert, 0))
        in_specs.append(spec)
        args.append(w)
        out_specs.append(spec)
        out_shape.append(jax.ShapeDtypeStruct(w.shape, bf16))
    return pl.pallas_call(
        functools.partial(_scan_kernel, nc=nc, has_init=s0_bd is not None, emit_state=emit_state,
                          n_casts=len(casts)),
        grid=(n_seq, nc),
        in_specs=in_specs,
        out_specs=out_specs,
        out_shape=out_shape,
        scratch_shapes=[pltpu.VMEM(state_block[1:], f32), pltpu.VMEM((2, CHUNK, D), f32)],
        compiler_params=pltpu.CompilerParams(
            dimension_semantics=("parallel", "arbitrary"), vmem_limit_bytes=VMEM_LIMIT),
        name="rwkv_scan_prompt" if emit_state else "rwkv_scan_sample",
    )(*args)


def _scan_constants():
    t = jnp.arange(CHUNK)
    lower = (t[:, None] >= t[None, :])
    tri = jnp.stack([lower, lower.T]).astype(bf16)
    masks = []
    for d in range(2):
        strict = (t[:, None] > t[None, :]) if d == 0 else (t[:, None] < t[None, :])
        incl = lower if d == 0 else lower.T
        blocks = []
        for m in (strict, incl):
            bd = jnp.kron(jnp.eye(2, dtype=f32), m.astype(f32))
            blocks.append(jnp.concatenate([bd, bd], axis=1))
        masks.append(jnp.concatenate(blocks, axis=0))
    return tri, jnp.stack(masks)


def _state_to_blockdiag(s):
    n = s.shape[0]
    s = s.reshape(n, 2, RW_HEADS // 2, 2, HD, HD)
    z = jnp.zeros_like(s[:, :, :, 0])
    top = jnp.concatenate([s[:, :, :, 0], z], axis=-1)
    bot = jnp.concatenate([z, s[:, :, :, 1]], axis=-1)
    return jnp.concatenate([top, bot], axis=-2)


def _rope_tables():
    t = jnp.arange(SAMPLE_LEN)
    rows = (t // GRID_W).astype(f32)
    cols = (t % GRID_W).astype(f32)
    nf = HD // 4
    inv = 10000.0 ** (-jnp.arange(nf, dtype=f32) / nf)
    ang = jnp.concatenate([rows[:, None] * inv, cols[:, None] * inv], axis=-1)
    cos = jnp.repeat(jnp.cos(ang), 2, axis=-1)
    sin = jnp.repeat(jnp.sin(ang), 2, axis=-1) * jnp.tile(jnp.array([-1.0, 1.0], f32), HD // 2)
    return jnp.tile(cos, (1, 2)), jnp.tile(sin, (1, 2))


def kernel(x_prompt, x_sample, cache_diff_k, cache_diff_v, cache_na_k, cache_na_v, state_rwkv, c, c_ctx, w_ada, b_ada, g_mix, g_ffn, w_in, w_out, diff_q_g, diff_k_g, diff_lam_q1, diff_lam_k1, diff_lam_q2, diff_lam_k2, diff_subln_g, na_q_g, na_k_g, na_rpb, ffn_w1, ffn_w3, ffn_w2, rw_mu, rw_wr, rw_wk, rw_wv, rw_wo, rw_w0, rw_w1, rw_w2, rw_a0, rw_a1, rw_a2, rw_g1, rw_g2, rw_k_k, rw_k_a, rw_r_k, rw_ln_g, rw_ln_b, moe_router, moe_router_b, moe_w1, moe_w3, moe_w2):
    xp = x_prompt.reshape(NP, D)
    xs = x_sample.reshape(NS, D)
    cond8 = jnp.concatenate([c_ctx[None, :], c, jnp.zeros((3, D), f32)], axis=0)
    mod = _ada_table(cond8, w_ada, b_ada)
    gmat = jnp.kron(jnp.eye(4, dtype=f32), jnp.ones((HD, HD), f32)).astype(bf16)

    lam_init = 0.8 - 0.6 * math.exp(-0.3 * 0)
    ones_seg = jnp.ones((SEG,), f32)
    tile8 = lambda gvec: jnp.tile(gvec, SEG // HD)
    gains = jnp.stack([tile8(diff_q_g[0]), tile8(diff_k_g[0]), ones_seg,
                       tile8(na_q_g[0]), tile8(na_k_g[0]), ones_seg]).reshape(6, 1, SEG)
    cos_t, sin_t = _rope_tables()
    proj, dk_p, dv_p, nk_p, nv_p, ffn_w1_bf, ffn_w3_bf, ffn_w2_bf = _in_projection(
        xp, xs, mod[0][0], mod[0][1], g_mix[0][None, :], w_in[0].astype(bf16), gains, gmat, cos_t, sin_t,
        (ffn_w1[0], ffn_w3[0], ffn_w2[0]))
    lamp = jnp.stack([diff_lam_q1[0], diff_lam_k1[0], diff_lam_q2[0], diff_lam_k2[0]])
    subg = diff_subln_g[0][None, :]
    o_prompt = _prompt_attention(proj, lamp, subg, lam_init)
    o_diff = _latent_diff_attention(proj, cache_diff_k[:, 0].reshape(N_SAMPLE_SEQ, PAST, SEG),
                                    cache_diff_v[:, 0].reshape(N_SAMPLE_SEQ, PAST, SEG), lamp, subg, lam_init)
    o_na = _latent_na_attention(proj, cache_na_k[:, 0].reshape(N_SAMPLE_SEQ, PAST, SEG),
                                cache_na_v[:, 0].reshape(N_SAMPLE_SEQ, PAST, SEG), _rpb_table(na_rpb[0]))
    x = _out_projection(o_prompt, o_diff, o_na, w_out[0].astype(bf16), xp, xs, mod[0][2])
    x = _dense_ffn(x, mod[0][3], mod[0][4], mod[0][5], g_ffn[0][None, :],
                   ffn_w1_bf, ffn_w3_bf, ffn_w2_bf)

    pad_g = GATE_LORA_PAD - GATE_LORA
    g1 = jnp.pad(rw_g1[0], ((0, 0), (0, pad_g))).astype(bf16)
    g2 = jnp.pad(rw_g2[0], ((0, pad_g), (0, 0))).astype(bf16)
    w1cat = jnp.concatenate([rw_w1[0, 0], rw_w1[0, 1]], axis=1).astype(bf16)
    w2cat = jnp.concatenate([rw_w2[0, 0], rw_w2[0, 1]], axis=0).astype(bf16)
    a1cat = jnp.concatenate([rw_a1[0, 0], rw_a1[0, 1]], axis=1).astype(bf16)
    a2cat = jnp.concatenate([rw_a2[0, 0], rw_a2[0, 1]], axis=0).astype(bf16)
    k_a = rw_k_a[0][None, :]
    (r, v, kkn, bonus, gate_lora, lwf, lwb, af, ab, k) = _rwkv_projection(
        x, mod[1][0], mod[1][1], g_mix[1][None, :], rw_mu[0],
        rw_wr[0].astype(bf16), rw_wk[0].astype(bf16), rw_wv[0].astype(bf16), g1, g2,
        w1cat, w2cat, a1cat, a2cat, rw_w0[0], rw_a0[0],
        rw_k_k[0][None, :], k_a, rw_r_k[0].reshape(1, D), gmat)
    tri, mask = _scan_constants()
    streams_f = (r, v, kkn, lwf, af, k)
    streams_b = (r, v, kkn, lwb, ab, k)
    yf_p, yb_p, st_p, moe_w1_bf, moe_w3_bf = _rwkv_scan(
        streams_f, streams_b, k_a, tri, mask, None, (moe_w1[0], moe_w3[0]),
        n_seq=N_PROMPT_SEQ, seq_len=PROMPT_LEN, row0=0, emit_state=True)
    yf_s, yb_s, moe_w2_bf = _rwkv_scan(
        streams_f, streams_b, k_a, tri, mask, _state_to_blockdiag(state_rwkv[:, 0]), (moe_w2[0],),
        n_seq=N_SAMPLE_SEQ, seq_len=SAMPLE_LEN, row0=NP, emit_state=False)
    x = _rwkv_out_projection(yf_p, yb_p, yf_s, yb_s, bonus, gate_lora, rw_ln_g[0][None, :], rw_ln_b[0][None, :], gmat,
                             rw_wo[0].astype(bf16), x, mod[1][2])
    w_router = jnp.pad(moe_router[0], ((0, 0), (0, 128 - N_EXPERTS)))
    b_router = jnp.pad(moe_router_b[0], (0, 128 - N_EXPERTS))[None, :]
    h_bf, gates, rank, rank_t, cnt = _moe_route(x, mod[1][3], mod[1][4], g_ffn[1][None, :], w_router, b_router)
    y_p, y_s = _moe_experts(cnt[:, 0, :N_EXPERTS].reshape(-1), h_bf, rank, rank_t, gates, x, mod[1][5],
                            moe_w1_bf, moe_w3_bf, moe_w2_bf)

    new_dk = dk_p.reshape(N_PROMPT_SEQ, 1, PROMPT_LEN, DIFF_HEADS, 2 * HD)
    new_dv = dv_p.reshape(N_PROMPT_SEQ, 1, PROMPT_LEN, DIFF_HEADS, 2 * HD)
    new_nk = nk_p.reshape(N_PROMPT_SEQ, 1, PROMPT_LEN, NA_HEADS, HD)
    new_nv = nv_p.reshape(N_PROMPT_SEQ, 1, PROMPT_LEN, NA_HEADS, HD)
    new_state = st_p[:, None]
    return (y_p.reshape(N_PROMPT_SEQ, PROMPT_LEN, D), y_s.reshape(N_SAMPLE_SEQ, SAMPLE_LEN, D),
            new_dk, new_dv, new_nk, new_nv, new_state)
```
